```python
import jax, jax.numpy as jnp
from jax import lax
import numpy as np

D_MODEL = 1024
BATCH = 8
SEQ = 2048
DEPTH = 4

HEAD_DIM = 64
N_HEADS_TOTAL = D_MODEL // HEAD_DIM
N_HEADS_SB = N_HEADS_TOTAL // 4
N_HEADS_MOBA = N_HEADS_TOTAL // 2
N_GROUPS_GMLP = N_HEADS_TOTAL - N_HEADS_SB - N_HEADS_MOBA
W_SB = N_HEADS_SB * HEAD_DIM
W_MOBA = N_HEADS_MOBA * HEAD_DIM
W_GMLP = N_GROUPS_GMLP * HEAD_DIM
D_MIX = W_SB + W_MOBA + W_GMLP
IN_COLS = 3 * W_SB + 3 * W_MOBA + 2 * W_GMLP

SB_Q_BLOCK = 128
MOBA_BLOCK = 256
MOBA_TOPK = 3
MOBA_Q_CHUNK = 16
GMLP_CHUNK = 128
ROPE_THETA = 500000.0
ROT_DIM = HEAD_DIM // 4

N_EXPERT_GROUPS = 4
EXPERTS_PER_GROUP = 8
N_EXPERTS = N_EXPERT_GROUPS * EXPERTS_PER_GROUP
D_EXPERT = D_MODEL // 2
MOE_TOP_K = 2
MOE_BLOCK = 256
NORM_EPS = 1e-6
OUT_SCALE = (2 * DEPTH) ** -0.5

kernel_name = "hybrid_stickbreak_moba_gmlp_hmoe"


def _rmsnorm(x, g):
    xf = x.astype(jnp.float32)
    y = xf * lax.rsqrt(jnp.mean(xf * xf, axis=-1, keepdims=True) + NORM_EPS)
    return y.astype(x.dtype) * g


def _partial_rope(x, positions):
    half = ROT_DIM // 2
    inv_freq = jnp.power(ROPE_THETA, -jnp.arange(half, dtype=jnp.float32) / half)
    ang = positions.astype(jnp.float32)[..., None] * inv_freq
    cos = jnp.cos(ang)[:, :, None, :]
    sin = jnp.sin(ang)[:, :, None, :]
    xr = x[..., :ROT_DIM].astype(jnp.float32)
    x1, x2 = xr[..., :half], xr[..., half:]
    rot = jnp.concatenate([x1 * cos - x2 * sin, x2 * cos + x1 * sin], axis=-1)
    return jnp.concatenate([rot.astype(x.dtype), x[..., ROT_DIM:]], axis=-1)


def _stick_breaking_attention(q, k, v):
    S = q.shape[1]
    scale = HEAD_DIM ** -0.5
    outs = []
    for i in range(S // SB_Q_BLOCK):
        t0 = i * SB_Q_BLOCK
        end = t0 + SB_Q_BLOCK
        z = jnp.einsum('bthd,bshd->bhts', q[:, t0:end], k[:, :end]).astype(jnp.float32) * scale
        t_pos = t0 + jnp.arange(SB_Q_BLOCK)
        s_pos = jnp.arange(end)
        causal = s_pos[None, :] < t_pos[:, None]
        log_rest = jnp.where(causal, jax.nn.log_sigmoid(-z), 0.0)
        log_after = lax.cumsum(log_rest, axis=3, reverse=True) - log_rest
        w = jnp.where(causal, jnp.exp(jax.nn.log_sigmoid(z) + log_after), 0.0)
        outs.append(jnp.einsum('bhts,bshd->bthd', w.astype(v.dtype), v[:, :end]))
    return jnp.concatenate(outs, axis=1)


def _moba_attention(q, k, v):
    B, S, H, d = q.shape
    n_blk = -(-S // MOBA_BLOCK)
    s_pad = n_blk * MOBA_BLOCK
    topk = min(MOBA_TOPK, max(n_blk - 1, 1))
    scale = HEAD_DIM ** -0.5
    neg = jnp.float32(-1e30)
    qt = q.transpose(0, 2, 1, 3)
    pad = ((0, 0), (0, 0), (0, s_pad - S), (0, 0))
    kt = jnp.pad(k.transpose(0, 2, 1, 3), pad).reshape(B, H, n_blk, MOBA_BLOCK, d)
    vt = jnp.pad(v.transpose(0, 2, 1, 3), pad).reshape(B, H, n_blk, MOBA_BLOCK, d)
    k_mean = jnp.mean(kt.astype(jnp.float32), axis=3)
    b_idx = jnp.arange(B)[:, None, None, None]
    h_idx = jnp.arange(H)[None, :, None, None]

    def chunk(c):
        t0 = c * MOBA_Q_CHUNK
        qc = lax.dynamic_slice_in_dim(qt, t0, MOBA_Q_CHUNK, axis=2)
        t_pos = t0 + jnp.arange(MOBA_Q_CHUNK)
        own = t0 // MOBA_BLOCK
        gate = jnp.einsum('bhqd,bhnd->bhqn', qc.astype(jnp.float32), k_mean)
        gate = jnp.where(jnp.arange(n_blk) < own, gate, neg)
        _, sel = lax.top_k(gate, topk)
        sel_ok = sel < own
        k_sel = kt[b_idx, h_idx, sel]
        v_sel = vt[b_idx, h_idx, sel]
        s_sel = jnp.einsum('bhqd,bhqnkd->bhqnk', qc, k_sel).astype(jnp.float32) * scale
        s_sel = jnp.where(sel_ok[..., None], s_sel, neg).reshape(B, H, MOBA_Q_CHUNK, topk * MOBA_BLOCK)
        k_own = lax.dynamic_index_in_dim(kt, own, axis=2, keepdims=False)
        v_own = lax.dynamic_index_in_dim(vt, own, axis=2, keepdims=False)
        s_own = jnp.einsum('bhqd,bhkd->bhqk', qc, k_own).astype(jnp.float32) * scale
        key_pos = own * MOBA_BLOCK + jnp.arange(MOBA_BLOCK)
        s_own = jnp.where(key_pos[None, :] <= t_pos[:, None], s_own, neg)
        p = jax.nn.softmax(jnp.concatenate([s_sel, s_own], axis=-1), axis=-1).astype(v.dtype)
        p_sel, p_own = p[..., :topk * MOBA_BLOCK], p[..., topk * MOBA_BLOCK:]
        v_sel = v_sel.reshape(B, H, MOBA_Q_CHUNK, topk * MOBA_BLOCK, d)
        return (jnp.einsum('bhqn,bhqnd->bhqd', p_sel, v_sel)
                + jnp.einsum('bhqk,bhkd->bhqd', p_own, v_own))

    o = lax.map(chunk, jnp.arange(S // MOBA_Q_CHUNK))
    return o.transpose(1, 0, 3, 2, 4).reshape(B, S, H, d)


def _gmlp_spatial_gating(u, v, g_vnorm, w_s, b_s):
    B, S, G, d = v.shape
    u = jax.nn.gelu(u)
    v = _rmsnorm(jax.nn.gelu(v), g_vnorm)
    mask = jnp.tril(jnp.ones((GMLP_CHUNK, GMLP_CHUNK), dtype=w_s.dtype))
    vc = v.reshape(B, S // GMLP_CHUNK, GMLP_CHUNK, G, d)
    mixed = jnp.einsum('gts,bcsgd->bctgd', w_s * mask, vc) + b_s.T[:, :, None]
    return u * mixed.reshape(B, S, G, d)


def _hierarchical_moe(h, w_rg, w_re, w_gate, w_up, w_down):
    T, D = h.shape
    hf = h.astype(jnp.float32)
    p_group = jax.nn.softmax(hf @ w_rg.astype(jnp.float32), axis=-1)
    g_sel = jnp.argmax(p_group, axis=-1)
    p_g = jnp.max(p_group, axis=-1)
    logit_e = (hf @ w_re.astype(jnp.float32)).reshape(T, N_EXPERT_GROUPS, EXPERTS_PER_GROUP)
    logit_e = jnp.take_along_axis(logit_e, g_sel[:, None, None], axis=1)[:, 0]
    top_p, top_i = lax.top_k(jax.nn.softmax(logit_e, axis=-1), MOE_TOP_K)
    top_p = top_p / jnp.sum(top_p, axis=-1, keepdims=True)
    expert_id = (g_sel[:, None] * EXPERTS_PER_GROUP + top_i).reshape(-1).astype(jnp.int32)
    gate_w = (p_g[:, None] * top_p).reshape(-1).astype(h.dtype)
    token_id = jnp.repeat(jnp.arange(T, dtype=jnp.int32), MOE_TOP_K)
    n_slots = T * MOE_TOP_K
    n_blocks = -(-n_slots // MOE_BLOCK) + N_EXPERTS
    order = jnp.argsort(expert_id)
    e_sorted = expert_id[order]
    counts = jnp.bincount(expert_id, length=N_EXPERTS)
    padded = (counts + MOE_BLOCK - 1) // MOE_BLOCK * MOE_BLOCK
    pad_end = jnp.cumsum(padded)
    pad_start = pad_end - padded
    start = jnp.cumsum(counts) - counts
    dest = pad_start[e_sorted] + jnp.arange(n_slots) - start[e_sorted]
    buf_tok = jnp.full((n_blocks * MOE_BLOCK,), T, dtype=jnp.int32).at[dest].set(token_id[order])
    buf_w = jnp.zeros((n_blocks * MOE_BLOCK,), h.dtype).at[dest].set(gate_w[order])
    blk_expert = jnp.minimum(
        jnp.searchsorted(pad_end, jnp.arange(n_blocks) * MOE_BLOCK, side='right'), N_EXPERTS - 1)
    h_pad = jnp.concatenate([h, jnp.zeros((1, D), h.dtype)], axis=0)

    def expert_block(args):
        tok, e = args
        xb = h_pad[tok]
        a = jax.nn.silu(xb @ w_gate[e]) * (xb @ w_up[e])
        return a @ w_down[e]

    y = lax.map(expert_block, (buf_tok.reshape(n_blocks, MOE_BLOCK), blk_expert))
    out = jnp.zeros((T + 1, D), h.dtype).at[buf_tok].add(y.reshape(-1, D) * buf_w[:, None])
    return out[:T]


def setup_inputs(seed: int = 0) -> dict:
    key = jax.random.key(seed)
    ks = jax.random.split(key, 16)
    f32 = jnp.float32
    x = jax.random.normal(ks[0], (BATCH, SEQ, D_MODEL), f32)
    start = jax.random.randint(ks[1], (BATCH, 1), 0, 4096, dtype=jnp.int32)
    positions = (start + jnp.arange(SEQ, dtype=jnp.int32)[None, :]).astype(jnp.int32)
    w_in = jax.random.normal(ks[2], (DEPTH, D_MODEL, IN_COLS), f32) * D_MODEL ** -0.5
    w_out = jax.random.normal(ks[3], (DEPTH, D_MIX, D_MODEL), f32) * D_MIX ** -0.5 * OUT_SCALE
    g_mix_norm = 1.0 + 0.02 * jax.random.normal(ks[4], (DEPTH, D_MODEL), f32)
    g_head_norm = 1.0 + 0.02 * jax.random.normal(ks[5], (DEPTH, N_HEADS_TOTAL, HEAD_DIM), f32)
    g_gmlp_vnorm = 1.0 + 0.02 * jax.random.normal(ks[6], (DEPTH, N_GROUPS_GMLP, HEAD_DIM), f32)
    w_spatial = jax.random.normal(ks[7], (DEPTH, N_GROUPS_GMLP, GMLP_CHUNK, GMLP_CHUNK), f32) * GMLP_CHUNK ** -0.5
    b_spatial = 1.0 + 0.1 * jax.random.normal(ks[8], (DEPTH, N_GROUPS_GMLP, GMLP_CHUNK), f32)
    g_ffn_norm = 1.0 + 0.02 * jax.random.normal(ks[9], (DEPTH, D_MODEL), f32)
    w_router_group = jax.random.normal(ks[10], (DEPTH, D_MODEL, N_EXPERT_GROUPS), f32) * D_MODEL ** -0.5
    w_router_expert = jax.random.normal(ks[11], (DEPTH, D_MODEL, N_EXPERTS), f32) * D_MODEL ** -0.5
    w_expert_gate = jax.random.normal(ks[12], (DEPTH, N_EXPERTS, D_MODEL, D_EXPERT), f32) * D_MODEL ** -0.5
    w_expert_up = jax.random.normal(ks[13], (DEPTH, N_EXPERTS, D_MODEL, D_EXPERT), f32) * D_MODEL ** -0.5
    w_expert_down = jax.random.normal(ks[14], (DEPTH, N_EXPERTS, D_EXPERT, D_MODEL), f32) * D_EXPERT ** -0.5 * OUT_SCALE
    g_final = 1.0 + 0.02 * jax.random.normal(ks[15], (D_MODEL,), f32)
    return {"x": x, "positions": positions, "w_in": w_in, "w_out": w_out,
            "g_mix_norm": g_mix_norm, "g_head_norm": g_head_norm, "g_gmlp_vnorm": g_gmlp_vnorm,
            "w_spatial": w_spatial, "b_spatial": b_spatial, "g_ffn_norm": g_ffn_norm,
            "w_router_group": w_router_group, "w_router_expert": w_router_expert,
            "w_expert_gate": w_expert_gate, "w_expert_up": w_expert_up,
            "w_expert_down": w_expert_down, "g_final": g_final}


def reference(x, positions, w_in, w_out, g_mix_norm, g_head_norm, g_gmlp_vnorm, w_spatial,
              b_spatial, g_ffn_norm, w_router_group, w_router_expert, w_expert_gate,
              w_expert_up, w_expert_down, g_final):
    B, S, _ = x.shape
    widths = [W_SB] * 3 + [W_MOBA] * 3 + [W_GMLP] * 2
    split_points = [int(p) for p in np.cumsum(widths)[:-1]]
    for l in range(DEPTH):
        h = _rmsnorm(x, g_mix_norm[l])
        proj = h @ w_in[l]
        qa, ka, va, qb, kb, vb, uc, vc = jnp.split(proj, split_points, axis=-1)
        o_sb = _stick_breaking_attention(qa.reshape(B, S, N_HEADS_SB, HEAD_DIM),
                                         ka.reshape(B, S, N_HEADS_SB, HEAD_DIM),
                                         va.reshape(B, S, N_HEADS_SB, HEAD_DIM))
        qb = _partial_rope(qb.reshape(B, S, N_HEADS_MOBA, HEAD_DIM), positions)
        kb = _partial_rope(kb.reshape(B, S, N_HEADS_MOBA, HEAD_DIM), positions)
        o_moba = _moba_attention(qb, kb, vb.reshape(B, S, N_HEADS_MOBA, HEAD_DIM))
        o_gmlp = _gmlp_spatial_gating(uc.reshape(B, S, N_GROUPS_GMLP, HEAD_DIM),
                                      vc.reshape(B, S, N_GROUPS_GMLP, HEAD_DIM),
                                      g_gmlp_vnorm[l], w_spatial[l], b_spatial[l])
        o = jnp.concatenate([o_sb, o_moba, o_gmlp], axis=2)
        o = _rmsnorm(o, g_head_norm[l]).reshape(B, S, D_MIX)
        x = x + o @ w_out[l]
        h = _rmsnorm(x, g_ffn_norm[l]).reshape(B * S, D_MODEL)
        y = _hierarchical_moe(h, w_router_group[l], w_router_expert[l], w_expert_gate[l],
                              w_expert_up[l], w_expert_down[l])
        x = x + y.reshape(B, S, D_MODEL)
    return _rmsnorm(x, g_final)
```

```python
import functools

import jax
import jax.numpy as jnp
from jax import lax
from jax.experimental import pallas as pl
from jax.experimental.pallas import tpu as pltpu

F32 = jnp.float32
BF16 = jnp.bfloat16

HEAD_DIM = 64
LANES = 128
MXU_COLS = 256
N_HEADS_SB = 4
N_HEADS_MOBA = 8
N_GROUPS_GMLP = 4
W_SB = N_HEADS_SB * HEAD_DIM
W_MOBA = N_HEADS_MOBA * HEAD_DIM
W_GMLP = N_GROUPS_GMLP * HEAD_DIM
MOBA_BLOCK = 256
MOBA_TOPK = 3
GMLP_CHUNK = 128
ROPE_THETA = 500000.0
ROT_DIM = HEAD_DIM // 4
N_EXPERT_GROUPS = 4
EXPERTS_PER_GROUP = 8
N_EXPERTS = N_EXPERT_GROUPS * EXPERTS_PER_GROUP
MOE_TOP_K = 2
MOE_BLOCK = 256
NORM_EPS = 1e-6
ATTN_SCALE = HEAD_DIM ** -0.5
NEG = -1e30
ROUTER_ROWS = 64
VMEM_LIMIT = 48 * 1024 * 1024

C_QA, C_KA, C_VA = 0, W_SB, 2 * W_SB
C_QB = 3 * W_SB
C_KB = C_QB + W_MOBA
C_VB = C_KB + W_MOBA
C_UC = C_VB + W_MOBA
C_VC = C_UC + W_GMLP
IN_COLS = C_VC + W_GMLP


def _nt_dot(a, b):
    return lax.dot_general(a, b, (((1,), (1,)), ((), ())), preferred_element_type=F32)


def _dot(a, b):
    return jnp.dot(a, b, preferred_element_type=F32)


def _rms(x, g):
    return x * lax.rsqrt(jnp.mean(x * x, axis=-1, keepdims=True) + NORM_EPS) * g


def _split_bf16(x):
    hi = x.astype(BF16)
    lo = (x - hi.astype(F32)).astype(BF16)
    return hi, lo


def _inproj_kernel(x_ref, g_ref, w_ref, rc_ref, rs1_ref, rs2_ref, o_ref):
    y = _rms(x_ref[...], g_ref[...]).astype(BF16)
    wide = lambda t_ref: jnp.concatenate([t_ref[...]] * (MXU_COLS // LANES), axis=1)
    rc, rs1, rs2 = wide(rc_ref), wide(rs1_ref), wide(rs2_ref)
    half = ROT_DIM // 2
    for c0 in range(0, IN_COLS, MXU_COLS):
        p = _dot(y, w_ref[:, c0:c0 + MXU_COLS])
        if C_QB <= c0 < C_VB:
            p = (p * rc + pltpu.roll(p, MXU_COLS - half, axis=1) * rs1
                 + pltpu.roll(p, half, axis=1) * rs2)
        if c0 < C_KA or C_QB <= c0 < C_KB:
            p = p * ATTN_SCALE
        o_ref[:, c0:c0 + MXU_COLS] = p.astype(BF16)


def _inproj(x2d, g, w_bf16, rc, rs1, rs2, tm):
    T, D = x2d.shape
    row = lambda i: (i, 0)
    const = lambda i: (0, 0)
    return pl.pallas_call(
        _inproj_kernel,
        grid=(T // tm,),
        in_specs=[pl.BlockSpec((tm, D), row), pl.BlockSpec((1, D), const),
                  pl.BlockSpec((D, IN_COLS), const),
                  pl.BlockSpec((tm, LANES), row), pl.BlockSpec((tm, LANES), row),
                  pl.BlockSpec((tm, LANES), row)],
        out_specs=pl.BlockSpec((tm, IN_COLS), row),
        out_shape=jax.ShapeDtypeStruct((T, IN_COLS), BF16),
        compiler_params=pltpu.CompilerParams(dimension_semantics=("arbitrary",),
                                             vmem_limit_bytes=VMEM_LIMIT),
        name="inproj",
    )(x2d, g, w_bf16, rc, rs1, rs2)


def _sb_kernel(q_ref, k_ref, v_ref, gh_ref, o_ref, *, tq):
    i = pl.program_id(2)
    row = lax.broadcasted_iota(jnp.int32, (tq, tq), 0)
    col = lax.broadcasted_iota(jnp.int32, (tq, tq), 1)
    causal = col < row
    later = jnp.where(row > col, 1.0, 0.0).astype(BF16)
    outs = []
    for hh in range(LANES // HEAD_DIM):
        c0, c1 = hh * HEAD_DIM, (hh + 1) * HEAD_DIM
        qh = q_ref[:, c0:c1]

        def tile(j, carry, diag, c0=c0, c1=c1, qh=qh):
            run, acc = carry
            start = pl.multiple_of(j * tq, tq)
            kj = k_ref[pl.ds(start, tq), c0:c1]
            vj = v_ref[pl.ds(start, tq), c0:c1]
            z = _nt_dot(qh, kj)
            lr = -(jnp.maximum(z, 0.0) + jnp.log(1.0 + jnp.exp(-jnp.abs(z))))
            if diag:
                lr = jnp.where(causal, lr, 0.0)
            hi, lo = _split_bf16(lr)
            after = _dot(hi, later) + _dot(lo, later)
            w = jnp.exp(z + lr + after + run)
            if diag:
                w = jnp.where(causal, w, 0.0)
            acc = acc + _dot(w.astype(BF16), vj)
            run = run + jnp.sum(lr, axis=1, keepdims=True)
            return run, acc

        init = (jnp.zeros((tq, 1), F32), jnp.zeros((tq, HEAD_DIM), F32))
        carry = tile(i, init, True)
        _, acc = lax.fori_loop(0, i, lambda n, c: tile(i - 1 - n, c, False), carry)
        outs.append(_rms(acc, gh_ref[:, c0:c1]))
    o_ref[...] = jnp.concatenate(outs, axis=1).astype(BF16)


def _sb_attention(proj, gh, B, S, tq):
    nq = S // tq
    n_pairs = W_SB // LANES
    return pl.pallas_call(
        functools.partial(_sb_kernel, tq=tq),
        grid=(B, n_pairs, nq),
        in_specs=[pl.BlockSpec((tq, LANES), lambda b, p, i: (b * nq + i, C_QA // LANES + p)),
                  pl.BlockSpec((S, LANES), lambda b, p, i: (b, C_KA // LANES + p)),
                  pl.BlockSpec((S, LANES), lambda b, p, i: (b, C_VA // LANES + p)),
                  pl.BlockSpec((1, LANES), lambda b, p, i: (0, p))],
        out_specs=pl.BlockSpec((tq, LANES), lambda b, p, i: (b * nq + i, p)),
        out_shape=jax.ShapeDtypeStruct((B * S, W_SB), BF16),
        compiler_params=pltpu.CompilerParams(
            dimension_semantics=("arbitrary", "arbitrary", "arbitrary"),
            vmem_limit_bytes=VMEM_LIMIT),
        name="sb_attn",
    )(proj, proj, proj, gh)


def _moba_kernel(q_ref, k_ref, v_ref, gh_ref, o_ref, kaug_ref, rhi_ref, rlo_ref, *, tq, topk):
    i = pl.program_id(2)
    S = k_ref.shape[0]
    n_heads = LANES // HEAD_DIM

    @pl.when(i == 0)
    def _prepare_keys():
        kp = k_ref[...]
        srow = lax.broadcasted_iota(jnp.int32, (S, HEAD_DIM), 0)
        scol = lax.broadcasted_iota(jnp.int32, (S, HEAD_DIM), 1)
        onehot = jnp.where(srow // MOBA_BLOCK == scol, 1.0, 0.0).astype(BF16)
        arow = lax.broadcasted_iota(jnp.int32, (LANES, S), 0) - HEAD_DIM
        acol = lax.broadcasted_iota(jnp.int32, (LANES, S), 1) // MOBA_BLOCK
        avg = jnp.where(arow == acol, 1.0 / MOBA_BLOCK, 0.0).astype(BF16)
        zeros = jnp.zeros((S, HEAD_DIM), BF16)
        for hh in range(n_heads):
            kh = kp[:, hh * HEAD_DIM:(hh + 1) * HEAD_DIM]
            kaug_ref[hh] = jnp.concatenate([kh, onehot], axis=1)
            kmean = _dot(avg, jnp.concatenate([kh, zeros], axis=1))
            hi, lo = _split_bf16(kmean)
            rhi_ref[hh] = hi
            rlo_ref[hh] = lo

    row = lax.broadcasted_iota(jnp.int32, (tq, tq), 0)
    col = lax.broadcasted_iota(jnp.int32, (tq, tq), 1)
    causal = col <= row
    bidx = lax.broadcasted_iota(jnp.int32, (tq, LANES), 1) - HEAD_DIM
    group = 8
    outs = []
    for hh in range(n_heads):
        c0, c1 = hh * HEAD_DIM, (hh + 1) * HEAD_DIM
        qh = q_ref[:, c0:c1]
        qz = jnp.concatenate([qh, jnp.zeros((tq, HEAD_DIM), BF16)], axis=1)
        gate = _nt_dot(qz, rhi_ref[hh]) + _nt_dot(qz, rlo_ref[hh])
        valid = (bidx >= 0) & (bidx < i)
        gm = jnp.where(valid, gate, NEG)
        rank = jnp.zeros((tq, LANES), F32)
        for d in range(1, group):
            wrapped = bidx + d >= group
            nb = jnp.where(wrapped, pltpu.roll(gm, group - d, axis=1),
                           pltpu.roll(gm, LANES - d, axis=1))
            rank = rank + jnp.where(wrapped, jnp.where(nb >= gm, 1.0, 0.0),
                                    jnp.where(nb > gm, 1.0, 0.0))
        allowed = (valid & (rank < topk)) | (bidx == i)
        in_group = (bidx >= 0) & (bidx < group)
        bias = jnp.where(in_group & jnp.logical_not(allowed), NEG, 0.0)
        qaug = jnp.concatenate([qh, bias[:, HEAD_DIM:].astype(BF16)], axis=1)

        def tile(j, carry, diag, hh=hh, c0=c0, c1=c1, qaug=qaug):
            m, l, acc = carry
            start = pl.multiple_of(j * tq, tq)
            kj = kaug_ref[hh, pl.ds(start, tq), :]
            vj = v_ref[pl.ds(start, tq), c0:c1]
            s = _nt_dot(qaug, kj)
            if diag:
                s = jnp.where(causal, s, NEG)
            m_new = jnp.maximum(m, jnp.max(s, axis=1, keepdims=True))
            alpha = jnp.exp(m - m_new)
            p = jnp.exp(s - m_new)
            l = alpha * l + jnp.sum(p, axis=1, keepdims=True)
            acc = alpha * acc + _dot(p.astype(BF16), vj)
            return m_new, l, acc

        init = (jnp.full((tq, 1), NEG, F32), jnp.zeros((tq, 1), F32),
                jnp.zeros((tq, HEAD_DIM), F32))
        carry = tile(i, init, True)
        _, l, acc = lax.fori_loop(0, i, lambda j, c: tile(j, c, False), carry)
        outs.append(_rms(acc / l, gh_ref[:, c0:c1]))
    o_ref[...] = jnp.concatenate(outs, axis=1).astype(BF16)


def _moba_attention(proj, gh, B, S):
    tq = MOBA_BLOCK
    assert S % tq == 0
    nq = S // tq
    n_blk = S // MOBA_BLOCK
    assert n_blk <= 8
    topk = min(MOBA_TOPK, max(n_blk - 1, 1))
    n_pairs = W_MOBA // LANES
    return pl.pallas_call(
        functools.partial(_moba_kernel, tq=tq, topk=topk),
        grid=(B, n_pairs, nq),
        in_specs=[pl.BlockSpec((tq, LANES), lambda b, p, i: (b * nq + i, C_QB // LANES + p)),
                  pl.BlockSpec((S, LANES), lambda b, p, i: (b, C_KB // LANES + p)),
                  pl.BlockSpec((S, LANES), lambda b, p, i: (b, C_VB // LANES + p)),
                  pl.BlockSpec((1, LANES), lambda b, p, i: (0, p))],
        out_specs=pl.BlockSpec((tq, LANES), lambda b, p, i: (b * nq + i, p)),
        out_shape=jax.ShapeDtypeStruct((B * S, W_MOBA), BF16),
        scratch_shapes=[pltpu.VMEM((LANES // HEAD_DIM, S, LANES), BF16),
                        pltpu.VMEM((LANES // HEAD_DIM, LANES, LANES), BF16),
                        pltpu.VMEM((LANES // HEAD_DIM, LANES, LANES), BF16)],
        compiler_params=pltpu.CompilerParams(
            dimension_semantics=("arbitrary", "arbitrary", "arbitrary"),
            vmem_limit_bytes=VMEM_LIMIT),
        name="moba_attn",
    )(proj, proj, proj, gh)


def _gmlp_kernel(u_ref, v_ref, gv_ref, ws_ref, b_ref, gh_ref, o_ref, *, tm):
    gu = jax.nn.gelu(u_ref[...].astype(F32))
    gv = jax.nn.gelu(v_ref[...].astype(F32))
    row = lax.broadcasted_iota(jnp.int32, (GMLP_CHUNK, GMLP_CHUNK), 0)
    col = lax.broadcasted_iota(jnp.int32, (GMLP_CHUNK, GMLP_CHUNK), 1)
    outs = []
    for g in range(N_GROUPS_GMLP):
        c0, c1 = g * HEAD_DIM, (g + 1) * HEAD_DIM
        vn = _rms(gv[:, c0:c1], gv_ref[:, c0:c1]).astype(BF16)
        wm = jnp.where(col <= row, ws_ref[g], 0.0).astype(BF16)
        bias = b_ref[:, c0:c1]
        mixed = jnp.concatenate(
            [_dot(wm, vn[c * GMLP_CHUNK:(c + 1) * GMLP_CHUNK]) + bias
             for c in range(tm // GMLP_CHUNK)], axis=0)
        outs.append(_rms(gu[:, c0:c1] * mixed, gh_ref[:, c0:c1]))
    o_ref[...] = jnp.concatenate(outs, axis=1).astype(BF16)


def _gmlp(proj, gv, ws, b_exp, gh, tm):
    T = proj.shape[0]
    nu, nv = C_UC // W_GMLP, C_VC // W_GMLP
    return pl.pallas_call(
        functools.partial(_gmlp_kernel, tm=tm),
        grid=(T // tm,),
        in_specs=[pl.BlockSpec((tm, W_GMLP), lambda i: (i, nu)),
                  pl.BlockSpec((tm, W_GMLP), lambda i: (i, nv)),
                  pl.BlockSpec((1, W_GMLP), lambda i: (0, 0)),
                  pl.BlockSpec((N_GROUPS_GMLP, GMLP_CHUNK, GMLP_CHUNK), lambda i: (0, 0, 0)),
                  pl.BlockSpec((GMLP_CHUNK, W_GMLP), lambda i: (0, 0)),
                  pl.BlockSpec((1, W_GMLP), lambda i: (0, 0))],
        out_specs=pl.BlockSpec((tm, W_GMLP), lambda i: (i, 0)),
        out_shape=jax.ShapeDtypeStruct((T, W_GMLP), BF16),
        compiler_params=pltpu.CompilerParams(dimension_semantics=("arbitrary",),
                                             vmem_limit_bytes=VMEM_LIMIT),
        name="gmlp",
    )(proj, proj, gv, ws, b_exp, gh)


def _out_kernel(osb_ref, omoba_ref, ogmlp_ref, x_ref, w_ref, g_ref, rhi_ref, rlo_ref,
                x1_ref, h_ref, lg_ref):
    x1 = (x_ref[...]
          + _dot(osb_ref[...], w_ref[0:W_SB, :])
          + _dot(omoba_ref[...], w_ref[W_SB:W_SB + W_MOBA, :])
          + _dot(ogmlp_ref[...], w_ref[W_SB + W_MOBA:, :]))
    x1_ref[...] = x1
    hn = _rms(x1, g_ref[...])
    hi, lo = _split_bf16(hn)
    h_ref[...] = hi
    lg_ref[...] = (_nt_dot(rhi_ref[...], hi) + _nt_dot(rhi_ref[...], lo)
                   + _nt_dot(rlo_ref[...], hi))


def _out_proj(o_sb, o_moba, o_gmlp, x2d, w_bf16, g, r_hi, r_lo, tm):
    T, D = x2d.shape
    row = lambda i: (i, 0)
    const = lambda i: (0, 0)
    return pl.pallas_call(
        _out_kernel,
        grid=(T // tm,),
        in_specs=[pl.BlockSpec((tm, W_SB), row), pl.BlockSpec((tm, W_MOBA), row),
                  pl.BlockSpec((tm, W_GMLP), row), pl.BlockSpec((tm, D), row),
                  pl.BlockSpec((D, D), const), pl.BlockSpec((1, D), const),
                  pl.BlockSpec((ROUTER_ROWS, D), const), pl.BlockSpec((ROUTER_ROWS, D), const)],
        out_specs=[pl.BlockSpec((tm, D), row), pl.BlockSpec((tm, D), row),
                   pl.BlockSpec((ROUTER_ROWS, tm), lambda i: (0, i))],
        out_shape=[jax.ShapeDtypeStruct((T, D), F32), jax.ShapeDtypeStruct((T, D), BF16),
                   jax.ShapeDtypeStruct((ROUTER_ROWS, T), F32)],
        compiler_params=pltpu.CompilerParams(dimension_semantics=("arbitrary",),
                                             vmem_limit_bytes=VMEM_LIMIT),
        name="outproj",
    )(o_sb, o_moba, o_gmlp, x2d, w_bf16, g, r_hi, r_lo)


def _moe_kernel(be_ref, na_ref, xs_ref, wg_ref, wu_ref, wd_ref, ys_ref, wgb, wub, wdb):
    b = pl.program_id(0)
    e = be_ref[b]
    prev = be_ref[jnp.maximum(b - 1, 0)]
    active = b < na_ref[0]

    @pl.when(active & ((b == 0) | (e != prev)))
    def _load_expert():
        wgb[...] = wg_ref[...].astype(BF16)
        wub[...] = wu_ref[...].astype(BF16)
        wdb[...] = wd_ref[...].astype(BF16)

    @pl.when(active)
    def _compute():
        xb = xs_ref[...]
        a = jax.nn.silu(_dot(xb, wgb[...])) * _dot(xb, wub[...])
        ys_ref[...] = _dot(a.astype(BF16), wdb[...]).astype(ys_ref.dtype)

    @pl.when(jnp.logical_not(active))
    def _pad():
        ys_ref[...] = jnp.zeros_like(ys_ref)


def _moe_experts(blk_expert, n_active, xs, w_gate, w_up, w_down, layer):
    n_rows, D = xs.shape
    n_blocks = n_rows // MOE_BLOCK
    DE = w_gate.shape[-1]
    grid_spec = pltpu.PrefetchScalarGridSpec(
        num_scalar_prefetch=2,
        grid=(n_blocks,),
        in_specs=[pl.BlockSpec((MOE_BLOCK, D), lambda b, be, na: (b, 0)),
                  pl.BlockSpec((None, None, D, DE), lambda b, be, na: (layer, be[b], 0, 0)),
                  pl.BlockSpec((None, None, D, DE), lambda b, be, na: (layer, be[b], 0, 0)),
                  pl.BlockSpec((None, None, DE, D), lambda b, be, na: (layer, be[b], 0, 0))],
        out_specs=pl.BlockSpec((MOE_BLOCK, D), lambda b, be, na: (b, 0)),
        scratch_shapes=[pltpu.VMEM((D, DE), BF16), pltpu.VMEM((D, DE), BF16),
                        pltpu.VMEM((DE, D), BF16)])
    return pl.pallas_call(
        _moe_kernel,
        grid_spec=grid_spec,
        out_shape=jax.ShapeDtypeStruct((n_rows, D), BF16),
        compiler_params=pltpu.CompilerParams(dimension_semantics=("arbitrary",),
                                             vmem_limit_bytes=VMEM_LIMIT),
        name="moe_experts",
    )(blk_expert, n_active, xs, w_gate, w_up, w_down)


def _final_kernel(x_ref, g_ref, o_ref):
    o_ref[...] = _rms(x_ref[...], g_ref[...])


def _final_norm(x2d, g, tm):
    T, D = x2d.shape
    return pl.pallas_call(
        _final_kernel,
        grid=(T // tm,),
        in_specs=[pl.BlockSpec((tm, D), lambda i: (i, 0)), pl.BlockSpec((1, D), lambda i: (0, 0))],
        out_specs=pl.BlockSpec((tm, D), lambda i: (i, 0)),
        out_shape=jax.ShapeDtypeStruct((T, D), F32),
        compiler_params=pltpu.CompilerParams(dimension_semantics=("arbitrary",)),
        name="final_norm",
    )(x2d, g)


def _rope_tables(positions):
    half = ROT_DIM // 2
    inv_freq = jnp.power(ROPE_THETA, -jnp.arange(half, dtype=F32) / half)
    ang = positions.astype(F32).reshape(-1)[:, None] * inv_freq
    cos, sin = jnp.cos(ang), jnp.sin(ang)
    T = ang.shape[0]
    ones = jnp.ones((T, HEAD_DIM - ROT_DIM), F32)
    zeros = jnp.zeros((T, HEAD_DIM - ROT_DIM), F32)
    z8 = jnp.zeros((T, half), F32)
    rc = jnp.concatenate([cos, cos, ones], axis=1)
    rs1 = jnp.concatenate([-sin, z8, zeros], axis=1)
    rs2 = jnp.concatenate([z8, sin, zeros], axis=1)
    rep = LANES // HEAD_DIM
    return jnp.tile(rc, (1, rep)), jnp.tile(rs1, (1, rep)), jnp.tile(rs2, (1, rep))


def _route(logits_t, T):
    lg = logits_t.T
    p_group = jax.nn.softmax(lg[:, :N_EXPERT_GROUPS], axis=-1)
    g_sel = jnp.argmax(p_group, axis=-1)
    p_g = jnp.max(p_group, axis=-1)
    logit_e = lg[:, N_EXPERT_GROUPS:N_EXPERT_GROUPS + N_EXPERTS].reshape(
        T, N_EXPERT_GROUPS, EXPERTS_PER_GROUP)
    logit_e = jnp.take_along_axis(logit_e, g_sel[:, None, None], axis=1)[:, 0]
    top_p, top_i = lax.top_k(jax.nn.softmax(logit_e, axis=-1), MOE_TOP_K)
    top_p = top_p / jnp.sum(top_p, axis=-1, keepdims=True)
    expert_id = (g_sel[:, None] * EXPERTS_PER_GROUP + top_i).astype(jnp.int32)
    gate_w = p_g[:, None] * top_p
    return expert_id, gate_w


def _dispatch_plan(expert_id, T):
    n_slots = T * MOE_TOP_K
    n_blocks = -(-n_slots // MOE_BLOCK) + N_EXPERTS
    eid = expert_id.reshape(-1)
    onehot = (eid[:, None] == jnp.arange(N_EXPERTS, dtype=jnp.int32)[None, :]).astype(jnp.int32)
    csum = jnp.cumsum(onehot, axis=0)
    counts = csum[-1]
    rank = jnp.sum(onehot * csum, axis=1) - 1
    padded = (counts + MOE_BLOCK - 1) // MOE_BLOCK * MOE_BLOCK
    pad_end = jnp.cumsum(padded)
    pad_start = pad_end - padded
    pos = pad_start[eid] + rank
    token_id = jnp.repeat(jnp.arange(T, dtype=jnp.int32), MOE_TOP_K)
    buf_tok = jnp.full((n_blocks * MOE_BLOCK,), T, jnp.int32).at[pos].set(token_id)
    blk_expert = jnp.minimum(
        jnp.searchsorted(pad_end, jnp.arange(n_blocks, dtype=jnp.int32) * MOE_BLOCK, side='right'),
        N_EXPERTS - 1).astype(jnp.int32)
    n_active = (pad_end[-1] // MOE_BLOCK).astype(jnp.int32).reshape(1)
    return pos.reshape(T, MOE_TOP_K), buf_tok, blk_expert, n_active


def kernel(x, positions, w_in, w_out, g_mix_norm, g_head_norm, g_gmlp_vnorm, w_spatial, b_spatial,
           g_ffn_norm, w_router_group, w_router_expert, w_expert_gate, w_expert_up, w_expert_down,
           g_final):
    B, S, D = x.shape
    T = B * S
    depth = w_in.shape[0]
    tm = min(512, T)
    rc, rs1, rs2 = _rope_tables(positions)
    xc = x.reshape(T, D)
    for l in range(depth):
        gh = g_head_norm[l].reshape(1, -1)
        proj = _inproj(xc, g_mix_norm[l].reshape(1, D), w_in[l].astype(BF16), rc, rs1, rs2, tm)
        o_sb = _sb_attention(proj, gh[:, :W_SB], B, S, min(256, S))
        o_moba = _moba_attention(proj, gh[:, W_SB:W_SB + W_MOBA], B, S)
        b_exp = jnp.repeat(b_spatial[l].T, HEAD_DIM, axis=1)
        o_gmlp = _gmlp(proj, g_gmlp_vnorm[l].reshape(1, -1), w_spatial[l], b_exp,
                       gh[:, W_SB + W_MOBA:], tm)
        w_r = jnp.concatenate([w_router_group[l], w_router_expert[l]], axis=1).T
        w_r = jnp.pad(w_r, ((0, ROUTER_ROWS - w_r.shape[0]), (0, 0)))
        r_hi, r_lo = _split_bf16(w_r)
        x1, h, logits_t = _out_proj(o_sb, o_moba, o_gmlp, xc, w_out[l].astype(BF16),
                                    g_ffn_norm[l].reshape(1, D), r_hi, r_lo, tm)
        expert_id, gate_w = _route(logits_t, T)
        pos, buf_tok, blk_expert, n_active = _dispatch_plan(expert_id, T)
        h_pad = jnp.concatenate([h, jnp.zeros((1, D), h.dtype)], axis=0)
        xs = h_pad[buf_tok]
        ys = _moe_experts(blk_expert, n_active, xs, w_expert_gate, w_expert_up, w_expert_down, l)
        y = (ys[pos[:, 0]].astype(F32) * gate_w[:, 0:1]
             + ys[pos[:, 1]].astype(F32) * gate_w[:, 1:2])
        xc = x1 + y
    return _final_norm(xc, g_final.reshape(1, D), tm).reshape(B, S, D)
```

```python
import functools

import jax
import jax.numpy as jnp
from jax import lax
from jax.experimental import pallas as pl
from jax.experimental.pallas import tpu as pltpu

F32 = jnp.float32
BF16 = jnp.bfloat16

HEAD_DIM = 64
LANES = 128
MXU_COLS = 256
N_HEADS_SB = 4
N_HEADS_MOBA = 8
N_GROUPS_GMLP = 4
W_SB = N_HEADS_SB * HEAD_DIM
W_MOBA = N_HEADS_MOBA * HEAD_DIM
W_GMLP = N_GROUPS_GMLP * HEAD_DIM
MOBA_BLOCK = 256
MOBA_TOPK = 3
GMLP_CHUNK = 128
ROPE_THETA = 500000.0
ROT_DIM = HEAD_DIM // 4
N_EXPERT_GROUPS = 4
EXPERTS_PER_GROUP = 8
N_EXPERTS = N_EXPERT_GROUPS * EXPERTS_PER_GROUP
MOE_TOP_K = 2
MOE_BLOCK = 256
NORM_EPS = 1e-6
ATTN_SCALE = HEAD_DIM ** -0.5
NEG = -1e30
LOG2E = 1.4426950408889634
ROUTER_ROWS = 64
VMEM_LIMIT = 48 * 1024 * 1024

C_QA, C_KA, C_VA = 0, W_SB, 2 * W_SB
C_QB = 3 * W_SB
C_KB = C_QB + W_MOBA
C_VB = C_KB + W_MOBA
C_UC = C_VB + W_MOBA
C_VC = C_UC + W_GMLP
IN_COLS = C_VC + W_GMLP


def _nt_dot(a, b):
    return lax.dot_general(a, b, (((1,), (1,)), ((), ())), preferred_element_type=F32)


def _dot(a, b):
    return jnp.dot(a, b, preferred_element_type=F32)


def _rms(x, g):
    return x * lax.rsqrt(jnp.mean(x * x, axis=-1, keepdims=True) + NORM_EPS) * g


def _split_bf16(x):
    hi = x.astype(BF16)
    lo = (x - hi.astype(F32)).astype(BF16)
    return hi, lo


def _inproj_kernel(x_ref, g_ref, w_ref, rc_ref, rs1_ref, rs2_ref, o_ref):
    y = _rms(x_ref[...], g_ref[...]).astype(BF16)
    wide = lambda t_ref: jnp.concatenate([t_ref[...]] * (MXU_COLS // LANES), axis=1)
    rc, rs1, rs2 = wide(rc_ref), wide(rs1_ref), wide(rs2_ref)
    half = ROT_DIM // 2
    for c0 in range(0, IN_COLS, MXU_COLS):
        p = _dot(y, w_ref[:, c0:c0 + MXU_COLS])
        if C_QB <= c0 < C_VB:
            p = (p * rc + pltpu.roll(p, MXU_COLS - half, axis=1) * rs1
                 + pltpu.roll(p, half, axis=1) * rs2)
        if c0 < C_KA or C_QB <= c0 < C_KB:
            p = p * (ATTN_SCALE * LOG2E)
        o_ref[:, c0:c0 + MXU_COLS] = p.astype(BF16)


def _inproj(x2d, g, w_bf16, rc, rs1, rs2, tm):
    T, D = x2d.shape
    row = lambda i: (i, 0)
    const = lambda i: (0, 0)
    return pl.pallas_call(
        _inproj_kernel,
        grid=(T // tm,),
        in_specs=[pl.BlockSpec((tm, D), row), pl.BlockSpec((1, D), const),
                  pl.BlockSpec((D, IN_COLS), const),
                  pl.BlockSpec((tm, LANES), row), pl.BlockSpec((tm, LANES), row),
                  pl.BlockSpec((tm, LANES), row)],
        out_specs=pl.BlockSpec((tm, IN_COLS), row),
        out_shape=jax.ShapeDtypeStruct((T, IN_COLS), BF16),
        compiler_params=pltpu.CompilerParams(dimension_semantics=("arbitrary",),
                                             vmem_limit_bytes=VMEM_LIMIT),
        name="inproj",
    )(x2d, g, w_bf16, rc, rs1, rs2)


def _sb_kernel(q_ref, k_ref, v_ref, gh_ref, o_ref, *, tq, tk, sub):
    i = pl.program_id(2)
    S = k_ref.shape[0]
    n_heads = LANES // HEAD_DIM
    n_sub = tk // sub
    qt = q_ref[...].astype(F32).T
    zeros_t = jnp.zeros((HEAD_DIM, tq), F32)
    qz = [jnp.concatenate([qt[hh * HEAD_DIM:(hh + 1) * HEAD_DIM] if h2 == hh else zeros_t
                           for h2 in range(n_heads)], axis=0).astype(BF16)
          for hh in range(n_heads)]
    srow = lax.broadcasted_iota(jnp.int32, (sub, sub), 0)
    scol = lax.broadcasted_iota(jnp.int32, (sub, sub), 1)
    later = jnp.where(scol > srow, 1.0, 0.0).astype(BF16)
    krow = lax.broadcasted_iota(jnp.int32, (tk, tq), 0)
    qcol = lax.broadcasted_iota(jnp.int32, (tk, tq), 1)
    last = (i * tq) // tk
    causal = krow - qcol < i * tq - last * tk

    def scores(t):
        return [_dot(k_ref[t * tk:(t + 1) * tk, :], qz[hh]) for hh in range(n_heads)]

    def weights_pv(zs, t, carry, diag):
        wts, runs = [], []
        for hh in range(n_heads):
            z = zs[hh]
            nz = -z
            lr = jnp.minimum(nz, 0.0) - jnp.log2(1.0 + jnp.exp2(jnp.minimum(z, nz)))
            if diag:
                lr = jnp.where(causal, lr, 0.0)
            lrb = lr.astype(BF16)
            tail = carry[hh][0]
            afters = [None] * n_sub
            for sb in reversed(range(n_sub)):
                afters[sb] = _dot(later, lrb[sb * sub:(sb + 1) * sub]) + tail
                tail = afters[sb][0:1] + lr[sb * sub:sb * sub + 1]
            w = jnp.exp2(z + lr + jnp.concatenate(afters, axis=0))
            if diag:
                w = jnp.where(causal, w, 0.0)
            wts.append(w.astype(BF16))
            runs.append(tail)
        new = []
        for hh in range(n_heads):
            pv = lax.dot_general(v_ref[t * tk:(t + 1) * tk, :], wts[hh],
                                 (((0,), (0,)), ((), ())), preferred_element_type=F32)
            new.append((runs[hh], carry[hh][1] + pv[hh * HEAD_DIM:(hh + 1) * HEAD_DIM]))
        return tuple(new)

    def attend(n_past):
        order = list(range(n_past, -1, -1))
        carry = tuple((jnp.zeros((1, tq), F32), jnp.zeros((HEAD_DIM, tq), F32))
                      for _ in range(n_heads))
        zs = scores(order[0])
        for n, t in enumerate(order):
            nxt = scores(order[n + 1]) if n + 1 < len(order) else None
            carry = weights_pv(zs, t, carry, diag=(n == 0))
            zs = nxt
        outs = []
        for hh in range(n_heads):
            o = carry[hh][1]
            outs.append(o * lax.rsqrt(jnp.mean(o * o, axis=0, keepdims=True) + NORM_EPS))
        o_ref[...] = (jnp.concatenate(outs, axis=0).T * gh_ref[...]).astype(BF16)

    for n_past in range(S // tk):
        pl.when(last == n_past)(functools.partial(attend, n_past))


def _sb_attention(proj, gh, B, S):
    tq, sub = 256, 256
    tk = 2 * sub
    assert S % tk == 0
    nq = S // tq
    n_pairs = W_SB // LANES
    return pl.pallas_call(
        functools.partial(_sb_kernel, tq=tq, tk=tk, sub=sub),
        grid=(B, n_pairs, nq),
        in_specs=[pl.BlockSpec((tq, LANES), lambda b, p, i: (b * nq + i, C_QA // LANES + p)),
                  pl.BlockSpec((S, LANES), lambda b, p, i: (b, C_KA // LANES + p)),
                  pl.BlockSpec((S, LANES), lambda b, p, i: (b, C_VA // LANES + p)),
                  pl.BlockSpec((1, LANES), lambda b, p, i: (0, p))],
        out_specs=pl.BlockSpec((tq, LANES), lambda b, p, i: (b * nq + i, p)),
        out_shape=jax.ShapeDtypeStruct((B * S, W_SB), BF16),
        compiler_params=pltpu.CompilerParams(
            dimension_semantics=("arbitrary", "arbitrary", "arbitrary"),
            vmem_limit_bytes=VMEM_LIMIT),
        name="sb_attn",
    )(proj, proj, proj, gh)


def _moba_kernel(q_ref, k_ref, v_ref, gh_ref, o_ref, kaug_ref, vaug_ref, rhi_ref, rlo_ref,
                 *, tq, tk, topk):
    i = pl.program_id(2)
    S = k_ref.shape[0]
    n_heads = LANES // HEAD_DIM

    @pl.when(i == 0)
    def _prepare_keys():
        kp = k_ref[...]
        vp = v_ref[...]
        srow = lax.broadcasted_iota(jnp.int32, (S, HEAD_DIM), 0)
        scol = lax.broadcasted_iota(jnp.int32, (S, HEAD_DIM), 1)
        onehot = jnp.where(srow // MOBA_BLOCK == scol, 1.0, 0.0).astype(BF16)
        arow = lax.broadcasted_iota(jnp.int32, (LANES, S), 0) - HEAD_DIM
        acol = lax.broadcasted_iota(jnp.int32, (LANES, S), 1) // MOBA_BLOCK
        avg = jnp.where(arow == acol, 1.0 / MOBA_BLOCK, 0.0).astype(BF16)
        zeros = jnp.zeros((S, HEAD_DIM), BF16)
        ones = jnp.ones((S, HEAD_DIM), BF16)
        for hh in range(n_heads):
            kh = kp[:, hh * HEAD_DIM:(hh + 1) * HEAD_DIM]
            kaug_ref[hh] = jnp.concatenate([kh, onehot], axis=1)
            vaug_ref[hh] = jnp.concatenate([vp[:, hh * HEAD_DIM:(hh + 1) * HEAD_DIM], ones], axis=1)
            kmean = _dot(avg, jnp.concatenate([kh, zeros], axis=1))
            hi, lo = _split_bf16(kmean)
            rhi_ref[hh] = hi
            rlo_ref[hh] = lo

    qt = q_ref[...].astype(F32).T
    group = 8
    bidx = lax.broadcasted_iota(jnp.int32, (group, tq), 0)
    zeros_t = jnp.zeros((HEAD_DIM, tq), F32)
    qaugs = []
    for hh in range(n_heads):
        qh = qt[hh * HEAD_DIM:(hh + 1) * HEAD_DIM]
        qz = jnp.concatenate([qh, zeros_t], axis=0).astype(BF16)
        gate = (_dot(rhi_ref[hh], qz) + _dot(rlo_ref[hh], qz))[HEAD_DIM:HEAD_DIM + group]
        valid = bidx < i
        gm = jnp.where(valid, gate, NEG)
        rank = jnp.zeros((group, tq), F32)
        for d in range(1, group):
            nb = pltpu.roll(gm, d, axis=0)
            rank = rank + jnp.where(bidx >= d, jnp.where(nb >= gm, 1.0, 0.0),
                                    jnp.where(nb > gm, 1.0, 0.0))
        allowed = (valid & (rank < topk)) | (bidx == i)
        bias = jnp.where(allowed, 0.0, NEG)
        qaugs.append(jnp.concatenate(
            [qh, bias, jnp.zeros((HEAD_DIM - group, tq), F32)], axis=0).astype(BF16))

    krow = lax.broadcasted_iota(jnp.int32, (tk, tq), 0)
    qcol = lax.broadcasted_iota(jnp.int32, (tk, tq), 1)
    last = (i * tq) // tk
    causal = krow - qcol <= i * tq - last * tk

    def scores(t):
        return [_dot(kaug_ref[hh, t * tk:(t + 1) * tk, :], qaugs[hh]) for hh in range(n_heads)]

    def softmax_pv(sts, t, carry, diag):
        pts, stats = [], []
        for hh in range(n_heads):
            m, l, acc = carry[hh]
            st = jnp.where(causal, sts[hh], NEG) if diag else sts[hh]
            m_new = jnp.maximum(m, jnp.max(st, axis=0, keepdims=True))
            pts.append(jnp.exp2(st - m_new).astype(BF16))
            stats.append((m_new, jnp.exp2(m - m_new)))
        new = []
        for hh in range(n_heads):
            _, l, acc = carry[hh]
            m_new, alpha = stats[hh]
            pv = lax.dot_general(vaug_ref[hh, t * tk:(t + 1) * tk, :], pts[hh],
                                 (((0,), (0,)), ((), ())), preferred_element_type=F32)
            new.append((m_new, alpha * l + pv[HEAD_DIM:HEAD_DIM + 1],
                        alpha * acc + pv[:HEAD_DIM]))
        return tuple(new)

    def attend(n_past):
        order = [n_past] + list(range(n_past))
        carry = tuple((jnp.full((1, tq), NEG, F32), jnp.zeros((1, tq), F32),
                       jnp.zeros((HEAD_DIM, tq), F32)) for _ in range(n_heads))
        sts = scores(order[0])
        for n, t in enumerate(order):
            nxt = scores(order[n + 1]) if n + 1 < len(order) else None
            carry = softmax_pv(sts, t, carry, diag=(n == 0))
            sts = nxt
        outs = []
        for hh in range(n_heads):
            _, l, acc = carry[hh]
            o = acc / l
            outs.append(o * lax.rsqrt(jnp.mean(o * o, axis=0, keepdims=True) + NORM_EPS))
        o_ref[...] = (jnp.concatenate(outs, axis=0).T * gh_ref[...]).astype(BF16)

    for n_past in range(S // tk):
        pl.when(last == n_past)(functools.partial(attend, n_past))


def _moba_attention(proj, gh, B, S):
    tq = MOBA_BLOCK
    tk = 2 * MOBA_BLOCK
    assert S % tk == 0
    nq = S // tq
    n_blk = S // MOBA_BLOCK
    assert n_blk <= 8
    topk = min(MOBA_TOPK, max(n_blk - 1, 1))
    n_pairs = W_MOBA // LANES
    n_heads = LANES // HEAD_DIM
    return pl.pallas_call(
        functools.partial(_moba_kernel, tq=tq, tk=tk, topk=topk),
        grid=(B, n_pairs, nq),
        in_specs=[pl.BlockSpec((tq, LANES), lambda b, p, i: (b * nq + i, C_QB // LANES + p)),
                  pl.BlockSpec((S, LANES), lambda b, p, i: (b, C_KB // LANES + p)),
                  pl.BlockSpec((S, LANES), lambda b, p, i: (b, C_VB // LANES + p)),
                  pl.BlockSpec((1, LANES), lambda b, p, i: (0, p))],
        out_specs=pl.BlockSpec((tq, LANES), lambda b, p, i: (b * nq + i, p)),
        out_shape=jax.ShapeDtypeStruct((B * S, W_MOBA), BF16),
        scratch_shapes=[pltpu.VMEM((n_heads, S, LANES), BF16),
                        pltpu.VMEM((n_heads, S, LANES), BF16),
                        pltpu.VMEM((n_heads, LANES, LANES), BF16),
                        pltpu.VMEM((n_heads, LANES, LANES), BF16)],
        compiler_params=pltpu.CompilerParams(
            dimension_semantics=("arbitrary", "arbitrary", "arbitrary"),
            vmem_limit_bytes=VMEM_LIMIT),
        name="moba_attn",
    )(proj, proj, proj, gh)


def _gmlp_kernel(u_ref, v_ref, gv_ref, ws_ref, b_ref, gh_ref, o_ref, *, tm):
    gu = jax.nn.gelu(u_ref[...].astype(F32))
    gv = jax.nn.gelu(v_ref[...].astype(F32))
    row = lax.broadcasted_iota(jnp.int32, (GMLP_CHUNK, GMLP_CHUNK), 0)
    col = lax.broadcasted_iota(jnp.int32, (GMLP_CHUNK, GMLP_CHUNK), 1)
    outs = []
    for g in range(N_GROUPS_GMLP):
        c0, c1 = g * HEAD_DIM, (g + 1) * HEAD_DIM
        vn = _rms(gv[:, c0:c1], gv_ref[:, c0:c1]).astype(BF16)
        wm = jnp.where(col <= row, ws_ref[g], 0.0).astype(BF16)
        bias = b_ref[:, c0:c1]
        mixed = jnp.concatenate(
            [_dot(wm, vn[c * GMLP_CHUNK:(c + 1) * GMLP_CHUNK]) + bias
             for c in range(tm // GMLP_CHUNK)], axis=0)
        outs.append(_rms(gu[:, c0:c1] * mixed, gh_ref[:, c0:c1]))
    o_ref[...] = jnp.concatenate(outs, axis=1).astype(BF16)


def _gmlp(proj, gv, ws, b_exp, gh, tm):
    T = proj.shape[0]
    nu, nv = C_UC // W_GMLP, C_VC // W_GMLP
    return pl.pallas_call(
        functools.partial(_gmlp_kernel, tm=tm),
        grid=(T // tm,),
        in_specs=[pl.BlockSpec((tm, W_GMLP), lambda i: (i, nu)),
                  pl.BlockSpec((tm, W_GMLP), lambda i: (i, nv)),
                  pl.BlockSpec((1, W_GMLP), lambda i: (0, 0)),
                  pl.BlockSpec((N_GROUPS_GMLP, GMLP_CHUNK, GMLP_CHUNK), lambda i: (0, 0, 0)),
                  pl.BlockSpec((GMLP_CHUNK, W_GMLP), lambda i: (0, 0)),
                  pl.BlockSpec((1, W_GMLP), lambda i: (0, 0))],
        out_specs=pl.BlockSpec((tm, W_GMLP), lambda i: (i, 0)),
        out_shape=jax.ShapeDtypeStruct((T, W_GMLP), BF16),
        compiler_params=pltpu.CompilerParams(dimension_semantics=("arbitrary",),
                                             vmem_limit_bytes=VMEM_LIMIT),
        name="gmlp",
    )(proj, proj, gv, ws, b_exp, gh)


def _out_kernel(osb_ref, omoba_ref, ogmlp_ref, x_ref, w_ref, g_ref, rhi_ref, rlo_ref,
                x1_ref, h_ref, lg_ref):
    x1 = (x_ref[...]
          + _dot(osb_ref[...], w_ref[0:W_SB, :])
          + _dot(omoba_ref[...], w_ref[W_SB:W_SB + W_MOBA, :])
          + _dot(ogmlp_ref[...], w_ref[W_SB + W_MOBA:, :]))
    x1_ref[...] = x1
    hn = _rms(x1, g_ref[...])
    hi, lo = _split_bf16(hn)
    h_ref[...] = hi
    lg_ref[...] = (_nt_dot(rhi_ref[...], hi) + _nt_dot(rhi_ref[...], lo)
                   + _nt_dot(rlo_ref[...], hi))


def _out_proj(o_sb, o_moba, o_gmlp, x2d, w_bf16, g, r_hi, r_lo, tm):
    T, D = x2d.shape
    row = lambda i: (i, 0)
    const = lambda i: (0, 0)
    return pl.pallas_call(
        _out_kernel,
        grid=(T // tm,),
        in_specs=[pl.BlockSpec((tm, W_SB), row), pl.BlockSpec((tm, W_MOBA), row),
                  pl.BlockSpec((tm, W_GMLP), row), pl.BlockSpec((tm, D), row),
                  pl.BlockSpec((D, D), const), pl.BlockSpec((1, D), const),
                  pl.BlockSpec((ROUTER_ROWS, D), const), pl.BlockSpec((ROUTER_ROWS, D), const)],
        out_specs=[pl.BlockSpec((tm, D), row), pl.BlockSpec((tm, D), row),
                   pl.BlockSpec((ROUTER_ROWS, tm), lambda i: (0, i))],
        out_shape=[jax.ShapeDtypeStruct((T, D), F32), jax.ShapeDtypeStruct((T, D), BF16),
                   jax.ShapeDtypeStruct((ROUTER_ROWS, T), F32)],
        compiler_params=pltpu.CompilerParams(dimension_semantics=("arbitrary",),
                                             vmem_limit_bytes=VMEM_LIMIT),
        name="outproj",
    )(o_sb, o_moba, o_gmlp, x2d, w_bf16, g, r_hi, r_lo)


def _moe_kernel(be_ref, na_ref, xs_ref, wg_ref, wu_ref, wd_ref, ys_ref, wgb, wub, wdb):
    b = pl.program_id(0)
    e = be_ref[b]
    prev = be_ref[jnp.maximum(b - 1, 0)]
    active = b < na_ref[0]

    @pl.when(active & ((b == 0) | (e != prev)))
    def _load_expert():
        wgb[...] = wg_ref[...].astype(BF16)
        wub[...] = wu_ref[...].astype(BF16)
        wdb[...] = wd_ref[...].astype(BF16)

    @pl.when(active)
    def _compute():
        xb = xs_ref[...]
        a = jax.nn.silu(_dot(xb, wgb[...])) * _dot(xb, wub[...])
        ys_ref[...] = _dot(a.astype(BF16), wdb[...]).astype(ys_ref.dtype)

    @pl.when(jnp.logical_not(active))
    def _pad():
        ys_ref[...] = jnp.zeros_like(ys_ref)


def _moe_experts(blk_expert, n_active, xs, w_gate, w_up, w_down, layer):
    n_rows, D = xs.shape
    n_blocks = n_rows // MOE_BLOCK
    DE = w_gate.shape[-1]
    grid_spec = pltpu.PrefetchScalarGridSpec(
        num_scalar_prefetch=2,
        grid=(n_blocks,),
        in_specs=[pl.BlockSpec((MOE_BLOCK, D), lambda b, be, na: (b, 0)),
                  pl.BlockSpec((None, None, D, DE), lambda b, be, na: (layer, be[b], 0, 0)),
                  pl.BlockSpec((None, None, D, DE), lambda b, be, na: (layer, be[b], 0, 0)),
                  pl.BlockSpec((None, None, DE, D), lambda b, be, na: (layer, be[b], 0, 0))],
        out_specs=pl.BlockSpec((MOE_BLOCK, D), lambda b, be, na: (b, 0)),
        scratch_shapes=[pltpu.VMEM((D, DE), BF16), pltpu.VMEM((D, DE), BF16),
                        pltpu.VMEM((DE, D), BF16)])
    return pl.pallas_call(
        _moe_kernel,
        grid_spec=grid_spec,
        out_shape=jax.ShapeDtypeStruct((n_rows, D), BF16),
        compiler_params=pltpu.CompilerParams(dimension_semantics=("arbitrary",),
                                             vmem_limit_bytes=VMEM_LIMIT),
        name="moe_experts",
    )(blk_expert, n_active, xs, w_gate, w_up, w_down)


def _final_kernel(x_ref, g_ref, o_ref):
    o_ref[...] = _rms(x_ref[...], g_ref[...])


def _final_norm(x2d, g, tm):
    T, D = x2d.shape
    return pl.pallas_call(
        _final_kernel,
        grid=(T // tm,),
        in_specs=[pl.BlockSpec((tm, D), lambda i: (i, 0)), pl.BlockSpec((1, D), lambda i: (0, 0))],
        out_specs=pl.BlockSpec((tm, D), lambda i: (i, 0)),
        out_shape=jax.ShapeDtypeStruct((T, D), F32),
        compiler_params=pltpu.CompilerParams(dimension_semantics=("arbitrary",)),
        name="final_norm",
    )(x2d, g)


def _rope_tables(positions):
    half = ROT_DIM // 2
    inv_freq = jnp.power(ROPE_THETA, -jnp.arange(half, dtype=F32) / half)
    ang = positions.astype(F32).reshape(-1)[:, None] * inv_freq
    cos, sin = jnp.cos(ang), jnp.sin(ang)
    T = ang.shape[0]
    ones = jnp.ones((T, HEAD_DIM - ROT_DIM), F32)
    zeros = jnp.zeros((T, HEAD_DIM - ROT_DIM), F32)
    z8 = jnp.zeros((T, half), F32)
    rc = jnp.concatenate([cos, cos, ones], axis=1)
    rs1 = jnp.concatenate([-sin, z8, zeros], axis=1)
    rs2 = jnp.concatenate([z8, sin, zeros], axis=1)
    rep = LANES // HEAD_DIM
    return jnp.tile(rc, (1, rep)), jnp.tile(rs1, (1, rep)), jnp.tile(rs2, (1, rep))


def _route(logits_t, T):
    lg = logits_t.T
    p_group = jax.nn.softmax(lg[:, :N_EXPERT_GROUPS], axis=-1)
    g_sel = jnp.argmax(p_group, axis=-1)
    p_g = jnp.max(p_group, axis=-1)
    logit_e = lg[:, N_EXPERT_GROUPS:N_EXPERT_GROUPS + N_EXPERTS].reshape(
        T, N_EXPERT_GROUPS, EXPERTS_PER_GROUP)
    logit_e = jnp.take_along_axis(logit_e, g_sel[:, None, None], axis=1)[:, 0]
    top_p, top_i = lax.top_k(jax.nn.softmax(logit_e, axis=-1), MOE_TOP_K)
    top_p = top_p / jnp.sum(top_p, axis=-1, keepdims=True)
    expert_id = (g_sel[:, None] * EXPERTS_PER_GROUP + top_i).astype(jnp.int32)
    gate_w = p_g[:, None] * top_p
    return expert_id, gate_w


def _dispatch_plan(expert_id, T):
    n_slots = T * MOE_TOP_K
    n_blocks = -(-n_slots // MOE_BLOCK) + N_EXPERTS
    eid = expert_id.reshape(-1)
    onehot = (eid[:, None] == jnp.arange(N_EXPERTS, dtype=jnp.int32)[None, :]).astype(jnp.int32)
    csum = jnp.cumsum(onehot, axis=0)
    counts = csum[-1]
    rank = jnp.sum(onehot * csum, axis=1) - 1
    padded = (counts + MOE_BLOCK - 1) // MOE_BLOCK * MOE_BLOCK
    pad_end = jnp.cumsum(padded)
    pad_start = pad_end - padded
    pos = pad_start[eid] + rank
    token_id = jnp.repeat(jnp.arange(T, dtype=jnp.int32), MOE_TOP_K)
    buf_tok = jnp.full((n_blocks * MOE_BLOCK,), T, jnp.int32).at[pos].set(token_id)
    blk_expert = jnp.minimum(
        jnp.searchsorted(pad_end, jnp.arange(n_blocks, dtype=jnp.int32) * MOE_BLOCK, side='right'),
        N_EXPERTS - 1).astype(jnp.int32)
    n_active = (pad_end[-1] // MOE_BLOCK).astype(jnp.int32).reshape(1)
    return pos.reshape(T, MOE_TOP_K), buf_tok, blk_expert, n_active


def kernel(x, positions, w_in, w_out, g_mix_norm, g_head_norm, g_gmlp_vnorm, w_spatial, b_spatial,
           g_ffn_norm, w_router_group, w_router_expert, w_expert_gate, w_expert_up, w_expert_down,
           g_final):
    B, S, D = x.shape
    T = B * S
    depth = w_in.shape[0]
    tm = min(512, T)
    rc, rs1, rs2 = _rope_tables(positions)
    xc = x.reshape(T, D)
    for l in range(depth):
        gh = g_head_norm[l].reshape(1, -1)
        proj = _inproj(xc, g_mix_norm[l].reshape(1, D), w_in[l].astype(BF16), rc, rs1, rs2, tm)
        o_sb = _sb_attention(proj, gh[:, :W_SB], B, S)
        o_moba = _moba_attention(proj, gh[:, W_SB:W_SB + W_MOBA], B, S)
        b_exp = jnp.repeat(b_spatial[l].T, HEAD_DIM, axis=1)
        o_gmlp = _gmlp(proj, g_gmlp_vnorm[l].reshape(1, -1), w_spatial[l], b_exp,
                       gh[:, W_SB + W_MOBA:], tm)
        w_r = jnp.concatenate([w_router_group[l], w_router_expert[l]], axis=1).T
        w_r = jnp.pad(w_r, ((0, ROUTER_ROWS - w_r.shape[0]), (0, 0)))
        r_hi, r_lo = _split_bf16(w_r)
        x1, h, logits_t = _out_proj(o_sb, o_moba, o_gmlp, xc, w_out[l].astype(BF16),
                                    g_ffn_norm[l].reshape(1, D), r_hi, r_lo, tm)
        expert_id, gate_w = _route(logits_t, T)
        pos, buf_tok, blk_expert, n_active = _dispatch_plan(expert_id, T)
        h_pad = jnp.concatenate([h, jnp.zeros((1, D), h.dtype)], axis=0)
        xs = h_pad[buf_tok]
        ys = _moe_experts(blk_expert, n_active, xs, w_expert_gate, w_expert_up, w_expert_down, l)
        y = (ys[pos[:, 0]].astype(F32) * gate_w[:, 0:1]
             + ys[pos[:, 1]].astype(F32) * gate_w[:, 1:2])
        xc = x1 + y
    return _final_norm(xc, g_final.reshape(1, D), tm).reshape(B, S, D)
```

```python
import functools

import jax
import jax.numpy as jnp
from jax import lax
from jax.experimental import pallas as pl
from jax.experimental.pallas import tpu as pltpu
from jax.experimental.pallas import tpu_sc as plsc

F32 = jnp.float32
BF16 = jnp.bfloat16

HEAD_DIM = 64
LANES = 128
MXU_COLS = 256
N_HEADS_SB = 4
N_HEADS_MOBA = 8
N_GROUPS_GMLP = 4
W_SB = N_HEADS_SB * HEAD_DIM
W_MOBA = N_HEADS_MOBA * HEAD_DIM
W_GMLP = N_GROUPS_GMLP * HEAD_DIM
MOBA_BLOCK = 256
MOBA_TOPK = 3
GMLP_CHUNK = 128
ROPE_THETA = 500000.0
ROT_DIM = HEAD_DIM // 4
N_EXPERT_GROUPS = 4
EXPERTS_PER_GROUP = 8
N_EXPERTS = N_EXPERT_GROUPS * EXPERTS_PER_GROUP
MOE_TOP_K = 2
MOE_BLOCK = 256
NORM_EPS = 1e-6
ATTN_SCALE = HEAD_DIM ** -0.5
NEG = -1e30
LOG2E = 1.4426950408889634
ROUTER_ROWS = 64
VMEM_LIMIT = 48 * 1024 * 1024

C_QA, C_KA, C_VA = 0, W_SB, 2 * W_SB
C_QB = 3 * W_SB
C_KB = C_QB + W_MOBA
C_VB = C_KB + W_MOBA
C_UC = C_VB + W_MOBA
C_VC = C_UC + W_GMLP
IN_COLS = C_VC + W_GMLP


def _nt_dot(a, b):
    return lax.dot_general(a, b, (((1,), (1,)), ((), ())), preferred_element_type=F32)


def _dot(a, b):
    return jnp.dot(a, b, preferred_element_type=F32)


def _rms(x, g):
    return x * lax.rsqrt(jnp.mean(x * x, axis=-1, keepdims=True) + NORM_EPS) * g


def _pack_halves(x):
    w = x.shape[1] // 2
    lo = lax.bitcast_convert_type(x[:, :w].astype(BF16).astype(F32), jnp.uint32)
    hi = lax.bitcast_convert_type(x[:, w:].astype(BF16).astype(F32), jnp.uint32)
    return (lo >> 16) | hi


def _unpack_halves(words):
    lo = lax.bitcast_convert_type(words << 16, F32)
    hi = lax.bitcast_convert_type(words & jnp.uint32(0xFFFF0000), F32)
    return jnp.concatenate([lo, hi], axis=1)


def _split_bf16(x):
    hi = x.astype(BF16)
    lo = (x - hi.astype(F32)).astype(BF16)
    return hi, lo


def _combine(x1_ref, yg_ref, gw_ref):
    gw = gw_ref[...]
    return (x1_ref[...] + _unpack_halves(yg_ref[0]) * gw[:, 0:1]
            + _unpack_halves(yg_ref[1]) * gw[:, 1:2])


def _inproj_kernel(*refs, combine):
    if combine:
        x1_ref, yg_ref, gw_ref, g_ref, w_ref, rc_ref, rs1_ref, rs2_ref, o_ref, x_ref = refs
        x = _combine(x1_ref, yg_ref, gw_ref)
        x_ref[...] = x
    else:
        x_ref, g_ref, w_ref, rc_ref, rs1_ref, rs2_ref, o_ref = refs
        x = x_ref[...]
    y = _rms(x, g_ref[...]).astype(BF16)
    wide = lambda t_ref: jnp.concatenate([t_ref[...]] * (MXU_COLS // LANES), axis=1)
    rc, rs1, rs2 = wide(rc_ref), wide(rs1_ref), wide(rs2_ref)
    half = ROT_DIM // 2
    for c0 in range(0, IN_COLS, MXU_COLS):
        p = _dot(y, w_ref[:, c0:c0 + MXU_COLS])
        if C_QB <= c0 < C_VB:
            p = (p * rc + pltpu.roll(p, MXU_COLS - half, axis=1) * rs1
                 + pltpu.roll(p, half, axis=1) * rs2)
        if c0 < C_KA or C_QB <= c0 < C_KB:
            p = p * (ATTN_SCALE * LOG2E)
        o_ref[:, c0:c0 + MXU_COLS] = p.astype(BF16)


def _inproj(x2d, moe_out, g, w_bf16, rc, rs1, rs2, tm):
    T, D = x2d.shape
    row = lambda i: (i, 0)
    const = lambda i: (0, 0)
    combine = moe_out is not None
    x_specs = [pl.BlockSpec((tm, D), row)]
    x_args = [x2d]
    out_specs = [pl.BlockSpec((tm, IN_COLS), row)]
    out_shape = [jax.ShapeDtypeStruct((T, IN_COLS), BF16)]
    if combine:
        yg, gw = moe_out
        x_specs += [pl.BlockSpec((MOE_TOP_K, tm, D // 2), lambda i: (0, i, 0)),
                    pl.BlockSpec((tm, MOE_TOP_K), row)]
        x_args += [yg, gw]
        out_specs.append(pl.BlockSpec((tm, D), row))
        out_shape.append(jax.ShapeDtypeStruct((T, D), F32))
    outs = pl.pallas_call(
        functools.partial(_inproj_kernel, combine=combine),
        grid=(T // tm,),
        in_specs=x_specs + [pl.BlockSpec((1, D), const), pl.BlockSpec((D, IN_COLS), const),
                            pl.BlockSpec((tm, LANES), row), pl.BlockSpec((tm, LANES), row),
                            pl.BlockSpec((tm, LANES), row)],
        out_specs=out_specs,
        out_shape=out_shape,
        compiler_params=pltpu.CompilerParams(dimension_semantics=("arbitrary",),
                                             vmem_limit_bytes=VMEM_LIMIT),
        name="inproj",
    )(*x_args, g, w_bf16, rc, rs1, rs2)
    return (outs[0], outs[1]) if combine else (outs[0], x2d)


def _sb_kernel(q_ref, k_ref, v_ref, gh_ref, o_ref, *, tq, tk, sub):
    i = pl.program_id(2)
    S = k_ref.shape[0]
    n_heads = LANES // HEAD_DIM
    n_sub = tk // sub
    qt = q_ref[...].astype(F32).T
    zeros_t = jnp.zeros((HEAD_DIM, tq), F32)
    qz = [jnp.concatenate([qt[hh * HEAD_DIM:(hh + 1) * HEAD_DIM] if h2 == hh else zeros_t
                           for h2 in range(n_heads)], axis=0).astype(BF16)
          for hh in range(n_heads)]
    srow = lax.broadcasted_iota(jnp.int32, (sub, sub), 0)
    scol = lax.broadcasted_iota(jnp.int32, (sub, sub), 1)
    later = jnp.where(scol > srow, 1.0, 0.0).astype(BF16)
    krow = lax.broadcasted_iota(jnp.int32, (tk, tq), 0)
    qcol = lax.broadcasted_iota(jnp.int32, (tk, tq), 1)
    last = (i * tq) // tk
    causal = krow - qcol < i * tq - last * tk

    def scores(t):
        return [_dot(k_ref[t * tk:(t + 1) * tk, :], qz[hh]) for hh in range(n_heads)]

    def weights_pv(zs, t, carry, diag):
        wts, runs = [], []
        for hh in range(n_heads):
            z = zs[hh]
            nz = -z
            lr = jnp.minimum(nz, 0.0) - jnp.log2(1.0 + jnp.exp2(jnp.minimum(z, nz)))
            if diag:
                lr = jnp.where(causal, lr, 0.0)
            lrb = lr.astype(BF16)
            tail = carry[hh][0]
            afters = [None] * n_sub
            for sb in reversed(range(n_sub)):
                afters[sb] = _dot(later, lrb[sb * sub:(sb + 1) * sub]) + tail
                tail = afters[sb][0:1] + lr[sb * sub:sb * sub + 1]
            w = jnp.exp2(z + lr + jnp.concatenate(afters, axis=0))
            if diag:
                w = jnp.where(causal, w, 0.0)
            wts.append(w.astype(BF16))
            runs.append(tail)
        new = []
        for hh in range(n_heads):
            pv = lax.dot_general(v_ref[t * tk:(t + 1) * tk, :], wts[hh],
                                 (((0,), (0,)), ((), ())), preferred_element_type=F32)
            new.append((runs[hh], carry[hh][1] + pv[hh * HEAD_DIM:(hh + 1) * HEAD_DIM]))
        return tuple(new)

    def attend(n_past):
        order = list(range(n_past, -1, -1))
        carry = tuple((jnp.zeros((1, tq), F32), jnp.zeros((HEAD_DIM, tq), F32))
                      for _ in range(n_heads))
        zs = scores(order[0])
        for n, t in enumerate(order):
            nxt = scores(order[n + 1]) if n + 1 < len(order) else None
            carry = weights_pv(zs, t, carry, diag=(n == 0))
            zs = nxt
        outs = []
        for hh in range(n_heads):
            o = carry[hh][1]
            outs.append(o * lax.rsqrt(jnp.mean(o * o, axis=0, keepdims=True) + NORM_EPS))
        o_ref[...] = (jnp.concatenate(outs, axis=0).T * gh_ref[...]).astype(BF16)

    for n_past in range(S // tk):
        pl.when(last == n_past)(functools.partial(attend, n_past))


def _sb_attention(proj, gh, B, S):
    tq, sub = 256, 256
    tk = 2 * sub
    assert S % tk == 0
    nq = S // tq
    n_pairs = W_SB // LANES
    return pl.pallas_call(
        functools.partial(_sb_kernel, tq=tq, tk=tk, sub=sub),
        grid=(B, n_pairs, nq),
        in_specs=[pl.BlockSpec((tq, LANES), lambda b, p, i: (b * nq + i, C_QA // LANES + p)),
                  pl.BlockSpec((S, LANES), lambda b, p, i: (b, C_KA // LANES + p)),
                  pl.BlockSpec((S, LANES), lambda b, p, i: (b, C_VA // LANES + p)),
                  pl.BlockSpec((1, LANES), lambda b, p, i: (0, p))],
        out_specs=pl.BlockSpec((tq, LANES), lambda b, p, i: (b * nq + i, p)),
        out_shape=jax.ShapeDtypeStruct((B * S, W_SB), BF16),
        compiler_params=pltpu.CompilerParams(
            dimension_semantics=("arbitrary", "arbitrary", "arbitrary"),
            vmem_limit_bytes=VMEM_LIMIT),
        name="sb_attn",
    )(proj, proj, proj, gh)


def _moba_kernel(q_ref, k_ref, v_ref, gh_ref, o_ref, kaug_ref, vaug_ref, rhi_ref, rlo_ref,
                 *, tq, tk, topk):
    i = pl.program_id(2)
    S = k_ref.shape[0]
    n_heads = LANES // HEAD_DIM

    @pl.when(i == 0)
    def _prepare_keys():
        kp = k_ref[...]
        vp = v_ref[...]
        srow = lax.broadcasted_iota(jnp.int32, (S, HEAD_DIM), 0)
        scol = lax.broadcasted_iota(jnp.int32, (S, HEAD_DIM), 1)
        onehot = jnp.where(srow // MOBA_BLOCK == scol, 1.0, 0.0).astype(BF16)
        arow = lax.broadcasted_iota(jnp.int32, (LANES, S), 0) - HEAD_DIM
        acol = lax.broadcasted_iota(jnp.int32, (LANES, S), 1) // MOBA_BLOCK
        avg = jnp.where(arow == acol, 1.0 / MOBA_BLOCK, 0.0).astype(BF16)
        zeros = jnp.zeros((S, HEAD_DIM), BF16)
        ones = jnp.ones((S, HEAD_DIM), BF16)
        for hh in range(n_heads):
            kh = kp[:, hh * HEAD_DIM:(hh + 1) * HEAD_DIM]
            kaug_ref[hh] = jnp.concatenate([kh, onehot], axis=1)
            vaug_ref[hh] = jnp.concatenate([vp[:, hh * HEAD_DIM:(hh + 1) * HEAD_DIM], ones], axis=1)
            kmean = _dot(avg, jnp.concatenate([kh, zeros], axis=1))
            hi, lo = _split_bf16(kmean)
            rhi_ref[hh] = hi
            rlo_ref[hh] = lo

    qt = q_ref[...].astype(F32).T
    group = 8
    bidx = lax.broadcasted_iota(jnp.int32, (group, tq), 0)
    zeros_t = jnp.zeros((HEAD_DIM, tq), F32)
    qaugs = []
    for hh in range(n_heads):
        qh = qt[hh * HEAD_DIM:(hh + 1) * HEAD_DIM]
        qz = jnp.concatenate([qh, zeros_t], axis=0).astype(BF16)
        gate = (_dot(rhi_ref[hh], qz) + _dot(rlo_ref[hh], qz))[HEAD_DIM:HEAD_DIM + group]
        valid = bidx < i
        gm = jnp.where(valid, gate, NEG)
        rank = jnp.zeros((group, tq), F32)
        for d in range(1, group):
            nb = pltpu.roll(gm, d, axis=0)
            rank = rank + jnp.where(bidx >= d, jnp.where(nb >= gm, 1.0, 0.0),
                                    jnp.where(nb > gm, 1.0, 0.0))
        allowed = (valid & (rank < topk)) | (bidx == i)
        bias = jnp.where(allowed, 0.0, NEG)
        qaugs.append(jnp.concatenate(
            [qh, bias, jnp.zeros((HEAD_DIM - group, tq), F32)], axis=0).astype(BF16))

    krow = lax.broadcasted_iota(jnp.int32, (tk, tq), 0)
    qcol = lax.broadcasted_iota(jnp.int32, (tk, tq), 1)
    last = (i * tq) // tk
    causal = krow - qcol <= i * tq - last * tk

    def scores(t):
        return [_dot(kaug_ref[hh, t * tk:(t + 1) * tk, :], qaugs[hh]) for hh in range(n_heads)]

    def softmax_pv(sts, t, carry, diag):
        pts, stats = [], []
        for hh in range(n_heads):
            m, l, acc = carry[hh]
            st = jnp.where(causal, sts[hh], NEG) if diag else sts[hh]
            m_new = jnp.maximum(m, jnp.max(st, axis=0, keepdims=True))
            pts.append(jnp.exp2(st - m_new).astype(BF16))
            stats.append((m_new, jnp.exp2(m - m_new)))
        new = []
        for hh in range(n_heads):
            _, l, acc = carry[hh]
            m_new, alpha = stats[hh]
            pv = lax.dot_general(vaug_ref[hh, t * tk:(t + 1) * tk, :], pts[hh],
                                 (((0,), (0,)), ((), ())), preferred_element_type=F32)
            new.append((m_new, alpha * l + pv[HEAD_DIM:HEAD_DIM + 1],
                        alpha * acc + pv[:HEAD_DIM]))
        return tuple(new)

    def attend(n_past):
        order = [n_past] + list(range(n_past))
        carry = tuple((jnp.full((1, tq), NEG, F32), jnp.zeros((1, tq), F32),
                       jnp.zeros((HEAD_DIM, tq), F32)) for _ in range(n_heads))
        sts = scores(order[0])
        for n, t in enumerate(order):
            nxt = scores(order[n + 1]) if n + 1 < len(order) else None
            carry = softmax_pv(sts, t, carry, diag=(n == 0))
            sts = nxt
        outs = []
        for hh in range(n_heads):
            _, l, acc = carry[hh]
            o = acc / l
            outs.append(o * lax.rsqrt(jnp.mean(o * o, axis=0, keepdims=True) + NORM_EPS))
        o_ref[...] = (jnp.concatenate(outs, axis=0).T * gh_ref[...]).astype(BF16)

    for n_past in range(S // tk):
        pl.when(last == n_past)(functools.partial(attend, n_past))


def _moba_attention(proj, gh, B, S):
    tq = MOBA_BLOCK
    tk = 2 * MOBA_BLOCK
    assert S % tk == 0
    nq = S // tq
    n_blk = S // MOBA_BLOCK
    assert n_blk <= 8
    topk = min(MOBA_TOPK, max(n_blk - 1, 1))
    n_pairs = W_MOBA // LANES
    n_heads = LANES // HEAD_DIM
    return pl.pallas_call(
        functools.partial(_moba_kernel, tq=tq, tk=tk, topk=topk),
        grid=(B, n_pairs, nq),
        in_specs=[pl.BlockSpec((tq, LANES), lambda b, p, i: (b * nq + i, C_QB // LANES + p)),
                  pl.BlockSpec((S, LANES), lambda b, p, i: (b, C_KB // LANES + p)),
                  pl.BlockSpec((S, LANES), lambda b, p, i: (b, C_VB // LANES + p)),
                  pl.BlockSpec((1, LANES), lambda b, p, i: (0, p))],
        out_specs=pl.BlockSpec((tq, LANES), lambda b, p, i: (b * nq + i, p)),
        out_shape=jax.ShapeDtypeStruct((B * S, W_MOBA), BF16),
        scratch_shapes=[pltpu.VMEM((n_heads, S, LANES), BF16),
                        pltpu.VMEM((n_heads, S, LANES), BF16),
                        pltpu.VMEM((n_heads, LANES, LANES), BF16),
                        pltpu.VMEM((n_heads, LANES, LANES), BF16)],
        compiler_params=pltpu.CompilerParams(
            dimension_semantics=("arbitrary", "arbitrary", "arbitrary"),
            vmem_limit_bytes=VMEM_LIMIT),
        name="moba_attn",
    )(proj, proj, proj, gh)


def _gmlp_kernel(u_ref, v_ref, gv_ref, ws_ref, b_ref, gh_ref, o_ref, *, tm):
    gu = jax.nn.gelu(u_ref[...].astype(F32))
    gv = jax.nn.gelu(v_ref[...].astype(F32))
    row = lax.broadcasted_iota(jnp.int32, (GMLP_CHUNK, GMLP_CHUNK), 0)
    col = lax.broadcasted_iota(jnp.int32, (GMLP_CHUNK, GMLP_CHUNK), 1)
    outs = []
    for g in range(N_GROUPS_GMLP):
        c0, c1 = g * HEAD_DIM, (g + 1) * HEAD_DIM
        vn = _rms(gv[:, c0:c1], gv_ref[:, c0:c1]).astype(BF16)
        wm = jnp.where(col <= row, ws_ref[g], 0.0).astype(BF16)
        bias = b_ref[:, c0:c1]
        mixed = jnp.concatenate(
            [_dot(wm, vn[c * GMLP_CHUNK:(c + 1) * GMLP_CHUNK]) + bias
             for c in range(tm // GMLP_CHUNK)], axis=0)
        outs.append(_rms(gu[:, c0:c1] * mixed, gh_ref[:, c0:c1]))
    o_ref[...] = jnp.concatenate(outs, axis=1).astype(BF16)


def _gmlp(proj, gv, ws, b_exp, gh, tm):
    T = proj.shape[0]
    nu, nv = C_UC // W_GMLP, C_VC // W_GMLP
    return pl.pallas_call(
        functools.partial(_gmlp_kernel, tm=tm),
        grid=(T // tm,),
        in_specs=[pl.BlockSpec((tm, W_GMLP), lambda i: (i, nu)),
                  pl.BlockSpec((tm, W_GMLP), lambda i: (i, nv)),
                  pl.BlockSpec((1, W_GMLP), lambda i: (0, 0)),
                  pl.BlockSpec((N_GROUPS_GMLP, GMLP_CHUNK, GMLP_CHUNK), lambda i: (0, 0, 0)),
                  pl.BlockSpec((GMLP_CHUNK, W_GMLP), lambda i: (0, 0)),
                  pl.BlockSpec((1, W_GMLP), lambda i: (0, 0))],
        out_specs=pl.BlockSpec((tm, W_GMLP), lambda i: (i, 0)),
        out_shape=jax.ShapeDtypeStruct((T, W_GMLP), BF16),
        compiler_params=pltpu.CompilerParams(dimension_semantics=("arbitrary",),
                                             vmem_limit_bytes=VMEM_LIMIT),
        name="gmlp",
    )(proj, proj, gv, ws, b_exp, gh)


def _out_kernel(osb_ref, omoba_ref, ogmlp_ref, x_ref, w_ref, g_ref, rhi_ref, rlo_ref,
                x1_ref, h_ref, lg_ref):
    x1 = (x_ref[...]
          + _dot(osb_ref[...], w_ref[0:W_SB, :])
          + _dot(omoba_ref[...], w_ref[W_SB:W_SB + W_MOBA, :])
          + _dot(ogmlp_ref[...], w_ref[W_SB + W_MOBA:, :]))
    x1_ref[...] = x1
    hn = _rms(x1, g_ref[...])
    hi, lo = _split_bf16(hn)
    h_ref[...] = _pack_halves(hn)
    lg_ref[...] = (_nt_dot(rhi_ref[...], hi) + _nt_dot(rhi_ref[...], lo)
                   + _nt_dot(rlo_ref[...], hi))


def _out_proj(o_sb, o_moba, o_gmlp, x2d, w_bf16, g, r_hi, r_lo, tm):
    T, D = x2d.shape
    row = lambda i: (i, 0)
    const = lambda i: (0, 0)
    return pl.pallas_call(
        _out_kernel,
        grid=(T // tm,),
        in_specs=[pl.BlockSpec((tm, W_SB), row), pl.BlockSpec((tm, W_MOBA), row),
                  pl.BlockSpec((tm, W_GMLP), row), pl.BlockSpec((tm, D), row),
                  pl.BlockSpec((D, D), const), pl.BlockSpec((1, D), const),
                  pl.BlockSpec((ROUTER_ROWS, D), const), pl.BlockSpec((ROUTER_ROWS, D), const)],
        out_specs=[pl.BlockSpec((tm, D), row), pl.BlockSpec((tm, D // 2), row),
                   pl.BlockSpec((ROUTER_ROWS, tm), lambda i: (0, i))],
        out_shape=[jax.ShapeDtypeStruct((T, D), F32), jax.ShapeDtypeStruct((T, D // 2), jnp.uint32),
                   jax.ShapeDtypeStruct((ROUTER_ROWS, T), F32)],
        compiler_params=pltpu.CompilerParams(dimension_semantics=("arbitrary",),
                                             vmem_limit_bytes=VMEM_LIMIT),
        name="outproj",
    )(o_sb, o_moba, o_gmlp, x2d, w_bf16, g, r_hi, r_lo)


def _moe_kernel(be_ref, na_ref, xs_ref, wg_ref, wu_ref, wd_ref, ys_ref, wgb, wub, wdb):
    b = pl.program_id(0)
    e = be_ref[b]
    prev = be_ref[jnp.maximum(b - 1, 0)]
    active = b < na_ref[0]

    @pl.when(active & ((b == 0) | (e != prev)))
    def _load_expert():
        wgb[...] = wg_ref[...].astype(BF16)
        wub[...] = wu_ref[...].astype(BF16)
        wdb[...] = wd_ref[...].astype(BF16)

    @pl.when(active)
    def _compute():
        xb = _unpack_halves(xs_ref[...]).astype(BF16)
        a = jax.nn.silu(_dot(xb, wgb[...])) * _dot(xb, wub[...])
        ys_ref[...] = _pack_halves(_dot(a.astype(BF16), wdb[...]))

    @pl.when(jnp.logical_not(active))
    def _pad():
        ys_ref[...] = jnp.zeros_like(ys_ref)


def _moe_experts(blk_expert, n_active, xs, w_gate, w_up, w_down, layer):
    n_rows = xs.shape[0]
    D = 2 * xs.shape[1]
    n_blocks = n_rows // MOE_BLOCK
    DE = w_gate.shape[-1]
    grid_spec = pltpu.PrefetchScalarGridSpec(
        num_scalar_prefetch=2,
        grid=(n_blocks,),
        in_specs=[pl.BlockSpec((MOE_BLOCK, D // 2), lambda b, be, na: (b, 0)),
                  pl.BlockSpec((None, None, D, DE), lambda b, be, na: (layer, be[b], 0, 0)),
                  pl.BlockSpec((None, None, D, DE), lambda b, be, na: (layer, be[b], 0, 0)),
                  pl.BlockSpec((None, None, DE, D), lambda b, be, na: (layer, be[b], 0, 0))],
        out_specs=pl.BlockSpec((MOE_BLOCK, D // 2), lambda b, be, na: (b, 0)),
        scratch_shapes=[pltpu.VMEM((D, DE), BF16), pltpu.VMEM((D, DE), BF16),
                        pltpu.VMEM((DE, D), BF16)])
    return pl.pallas_call(
        _moe_kernel,
        grid_spec=grid_spec,
        out_shape=jax.ShapeDtypeStruct((n_rows, D // 2), jnp.uint32),
        compiler_params=pltpu.CompilerParams(dimension_semantics=("arbitrary",),
                                             vmem_limit_bytes=VMEM_LIMIT),
        name="moe_experts",
    )(blk_expert, n_active, xs, w_gate, w_up, w_down)


def _final_kernel(x1_ref, yg_ref, gw_ref, g_ref, o_ref):
    o_ref[...] = _rms(_combine(x1_ref, yg_ref, gw_ref), g_ref[...])


def _final_norm(x2d, yg, gw, g, tm):
    T, D = x2d.shape
    row = lambda i: (i, 0)
    return pl.pallas_call(
        _final_kernel,
        grid=(T // tm,),
        in_specs=[pl.BlockSpec((tm, D), row),
                  pl.BlockSpec((MOE_TOP_K, tm, D // 2), lambda i: (0, i, 0)),
                  pl.BlockSpec((tm, MOE_TOP_K), row), pl.BlockSpec((1, D), lambda i: (0, 0))],
        out_specs=pl.BlockSpec((tm, D), row),
        out_shape=jax.ShapeDtypeStruct((T, D), F32),
        compiler_params=pltpu.CompilerParams(dimension_semantics=("arbitrary",)),
        name="final_norm",
    )(x2d, yg, gw, g)


SC_GATHER_ROWS = 64


def _sc_gather(table, idx):
    info = plsc.get_sparse_core_info()
    n_cores, n_workers = info.num_cores, info.num_cores * info.num_subcores
    N, W = idx.shape[0], table.shape[1]
    per_w = N // n_workers
    n_ch = per_w // SC_GATHER_ROWS
    assert per_w * n_workers == N and n_ch * SC_GATHER_ROWS == per_w
    mesh = plsc.VectorSubcoreMesh(core_axis_name="c", subcore_axis_name="s")

    @functools.partial(
        pl.kernel, mesh=mesh,
        out_type=jax.ShapeDtypeStruct((N, W), table.dtype),
        scratch_types=[pltpu.VMEM((n_ch, SC_GATHER_ROWS), jnp.int32),
                       pltpu.VMEM((2, SC_GATHER_ROWS, W), table.dtype),
                       pltpu.SemaphoreType.DMA((2,)),
                       pltpu.SemaphoreType.DMA((2,))])
    def gather_kernel(table_hbm, idx_hbm, out_hbm, idx_v, rows_v, gsem, ssem):
        wid = lax.axis_index("s") * n_cores + lax.axis_index("c")
        base = wid * per_w
        pltpu.sync_copy(idx_hbm.at[wid], idx_v)

        def fetch(c):
            return pltpu.make_async_copy(table_hbm.at[idx_v.at[c]], rows_v.at[c % 2],
                                         gsem.at[c % 2])

        def write(c):
            return pltpu.make_async_copy(
                rows_v.at[c % 2], out_hbm.at[pl.ds(base + c * SC_GATHER_ROWS, SC_GATHER_ROWS)],
                ssem.at[c % 2])

        fetch(0).start()
        for c in range(n_ch):
            if c + 1 < n_ch:
                if c >= 1:
                    write(c - 1).wait()
                fetch(c + 1).start()
            fetch(c).wait()
            write(c).start()
        if n_ch >= 2:
            write(n_ch - 2).wait()
        write(n_ch - 1).wait()

    return gather_kernel(table, idx.reshape(n_workers, n_ch, SC_GATHER_ROWS))


def _rope_tables(positions):
    half = ROT_DIM // 2
    inv_freq = jnp.power(ROPE_THETA, -jnp.arange(half, dtype=F32) / half)
    ang = positions.astype(F32).reshape(-1)[:, None] * inv_freq
    cos, sin = jnp.cos(ang), jnp.sin(ang)
    T = ang.shape[0]
    ones = jnp.ones((T, HEAD_DIM - ROT_DIM), F32)
    zeros = jnp.zeros((T, HEAD_DIM - ROT_DIM), F32)
    z8 = jnp.zeros((T, half), F32)
    rc = jnp.concatenate([cos, cos, ones], axis=1)
    rs1 = jnp.concatenate([-sin, z8, zeros], axis=1)
    rs2 = jnp.concatenate([z8, sin, zeros], axis=1)
    rep = LANES // HEAD_DIM
    return jnp.tile(rc, (1, rep)), jnp.tile(rs1, (1, rep)), jnp.tile(rs2, (1, rep))


def _route(logits_t, T):
    lg = logits_t.T
    p_group = jax.nn.softmax(lg[:, :N_EXPERT_GROUPS], axis=-1)
    g_sel = jnp.argmax(p_group, axis=-1)
    p_g = jnp.max(p_group, axis=-1)
    logit_e = lg[:, N_EXPERT_GROUPS:N_EXPERT_GROUPS + N_EXPERTS].reshape(
        T, N_EXPERT_GROUPS, EXPERTS_PER_GROUP)
    logit_e = jnp.take_along_axis(logit_e, g_sel[:, None, None], axis=1)[:, 0]
    top_p, top_i = lax.top_k(jax.nn.softmax(logit_e, axis=-1), MOE_TOP_K)
    top_p = top_p / jnp.sum(top_p, axis=-1, keepdims=True)
    expert_id = (g_sel[:, None] * EXPERTS_PER_GROUP + top_i).astype(jnp.int32)
    gate_w = p_g[:, None] * top_p
    return expert_id, gate_w


def _dispatch_plan(expert_id, T):
    n_slots = T * MOE_TOP_K
    n_blocks = -(-n_slots // MOE_BLOCK) + N_EXPERTS
    eid = expert_id.reshape(-1)
    onehot = (eid[:, None] == jnp.arange(N_EXPERTS, dtype=jnp.int32)[None, :]).astype(jnp.int32)
    csum = jnp.cumsum(onehot, axis=0)
    counts = csum[-1]
    rank = jnp.sum(onehot * csum, axis=1) - 1
    padded = (counts + MOE_BLOCK - 1) // MOE_BLOCK * MOE_BLOCK
    pad_end = jnp.cumsum(padded)
    pad_start = pad_end - padded
    pos = pad_start[eid] + rank
    token_id = jnp.repeat(jnp.arange(T, dtype=jnp.int32), MOE_TOP_K)
    buf_tok = jnp.zeros((n_blocks * MOE_BLOCK,), jnp.int32).at[pos].set(token_id)
    blk_expert = jnp.minimum(
        jnp.searchsorted(pad_end, jnp.arange(n_blocks, dtype=jnp.int32) * MOE_BLOCK, side='right'),
        N_EXPERTS - 1).astype(jnp.int32)
    n_active = (pad_end[-1] // MOE_BLOCK).astype(jnp.int32).reshape(1)
    return pos.reshape(T, MOE_TOP_K), buf_tok, blk_expert, n_active


def kernel(x, positions, w_in, w_out, g_mix_norm, g_head_norm, g_gmlp_vnorm, w_spatial, b_spatial,
           g_ffn_norm, w_router_group, w_router_expert, w_expert_gate, w_expert_up, w_expert_down,
           g_final):
    B, S, D = x.shape
    T = B * S
    depth = w_in.shape[0]
    tm = min(512, T)
    rc, rs1, rs2 = _rope_tables(positions)
    xc = x.reshape(T, D)
    moe_out = None
    for l in range(depth):
        gh = g_head_norm[l].reshape(1, -1)
        proj, xc = _inproj(xc, moe_out, g_mix_norm[l].reshape(1, D), w_in[l].astype(BF16),
                           rc, rs1, rs2, tm)
        o_sb = _sb_attention(proj, gh[:, :W_SB], B, S)
        o_moba = _moba_attention(proj, gh[:, W_SB:W_SB + W_MOBA], B, S)
        b_exp = jnp.repeat(b_spatial[l].T, HEAD_DIM, axis=1)
        o_gmlp = _gmlp(proj, g_gmlp_vnorm[l].reshape(1, -1), w_spatial[l], b_exp,
                       gh[:, W_SB + W_MOBA:], tm)
        w_r = jnp.concatenate([w_router_group[l], w_router_expert[l]], axis=1).T
        w_r = jnp.pad(w_r, ((0, ROUTER_ROWS - w_r.shape[0]), (0, 0)))
        r_hi, r_lo = _split_bf16(w_r)
        xc, h, logits_t = _out_proj(o_sb, o_moba, o_gmlp, xc, w_out[l].astype(BF16),
                                    g_ffn_norm[l].reshape(1, D), r_hi, r_lo, tm)
        expert_id, gate_w = _route(logits_t, T)
        pos, buf_tok, blk_expert, n_active = _dispatch_plan(expert_id, T)
        xs = _sc_gather(h, buf_tok)
        ys = _moe_experts(blk_expert, n_active, xs, w_expert_gate, w_expert_up, w_expert_down, l)
        yg = _sc_gather(ys, pos.T.reshape(-1)).reshape(MOE_TOP_K, T, D // 2)
        moe_out = (yg, gate_w)
    return _final_norm(xc, moe_out[0], moe_out[1], g_final.reshape(1, D), tm).reshape(B, S, D)
```

```python
import functools

import jax
import jax.numpy as jnp
from jax import lax
from jax.experimental import pallas as pl
from jax.experimental.pallas import tpu as pltpu
from jax.experimental.pallas import tpu_sc as plsc

F32 = jnp.float32
BF16 = jnp.bfloat16

HEAD_DIM = 64
LANES = 128
MXU_COLS = 256
N_HEADS_SB = 4
N_HEADS_MOBA = 8
N_GROUPS_GMLP = 4
W_SB = N_HEADS_SB * HEAD_DIM
W_MOBA = N_HEADS_MOBA * HEAD_DIM
W_GMLP = N_GROUPS_GMLP * HEAD_DIM
MOBA_BLOCK = 256
MOBA_TOPK = 3
GMLP_CHUNK = 128
ROPE_THETA = 500000.0
ROT_DIM = HEAD_DIM // 4
N_EXPERT_GROUPS = 4
EXPERTS_PER_GROUP = 8
N_EXPERTS = N_EXPERT_GROUPS * EXPERTS_PER_GROUP
MOE_TOP_K = 2
MOE_BLOCK = 256
NORM_EPS = 1e-6
ATTN_SCALE = HEAD_DIM ** -0.5
NEG = -1e30
LOG2E = 1.4426950408889634
ROUTER_ROWS = 64
VMEM_LIMIT = 48 * 1024 * 1024

C_QA, C_KA, C_VA = 0, W_SB, 2 * W_SB
C_QB = 3 * W_SB
C_KB = C_QB + W_MOBA
C_VB = C_KB + W_MOBA
C_UC = C_VB + W_MOBA
C_VC = C_UC + W_GMLP
IN_COLS = C_VC + W_GMLP


def _nt_dot(a, b):
    return lax.dot_general(a, b, (((1,), (1,)), ((), ())), preferred_element_type=F32)


def _dot(a, b):
    return jnp.dot(a, b, preferred_element_type=F32)


def _rms(x, g):
    return x * lax.rsqrt(jnp.mean(x * x, axis=-1, keepdims=True) + NORM_EPS) * g


def _pack_halves(x):
    w = x.shape[1] // 2
    lo = lax.bitcast_convert_type(x[:, :w].astype(BF16).astype(F32), jnp.uint32)
    hi = lax.bitcast_convert_type(x[:, w:].astype(BF16).astype(F32), jnp.uint32)
    return (lo >> 16) | hi


def _unpack_halves(words):
    lo = lax.bitcast_convert_type(words << 16, F32)
    hi = lax.bitcast_convert_type(words & jnp.uint32(0xFFFF0000), F32)
    return jnp.concatenate([lo, hi], axis=1)


def _split_bf16(x):
    hi = x.astype(BF16)
    lo = (x - hi.astype(F32)).astype(BF16)
    return hi, lo


def _combine(x1_ref, yg_ref, gw_ref):
    gw = gw_ref[...]
    return (x1_ref[...] + _unpack_halves(yg_ref[0]) * gw[:, 0:1]
            + _unpack_halves(yg_ref[1]) * gw[:, 1:2])


def _inproj_kernel(*refs, combine):
    if combine:
        x1_ref, yg_ref, gw_ref, g_ref, w_ref, rc_ref, rs1_ref, rs2_ref, o_ref, x_ref = refs
        x = _combine(x1_ref, yg_ref, gw_ref)
        x_ref[...] = x
    else:
        x_ref, g_ref, w_ref, rc_ref, rs1_ref, rs2_ref, o_ref = refs
        x = x_ref[...]
    y = _rms(x, g_ref[...]).astype(BF16)
    wide = lambda t_ref: jnp.concatenate([t_ref[...]] * (MXU_COLS // LANES), axis=1)
    rc, rs1, rs2 = wide(rc_ref), wide(rs1_ref), wide(rs2_ref)
    half = ROT_DIM // 2
    for c0 in range(0, IN_COLS, MXU_COLS):
        p = _dot(y, w_ref[:, c0:c0 + MXU_COLS])
        if C_QB <= c0 < C_VB:
            p = (p * rc + pltpu.roll(p, MXU_COLS - half, axis=1) * rs1
                 + pltpu.roll(p, half, axis=1) * rs2)
        if c0 < C_KA or C_QB <= c0 < C_KB:
            p = p * (ATTN_SCALE * LOG2E)
        o_ref[:, c0:c0 + MXU_COLS] = p.astype(BF16)


def _inproj(x2d, moe_out, g, w_bf16, rc, rs1, rs2, tm):
    T, D = x2d.shape
    row = lambda i: (i, 0)
    const = lambda i: (0, 0)
    combine = moe_out is not None
    x_specs = [pl.BlockSpec((tm, D), row)]
    x_args = [x2d]
    out_specs = [pl.BlockSpec((tm, IN_COLS), row)]
    out_shape = [jax.ShapeDtypeStruct((T, IN_COLS), BF16)]
    if combine:
        yg, gw = moe_out
        x_specs += [pl.BlockSpec((MOE_TOP_K, tm, D // 2), lambda i: (0, i, 0)),
                    pl.BlockSpec((tm, MOE_TOP_K), row)]
        x_args += [yg, gw]
        out_specs.append(pl.BlockSpec((tm, D), row))
        out_shape.append(jax.ShapeDtypeStruct((T, D), F32))
    outs = pl.pallas_call(
        functools.partial(_inproj_kernel, combine=combine),
        grid=(T // tm,),
        in_specs=x_specs + [pl.BlockSpec((1, D), const), pl.BlockSpec((D, IN_COLS), const),
                            pl.BlockSpec((tm, LANES), row), pl.BlockSpec((tm, LANES), row),
                            pl.BlockSpec((tm, LANES), row)],
        out_specs=out_specs,
        out_shape=out_shape,
        compiler_params=pltpu.CompilerParams(dimension_semantics=("arbitrary",),
                                             vmem_limit_bytes=VMEM_LIMIT),
        name="inproj",
    )(*x_args, g, w_bf16, rc, rs1, rs2)
    return (outs[0], outs[1]) if combine else (outs[0], x2d)


def _sb_kernel(q_ref, k_ref, v_ref, gh_ref, o_ref, *, tq, tk, sub):
    i = pl.program_id(2)
    S = k_ref.shape[0]
    n_heads = LANES // HEAD_DIM
    n_sub = tk // sub
    qt = q_ref[...].astype(F32).T
    zeros_t = jnp.zeros((HEAD_DIM, tq), F32)
    qz = [jnp.concatenate([qt[hh * HEAD_DIM:(hh + 1) * HEAD_DIM] if h2 == hh else zeros_t
                           for h2 in range(n_heads)], axis=0).astype(BF16)
          for hh in range(n_heads)]
    srow = lax.broadcasted_iota(jnp.int32, (sub, sub), 0)
    scol = lax.broadcasted_iota(jnp.int32, (sub, sub), 1)
    later = jnp.where(scol > srow, 1.0, 0.0).astype(BF16)
    krow = lax.broadcasted_iota(jnp.int32, (tk, tq), 0)
    qcol = lax.broadcasted_iota(jnp.int32, (tk, tq), 1)
    last = (i * tq) // tk
    causal = krow - qcol < i * tq - last * tk

    def scores(t):
        return [_dot(k_ref[t * tk:(t + 1) * tk, :], qz[hh]) for hh in range(n_heads)]

    def weights_pv(zs, t, carry, diag):
        wts, runs = [], []
        for hh in range(n_heads):
            z = zs[hh]
            nz = -z
            lr = jnp.minimum(nz, 0.0) - jnp.log2(1.0 + jnp.exp2(jnp.minimum(z, nz)))
            if diag:
                lr = jnp.where(causal, lr, 0.0)
            lrb = lr.astype(BF16)
            tail = carry[hh][0]
            afters = [None] * n_sub
            for sb in reversed(range(n_sub)):
                afters[sb] = _dot(later, lrb[sb * sub:(sb + 1) * sub]) + tail
                tail = afters[sb][0:1] + lr[sb * sub:sb * sub + 1]
            w = jnp.exp2(z + lr + jnp.concatenate(afters, axis=0))
            if diag:
                w = jnp.where(causal, w, 0.0)
            wts.append(w.astype(BF16))
            runs.append(tail)
        new = []
        for hh in range(n_heads):
            pv = lax.dot_general(v_ref[t * tk:(t + 1) * tk, :], wts[hh],
                                 (((0,), (0,)), ((), ())), preferred_element_type=F32)
            new.append((runs[hh], carry[hh][1] + pv[hh * HEAD_DIM:(hh + 1) * HEAD_DIM]))
        return tuple(new)

    def attend(n_past):
        order = list(range(n_past, -1, -1))
        carry = tuple((jnp.zeros((1, tq), F32), jnp.zeros((HEAD_DIM, tq), F32))
                      for _ in range(n_heads))
        zs = scores(order[0])
        for n, t in enumerate(order):
            nxt = scores(order[n + 1]) if n + 1 < len(order) else None
            carry = weights_pv(zs, t, carry, diag=(n == 0))
            zs = nxt
        outs = []
        for hh in range(n_heads):
            o = carry[hh][1]
            outs.append(o * lax.rsqrt(jnp.mean(o * o, axis=0, keepdims=True) + NORM_EPS))
        o_ref[...] = (jnp.concatenate(outs, axis=0).T * gh_ref[...]).astype(BF16)

    for n_past in range(S // tk):
        pl.when(last == n_past)(functools.partial(attend, n_past))


def _sb_attention(proj, gh, B, S):
    tq, sub = 256, 256
    tk = 2 * sub
    assert S % tk == 0
    nq = S // tq
    n_pairs = W_SB // LANES
    return pl.pallas_call(
        functools.partial(_sb_kernel, tq=tq, tk=tk, sub=sub),
        grid=(B, n_pairs, nq),
        in_specs=[pl.BlockSpec((tq, LANES), lambda b, p, i: (b * nq + i, C_QA // LANES + p)),
                  pl.BlockSpec((S, LANES), lambda b, p, i: (b, C_KA // LANES + p)),
                  pl.BlockSpec((S, LANES), lambda b, p, i: (b, C_VA // LANES + p)),
                  pl.BlockSpec((1, LANES), lambda b, p, i: (0, p))],
        out_specs=pl.BlockSpec((tq, LANES), lambda b, p, i: (b * nq + i, p)),
        out_shape=jax.ShapeDtypeStruct((B * S, W_SB), BF16),
        compiler_params=pltpu.CompilerParams(
            dimension_semantics=("arbitrary", "arbitrary", "arbitrary"),
            vmem_limit_bytes=VMEM_LIMIT),
        name="sb_attn",
    )(proj, proj, proj, gh)


def _moba_kernel(q_ref, k_ref, v_ref, gh_ref, o_ref, kaug_ref, vaug_ref, rhi_ref, rlo_ref,
                 *, tq, tk, topk):
    i = pl.program_id(2)
    S = k_ref.shape[0]
    n_heads = LANES // HEAD_DIM

    @pl.when(i == 0)
    def _prepare_keys():
        kp = k_ref[...]
        vp = v_ref[...]
        srow = lax.broadcasted_iota(jnp.int32, (S, HEAD_DIM), 0)
        scol = lax.broadcasted_iota(jnp.int32, (S, HEAD_DIM), 1)
        onehot = jnp.where(srow // MOBA_BLOCK == scol, 1.0, 0.0).astype(BF16)
        arow = lax.broadcasted_iota(jnp.int32, (LANES, S), 0) - HEAD_DIM
        acol = lax.broadcasted_iota(jnp.int32, (LANES, S), 1) // MOBA_BLOCK
        avg = jnp.where(arow == acol, 1.0 / MOBA_BLOCK, 0.0).astype(BF16)
        zeros = jnp.zeros((S, HEAD_DIM), BF16)
        ones = jnp.ones((S, HEAD_DIM), BF16)
        for hh in range(n_heads):
            kh = kp[:, hh * HEAD_DIM:(hh + 1) * HEAD_DIM]
            kaug_ref[hh] = jnp.concatenate([kh, onehot], axis=1)
            vaug_ref[hh] = jnp.concatenate([vp[:, hh * HEAD_DIM:(hh + 1) * HEAD_DIM], ones], axis=1)
            kmean = _dot(avg, jnp.concatenate([kh, zeros], axis=1))
            hi, lo = _split_bf16(kmean)
            rhi_ref[hh] = hi
            rlo_ref[hh] = lo

    qt = q_ref[...].astype(F32).T
    group = 8
    bidx = lax.broadcasted_iota(jnp.int32, (group, tq), 0)
    zeros_t = jnp.zeros((HEAD_DIM, tq), F32)
    qaugs = []
    for hh in range(n_heads):
        qh = qt[hh * HEAD_DIM:(hh + 1) * HEAD_DIM]
        qz = jnp.concatenate([qh, zeros_t], axis=0).astype(BF16)
        gate = (_dot(rhi_ref[hh], qz) + _dot(rlo_ref[hh], qz))[HEAD_DIM:HEAD_DIM + group]
        valid = bidx < i
        gm = jnp.where(valid, gate, NEG)
        rank = jnp.zeros((group, tq), F32)
        for d in range(1, group):
            nb = pltpu.roll(gm, d, axis=0)
            rank = rank + jnp.where(bidx >= d, jnp.where(nb >= gm, 1.0, 0.0),
                                    jnp.where(nb > gm, 1.0, 0.0))
        allowed = (valid & (rank < topk)) | (bidx == i)
        bias = jnp.where(allowed, 0.0, NEG)
        qaugs.append(jnp.concatenate(
            [qh, bias, jnp.zeros((HEAD_DIM - group, tq), F32)], axis=0).astype(BF16))

    krow = lax.broadcasted_iota(jnp.int32, (tk, tq), 0)
    qcol = lax.broadcasted_iota(jnp.int32, (tk, tq), 1)
    last = (i * tq) // tk
    causal = krow - qcol <= i * tq - last * tk

    def scores(t):
        return [_dot(kaug_ref[hh, t * tk:(t + 1) * tk, :], qaugs[hh]) for hh in range(n_heads)]

    def softmax_pv(sts, t, carry, diag):
        pts, stats = [], []
        for hh in range(n_heads):
            m, l, acc = carry[hh]
            st = jnp.where(causal, sts[hh], NEG) if diag else sts[hh]
            m_new = jnp.maximum(m, jnp.max(st, axis=0, keepdims=True))
            pts.append(jnp.exp2(st - m_new).astype(BF16))
            stats.append((m_new, jnp.exp2(m - m_new)))
        new = []
        for hh in range(n_heads):
            _, l, acc = carry[hh]
            m_new, alpha = stats[hh]
            pv = lax.dot_general(vaug_ref[hh, t * tk:(t + 1) * tk, :], pts[hh],
                                 (((0,), (0,)), ((), ())), preferred_element_type=F32)
            new.append((m_new, alpha * l + pv[HEAD_DIM:HEAD_DIM + 1],
                        alpha * acc + pv[:HEAD_DIM]))
        return tuple(new)

    def attend(n_past):
        order = [n_past] + list(range(n_past))
        carry = tuple((jnp.full((1, tq), NEG, F32), jnp.zeros((1, tq), F32),
                       jnp.zeros((HEAD_DIM, tq), F32)) for _ in range(n_heads))
        sts = scores(order[0])
        for n, t in enumerate(order):
            nxt = scores(order[n + 1]) if n + 1 < len(order) else None
            carry = softmax_pv(sts, t, carry, diag=(n == 0))
            sts = nxt
        outs = []
        for hh in range(n_heads):
            _, l, acc = carry[hh]
            o = acc / l
            outs.append(o * lax.rsqrt(jnp.mean(o * o, axis=0, keepdims=True) + NORM_EPS))
        o_ref[...] = (jnp.concatenate(outs, axis=0).T * gh_ref[...]).astype(BF16)

    for n_past in range(S // tk):
        pl.when(last == n_past)(functools.partial(attend, n_past))


def _moba_attention(proj, gh, B, S):
    tq = MOBA_BLOCK
    tk = 2 * MOBA_BLOCK
    assert S % tk == 0
    nq = S // tq
    n_blk = S // MOBA_BLOCK
    assert n_blk <= 8
    topk = min(MOBA_TOPK, max(n_blk - 1, 1))
    n_pairs = W_MOBA // LANES
    n_heads = LANES // HEAD_DIM
    return pl.pallas_call(
        functools.partial(_moba_kernel, tq=tq, tk=tk, topk=topk),
        grid=(B, n_pairs, nq),
        in_specs=[pl.BlockSpec((tq, LANES), lambda b, p, i: (b * nq + i, C_QB // LANES + p)),
                  pl.BlockSpec((S, LANES), lambda b, p, i: (b, C_KB // LANES + p)),
                  pl.BlockSpec((S, LANES), lambda b, p, i: (b, C_VB // LANES + p)),
                  pl.BlockSpec((1, LANES), lambda b, p, i: (0, p))],
        out_specs=pl.BlockSpec((tq, LANES), lambda b, p, i: (b * nq + i, p)),
        out_shape=jax.ShapeDtypeStruct((B * S, W_MOBA), BF16),
        scratch_shapes=[pltpu.VMEM((n_heads, S, LANES), BF16),
                        pltpu.VMEM((n_heads, S, LANES), BF16),
                        pltpu.VMEM((n_heads, LANES, LANES), BF16),
                        pltpu.VMEM((n_heads, LANES, LANES), BF16)],
        compiler_params=pltpu.CompilerParams(
            dimension_semantics=("arbitrary", "arbitrary", "arbitrary"),
            vmem_limit_bytes=VMEM_LIMIT),
        name="moba_attn",
    )(proj, proj, proj, gh)


def _gmlp_kernel(u_ref, v_ref, gv_ref, ws_ref, b_ref, gh_ref, o_ref, *, tm):
    gu = jax.nn.gelu(u_ref[...].astype(F32))
    gv = jax.nn.gelu(v_ref[...].astype(F32))
    row = lax.broadcasted_iota(jnp.int32, (GMLP_CHUNK, GMLP_CHUNK), 0)
    col = lax.broadcasted_iota(jnp.int32, (GMLP_CHUNK, GMLP_CHUNK), 1)
    outs = []
    for g in range(N_GROUPS_GMLP):
        c0, c1 = g * HEAD_DIM, (g + 1) * HEAD_DIM
        vn = _rms(gv[:, c0:c1], gv_ref[:, c0:c1]).astype(BF16)
        wm = jnp.where(col <= row, ws_ref[g], 0.0).astype(BF16)
        bias = b_ref[:, c0:c1]
        mixed = jnp.concatenate(
            [_dot(wm, vn[c * GMLP_CHUNK:(c + 1) * GMLP_CHUNK]) + bias
             for c in range(tm // GMLP_CHUNK)], axis=0)
        outs.append(_rms(gu[:, c0:c1] * mixed, gh_ref[:, c0:c1]))
    o_ref[...] = jnp.concatenate(outs, axis=1).astype(BF16)


def _gmlp(proj, gv, ws, b_exp, gh, tm):
    T = proj.shape[0]
    nu, nv = C_UC // W_GMLP, C_VC // W_GMLP
    return pl.pallas_call(
        functools.partial(_gmlp_kernel, tm=tm),
        grid=(T // tm,),
        in_specs=[pl.BlockSpec((tm, W_GMLP), lambda i: (i, nu)),
                  pl.BlockSpec((tm, W_GMLP), lambda i: (i, nv)),
                  pl.BlockSpec((1, W_GMLP), lambda i: (0, 0)),
                  pl.BlockSpec((N_GROUPS_GMLP, GMLP_CHUNK, GMLP_CHUNK), lambda i: (0, 0, 0)),
                  pl.BlockSpec((GMLP_CHUNK, W_GMLP), lambda i: (0, 0)),
                  pl.BlockSpec((1, W_GMLP), lambda i: (0, 0))],
        out_specs=pl.BlockSpec((tm, W_GMLP), lambda i: (i, 0)),
        out_shape=jax.ShapeDtypeStruct((T, W_GMLP), BF16),
        compiler_params=pltpu.CompilerParams(dimension_semantics=("arbitrary",),
                                             vmem_limit_bytes=VMEM_LIMIT),
        name="gmlp",
    )(proj, proj, gv, ws, b_exp, gh)


def _out_kernel(osb_ref, omoba_ref, ogmlp_ref, x_ref, w_ref, g_ref, rhi_ref, rlo_ref,
                x1_ref, h_ref, lg_ref):
    x1 = (x_ref[...]
          + _dot(osb_ref[...], w_ref[0:W_SB, :])
          + _dot(omoba_ref[...], w_ref[W_SB:W_SB + W_MOBA, :])
          + _dot(ogmlp_ref[...], w_ref[W_SB + W_MOBA:, :]))
    x1_ref[...] = x1
    hn = _rms(x1, g_ref[...])
    hi, lo = _split_bf16(hn)
    h_ref[...] = _pack_halves(hn)
    lg_ref[...] = (_nt_dot(rhi_ref[...], hi) + _nt_dot(rhi_ref[...], lo)
                   + _nt_dot(rlo_ref[...], hi))


def _out_proj(o_sb, o_moba, o_gmlp, x2d, w_bf16, g, r_hi, r_lo, tm):
    T, D = x2d.shape
    row = lambda i: (i, 0)
    const = lambda i: (0, 0)
    return pl.pallas_call(
        _out_kernel,
        grid=(T // tm,),
        in_specs=[pl.BlockSpec((tm, W_SB), row), pl.BlockSpec((tm, W_MOBA), row),
                  pl.BlockSpec((tm, W_GMLP), row), pl.BlockSpec((tm, D), row),
                  pl.BlockSpec((D, D), const), pl.BlockSpec((1, D), const),
                  pl.BlockSpec((ROUTER_ROWS, D), const), pl.BlockSpec((ROUTER_ROWS, D), const)],
        out_specs=[pl.BlockSpec((tm, D), row), pl.BlockSpec((tm, D // 2), row),
                   pl.BlockSpec((ROUTER_ROWS, tm), lambda i: (0, i))],
        out_shape=[jax.ShapeDtypeStruct((T, D), F32), jax.ShapeDtypeStruct((T, D // 2), jnp.uint32),
                   jax.ShapeDtypeStruct((ROUTER_ROWS, T), F32)],
        compiler_params=pltpu.CompilerParams(dimension_semantics=("arbitrary",),
                                             vmem_limit_bytes=VMEM_LIMIT),
        name="outproj",
    )(o_sb, o_moba, o_gmlp, x2d, w_bf16, g, r_hi, r_lo)


ROUTER_EXPERT_ROW = 8


def _first_max(p):
    rows = lax.broadcasted_iota(jnp.int32, p.shape, 0).astype(F32)
    top = jnp.max(p, axis=0, keepdims=True)
    idx = jnp.min(jnp.where(p == top, rows, float(p.shape[0])), axis=0, keepdims=True)
    return top, idx, rows


def _router_kernel(lg_ref, pos_ref, gw_ref, cnt_ref, cnt_acc, base_ref, *, tm):
    phase = pl.program_id(0)
    i = pl.program_id(1)
    lg = lg_ref[...]
    gl = lg[0:N_EXPERT_GROUPS]
    ge = jnp.exp(gl - jnp.max(gl, axis=0, keepdims=True))
    p_group = ge / jnp.sum(ge, axis=0, keepdims=True)
    p_g, g_sel, _ = _first_max(p_group)
    le = jnp.zeros((EXPERTS_PER_GROUP, tm), F32)
    for g in range(N_EXPERT_GROUPS):
        r0 = ROUTER_EXPERT_ROW + g * EXPERTS_PER_GROUP
        le = jnp.where(g_sel == g, lg[r0:r0 + EXPERTS_PER_GROUP], le)
    ee = jnp.exp(le - jnp.max(le, axis=0, keepdims=True))
    p = ee / jnp.sum(ee, axis=0, keepdims=True)
    p0, i0, rows = _first_max(p)
    p1, i1, _ = _first_max(jnp.where(rows == i0, -1.0, p))
    e0 = g_sel * EXPERTS_PER_GROUP + i0
    e1 = g_sel * EXPERTS_PER_GROUP + i1
    xrow = lax.broadcasted_iota(jnp.int32, (N_EXPERTS, tm), 0).astype(F32)
    oh0 = xrow == e0
    oh1 = xrow == e1
    slots = jnp.where(oh0, 1.0, 0.0) + jnp.where(oh1, 1.0, 0.0)
    tile_cnt = slots[:, 0:LANES]
    for c in range(1, tm // LANES):
        tile_cnt = tile_cnt + slots[:, c * LANES:(c + 1) * LANES]

    @pl.when((phase == 0) & (i == 0))
    def _init():
        cnt_acc[...] = jnp.zeros_like(cnt_acc)

    @pl.when(phase == 0)
    def _count():
        cnt_acc[...] += tile_cnt

    @pl.when((phase == 1) & (i == 0))
    def _starts():
        counts = jnp.sum(cnt_acc[...], axis=1, keepdims=True)
        n_blk = jnp.floor((counts + (MOE_BLOCK - 1)) * (1.0 / MOE_BLOCK))
        er = lax.broadcasted_iota(jnp.int32, (N_EXPERTS, N_EXPERTS), 0)
        ec = lax.broadcasted_iota(jnp.int32, (N_EXPERTS, N_EXPERTS), 1)
        before = jnp.where(ec < er, 1.0, 0.0).astype(BF16)
        start_blk = _dot(before, jnp.broadcast_to(n_blk, (N_EXPERTS, LANES)).astype(BF16))
        base_ref[...] = start_blk * MOE_BLOCK
        cnt_ref[...] = jnp.broadcast_to(counts, (N_EXPERTS, LANES)).astype(jnp.int32)

    @pl.when(phase == 1)
    def _assign():
        tr = lax.broadcasted_iota(jnp.int32, (tm, tm), 0)
        tc = lax.broadcasted_iota(jnp.int32, (tm, tm), 1)
        earlier = jnp.where(tr < tc, 1.0, 0.0).astype(BF16)
        row_of = _dot(slots.astype(BF16), earlier) + base_ref[:, 0:1]
        pos0 = jnp.sum(jnp.where(oh0, row_of, 0.0), axis=0, keepdims=True)
        pos1 = jnp.sum(jnp.where(oh1, row_of, 0.0), axis=0, keepdims=True)
        pos_ref[...] = jnp.concatenate([pos0, pos1], axis=0).astype(jnp.int32)
        scale = p_g / (p0 + p1)
        gw_ref[...] = jnp.concatenate([p0 * scale, p1 * scale], axis=0)
        base_ref[...] += jnp.sum(tile_cnt, axis=1, keepdims=True)


def _router(logits_t, tm):
    T = logits_t.shape[1]
    tok = lambda p, i: (0, i * p)
    return pl.pallas_call(
        functools.partial(_router_kernel, tm=tm),
        grid=(2, T // tm),
        in_specs=[pl.BlockSpec((ROUTER_ROWS, tm), lambda p, i: (0, i))],
        out_specs=[pl.BlockSpec((MOE_TOP_K, tm), tok), pl.BlockSpec((MOE_TOP_K, tm), tok),
                   pl.BlockSpec((N_EXPERTS, LANES), lambda p, i: (0, 0))],
        out_shape=[jax.ShapeDtypeStruct((MOE_TOP_K, T), jnp.int32),
                   jax.ShapeDtypeStruct((MOE_TOP_K, T), F32),
                   jax.ShapeDtypeStruct((N_EXPERTS, LANES), jnp.int32)],
        scratch_shapes=[pltpu.VMEM((N_EXPERTS, LANES), F32), pltpu.VMEM((N_EXPERTS, LANES), F32)],
        compiler_params=pltpu.CompilerParams(dimension_semantics=("arbitrary", "arbitrary")),
        name="router",
    )(logits_t)


def _moe_kernel(be_ref, na_ref, xs_ref, wg_ref, wu_ref, wd_ref, ys_ref, wgb, wub, wdb):
    b = pl.program_id(0)
    e = be_ref[b]
    prev = be_ref[jnp.maximum(b - 1, 0)]
    active = b < na_ref[0]

    @pl.when(active & ((b == 0) | (e != prev)))
    def _load_expert():
        wgb[...] = wg_ref[...].astype(BF16)
        wub[...] = wu_ref[...].astype(BF16)
        wdb[...] = wd_ref[...].astype(BF16)

    @pl.when(active)
    def _compute():
        xb = _unpack_halves(xs_ref[...]).astype(BF16)
        a = jax.nn.silu(_dot(xb, wgb[...])) * _dot(xb, wub[...])
        ys_ref[...] = _pack_halves(_dot(a.astype(BF16), wdb[...]))

    @pl.when(jnp.logical_not(active))
    def _pad():
        ys_ref[...] = jnp.zeros_like(ys_ref)


def _moe_experts(blk_expert, n_active, xs, w_gate, w_up, w_down, layer):
    n_rows = xs.shape[0]
    D = 2 * xs.shape[1]
    n_blocks = n_rows // MOE_BLOCK
    DE = w_gate.shape[-1]
    grid_spec = pltpu.PrefetchScalarGridSpec(
        num_scalar_prefetch=2,
        grid=(n_blocks,),
        in_specs=[pl.BlockSpec((MOE_BLOCK, D // 2), lambda b, be, na: (b, 0)),
                  pl.BlockSpec((None, None, D, DE), lambda b, be, na: (layer, be[b], 0, 0)),
                  pl.BlockSpec((None, None, D, DE), lambda b, be, na: (layer, be[b], 0, 0)),
                  pl.BlockSpec((None, None, DE, D), lambda b, be, na: (layer, be[b], 0, 0))],
        out_specs=pl.BlockSpec((MOE_BLOCK, D // 2), lambda b, be, na: (b, 0)),
        scratch_shapes=[pltpu.VMEM((D, DE), BF16), pltpu.VMEM((D, DE), BF16),
                        pltpu.VMEM((DE, D), BF16)])
    return pl.pallas_call(
        _moe_kernel,
        grid_spec=grid_spec,
        out_shape=jax.ShapeDtypeStruct((n_rows, D // 2), jnp.uint32),
        compiler_params=pltpu.CompilerParams(dimension_semantics=("arbitrary",),
                                             vmem_limit_bytes=VMEM_LIMIT),
        name="moe_experts",
    )(blk_expert, n_active, xs, w_gate, w_up, w_down)


def _final_kernel(x1_ref, yg_ref, gw_ref, g_ref, o_ref):
    o_ref[...] = _rms(_combine(x1_ref, yg_ref, gw_ref), g_ref[...])


def _final_norm(x2d, yg, gw, g, tm):
    T, D = x2d.shape
    row = lambda i: (i, 0)
    return pl.pallas_call(
        _final_kernel,
        grid=(T // tm,),
        in_specs=[pl.BlockSpec((tm, D), row),
                  pl.BlockSpec((MOE_TOP_K, tm, D // 2), lambda i: (0, i, 0)),
                  pl.BlockSpec((tm, MOE_TOP_K), row), pl.BlockSpec((1, D), lambda i: (0, 0))],
        out_specs=pl.BlockSpec((tm, D), row),
        out_shape=jax.ShapeDtypeStruct((T, D), F32),
        compiler_params=pltpu.CompilerParams(dimension_semantics=("arbitrary",)),
        name="final_norm",
    )(x2d, yg, gw, g)


SC_GATHER_ROWS = 64


def _sc_gather(table, idx):
    info = plsc.get_sparse_core_info()
    n_cores, n_workers = info.num_cores, info.num_cores * info.num_subcores
    N, W = idx.shape[0], table.shape[1]
    per_w = N // n_workers
    n_ch = per_w // SC_GATHER_ROWS
    assert per_w * n_workers == N and n_ch * SC_GATHER_ROWS == per_w
    mesh = plsc.VectorSubcoreMesh(core_axis_name="c", subcore_axis_name="s")

    @functools.partial(
        pl.kernel, mesh=mesh,
        out_type=jax.ShapeDtypeStruct((N, W), table.dtype),
        scratch_types=[pltpu.VMEM((n_ch, SC_GATHER_ROWS), jnp.int32),
                       pltpu.VMEM((2, SC_GATHER_ROWS, W), table.dtype),
                       pltpu.SemaphoreType.DMA((2,)),
                       pltpu.SemaphoreType.DMA((2,))])
    def gather_kernel(table_hbm, idx_hbm, out_hbm, idx_v, rows_v, gsem, ssem):
        wid = lax.axis_index("s") * n_cores + lax.axis_index("c")
        base = wid * per_w
        pltpu.sync_copy(idx_hbm.at[wid], idx_v)

        def fetch(c):
            return pltpu.make_async_copy(table_hbm.at[idx_v.at[c]], rows_v.at[c % 2],
                                         gsem.at[c % 2])

        def write(c):
            return pltpu.make_async_copy(
                rows_v.at[c % 2], out_hbm.at[pl.ds(base + c * SC_GATHER_ROWS, SC_GATHER_ROWS)],
                ssem.at[c % 2])

        fetch(0).start()
        for c in range(n_ch):
            if c + 1 < n_ch:
                if c >= 1:
                    write(c - 1).wait()
                fetch(c + 1).start()
            fetch(c).wait()
            write(c).start()
        if n_ch >= 2:
            write(n_ch - 2).wait()
        write(n_ch - 1).wait()

    return gather_kernel(table, idx.reshape(n_workers, n_ch, SC_GATHER_ROWS))


def _rope_tables(positions):
    half = ROT_DIM // 2
    inv_freq = jnp.power(ROPE_THETA, -jnp.arange(half, dtype=F32) / half)
    ang = positions.astype(F32).reshape(-1)[:, None] * inv_freq
    cos, sin = jnp.cos(ang), jnp.sin(ang)
    T = ang.shape[0]
    ones = jnp.ones((T, HEAD_DIM - ROT_DIM), F32)
    zeros = jnp.zeros((T, HEAD_DIM - ROT_DIM), F32)
    z8 = jnp.zeros((T, half), F32)
    rc = jnp.concatenate([cos, cos, ones], axis=1)
    rs1 = jnp.concatenate([-sin, z8, zeros], axis=1)
    rs2 = jnp.concatenate([z8, sin, zeros], axis=1)
    rep = LANES // HEAD_DIM
    return jnp.tile(rc, (1, rep)), jnp.tile(rs1, (1, rep)), jnp.tile(rs2, (1, rep))


def _block_plan(counts, T):
    n_blocks = -(-(T * MOE_TOP_K) // MOE_BLOCK) + N_EXPERTS
    blocks_end = jnp.cumsum((counts + MOE_BLOCK - 1) // MOE_BLOCK)
    blk_expert = jnp.minimum(
        jnp.searchsorted(blocks_end, jnp.arange(n_blocks, dtype=jnp.int32), side='right'),
        N_EXPERTS - 1).astype(jnp.int32)
    return n_blocks, blk_expert, blocks_end[-1:].astype(jnp.int32)


def _source_tokens(pos, n_rows, T):
    token_id = jnp.tile(jnp.arange(T, dtype=jnp.int32), MOE_TOP_K)
    return (jnp.arange(n_rows, dtype=jnp.int32) % T).at[pos.reshape(-1)].set(token_id)


def kernel(x, positions, w_in, w_out, g_mix_norm, g_head_norm, g_gmlp_vnorm, w_spatial, b_spatial,
           g_ffn_norm, w_router_group, w_router_expert, w_expert_gate, w_expert_up, w_expert_down,
           g_final):
    B, S, D = x.shape
    T = B * S
    depth = w_in.shape[0]
    tm = min(512, T)
    rc, rs1, rs2 = _rope_tables(positions)
    xc = x.reshape(T, D)
    moe_out = None
    for l in range(depth):
        gh = g_head_norm[l].reshape(1, -1)
        proj, xc = _inproj(xc, moe_out, g_mix_norm[l].reshape(1, D), w_in[l].astype(BF16),
                           rc, rs1, rs2, tm)
        o_sb = _sb_attention(proj, gh[:, :W_SB], B, S)
        o_moba = _moba_attention(proj, gh[:, W_SB:W_SB + W_MOBA], B, S)
        b_exp = jnp.repeat(b_spatial[l].T, HEAD_DIM, axis=1)
        o_gmlp = _gmlp(proj, g_gmlp_vnorm[l].reshape(1, -1), w_spatial[l], b_exp,
                       gh[:, W_SB + W_MOBA:], tm)
        w_r = jnp.concatenate(
            [w_router_group[l].T, jnp.zeros((ROUTER_EXPERT_ROW - N_EXPERT_GROUPS, D), F32),
             w_router_expert[l].T,
             jnp.zeros((ROUTER_ROWS - ROUTER_EXPERT_ROW - N_EXPERTS, D), F32)], axis=0)
        r_hi, r_lo = _split_bf16(w_r)
        xc, h, logits_t = _out_proj(o_sb, o_moba, o_gmlp, xc, w_out[l].astype(BF16),
                                    g_ffn_norm[l].reshape(1, D), r_hi, r_lo, tm)
        pos, gate_w, counts = _router(logits_t, tm)
        n_blocks, blk_expert, n_active = _block_plan(counts[:, 0], T)
        xs = _sc_gather(h, _source_tokens(pos, n_blocks * MOE_BLOCK, T))
        ys = _moe_experts(blk_expert, n_active, xs, w_expert_gate, w_expert_up, w_expert_down, l)
        yg = _sc_gather(ys, pos.reshape(-1)).reshape(MOE_TOP_K, T, D // 2)
        moe_out = (yg, gate_w.T)
    return _final_norm(xc, moe_out[0], moe_out[1], g_final.reshape(1, D), tm).reshape(B, S, D)
```

```python
import functools

import jax
import jax.numpy as jnp
from jax import lax
from jax.experimental import pallas as pl
from jax.experimental.pallas import tpu as pltpu
from jax.experimental.pallas import tpu_sc as plsc

F32 = jnp.float32
BF16 = jnp.bfloat16

HEAD_DIM = 64
LANES = 128
MXU_COLS = 256
N_HEADS_SB = 4
N_HEADS_MOBA = 8
N_GROUPS_GMLP = 4
W_SB = N_HEADS_SB * HEAD_DIM
W_MOBA = N_HEADS_MOBA * HEAD_DIM
W_GMLP = N_GROUPS_GMLP * HEAD_DIM
MOBA_BLOCK = 256
MOBA_TOPK = 3
GMLP_CHUNK = 128
ROPE_THETA = 500000.0
ROT_DIM = HEAD_DIM // 4
N_EXPERT_GROUPS = 4
EXPERTS_PER_GROUP = 8
N_EXPERTS = N_EXPERT_GROUPS * EXPERTS_PER_GROUP
MOE_TOP_K = 2
MOE_BLOCK = 256
NORM_EPS = 1e-6
ATTN_SCALE = HEAD_DIM ** -0.5
NEG = -1e30
LOG2E = 1.4426950408889634
ROUTER_ROWS = 64
VMEM_LIMIT = 48 * 1024 * 1024

C_QA, C_KA, C_VA = 0, W_SB, 2 * W_SB
C_QB = 3 * W_SB
C_KB = C_QB + W_MOBA
C_VB = C_KB + W_MOBA
C_UC = C_VB + W_MOBA
C_VC = C_UC + W_GMLP
IN_COLS = C_VC + W_GMLP


def _nt_dot(a, b):
    return lax.dot_general(a, b, (((1,), (1,)), ((), ())), preferred_element_type=F32)


def _dot(a, b):
    return jnp.dot(a, b, preferred_element_type=F32)


def _rms(x, g):
    return x * lax.rsqrt(jnp.mean(x * x, axis=-1, keepdims=True) + NORM_EPS) * g


def _pack_halves(x):
    w = x.shape[1] // 2
    lo = lax.bitcast_convert_type(x[:, :w].astype(BF16).astype(F32), jnp.uint32)
    hi = lax.bitcast_convert_type(x[:, w:].astype(BF16).astype(F32), jnp.uint32)
    return (lo >> 16) | hi


def _unpack_halves(words):
    lo = lax.bitcast_convert_type(words << 16, F32)
    hi = lax.bitcast_convert_type(words & jnp.uint32(0xFFFF0000), F32)
    return jnp.concatenate([lo, hi], axis=1)


def _split_bf16(x):
    hi = x.astype(BF16)
    lo = (x - hi.astype(F32)).astype(BF16)
    return hi, lo


def _combine(x1_ref, yg_ref, gw_ref):
    gw = gw_ref[...]
    return (x1_ref[...] + _unpack_halves(yg_ref[0]) * gw[:, 0:1]
            + _unpack_halves(yg_ref[1]) * gw[:, 1:2])


def _inproj_kernel(*refs, combine):
    if combine:
        x1_ref, yg_ref, gw_ref, g_ref, w_ref, rc_ref, rs1_ref, rs2_ref, o_ref, x_ref = refs
        x = _combine(x1_ref, yg_ref, gw_ref)
        x_ref[...] = x
    else:
        x_ref, g_ref, w_ref, rc_ref, rs1_ref, rs2_ref, o_ref = refs
        x = x_ref[...]
    y = _rms(x, g_ref[...]).astype(BF16)
    wide = lambda t_ref: jnp.concatenate([t_ref[...]] * (MXU_COLS // LANES), axis=1)
    rc, rs1, rs2 = wide(rc_ref), wide(rs1_ref), wide(rs2_ref)
    half = ROT_DIM // 2
    for c0 in range(0, IN_COLS, MXU_COLS):
        p = _dot(y, w_ref[:, c0:c0 + MXU_COLS])
        if C_QB <= c0 < C_VB:
            p = (p * rc + pltpu.roll(p, MXU_COLS - half, axis=1) * rs1
                 + pltpu.roll(p, half, axis=1) * rs2)
        if c0 < C_KA or C_QB <= c0 < C_KB:
            p = p * (ATTN_SCALE * LOG2E)
        o_ref[:, c0:c0 + MXU_COLS] = p.astype(BF16)


def _inproj(x2d, moe_out, g, w_bf16, rc, rs1, rs2, tm):
    T, D = x2d.shape
    row = lambda i: (i, 0)
    const = lambda i: (0, 0)
    combine = moe_out is not None
    x_specs = [pl.BlockSpec((tm, D), row)]
    x_args = [x2d]
    out_specs = [pl.BlockSpec((tm, IN_COLS), row)]
    out_shape = [jax.ShapeDtypeStruct((T, IN_COLS), BF16)]
    if combine:
        yg, gw = moe_out
        x_specs += [pl.BlockSpec((MOE_TOP_K, tm, D // 2), lambda i: (0, i, 0)),
                    pl.BlockSpec((tm, MOE_TOP_K), row)]
        x_args += [yg, gw]
        out_specs.append(pl.BlockSpec((tm, D), row))
        out_shape.append(jax.ShapeDtypeStruct((T, D), F32))
    outs = pl.pallas_call(
        functools.partial(_inproj_kernel, combine=combine),
        grid=(T // tm,),
        in_specs=x_specs + [pl.BlockSpec((1, D), const), pl.BlockSpec((D, IN_COLS), const),
                            pl.BlockSpec((tm, LANES), row), pl.BlockSpec((tm, LANES), row),
                            pl.BlockSpec((tm, LANES), row)],
        out_specs=out_specs,
        out_shape=out_shape,
        compiler_params=pltpu.CompilerParams(dimension_semantics=("arbitrary",),
                                             vmem_limit_bytes=VMEM_LIMIT),
        name="inproj",
    )(*x_args, g, w_bf16, rc, rs1, rs2)
    return (outs[0], outs[1]) if combine else (outs[0], x2d)


def _sb_kernel(q_ref, k_ref, v_ref, gh_ref, o_ref, *, tq, tk, sub):
    i = pl.program_id(2)
    S = k_ref.shape[0]
    n_heads = LANES // HEAD_DIM
    n_sub = tk // sub
    qt = q_ref[...].astype(F32).T
    zeros_t = jnp.zeros((HEAD_DIM, tq), F32)
    qz = [jnp.concatenate([qt[hh * HEAD_DIM:(hh + 1) * HEAD_DIM] if h2 == hh else zeros_t
                           for h2 in range(n_heads)], axis=0).astype(BF16)
          for hh in range(n_heads)]
    srow = lax.broadcasted_iota(jnp.int32, (sub, sub), 0)
    scol = lax.broadcasted_iota(jnp.int32, (sub, sub), 1)
    later = jnp.where(scol > srow, 1.0, 0.0).astype(BF16)
    krow = lax.broadcasted_iota(jnp.int32, (tk, tq), 0)
    qcol = lax.broadcasted_iota(jnp.int32, (tk, tq), 1)
    last = (i * tq) // tk
    causal = krow - qcol < i * tq - last * tk

    def scores(t):
        return [_dot(k_ref[t * tk:(t + 1) * tk, :], qz[hh]) for hh in range(n_heads)]

    def weights_pv(zs, t, carry, diag):
        wts, runs = [], []
        for hh in range(n_heads):
            z = zs[hh]
            nz = -z
            lr = jnp.minimum(nz, 0.0) - jnp.log2(1.0 + jnp.exp2(jnp.minimum(z, nz)))
            if diag:
                lr = jnp.where(causal, lr, 0.0)
            lrb = lr.astype(BF16)
            tail = carry[hh][0]
            afters = [None] * n_sub
            for sb in reversed(range(n_sub)):
                afters[sb] = _dot(later, lrb[sb * sub:(sb + 1) * sub]) + tail
                tail = afters[sb][0:1] + lr[sb * sub:sb * sub + 1]
            w = jnp.exp2(z + lr + jnp.concatenate(afters, axis=0))
            if diag:
                w = jnp.where(causal, w, 0.0)
            wts.append(w.astype(BF16))
            runs.append(tail)
        new = []
        for hh in range(n_heads):
            pv = lax.dot_general(v_ref[t * tk:(t + 1) * tk, :], wts[hh],
                                 (((0,), (0,)), ((), ())), preferred_element_type=F32)
            new.append((runs[hh], carry[hh][1] + pv[hh * HEAD_DIM:(hh + 1) * HEAD_DIM]))
        return tuple(new)

    def attend(n_past):
        order = list(range(n_past, -1, -1))
        carry = tuple((jnp.zeros((1, tq), F32), jnp.zeros((HEAD_DIM, tq), F32))
                      for _ in range(n_heads))
        zs = scores(order[0])
        for n, t in enumerate(order):
            nxt = scores(order[n + 1]) if n + 1 < len(order) else None
            carry = weights_pv(zs, t, carry, diag=(n == 0))
            zs = nxt
        outs = []
        for hh in range(n_heads):
            o = carry[hh][1]
            outs.append(o * lax.rsqrt(jnp.mean(o * o, axis=0, keepdims=True) + NORM_EPS))
        o_ref[...] = (jnp.concatenate(outs, axis=0).T * gh_ref[...]).astype(BF16)

    for n_past in range(S // tk):
        pl.when(last == n_past)(functools.partial(attend, n_past))


def _sb_attention(proj, gh, B, S):
    tq, sub = 256, 256
    tk = 2 * sub
    assert S % tk == 0
    nq = S // tq
    n_pairs = W_SB // LANES
    return pl.pallas_call(
        functools.partial(_sb_kernel, tq=tq, tk=tk, sub=sub),
        grid=(B, n_pairs, nq),
        in_specs=[pl.BlockSpec((tq, LANES), lambda b, p, i: (b * nq + i, C_QA // LANES + p)),
                  pl.BlockSpec((S, LANES), lambda b, p, i: (b, C_KA // LANES + p)),
                  pl.BlockSpec((S, LANES), lambda b, p, i: (b, C_VA // LANES + p)),
                  pl.BlockSpec((1, LANES), lambda b, p, i: (0, p))],
        out_specs=pl.BlockSpec((tq, LANES), lambda b, p, i: (b * nq + i, p)),
        out_shape=jax.ShapeDtypeStruct((B * S, W_SB), BF16),
        compiler_params=pltpu.CompilerParams(
            dimension_semantics=("arbitrary", "arbitrary", "arbitrary"),
            vmem_limit_bytes=VMEM_LIMIT),
        name="sb_attn",
    )(proj, proj, proj, gh)


def _moba_kernel(q_ref, k_ref, v_ref, gh_ref, o_ref, kaug_ref, vaug_ref, rhi_ref, rlo_ref,
                 *, tq, tk, topk):
    i = pl.program_id(2)
    S = k_ref.shape[0]
    n_heads = LANES // HEAD_DIM

    @pl.when(i == 0)
    def _prepare_keys():
        kp = k_ref[...]
        vp = v_ref[...]
        srow = lax.broadcasted_iota(jnp.int32, (S, HEAD_DIM), 0)
        scol = lax.broadcasted_iota(jnp.int32, (S, HEAD_DIM), 1)
        onehot = jnp.where(srow // MOBA_BLOCK == scol, 1.0, 0.0).astype(BF16)
        arow = lax.broadcasted_iota(jnp.int32, (LANES, S), 0) - HEAD_DIM
        acol = lax.broadcasted_iota(jnp.int32, (LANES, S), 1) // MOBA_BLOCK
        avg = jnp.where(arow == acol, 1.0 / MOBA_BLOCK, 0.0).astype(BF16)
        zeros = jnp.zeros((S, HEAD_DIM), BF16)
        ones = jnp.ones((S, HEAD_DIM), BF16)
        for hh in range(n_heads):
            kh = kp[:, hh * HEAD_DIM:(hh + 1) * HEAD_DIM]
            kaug_ref[hh] = jnp.concatenate([kh, onehot], axis=1)
            vaug_ref[hh] = jnp.concatenate([vp[:, hh * HEAD_DIM:(hh + 1) * HEAD_DIM], ones], axis=1)
            kmean = _dot(avg, jnp.concatenate([kh, zeros], axis=1))
            hi, lo = _split_bf16(kmean)
            rhi_ref[hh] = hi
            rlo_ref[hh] = lo

    qt = q_ref[...].astype(F32).T
    group = 8
    bidx = lax.broadcasted_iota(jnp.int32, (group, tq), 0)
    zeros_t = jnp.zeros((HEAD_DIM, tq), F32)
    qaugs = []
    for hh in range(n_heads):
        qh = qt[hh * HEAD_DIM:(hh + 1) * HEAD_DIM]
        qz = jnp.concatenate([qh, zeros_t], axis=0).astype(BF16)
        gate = (_dot(rhi_ref[hh], qz) + _dot(rlo_ref[hh], qz))[HEAD_DIM:HEAD_DIM + group]
        valid = bidx < i
        gm = jnp.where(valid, gate, NEG)
        rank = jnp.zeros((group, tq), F32)
        for d in range(1, group):
            nb = pltpu.roll(gm, d, axis=0)
            rank = rank + jnp.where(bidx >= d, jnp.where(nb >= gm, 1.0, 0.0),
                                    jnp.where(nb > gm, 1.0, 0.0))
        allowed = (valid & (rank < topk)) | (bidx == i)
        bias = jnp.where(allowed, 0.0, NEG)
        qaugs.append(jnp.concatenate(
            [qh, bias, jnp.zeros((HEAD_DIM - group, tq), F32)], axis=0).astype(BF16))

    krow = lax.broadcasted_iota(jnp.int32, (tk, tq), 0)
    qcol = lax.broadcasted_iota(jnp.int32, (tk, tq), 1)
    last = (i * tq) // tk
    causal = krow - qcol <= i * tq - last * tk

    def scores(t):
        return [_dot(kaug_ref[hh, t * tk:(t + 1) * tk, :], qaugs[hh]) for hh in range(n_heads)]

    def softmax_pv(sts, t, carry, diag):
        pts, stats = [], []
        for hh in range(n_heads):
            m, l, acc = carry[hh]
            st = jnp.where(causal, sts[hh], NEG) if diag else sts[hh]
            m_new = jnp.maximum(m, jnp.max(st, axis=0, keepdims=True))
            pts.append(jnp.exp2(st - m_new).astype(BF16))
            stats.append((m_new, jnp.exp2(m - m_new)))
        new = []
        for hh in range(n_heads):
            _, l, acc = carry[hh]
            m_new, alpha = stats[hh]
            pv = lax.dot_general(vaug_ref[hh, t * tk:(t + 1) * tk, :], pts[hh],
                                 (((0,), (0,)), ((), ())), preferred_element_type=F32)
            new.append((m_new, alpha * l + pv[HEAD_DIM:HEAD_DIM + 1],
                        alpha * acc + pv[:HEAD_DIM]))
        return tuple(new)

    def attend(n_past):
        order = [n_past] + list(range(n_past))
        carry = tuple((jnp.full((1, tq), NEG, F32), jnp.zeros((1, tq), F32),
                       jnp.zeros((HEAD_DIM, tq), F32)) for _ in range(n_heads))
        sts = scores(order[0])
        for n, t in enumerate(order):
            nxt = scores(order[n + 1]) if n + 1 < len(order) else None
            carry = softmax_pv(sts, t, carry, diag=(n == 0))
            sts = nxt
        outs = []
        for hh in range(n_heads):
            _, l, acc = carry[hh]
            o = acc / l
            outs.append(o * lax.rsqrt(jnp.mean(o * o, axis=0, keepdims=True) + NORM_EPS))
        o_ref[...] = (jnp.concatenate(outs, axis=0).T * gh_ref[...]).astype(BF16)

    for n_past in range(S // tk):
        pl.when(last == n_past)(functools.partial(attend, n_past))


def _moba_attention(proj, gh, B, S):
    tq = MOBA_BLOCK
    tk = 2 * MOBA_BLOCK
    assert S % tk == 0
    nq = S // tq
    n_blk = S // MOBA_BLOCK
    assert n_blk <= 8
    topk = min(MOBA_TOPK, max(n_blk - 1, 1))
    n_pairs = W_MOBA // LANES
    n_heads = LANES // HEAD_DIM
    return pl.pallas_call(
        functools.partial(_moba_kernel, tq=tq, tk=tk, topk=topk),
        grid=(B, n_pairs, nq),
        in_specs=[pl.BlockSpec((tq, LANES), lambda b, p, i: (b * nq + i, C_QB // LANES + p)),
                  pl.BlockSpec((S, LANES), lambda b, p, i: (b, C_KB // LANES + p)),
                  pl.BlockSpec((S, LANES), lambda b, p, i: (b, C_VB // LANES + p)),
                  pl.BlockSpec((1, LANES), lambda b, p, i: (0, p))],
        out_specs=pl.BlockSpec((tq, LANES), lambda b, p, i: (b * nq + i, p)),
        out_shape=jax.ShapeDtypeStruct((B * S, W_MOBA), BF16),
        scratch_shapes=[pltpu.VMEM((n_heads, S, LANES), BF16),
                        pltpu.VMEM((n_heads, S, LANES), BF16),
                        pltpu.VMEM((n_heads, LANES, LANES), BF16),
                        pltpu.VMEM((n_heads, LANES, LANES), BF16)],
        compiler_params=pltpu.CompilerParams(
            dimension_semantics=("arbitrary", "arbitrary", "arbitrary"),
            vmem_limit_bytes=VMEM_LIMIT),
        name="moba_attn",
    )(proj, proj, proj, gh)


def _gmlp_kernel(u_ref, v_ref, gv_ref, ws_ref, b_ref, gh_ref, o_ref, *, tm):
    gu = jax.nn.gelu(u_ref[...].astype(F32))
    gv = jax.nn.gelu(v_ref[...].astype(F32))
    row = lax.broadcasted_iota(jnp.int32, (GMLP_CHUNK, GMLP_CHUNK), 0)
    col = lax.broadcasted_iota(jnp.int32, (GMLP_CHUNK, GMLP_CHUNK), 1)
    outs = []
    for g in range(N_GROUPS_GMLP):
        c0, c1 = g * HEAD_DIM, (g + 1) * HEAD_DIM
        vn = _rms(gv[:, c0:c1], gv_ref[:, c0:c1]).astype(BF16)
        wm = jnp.where(col <= row, ws_ref[g], 0.0).astype(BF16)
        bias = b_ref[:, c0:c1]
        mixed = jnp.concatenate(
            [_dot(wm, vn[c * GMLP_CHUNK:(c + 1) * GMLP_CHUNK]) + bias
             for c in range(tm // GMLP_CHUNK)], axis=0)
        outs.append(_rms(gu[:, c0:c1] * mixed, gh_ref[:, c0:c1]))
    o_ref[...] = jnp.concatenate(outs, axis=1).astype(BF16)


def _gmlp(proj, gv, ws, b_exp, gh, tm):
    T = proj.shape[0]
    nu, nv = C_UC // W_GMLP, C_VC // W_GMLP
    return pl.pallas_call(
        functools.partial(_gmlp_kernel, tm=tm),
        grid=(T // tm,),
        in_specs=[pl.BlockSpec((tm, W_GMLP), lambda i: (i, nu)),
                  pl.BlockSpec((tm, W_GMLP), lambda i: (i, nv)),
                  pl.BlockSpec((1, W_GMLP), lambda i: (0, 0)),
                  pl.BlockSpec((N_GROUPS_GMLP, GMLP_CHUNK, GMLP_CHUNK), lambda i: (0, 0, 0)),
                  pl.BlockSpec((GMLP_CHUNK, W_GMLP), lambda i: (0, 0)),
                  pl.BlockSpec((1, W_GMLP), lambda i: (0, 0))],
        out_specs=pl.BlockSpec((tm, W_GMLP), lambda i: (i, 0)),
        out_shape=jax.ShapeDtypeStruct((T, W_GMLP), BF16),
        compiler_params=pltpu.CompilerParams(dimension_semantics=("arbitrary",),
                                             vmem_limit_bytes=VMEM_LIMIT),
        name="gmlp",
    )(proj, proj, gv, ws, b_exp, gh)


def _out_kernel(osb_ref, omoba_ref, ogmlp_ref, x_ref, w_ref, g_ref, rhi_ref, rlo_ref,
                x1_ref, h_ref, lg_ref):
    x1 = (x_ref[...]
          + _dot(osb_ref[...], w_ref[0:W_SB, :])
          + _dot(omoba_ref[...], w_ref[W_SB:W_SB + W_MOBA, :])
          + _dot(ogmlp_ref[...], w_ref[W_SB + W_MOBA:, :]))
    x1_ref[...] = x1
    hn = _rms(x1, g_ref[...])
    hi, lo = _split_bf16(hn)
    h_ref[...] = _pack_halves(hn)
    lg_ref[...] = (_nt_dot(rhi_ref[...], hi) + _nt_dot(rhi_ref[...], lo)
                   + _nt_dot(rlo_ref[...], hi))


def _out_proj(o_sb, o_moba, o_gmlp, x2d, w_bf16, g, r_hi, r_lo, tm):
    T, D = x2d.shape
    row = lambda i: (i, 0)
    const = lambda i: (0, 0)
    return pl.pallas_call(
        _out_kernel,
        grid=(T // tm,),
        in_specs=[pl.BlockSpec((tm, W_SB), row), pl.BlockSpec((tm, W_MOBA), row),
                  pl.BlockSpec((tm, W_GMLP), row), pl.BlockSpec((tm, D), row),
                  pl.BlockSpec((D, D), const), pl.BlockSpec((1, D), const),
                  pl.BlockSpec((ROUTER_ROWS, D), const), pl.BlockSpec((ROUTER_ROWS, D), const)],
        out_specs=[pl.BlockSpec((tm, D), row), pl.BlockSpec((tm, D // 2), row),
                   pl.BlockSpec((ROUTER_ROWS, tm), lambda i: (0, i))],
        out_shape=[jax.ShapeDtypeStruct((T, D), F32), jax.ShapeDtypeStruct((T, D // 2), jnp.uint32),
                   jax.ShapeDtypeStruct((ROUTER_ROWS, T), F32)],
        compiler_params=pltpu.CompilerParams(dimension_semantics=("arbitrary",),
                                             vmem_limit_bytes=VMEM_LIMIT),
        name="outproj",
    )(o_sb, o_moba, o_gmlp, x2d, w_bf16, g, r_hi, r_lo)


ROUTER_EXPERT_ROW = 8


def _first_max(p):
    rows = lax.broadcasted_iota(jnp.int32, p.shape, 0).astype(F32)
    top = jnp.max(p, axis=0, keepdims=True)
    idx = jnp.min(jnp.where(p == top, rows, float(p.shape[0])), axis=0, keepdims=True)
    return top, idx, rows


def _router_kernel(lg_ref, pos_ref, gw_ref, cnt_ref, cnt_acc, base_ref, *, tm):
    phase = pl.program_id(0)
    i = pl.program_id(1)
    lg = lg_ref[...]
    gl = lg[0:N_EXPERT_GROUPS]
    ge = jnp.exp(gl - jnp.max(gl, axis=0, keepdims=True))
    p_group = ge / jnp.sum(ge, axis=0, keepdims=True)
    p_g, g_sel, _ = _first_max(p_group)
    le = jnp.zeros((EXPERTS_PER_GROUP, tm), F32)
    for g in range(N_EXPERT_GROUPS):
        r0 = ROUTER_EXPERT_ROW + g * EXPERTS_PER_GROUP
        le = jnp.where(g_sel == g, lg[r0:r0 + EXPERTS_PER_GROUP], le)
    ee = jnp.exp(le - jnp.max(le, axis=0, keepdims=True))
    p = ee / jnp.sum(ee, axis=0, keepdims=True)
    p0, i0, rows = _first_max(p)
    p1, i1, _ = _first_max(jnp.where(rows == i0, -1.0, p))
    e0 = g_sel * EXPERTS_PER_GROUP + i0
    e1 = g_sel * EXPERTS_PER_GROUP + i1
    xrow = lax.broadcasted_iota(jnp.int32, (N_EXPERTS, tm), 0).astype(F32)
    oh0 = xrow == e0
    oh1 = xrow == e1
    slots = jnp.where(oh0, 1.0, 0.0) + jnp.where(oh1, 1.0, 0.0)
    tile_cnt = slots[:, 0:LANES]
    for c in range(1, tm // LANES):
        tile_cnt = tile_cnt + slots[:, c * LANES:(c + 1) * LANES]

    @pl.when((phase == 0) & (i == 0))
    def _init():
        cnt_acc[...] = jnp.zeros_like(cnt_acc)

    @pl.when(phase == 0)
    def _count():
        cnt_acc[...] += tile_cnt

    @pl.when((phase == 1) & (i == 0))
    def _starts():
        counts = jnp.sum(cnt_acc[...], axis=1, keepdims=True)
        n_blk = jnp.floor((counts + (MOE_BLOCK - 1)) * (1.0 / MOE_BLOCK))
        er = lax.broadcasted_iota(jnp.int32, (N_EXPERTS, N_EXPERTS), 0)
        ec = lax.broadcasted_iota(jnp.int32, (N_EXPERTS, N_EXPERTS), 1)
        before = jnp.where(ec < er, 1.0, 0.0).astype(BF16)
        start_blk = _dot(before, jnp.broadcast_to(n_blk, (N_EXPERTS, LANES)).astype(BF16))
        base_ref[...] = start_blk * MOE_BLOCK
        cnt_ref[...] = jnp.broadcast_to(counts, (N_EXPERTS, LANES)).astype(jnp.int32)

    @pl.when(phase == 1)
    def _assign():
        tr = lax.broadcasted_iota(jnp.int32, (tm, tm), 0)
        tc = lax.broadcasted_iota(jnp.int32, (tm, tm), 1)
        earlier = jnp.where(tr < tc, 1.0, 0.0).astype(BF16)
        row_of = _dot(slots.astype(BF16), earlier) + base_ref[:, 0:1]
        pos0 = jnp.sum(jnp.where(oh0, row_of, 0.0), axis=0, keepdims=True)
        pos1 = jnp.sum(jnp.where(oh1, row_of, 0.0), axis=0, keepdims=True)
        pos_ref[...] = jnp.concatenate([pos0, pos1], axis=0).astype(jnp.int32)
        scale = p_g / (p0 + p1)
        gw_ref[...] = jnp.concatenate([p0 * scale, p1 * scale], axis=0)
        base_ref[...] += jnp.sum(tile_cnt, axis=1, keepdims=True)


def _router(logits_t, tm):
    T = logits_t.shape[1]
    tok = lambda p, i: (0, i * p)
    return pl.pallas_call(
        functools.partial(_router_kernel, tm=tm),
        grid=(2, T // tm),
        in_specs=[pl.BlockSpec((ROUTER_ROWS, tm), lambda p, i: (0, i))],
        out_specs=[pl.BlockSpec((MOE_TOP_K, tm), tok), pl.BlockSpec((MOE_TOP_K, tm), tok),
                   pl.BlockSpec((N_EXPERTS, LANES), lambda p, i: (0, 0))],
        out_shape=[jax.ShapeDtypeStruct((MOE_TOP_K, T), jnp.int32),
                   jax.ShapeDtypeStruct((MOE_TOP_K, T), F32),
                   jax.ShapeDtypeStruct((N_EXPERTS, LANES), jnp.int32)],
        scratch_shapes=[pltpu.VMEM((N_EXPERTS, LANES), F32), pltpu.VMEM((N_EXPERTS, LANES), F32)],
        compiler_params=pltpu.CompilerParams(dimension_semantics=("arbitrary", "arbitrary")),
        name="router",
    )(logits_t)


def _moe_kernel(be_ref, first_ref, slot_ref, next_ref, na_ref, xs_ref, wg_hbm, wu_hbm, wd_hbm,
                ys_ref, wgf, wuf, wdf, wgb, wub, wdb, sem, *, layer):
    b = pl.program_id(0)
    e = be_ref[b]
    slot = slot_ref[b]
    active = b < na_ref[0]

    def fetch(expert, s):
        return (pltpu.make_async_copy(wg_hbm.at[layer, expert], wgf.at[s], sem.at[s, 0]),
                pltpu.make_async_copy(wu_hbm.at[layer, expert], wuf.at[s], sem.at[s, 1]),
                pltpu.make_async_copy(wd_hbm.at[layer, expert], wdf.at[s], sem.at[s, 2]))

    @pl.when(b == 0)
    def _first_fetch():
        for c in fetch(e, slot):
            c.start()

    @pl.when(active & (first_ref[b] == 1))
    def _load_expert():
        for c in fetch(e, slot):
            c.wait()
        wgb[...] = wgf[slot].astype(BF16)
        wub[...] = wuf[slot].astype(BF16)
        wdb[...] = wdf[slot].astype(BF16)
        nxt = next_ref[b]

        @pl.when(nxt >= 0)
        def _prefetch():
            for c in fetch(nxt, 1 - slot):
                c.start()

    @pl.when(active)
    def _compute():
        xb = _unpack_halves(xs_ref[...]).astype(BF16)
        a = jax.nn.silu(_dot(xb, wgb[...])) * _dot(xb, wub[...])
        ys_ref[...] = _pack_halves(_dot(a.astype(BF16), wdb[...]))

    @pl.when(jnp.logical_not(active))
    def _pad():
        ys_ref[...] = jnp.zeros_like(ys_ref)


def _moe_experts(plan, xs, w_gate, w_up, w_down, layer):
    n_rows = xs.shape[0]
    D = 2 * xs.shape[1]
    n_blocks = n_rows // MOE_BLOCK
    DE = w_gate.shape[-1]
    rows = lambda b, *_: (b, 0)
    grid_spec = pltpu.PrefetchScalarGridSpec(
        num_scalar_prefetch=len(plan),
        grid=(n_blocks,),
        in_specs=[pl.BlockSpec((MOE_BLOCK, D // 2), rows),
                  pl.BlockSpec(memory_space=pl.ANY), pl.BlockSpec(memory_space=pl.ANY),
                  pl.BlockSpec(memory_space=pl.ANY)],
        out_specs=pl.BlockSpec((MOE_BLOCK, D // 2), rows),
        scratch_shapes=[pltpu.VMEM((2, D, DE), F32), pltpu.VMEM((2, D, DE), F32),
                        pltpu.VMEM((2, DE, D), F32),
                        pltpu.VMEM((D, DE), BF16), pltpu.VMEM((D, DE), BF16),
                        pltpu.VMEM((DE, D), BF16),
                        pltpu.SemaphoreType.DMA((2, 3))])
    return pl.pallas_call(
        functools.partial(_moe_kernel, layer=layer),
        grid_spec=grid_spec,
        out_shape=jax.ShapeDtypeStruct((n_rows, D // 2), jnp.uint32),
        compiler_params=pltpu.CompilerParams(dimension_semantics=("arbitrary",),
                                             vmem_limit_bytes=VMEM_LIMIT),
        name="moe_experts",
    )(*plan, xs, w_gate, w_up, w_down)


def _final_kernel(x1_ref, yg_ref, gw_ref, g_ref, o_ref):
    o_ref[...] = _rms(_combine(x1_ref, yg_ref, gw_ref), g_ref[...])


def _final_norm(x2d, yg, gw, g, tm):
    T, D = x2d.shape
    row = lambda i: (i, 0)
    return pl.pallas_call(
        _final_kernel,
        grid=(T // tm,),
        in_specs=[pl.BlockSpec((tm, D), row),
                  pl.BlockSpec((MOE_TOP_K, tm, D // 2), lambda i: (0, i, 0)),
                  pl.BlockSpec((tm, MOE_TOP_K), row), pl.BlockSpec((1, D), lambda i: (0, 0))],
        out_specs=pl.BlockSpec((tm, D), row),
        out_shape=jax.ShapeDtypeStruct((T, D), F32),
        compiler_params=pltpu.CompilerParams(dimension_semantics=("arbitrary",)),
        name="final_norm",
    )(x2d, yg, gw, g)


SC_GATHER_ROWS = 64


def _sc_gather(table, idx):
    info = plsc.get_sparse_core_info()
    n_cores, n_workers = info.num_cores, info.num_cores * info.num_subcores
    N, W = idx.shape[0], table.shape[1]
    per_w = N // n_workers
    n_ch = per_w // SC_GATHER_ROWS
    assert per_w * n_workers == N and n_ch * SC_GATHER_ROWS == per_w
    mesh = plsc.VectorSubcoreMesh(core_axis_name="c", subcore_axis_name="s")

    @functools.partial(
        pl.kernel, mesh=mesh,
        out_type=jax.ShapeDtypeStruct((N, W), table.dtype),
        scratch_types=[pltpu.VMEM((n_ch, SC_GATHER_ROWS), jnp.int32),
                       pltpu.VMEM((2, SC_GATHER_ROWS, W), table.dtype),
                       pltpu.SemaphoreType.DMA((2,)),
                       pltpu.SemaphoreType.DMA((2,))])
    def gather_kernel(table_hbm, idx_hbm, out_hbm, idx_v, rows_v, gsem, ssem):
        wid = lax.axis_index("s") * n_cores + lax.axis_index("c")
        base = wid * per_w
        pltpu.sync_copy(idx_hbm.at[wid], idx_v)

        def fetch(c):
            return pltpu.make_async_copy(table_hbm.at[idx_v.at[c]], rows_v.at[c % 2],
                                         gsem.at[c % 2])

        def write(c):
            return pltpu.make_async_copy(
                rows_v.at[c % 2], out_hbm.at[pl.ds(base + c * SC_GATHER_ROWS, SC_GATHER_ROWS)],
                ssem.at[c % 2])

        fetch(0).start()
        for c in range(n_ch):
            if c + 1 < n_ch:
                if c >= 1:
                    write(c - 1).wait()
                fetch(c + 1).start()
            fetch(c).wait()
            write(c).start()
        if n_ch >= 2:
            write(n_ch - 2).wait()
        write(n_ch - 1).wait()

    return gather_kernel(table, idx.reshape(n_workers, n_ch, SC_GATHER_ROWS))


def _rope_tables(positions):
    half = ROT_DIM // 2
    inv_freq = jnp.power(ROPE_THETA, -jnp.arange(half, dtype=F32) / half)
    ang = positions.astype(F32).reshape(-1)[:, None] * inv_freq
    cos, sin = jnp.cos(ang), jnp.sin(ang)
    T = ang.shape[0]
    ones = jnp.ones((T, HEAD_DIM - ROT_DIM), F32)
    zeros = jnp.zeros((T, HEAD_DIM - ROT_DIM), F32)
    z8 = jnp.zeros((T, half), F32)
    rc = jnp.concatenate([cos, cos, ones], axis=1)
    rs1 = jnp.concatenate([-sin, z8, zeros], axis=1)
    rs2 = jnp.concatenate([z8, sin, zeros], axis=1)
    rep = LANES // HEAD_DIM
    return jnp.tile(rc, (1, rep)), jnp.tile(rs1, (1, rep)), jnp.tile(rs2, (1, rep))


def _block_plan(counts, T):
    i32 = jnp.int32
    n_blocks = -(-(T * MOE_TOP_K) // MOE_BLOCK) + N_EXPERTS
    experts = jnp.arange(N_EXPERTS, dtype=i32)
    blocks = jnp.arange(n_blocks, dtype=i32)
    n_blk = (counts + MOE_BLOCK - 1) // MOE_BLOCK
    blocks_end = jnp.cumsum(n_blk)
    blk_expert = jnp.minimum(jnp.sum(blocks_end[None, :] <= blocks[:, None], axis=1),
                             N_EXPERTS - 1).astype(i32)
    is_active = n_blk > 0
    ordinal = jnp.cumsum(is_active.astype(i32)) - 1
    later_active = is_active[None, :] & (experts[None, :] > experts[:, None])
    next_active = jnp.min(jnp.where(later_active, experts[None, :], N_EXPERTS), axis=1)
    next_active = jnp.where(next_active < N_EXPERTS, next_active, -1).astype(i32)
    blk_first = (blocks == (blocks_end - n_blk)[blk_expert]).astype(i32)
    plan = (blk_expert, blk_first, (ordinal[blk_expert] % 2).astype(i32),
            next_active[blk_expert], blocks_end[-1:].astype(i32))
    return n_blocks, plan


def _source_tokens(pos, n_rows, T):
    token_id = jnp.tile(jnp.arange(T, dtype=jnp.int32), MOE_TOP_K)
    return (jnp.arange(n_rows, dtype=jnp.int32) % T).at[pos.reshape(-1)].set(token_id)


def kernel(x, positions, w_in, w_out, g_mix_norm, g_head_norm, g_gmlp_vnorm, w_spatial, b_spatial,
           g_ffn_norm, w_router_group, w_router_expert, w_expert_gate, w_expert_up, w_expert_down,
           g_final):
    B, S, D = x.shape
    T = B * S
    depth = w_in.shape[0]
    tm = min(512, T)
    rc, rs1, rs2 = _rope_tables(positions)
    xc = x.reshape(T, D)
    moe_out = None
    for l in range(depth):
        gh = g_head_norm[l].reshape(1, -1)
        proj, xc = _inproj(xc, moe_out, g_mix_norm[l].reshape(1, D), w_in[l].astype(BF16),
                           rc, rs1, rs2, tm)
        o_sb = _sb_attention(proj, gh[:, :W_SB], B, S)
        o_moba = _moba_attention(proj, gh[:, W_SB:W_SB + W_MOBA], B, S)
        b_exp = jnp.repeat(b_spatial[l].T, HEAD_DIM, axis=1)
        o_gmlp = _gmlp(proj, g_gmlp_vnorm[l].reshape(1, -1), w_spatial[l], b_exp,
                       gh[:, W_SB + W_MOBA:], tm)
        w_r = jnp.concatenate(
            [w_router_group[l].T, jnp.zeros((ROUTER_EXPERT_ROW - N_EXPERT_GROUPS, D), F32),
             w_router_expert[l].T,
             jnp.zeros((ROUTER_ROWS - ROUTER_EXPERT_ROW - N_EXPERTS, D), F32)], axis=0)
        r_hi, r_lo = _split_bf16(w_r)
        xc, h, logits_t = _out_proj(o_sb, o_moba, o_gmlp, xc, w_out[l].astype(BF16),
                                    g_ffn_norm[l].reshape(1, D), r_hi, r_lo, tm)
        pos, gate_w, counts = _router(logits_t, tm)
        n_blocks, plan = _block_plan(counts[:, 0], T)
        xs = _sc_gather(h, _source_tokens(pos, n_blocks * MOE_BLOCK, T))
        ys = _moe_experts(plan, xs, w_expert_gate, w_expert_up, w_expert_down, l)
        yg = _sc_gather(ys, pos.reshape(-1)).reshape(MOE_TOP_K, T, D // 2)
        moe_out = (yg, gate_w.T)
    return _final_norm(xc, moe_out[0], moe_out[1], g_final.reshape(1, D), tm).reshape(B, S, D)
```

```python
import functools

import jax
import jax.numpy as jnp
from jax import lax
from jax.experimental import pallas as pl
from jax.experimental.pallas import tpu as pltpu
from jax.experimental.pallas import tpu_sc as plsc

F32 = jnp.float32
BF16 = jnp.bfloat16

HEAD_DIM = 64
LANES = 128
MXU_COLS = 256
N_HEADS_SB = 4
N_HEADS_MOBA = 8
N_GROUPS_GMLP = 4
W_SB = N_HEADS_SB * HEAD_DIM
W_MOBA = N_HEADS_MOBA * HEAD_DIM
W_GMLP = N_GROUPS_GMLP * HEAD_DIM
MOBA_BLOCK = 256
MOBA_TOPK = 3
GMLP_CHUNK = 128
ROPE_THETA = 500000.0
ROT_DIM = HEAD_DIM // 4
N_EXPERT_GROUPS = 4
EXPERTS_PER_GROUP = 8
N_EXPERTS = N_EXPERT_GROUPS * EXPERTS_PER_GROUP
MOE_TOP_K = 2
MOE_BLOCK = 256
NORM_EPS = 1e-6
ATTN_SCALE = HEAD_DIM ** -0.5
NEG = -1e30
LOG2E = 1.4426950408889634
ROUTER_ROWS = 64
VMEM_LIMIT = 48 * 1024 * 1024

C_QA, C_KA, C_VA = 0, W_SB, 2 * W_SB
C_QB = 3 * W_SB
C_KB = C_QB + W_MOBA
C_VB = C_KB + W_MOBA
C_UC = C_VB + W_MOBA
C_VC = C_UC + W_GMLP
IN_COLS = C_VC + W_GMLP


def _nt_dot(a, b):
    return lax.dot_general(a, b, (((1,), (1,)), ((), ())), preferred_element_type=F32)


def _dot(a, b):
    return jnp.dot(a, b, preferred_element_type=F32)


def _rms(x, g):
    return x * lax.rsqrt(jnp.mean(x * x, axis=-1, keepdims=True) + NORM_EPS) * g


def _pack_halves(x):
    w = x.shape[1] // 2
    lo = lax.bitcast_convert_type(x[:, :w].astype(BF16).astype(F32), jnp.uint32)
    hi = lax.bitcast_convert_type(x[:, w:].astype(BF16).astype(F32), jnp.uint32)
    return (lo >> 16) | hi


def _unpack_halves(words):
    lo = lax.bitcast_convert_type(words << 16, F32)
    hi = lax.bitcast_convert_type(words & jnp.uint32(0xFFFF0000), F32)
    return jnp.concatenate([lo, hi], axis=1)


def _split_bf16(x):
    hi = x.astype(BF16)
    lo = (x - hi.astype(F32)).astype(BF16)
    return hi, lo


def _combine(x1_ref, yg_ref, gw_ref):
    gw = gw_ref[...]
    return (x1_ref[...] + _unpack_halves(yg_ref[0]) * gw[:, 0:1]
            + _unpack_halves(yg_ref[1]) * gw[:, 1:2])


def _inproj_kernel(*refs, combine):
    if combine:
        x1_ref, yg_ref, gw_ref, g_ref, w_ref, rc_ref, rs1_ref, rs2_ref, o_ref, x_ref = refs
        x = _combine(x1_ref, yg_ref, gw_ref)
        x_ref[...] = x
    else:
        x_ref, g_ref, w_ref, rc_ref, rs1_ref, rs2_ref, o_ref = refs
        x = x_ref[...]
    y = _rms(x, g_ref[...]).astype(BF16)
    wide = lambda t_ref: jnp.concatenate([t_ref[...]] * (MXU_COLS // LANES), axis=1)
    rc, rs1, rs2 = wide(rc_ref), wide(rs1_ref), wide(rs2_ref)
    half = ROT_DIM // 2
    for c0 in range(0, IN_COLS, MXU_COLS):
        p = _dot(y, w_ref[:, c0:c0 + MXU_COLS])
        if C_QB <= c0 < C_VB:
            p = (p * rc + pltpu.roll(p, MXU_COLS - half, axis=1) * rs1
                 + pltpu.roll(p, half, axis=1) * rs2)
        if c0 < C_KA or C_QB <= c0 < C_KB:
            p = p * (ATTN_SCALE * LOG2E)
        o_ref[:, c0:c0 + MXU_COLS] = p.astype(BF16)


def _inproj(x2d, moe_out, g, w_bf16, rc, rs1, rs2, tm):
    T, D = x2d.shape
    row = lambda i: (i, 0)
    const = lambda i: (0, 0)
    combine = moe_out is not None
    x_specs = [pl.BlockSpec((tm, D), row)]
    x_args = [x2d]
    out_specs = [pl.BlockSpec((tm, IN_COLS), row)]
    out_shape = [jax.ShapeDtypeStruct((T, IN_COLS), BF16)]
    if combine:
        yg, gw = moe_out
        x_specs += [pl.BlockSpec((MOE_TOP_K, tm, D // 2), lambda i: (0, i, 0)),
                    pl.BlockSpec((tm, MOE_TOP_K), row)]
        x_args += [yg, gw]
        out_specs.append(pl.BlockSpec((tm, D), row))
        out_shape.append(jax.ShapeDtypeStruct((T, D), F32))
    outs = pl.pallas_call(
        functools.partial(_inproj_kernel, combine=combine),
        grid=(T // tm,),
        in_specs=x_specs + [pl.BlockSpec((1, D), const), pl.BlockSpec((D, IN_COLS), const),
                            pl.BlockSpec((tm, LANES), row), pl.BlockSpec((tm, LANES), row),
                            pl.BlockSpec((tm, LANES), row)],
        out_specs=out_specs,
        out_shape=out_shape,
        compiler_params=pltpu.CompilerParams(dimension_semantics=("arbitrary",),
                                             vmem_limit_bytes=VMEM_LIMIT),
        name="inproj",
    )(*x_args, g, w_bf16, rc, rs1, rs2)
    return (outs[0], outs[1]) if combine else (outs[0], x2d)


def _sb_kernel(q_ref, k_ref, v_ref, gh_ref, o_ref, *, tq, tk, sub):
    i = pl.program_id(2)
    S = k_ref.shape[0]
    n_heads = LANES // HEAD_DIM
    n_sub = tk // sub
    qt = q_ref[...].astype(F32).T
    zeros_t = jnp.zeros((HEAD_DIM, tq), F32)
    qz = [jnp.concatenate([qt[hh * HEAD_DIM:(hh + 1) * HEAD_DIM] if h2 == hh else zeros_t
                           for h2 in range(n_heads)], axis=0).astype(BF16)
          for hh in range(n_heads)]
    srow = lax.broadcasted_iota(jnp.int32, (sub, sub), 0)
    scol = lax.broadcasted_iota(jnp.int32, (sub, sub), 1)
    later = jnp.where(scol > srow, 1.0, 0.0).astype(BF16)
    krow = lax.broadcasted_iota(jnp.int32, (tk, tq), 0)
    qcol = lax.broadcasted_iota(jnp.int32, (tk, tq), 1)
    last = (i * tq) // tk
    causal = krow - qcol < i * tq - last * tk

    def scores(t):
        return [_dot(k_ref[t * tk:(t + 1) * tk, :], qz[hh]) for hh in range(n_heads)]

    def weights_pv(zs, t, carry, diag):
        wts, runs = [], []
        for hh in range(n_heads):
            z = zs[hh]
            nz = -z
            lr = jnp.minimum(nz, 0.0) - jnp.log2(1.0 + jnp.exp2(jnp.minimum(z, nz)))
            if diag:
                lr = jnp.where(causal, lr, 0.0)
            lrb = lr.astype(BF16)
            tail = carry[hh][0]
            afters = [None] * n_sub
            for sb in reversed(range(n_sub)):
                afters[sb] = _dot(later, lrb[sb * sub:(sb + 1) * sub]) + tail
                tail = afters[sb][0:1] + lr[sb * sub:sb * sub + 1]
            w = jnp.exp2(z + lr + jnp.concatenate(afters, axis=0))
            if diag:
                w = jnp.where(causal, w, 0.0)
            wts.append(w.astype(BF16))
            runs.append(tail)
        new = []
        for hh in range(n_heads):
            pv = lax.dot_general(v_ref[t * tk:(t + 1) * tk, :], wts[hh],
                                 (((0,), (0,)), ((), ())), preferred_element_type=F32)
            new.append((runs[hh], carry[hh][1] + pv[hh * HEAD_DIM:(hh + 1) * HEAD_DIM]))
        return tuple(new)

    def attend(n_past):
        order = list(range(n_past, -1, -1))
        carry = tuple((jnp.zeros((1, tq), F32), jnp.zeros((HEAD_DIM, tq), F32))
                      for _ in range(n_heads))
        zs = scores(order[0])
        for n, t in enumerate(order):
            nxt = scores(order[n + 1]) if n + 1 < len(order) else None
            carry = weights_pv(zs, t, carry, diag=(n == 0))
            zs = nxt
        outs = []
        for hh in range(n_heads):
            o = carry[hh][1]
            outs.append(o * lax.rsqrt(jnp.mean(o * o, axis=0, keepdims=True) + NORM_EPS))
        o_ref[...] = (jnp.concatenate(outs, axis=0).T * gh_ref[...]).astype(BF16)

    for n_past in range(S // tk):
        pl.when(last == n_past)(functools.partial(attend, n_past))


def _sb_attention(proj, gh, B, S):
    tq, sub = 256, 256
    tk = 2 * sub
    assert S % tk == 0
    nq = S // tq
    n_pairs = W_SB // LANES
    return pl.pallas_call(
        functools.partial(_sb_kernel, tq=tq, tk=tk, sub=sub),
        grid=(B, n_pairs, nq),
        in_specs=[pl.BlockSpec((tq, LANES), lambda b, p, i: (b * nq + i, C_QA // LANES + p)),
                  pl.BlockSpec((S, LANES), lambda b, p, i: (b, C_KA // LANES + p)),
                  pl.BlockSpec((S, LANES), lambda b, p, i: (b, C_VA // LANES + p)),
                  pl.BlockSpec((1, LANES), lambda b, p, i: (0, p))],
        out_specs=pl.BlockSpec((tq, LANES), lambda b, p, i: (b * nq + i, p)),
        out_shape=jax.ShapeDtypeStruct((B * S, W_SB), BF16),
        compiler_params=pltpu.CompilerParams(
            dimension_semantics=("arbitrary", "arbitrary", "arbitrary"),
            vmem_limit_bytes=VMEM_LIMIT),
        name="sb_attn",
    )(proj, proj, proj, gh)


def _moba_kernel(q_ref, k_ref, v_ref, gh_ref, o_ref, kaug_ref, vaug_ref, rhi_ref, rlo_ref,
                 *, tq, tk, topk):
    i = pl.program_id(2)
    S = k_ref.shape[0]
    n_heads = LANES // HEAD_DIM

    @pl.when(i == 0)
    def _prepare_keys():
        kp = k_ref[...]
        vp = v_ref[...]
        srow = lax.broadcasted_iota(jnp.int32, (S, HEAD_DIM), 0)
        scol = lax.broadcasted_iota(jnp.int32, (S, HEAD_DIM), 1)
        onehot = jnp.where(srow // MOBA_BLOCK == scol, 1.0, 0.0).astype(BF16)
        arow = lax.broadcasted_iota(jnp.int32, (LANES, S), 0) - HEAD_DIM
        acol = lax.broadcasted_iota(jnp.int32, (LANES, S), 1) // MOBA_BLOCK
        avg = jnp.where(arow == acol, 1.0 / MOBA_BLOCK, 0.0).astype(BF16)
        zeros = jnp.zeros((S, HEAD_DIM), BF16)
        ones = jnp.ones((S, HEAD_DIM), BF16)
        for hh in range(n_heads):
            kh = kp[:, hh * HEAD_DIM:(hh + 1) * HEAD_DIM]
            kaug_ref[hh] = jnp.concatenate([kh, onehot], axis=1)
            vaug_ref[hh] = jnp.concatenate([vp[:, hh * HEAD_DIM:(hh + 1) * HEAD_DIM], ones], axis=1)
            kmean = _dot(avg, jnp.concatenate([kh, zeros], axis=1))
            hi, lo = _split_bf16(kmean)
            rhi_ref[hh] = hi
            rlo_ref[hh] = lo

    qt = q_ref[...].astype(F32).T
    group = 8
    bidx = lax.broadcasted_iota(jnp.int32, (group, tq), 0)
    zeros_t = jnp.zeros((HEAD_DIM, tq), F32)
    qaugs = []
    for hh in range(n_heads):
        qh = qt[hh * HEAD_DIM:(hh + 1) * HEAD_DIM]
        qz = jnp.concatenate([qh, zeros_t], axis=0).astype(BF16)
        gate = (_dot(rhi_ref[hh], qz) + _dot(rlo_ref[hh], qz))[HEAD_DIM:HEAD_DIM + group]
        valid = bidx < i
        gm = jnp.where(valid, gate, NEG)
        rank = jnp.zeros((group, tq), F32)
        for d in range(1, group):
            nb = pltpu.roll(gm, d, axis=0)
            rank = rank + jnp.where(bidx >= d, jnp.where(nb >= gm, 1.0, 0.0),
                                    jnp.where(nb > gm, 1.0, 0.0))
        allowed = (valid & (rank < topk)) | (bidx == i)
        bias = jnp.where(allowed, 0.0, NEG)
        qaugs.append(jnp.concatenate(
            [qh, bias, jnp.zeros((HEAD_DIM - group, tq), F32)], axis=0).astype(BF16))

    krow = lax.broadcasted_iota(jnp.int32, (tk, tq), 0)
    qcol = lax.broadcasted_iota(jnp.int32, (tk, tq), 1)
    last = (i * tq) // tk
    causal = krow - qcol <= i * tq - last * tk

    def scores(t):
        return [_dot(kaug_ref[hh, t * tk:(t + 1) * tk, :], qaugs[hh]) for hh in range(n_heads)]

    def softmax_pv(sts, t, carry, diag):
        pts, stats = [], []
        for hh in range(n_heads):
            m, l, acc = carry[hh]
            st = jnp.where(causal, sts[hh], NEG) if diag else sts[hh]
            m_new = jnp.maximum(m, jnp.max(st, axis=0, keepdims=True))
            pts.append(jnp.exp2(st - m_new).astype(BF16))
            stats.append((m_new, jnp.exp2(m - m_new)))
        new = []
        for hh in range(n_heads):
            _, l, acc = carry[hh]
            m_new, alpha = stats[hh]
            pv = lax.dot_general(vaug_ref[hh, t * tk:(t + 1) * tk, :], pts[hh],
                                 (((0,), (0,)), ((), ())), preferred_element_type=F32)
            new.append((m_new, alpha * l + pv[HEAD_DIM:HEAD_DIM + 1],
                        alpha * acc + pv[:HEAD_DIM]))
        return tuple(new)

    def attend(n_past):
        order = [n_past] + list(range(n_past))
        carry = tuple((jnp.full((1, tq), NEG, F32), jnp.zeros((1, tq), F32),
                       jnp.zeros((HEAD_DIM, tq), F32)) for _ in range(n_heads))
        sts = scores(order[0])
        for n, t in enumerate(order):
            nxt = scores(order[n + 1]) if n + 1 < len(order) else None
            carry = softmax_pv(sts, t, carry, diag=(n == 0))
            sts = nxt
        outs = []
        for hh in range(n_heads):
            _, l, acc = carry[hh]
            o = acc / l
            outs.append(o * lax.rsqrt(jnp.mean(o * o, axis=0, keepdims=True) + NORM_EPS))
        o_ref[...] = (jnp.concatenate(outs, axis=0).T * gh_ref[...]).astype(BF16)

    for n_past in range(S // tk):
        pl.when(last == n_past)(functools.partial(attend, n_past))


def _moba_attention(proj, gh, B, S):
    tq = MOBA_BLOCK
    tk = 2 * MOBA_BLOCK
    assert S % tk == 0
    nq = S // tq
    n_blk = S // MOBA_BLOCK
    assert n_blk <= 8
    topk = min(MOBA_TOPK, max(n_blk - 1, 1))
    n_pairs = W_MOBA // LANES
    n_heads = LANES // HEAD_DIM
    return pl.pallas_call(
        functools.partial(_moba_kernel, tq=tq, tk=tk, topk=topk),
        grid=(B, n_pairs, nq),
        in_specs=[pl.BlockSpec((tq, LANES), lambda b, p, i: (b * nq + i, C_QB // LANES + p)),
                  pl.BlockSpec((S, LANES), lambda b, p, i: (b, C_KB // LANES + p)),
                  pl.BlockSpec((S, LANES), lambda b, p, i: (b, C_VB // LANES + p)),
                  pl.BlockSpec((1, LANES), lambda b, p, i: (0, p))],
        out_specs=pl.BlockSpec((tq, LANES), lambda b, p, i: (b * nq + i, p)),
        out_shape=jax.ShapeDtypeStruct((B * S, W_MOBA), BF16),
        scratch_shapes=[pltpu.VMEM((n_heads, S, LANES), BF16),
                        pltpu.VMEM((n_heads, S, LANES), BF16),
                        pltpu.VMEM((n_heads, LANES, LANES), BF16),
                        pltpu.VMEM((n_heads, LANES, LANES), BF16)],
        compiler_params=pltpu.CompilerParams(
            dimension_semantics=("arbitrary", "arbitrary", "arbitrary"),
            vmem_limit_bytes=VMEM_LIMIT),
        name="moba_attn",
    )(proj, proj, proj, gh)


def _gmlp_kernel(u_ref, v_ref, gv_ref, ws_ref, b_ref, gh_ref, o_ref, *, tm):
    gu = jax.nn.gelu(u_ref[...].astype(F32))
    gv = jax.nn.gelu(v_ref[...].astype(F32))
    row = lax.broadcasted_iota(jnp.int32, (GMLP_CHUNK, GMLP_CHUNK), 0)
    col = lax.broadcasted_iota(jnp.int32, (GMLP_CHUNK, GMLP_CHUNK), 1)
    outs = []
    for g in range(N_GROUPS_GMLP):
        c0, c1 = g * HEAD_DIM, (g + 1) * HEAD_DIM
        vn = _rms(gv[:, c0:c1], gv_ref[:, c0:c1]).astype(BF16)
        wm = jnp.where(col <= row, ws_ref[g], 0.0).astype(BF16)
        bias = b_ref[:, c0:c1]
        mixed = jnp.concatenate(
            [_dot(wm, vn[c * GMLP_CHUNK:(c + 1) * GMLP_CHUNK]) + bias
             for c in range(tm // GMLP_CHUNK)], axis=0)
        outs.append(_rms(gu[:, c0:c1] * mixed, gh_ref[:, c0:c1]))
    o_ref[...] = jnp.concatenate(outs, axis=1).astype(BF16)


def _gmlp(proj, gv, ws, b_exp, gh, tm):
    T = proj.shape[0]
    nu, nv = C_UC // W_GMLP, C_VC // W_GMLP
    return pl.pallas_call(
        functools.partial(_gmlp_kernel, tm=tm),
        grid=(T // tm,),
        in_specs=[pl.BlockSpec((tm, W_GMLP), lambda i: (i, nu)),
                  pl.BlockSpec((tm, W_GMLP), lambda i: (i, nv)),
                  pl.BlockSpec((1, W_GMLP), lambda i: (0, 0)),
                  pl.BlockSpec((N_GROUPS_GMLP, GMLP_CHUNK, GMLP_CHUNK), lambda i: (0, 0, 0)),
                  pl.BlockSpec((GMLP_CHUNK, W_GMLP), lambda i: (0, 0)),
                  pl.BlockSpec((1, W_GMLP), lambda i: (0, 0))],
        out_specs=pl.BlockSpec((tm, W_GMLP), lambda i: (i, 0)),
        out_shape=jax.ShapeDtypeStruct((T, W_GMLP), BF16),
        compiler_params=pltpu.CompilerParams(dimension_semantics=("arbitrary",),
                                             vmem_limit_bytes=VMEM_LIMIT),
        name="gmlp",
    )(proj, proj, gv, ws, b_exp, gh)


def _out_kernel(osb_ref, omoba_ref, ogmlp_ref, x_ref, w_ref, g_ref, rhi_ref, rlo_ref,
                x1_ref, h_ref, lg_ref):
    x1 = (x_ref[...]
          + _dot(osb_ref[...], w_ref[0:W_SB, :])
          + _dot(omoba_ref[...], w_ref[W_SB:W_SB + W_MOBA, :])
          + _dot(ogmlp_ref[...], w_ref[W_SB + W_MOBA:, :]))
    x1_ref[...] = x1
    hn = _rms(x1, g_ref[...])
    hi, lo = _split_bf16(hn)
    h_ref[...] = _pack_halves(hn)
    lg_ref[...] = (_nt_dot(rhi_ref[...], hi) + _nt_dot(rhi_ref[...], lo)
                   + _nt_dot(rlo_ref[...], hi))


def _out_proj(o_sb, o_moba, o_gmlp, x2d, w_bf16, g, r_hi, r_lo, tm):
    T, D = x2d.shape
    row = lambda i: (i, 0)
    const = lambda i: (0, 0)
    return pl.pallas_call(
        _out_kernel,
        grid=(T // tm,),
        in_specs=[pl.BlockSpec((tm, W_SB), row), pl.BlockSpec((tm, W_MOBA), row),
                  pl.BlockSpec((tm, W_GMLP), row), pl.BlockSpec((tm, D), row),
                  pl.BlockSpec((D, D), const), pl.BlockSpec((1, D), const),
                  pl.BlockSpec((ROUTER_ROWS, D), const), pl.BlockSpec((ROUTER_ROWS, D), const)],
        out_specs=[pl.BlockSpec((tm, D), row), pl.BlockSpec((tm, D // 2), row),
                   pl.BlockSpec((ROUTER_ROWS, tm), lambda i: (0, i))],
        out_shape=[jax.ShapeDtypeStruct((T, D), F32), jax.ShapeDtypeStruct((T, D // 2), jnp.uint32),
                   jax.ShapeDtypeStruct((ROUTER_ROWS, T), F32)],
        compiler_params=pltpu.CompilerParams(dimension_semantics=("arbitrary",),
                                             vmem_limit_bytes=VMEM_LIMIT),
        name="outproj",
    )(o_sb, o_moba, o_gmlp, x2d, w_bf16, g, r_hi, r_lo)


ROUTER_EXPERT_ROW = 8


def _first_max(p):
    rows = lax.broadcasted_iota(jnp.int32, p.shape, 0).astype(F32)
    top = jnp.max(p, axis=0, keepdims=True)
    idx = jnp.min(jnp.where(p == top, rows, float(p.shape[0])), axis=0, keepdims=True)
    return top, idx, rows


def _router_kernel(lg_ref, pos_ref, gw_ref, cnt_ref, cnt_acc, base_ref, *, tm):
    phase = pl.program_id(0)
    i = pl.program_id(1)
    lg = lg_ref[...]
    gl = lg[0:N_EXPERT_GROUPS]
    ge = jnp.exp(gl - jnp.max(gl, axis=0, keepdims=True))
    p_group = ge / jnp.sum(ge, axis=0, keepdims=True)
    p_g, g_sel, _ = _first_max(p_group)
    le = jnp.zeros((EXPERTS_PER_GROUP, tm), F32)
    for g in range(N_EXPERT_GROUPS):
        r0 = ROUTER_EXPERT_ROW + g * EXPERTS_PER_GROUP
        le = jnp.where(g_sel == g, lg[r0:r0 + EXPERTS_PER_GROUP], le)
    ee = jnp.exp(le - jnp.max(le, axis=0, keepdims=True))
    p = ee / jnp.sum(ee, axis=0, keepdims=True)
    p0, i0, rows = _first_max(p)
    p1, i1, _ = _first_max(jnp.where(rows == i0, -1.0, p))
    e0 = g_sel * EXPERTS_PER_GROUP + i0
    e1 = g_sel * EXPERTS_PER_GROUP + i1
    xrow = lax.broadcasted_iota(jnp.int32, (N_EXPERTS, tm), 0).astype(F32)
    oh0 = xrow == e0
    oh1 = xrow == e1
    slots = jnp.where(oh0, 1.0, 0.0) + jnp.where(oh1, 1.0, 0.0)
    tile_cnt = slots[:, 0:LANES]
    for c in range(1, tm // LANES):
        tile_cnt = tile_cnt + slots[:, c * LANES:(c + 1) * LANES]

    @pl.when((phase == 0) & (i == 0))
    def _init():
        cnt_acc[...] = jnp.zeros_like(cnt_acc)

    @pl.when(phase == 0)
    def _count():
        cnt_acc[...] += tile_cnt

    @pl.when((phase == 1) & (i == 0))
    def _starts():
        counts = jnp.sum(cnt_acc[...], axis=1, keepdims=True)
        n_blk = jnp.floor((counts + (MOE_BLOCK - 1)) * (1.0 / MOE_BLOCK))
        er = lax.broadcasted_iota(jnp.int32, (N_EXPERTS, N_EXPERTS), 0)
        ec = lax.broadcasted_iota(jnp.int32, (N_EXPERTS, N_EXPERTS), 1)
        before = jnp.where(ec < er, 1.0, 0.0).astype(BF16)
        start_blk = _dot(before, jnp.broadcast_to(n_blk, (N_EXPERTS, LANES)).astype(BF16))
        base_ref[...] = start_blk * MOE_BLOCK
        cnt_ref[...] = jnp.broadcast_to(counts, (N_EXPERTS, LANES)).astype(jnp.int32)

    @pl.when(phase == 1)
    def _assign():
        tr = lax.broadcasted_iota(jnp.int32, (tm, tm), 0)
        tc = lax.broadcasted_iota(jnp.int32, (tm, tm), 1)
        earlier = jnp.where(tr < tc, 1.0, 0.0).astype(BF16)
        row_of = _dot(slots.astype(BF16), earlier) + base_ref[:, 0:1]
        pos0 = jnp.sum(jnp.where(oh0, row_of, 0.0), axis=0, keepdims=True)
        pos1 = jnp.sum(jnp.where(oh1, row_of, 0.0), axis=0, keepdims=True)
        pos_ref[...] = jnp.concatenate([pos0, pos1], axis=0).astype(jnp.int32)
        scale = p_g / (p0 + p1)
        gw_ref[...] = jnp.concatenate([p0 * scale, p1 * scale], axis=0)
        base_ref[...] += jnp.sum(tile_cnt, axis=1, keepdims=True)


def _router(logits_t, tm):
    T = logits_t.shape[1]
    tok = lambda p, i: (0, i * p)
    return pl.pallas_call(
        functools.partial(_router_kernel, tm=tm),
        grid=(2, T // tm),
        in_specs=[pl.BlockSpec((ROUTER_ROWS, tm), lambda p, i: (0, i))],
        out_specs=[pl.BlockSpec((MOE_TOP_K, tm), tok), pl.BlockSpec((MOE_TOP_K, tm), tok),
                   pl.BlockSpec((N_EXPERTS, LANES), lambda p, i: (0, 0))],
        out_shape=[jax.ShapeDtypeStruct((MOE_TOP_K, T), jnp.int32),
                   jax.ShapeDtypeStruct((MOE_TOP_K, T), F32),
                   jax.ShapeDtypeStruct((N_EXPERTS, LANES), jnp.int32)],
        scratch_shapes=[pltpu.VMEM((N_EXPERTS, LANES), F32), pltpu.VMEM((N_EXPERTS, LANES), F32)],
        compiler_params=pltpu.CompilerParams(dimension_semantics=("arbitrary", "arbitrary")),
        name="router",
    )(logits_t)


def _moe_kernel(be_ref, first_ref, slot_ref, next_ref, rows_ref, na_ref, xs_ref, wg_hbm, wu_hbm,
                wd_hbm, ys_ref, wgf, wuf, wdf, wgb, wub, wdb, sem, *, layer):
    b = pl.program_id(0)
    e = be_ref[b]
    slot = slot_ref[b]
    active = b < na_ref[0]

    def fetch(expert, s):
        return (pltpu.make_async_copy(wg_hbm.at[layer, expert], wgf.at[s], sem.at[s, 0]),
                pltpu.make_async_copy(wu_hbm.at[layer, expert], wuf.at[s], sem.at[s, 1]),
                pltpu.make_async_copy(wd_hbm.at[layer, expert], wdf.at[s], sem.at[s, 2]))

    @pl.when(b == 0)
    def _first_fetch():
        for c in fetch(e, slot):
            c.start()

    @pl.when(active & (first_ref[b] == 1))
    def _load_expert():
        for c in fetch(e, slot):
            c.wait()
        wgb[...] = wgf[slot].astype(BF16)
        wub[...] = wuf[slot].astype(BF16)
        wdb[...] = wdf[slot].astype(BF16)
        nxt = next_ref[b]

        @pl.when(nxt >= 0)
        def _prefetch():
            for c in fetch(nxt, 1 - slot):
                c.start()

    @pl.when(active)
    def _compute():
        row = lax.broadcasted_iota(jnp.int32, xs_ref.shape, 0)
        words = jnp.where(row < rows_ref[b], xs_ref[...], jnp.uint32(0))
        xb = _unpack_halves(words).astype(BF16)
        a = jax.nn.silu(_dot(xb, wgb[...])) * _dot(xb, wub[...])
        ys_ref[...] = _pack_halves(_dot(a.astype(BF16), wdb[...]))

    @pl.when(jnp.logical_not(active))
    def _pad():
        ys_ref[...] = jnp.zeros_like(ys_ref)


def _moe_experts(plan, xs, w_gate, w_up, w_down, layer):
    n_rows = xs.shape[0]
    D = 2 * xs.shape[1]
    n_blocks = n_rows // MOE_BLOCK
    DE = w_gate.shape[-1]
    rows = lambda b, *_: (b, 0)
    grid_spec = pltpu.PrefetchScalarGridSpec(
        num_scalar_prefetch=len(plan),
        grid=(n_blocks,),
        in_specs=[pl.BlockSpec((MOE_BLOCK, D // 2), rows),
                  pl.BlockSpec(memory_space=pl.ANY), pl.BlockSpec(memory_space=pl.ANY),
                  pl.BlockSpec(memory_space=pl.ANY)],
        out_specs=pl.BlockSpec((MOE_BLOCK, D // 2), rows),
        scratch_shapes=[pltpu.VMEM((2, D, DE), F32), pltpu.VMEM((2, D, DE), F32),
                        pltpu.VMEM((2, DE, D), F32),
                        pltpu.VMEM((D, DE), BF16), pltpu.VMEM((D, DE), BF16),
                        pltpu.VMEM((DE, D), BF16),
                        pltpu.SemaphoreType.DMA((2, 3))])
    return pl.pallas_call(
        functools.partial(_moe_kernel, layer=layer),
        grid_spec=grid_spec,
        out_shape=jax.ShapeDtypeStruct((n_rows, D // 2), jnp.uint32),
        compiler_params=pltpu.CompilerParams(dimension_semantics=("arbitrary",),
                                             vmem_limit_bytes=VMEM_LIMIT),
        name="moe_experts",
    )(*plan, xs, w_gate, w_up, w_down)


def _final_kernel(x1_ref, yg_ref, gw_ref, g_ref, o_ref):
    o_ref[...] = _rms(_combine(x1_ref, yg_ref, gw_ref), g_ref[...])


def _final_norm(x2d, yg, gw, g, tm):
    T, D = x2d.shape
    row = lambda i: (i, 0)
    return pl.pallas_call(
        _final_kernel,
        grid=(T // tm,),
        in_specs=[pl.BlockSpec((tm, D), row),
                  pl.BlockSpec((MOE_TOP_K, tm, D // 2), lambda i: (0, i, 0)),
                  pl.BlockSpec((tm, MOE_TOP_K), row), pl.BlockSpec((1, D), lambda i: (0, 0))],
        out_specs=pl.BlockSpec((tm, D), row),
        out_shape=jax.ShapeDtypeStruct((T, D), F32),
        compiler_params=pltpu.CompilerParams(dimension_semantics=("arbitrary",)),
        name="final_norm",
    )(x2d, yg, gw, g)


SC_GATHER_ROWS = 64


def _sc_gather(table, idx):
    info = plsc.get_sparse_core_info()
    n_cores, n_workers = info.num_cores, info.num_cores * info.num_subcores
    N, W = idx.shape[0], table.shape[1]
    per_w = N // n_workers
    n_ch = per_w // SC_GATHER_ROWS
    assert per_w * n_workers == N and n_ch * SC_GATHER_ROWS == per_w
    mesh = plsc.VectorSubcoreMesh(core_axis_name="c", subcore_axis_name="s")

    @functools.partial(
        pl.kernel, mesh=mesh,
        out_type=jax.ShapeDtypeStruct((N, W), table.dtype),
        scratch_types=[pltpu.VMEM((n_ch, SC_GATHER_ROWS), jnp.int32),
                       pltpu.VMEM((2, SC_GATHER_ROWS, W), table.dtype),
                       pltpu.SemaphoreType.DMA((2,)),
                       pltpu.SemaphoreType.DMA((2,))])
    def gather_kernel(table_hbm, idx_hbm, out_hbm, idx_v, rows_v, gsem, ssem):
        wid = lax.axis_index("s") * n_cores + lax.axis_index("c")
        base = wid * per_w
        pltpu.sync_copy(idx_hbm.at[wid], idx_v)

        def fetch(c):
            return pltpu.make_async_copy(table_hbm.at[idx_v.at[c]], rows_v.at[c % 2],
                                         gsem.at[c % 2])

        def write(c):
            return pltpu.make_async_copy(
                rows_v.at[c % 2], out_hbm.at[pl.ds(base + c * SC_GATHER_ROWS, SC_GATHER_ROWS)],
                ssem.at[c % 2])

        fetch(0).start()
        for c in range(n_ch):
            if c + 1 < n_ch:
                if c >= 1:
                    write(c - 1).wait()
                fetch(c + 1).start()
            fetch(c).wait()
            write(c).start()
        if n_ch >= 2:
            write(n_ch - 2).wait()
        write(n_ch - 1).wait()

    return gather_kernel(table, idx.reshape(n_workers, n_ch, SC_GATHER_ROWS))


def _sc_scatter(rows, pos, n_out):
    info = plsc.get_sparse_core_info()
    n_cores, n_workers = info.num_cores, info.num_cores * info.num_subcores
    K, T = pos.shape
    W = rows.shape[1]
    per_w = T // n_workers
    n_ch = per_w // SC_GATHER_ROWS
    assert per_w * n_workers == T and n_ch * SC_GATHER_ROWS == per_w
    mesh = plsc.VectorSubcoreMesh(core_axis_name="c", subcore_axis_name="s")

    @functools.partial(
        pl.kernel, mesh=mesh,
        out_type=jax.ShapeDtypeStruct((n_out, W), rows.dtype),
        scratch_types=[pltpu.VMEM((K, n_ch, SC_GATHER_ROWS), jnp.int32),
                       pltpu.VMEM((2, SC_GATHER_ROWS, W), rows.dtype),
                       pltpu.SemaphoreType.DMA((2,)),
                       pltpu.SemaphoreType.DMA((2, K))])
    def scatter_kernel(rows_hbm, idx_hbm, out_hbm, idx_v, buf_v, lsem, ssem):
        wid = lax.axis_index("s") * n_cores + lax.axis_index("c")
        base = wid * per_w
        pltpu.sync_copy(idx_hbm.at[wid], idx_v)

        def load(c):
            return pltpu.make_async_copy(
                rows_hbm.at[pl.ds(base + c * SC_GATHER_ROWS, SC_GATHER_ROWS)], buf_v.at[c % 2],
                lsem.at[c % 2])

        def store(c, k):
            return pltpu.make_async_copy(buf_v.at[c % 2], out_hbm.at[idx_v.at[k, c]],
                                         ssem.at[c % 2, k])

        load(0).start()
        for c in range(n_ch):
            if c + 1 < n_ch:
                if c >= 1:
                    for k in range(K):
                        store(c - 1, k).wait()
                load(c + 1).start()
            load(c).wait()
            for k in range(K):
                store(c, k).start()
        for c in range(max(n_ch - 2, 0), n_ch):
            for k in range(K):
                store(c, k).wait()

    idx = pos.reshape(K, n_workers, n_ch, SC_GATHER_ROWS).transpose(1, 0, 2, 3)
    return scatter_kernel(rows, idx)


def _rope_tables(positions):
    half = ROT_DIM // 2
    inv_freq = jnp.power(ROPE_THETA, -jnp.arange(half, dtype=F32) / half)
    ang = positions.astype(F32).reshape(-1)[:, None] * inv_freq
    cos, sin = jnp.cos(ang), jnp.sin(ang)
    T = ang.shape[0]
    ones = jnp.ones((T, HEAD_DIM - ROT_DIM), F32)
    zeros = jnp.zeros((T, HEAD_DIM - ROT_DIM), F32)
    z8 = jnp.zeros((T, half), F32)
    rc = jnp.concatenate([cos, cos, ones], axis=1)
    rs1 = jnp.concatenate([-sin, z8, zeros], axis=1)
    rs2 = jnp.concatenate([z8, sin, zeros], axis=1)
    rep = LANES // HEAD_DIM
    return jnp.tile(rc, (1, rep)), jnp.tile(rs1, (1, rep)), jnp.tile(rs2, (1, rep))


def _block_plan(counts, T):
    i32 = jnp.int32
    n_blocks = -(-(T * MOE_TOP_K) // MOE_BLOCK) + N_EXPERTS
    experts = jnp.arange(N_EXPERTS, dtype=i32)
    blocks = jnp.arange(n_blocks, dtype=i32)
    n_blk = (counts + MOE_BLOCK - 1) // MOE_BLOCK
    blocks_end = jnp.cumsum(n_blk)
    blk_expert = jnp.minimum(jnp.sum(blocks_end[None, :] <= blocks[:, None], axis=1),
                             N_EXPERTS - 1).astype(i32)
    is_active = n_blk > 0
    ordinal = jnp.cumsum(is_active.astype(i32)) - 1
    later_active = is_active[None, :] & (experts[None, :] > experts[:, None])
    next_active = jnp.min(jnp.where(later_active, experts[None, :], N_EXPERTS), axis=1)
    next_active = jnp.where(next_active < N_EXPERTS, next_active, -1).astype(i32)
    blk_in_expert = blocks - (blocks_end - n_blk)[blk_expert]
    blk_first = (blk_in_expert == 0).astype(i32)
    blk_rows = jnp.clip(counts[blk_expert] - blk_in_expert * MOE_BLOCK, 0, MOE_BLOCK).astype(i32)
    plan = (blk_expert, blk_first, (ordinal[blk_expert] % 2).astype(i32),
            next_active[blk_expert], blk_rows, blocks_end[-1:].astype(i32))
    return n_blocks, plan


def kernel(x, positions, w_in, w_out, g_mix_norm, g_head_norm, g_gmlp_vnorm, w_spatial, b_spatial,
           g_ffn_norm, w_router_group, w_router_expert, w_expert_gate, w_expert_up, w_expert_down,
           g_final):
    B, S, D = x.shape
    T = B * S
    depth = w_in.shape[0]
    tm = min(512, T)
    rc, rs1, rs2 = _rope_tables(positions)
    xc = x.reshape(T, D)
    moe_out = None
    for l in range(depth):
        gh = g_head_norm[l].reshape(1, -1)
        proj, xc = _inproj(xc, moe_out, g_mix_norm[l].reshape(1, D), w_in[l].astype(BF16),
                           rc, rs1, rs2, tm)
        o_sb = _sb_attention(proj, gh[:, :W_SB], B, S)
        o_moba = _moba_attention(proj, gh[:, W_SB:W_SB + W_MOBA], B, S)
        b_exp = jnp.repeat(b_spatial[l].T, HEAD_DIM, axis=1)
        o_gmlp = _gmlp(proj, g_gmlp_vnorm[l].reshape(1, -1), w_spatial[l], b_exp,
                       gh[:, W_SB + W_MOBA:], tm)
        w_r = jnp.concatenate(
            [w_router_group[l].T, jnp.zeros((ROUTER_EXPERT_ROW - N_EXPERT_GROUPS, D), F32),
             w_router_expert[l].T,
             jnp.zeros((ROUTER_ROWS - ROUTER_EXPERT_ROW - N_EXPERTS, D), F32)], axis=0)
        r_hi, r_lo = _split_bf16(w_r)
        xc, h, logits_t = _out_proj(o_sb, o_moba, o_gmlp, xc, w_out[l].astype(BF16),
                                    g_ffn_norm[l].reshape(1, D), r_hi, r_lo, tm)
        pos, gate_w, counts = _router(logits_t, tm)
        n_blocks, plan = _block_plan(counts[:, 0], T)
        xs = _sc_scatter(h, pos, n_blocks * MOE_BLOCK)
        ys = _moe_experts(plan, xs, w_expert_gate, w_expert_up, w_expert_down, l)
        yg = _sc_gather(ys, pos.reshape(-1)).reshape(MOE_TOP_K, T, D // 2)
        moe_out = (yg, gate_w.T)
    return _final_norm(xc, moe_out[0], moe_out[1], g_final.reshape(1, D), tm).reshape(B, S, D)
```

```python
import functools

import jax
import jax.numpy as jnp
from jax import lax
from jax.experimental import pallas as pl
from jax.experimental.pallas import tpu as pltpu
from jax.experimental.pallas import tpu_sc as plsc

F32 = jnp.float32
BF16 = jnp.bfloat16

HEAD_DIM = 64
LANES = 128
MXU_COLS = 256
N_HEADS_SB = 4
N_HEADS_MOBA = 8
N_GROUPS_GMLP = 4
W_SB = N_HEADS_SB * HEAD_DIM
W_MOBA = N_HEADS_MOBA * HEAD_DIM
W_GMLP = N_GROUPS_GMLP * HEAD_DIM
MOBA_BLOCK = 256
MOBA_TOPK = 3
GMLP_CHUNK = 128
ROPE_THETA = 500000.0
ROT_DIM = HEAD_DIM // 4
N_EXPERT_GROUPS = 4
EXPERTS_PER_GROUP = 8
N_EXPERTS = N_EXPERT_GROUPS * EXPERTS_PER_GROUP
MOE_TOP_K = 2
MOE_BLOCK = 256
NORM_EPS = 1e-6
ATTN_SCALE = HEAD_DIM ** -0.5
NEG = -1e30
LOG2E = 1.4426950408889634
ROUTER_ROWS = 64
VMEM_LIMIT = 48 * 1024 * 1024

C_QA, C_KA, C_VA = 0, W_SB, 2 * W_SB
C_QB = 3 * W_SB
C_KB = C_QB + W_MOBA
C_VB = C_KB + W_MOBA
C_UC = C_VB + W_MOBA
C_VC = C_UC + W_GMLP
IN_COLS = C_VC + W_GMLP


def _nt_dot(a, b):
    return lax.dot_general(a, b, (((1,), (1,)), ((), ())), preferred_element_type=F32)


def _dot(a, b):
    return jnp.dot(a, b, preferred_element_type=F32)


def _rms(x, g):
    return x * lax.rsqrt(jnp.mean(x * x, axis=-1, keepdims=True) + NORM_EPS) * g


def _pack_halves(x):
    w = x.shape[1] // 2
    lo = lax.bitcast_convert_type(x[:, :w].astype(BF16).astype(F32), jnp.uint32)
    hi = lax.bitcast_convert_type(x[:, w:].astype(BF16).astype(F32), jnp.uint32)
    return (lo >> 16) | hi


def _unpack_halves(words):
    lo = lax.bitcast_convert_type(words << 16, F32)
    hi = lax.bitcast_convert_type(words & jnp.uint32(0xFFFF0000), F32)
    return jnp.concatenate([lo, hi], axis=1)


def _split_bf16(x):
    hi = x.astype(BF16)
    lo = (x - hi.astype(F32)).astype(BF16)
    return hi, lo


def _combine(x1_ref, yg_ref, gw_ref):
    gw = gw_ref[...]
    return (x1_ref[...] + _unpack_halves(yg_ref[0]) * gw[:, 0:1]
            + _unpack_halves(yg_ref[1]) * gw[:, 1:2])


def _inproj_kernel(*refs, combine):
    if combine:
        x1_ref, yg_ref, gw_ref, g_ref, w_ref, rc_ref, rs1_ref, rs2_ref, o_ref, x_ref = refs
        x = _combine(x1_ref, yg_ref, gw_ref)
        x_ref[...] = x
    else:
        x_ref, g_ref, w_ref, rc_ref, rs1_ref, rs2_ref, o_ref = refs
        x = x_ref[...]
    y = _rms(x, g_ref[...]).astype(BF16)
    wide = lambda t_ref: jnp.concatenate([t_ref[...]] * (MXU_COLS // LANES), axis=1)
    rc, rs1, rs2 = wide(rc_ref), wide(rs1_ref), wide(rs2_ref)
    half = ROT_DIM // 2
    for c0 in range(0, IN_COLS, MXU_COLS):
        p = _dot(y, w_ref[:, c0:c0 + MXU_COLS])
        if C_QB <= c0 < C_VB:
            p = (p * rc + pltpu.roll(p, MXU_COLS - half, axis=1) * rs1
                 + pltpu.roll(p, half, axis=1) * rs2)
        if c0 < C_KA or C_QB <= c0 < C_KB:
            p = p * (ATTN_SCALE * LOG2E)
        o_ref[:, c0:c0 + MXU_COLS] = p.astype(BF16)


def _inproj(x2d, moe_out, g, w_bf16, rc, rs1, rs2, tm):
    T, D = x2d.shape
    row = lambda i: (i, 0)
    const = lambda i: (0, 0)
    combine = moe_out is not None
    x_specs = [pl.BlockSpec((tm, D), row)]
    x_args = [x2d]
    out_specs = [pl.BlockSpec((tm, IN_COLS), row)]
    out_shape = [jax.ShapeDtypeStruct((T, IN_COLS), BF16)]
    if combine:
        yg, gw = moe_out
        x_specs += [pl.BlockSpec((MOE_TOP_K, tm, D // 2), lambda i: (0, i, 0)),
                    pl.BlockSpec((tm, MOE_TOP_K), row)]
        x_args += [yg, gw]
        out_specs.append(pl.BlockSpec((tm, D), row))
        out_shape.append(jax.ShapeDtypeStruct((T, D), F32))
    outs = pl.pallas_call(
        functools.partial(_inproj_kernel, combine=combine),
        grid=(T // tm,),
        in_specs=x_specs + [pl.BlockSpec((1, D), const), pl.BlockSpec((D, IN_COLS), const),
                            pl.BlockSpec((tm, LANES), row), pl.BlockSpec((tm, LANES), row),
                            pl.BlockSpec((tm, LANES), row)],
        out_specs=out_specs,
        out_shape=out_shape,
        compiler_params=pltpu.CompilerParams(dimension_semantics=("arbitrary",),
                                             vmem_limit_bytes=VMEM_LIMIT),
        name="inproj",
    )(*x_args, g, w_bf16, rc, rs1, rs2)
    return (outs[0], outs[1]) if combine else (outs[0], x2d)


def _sb_kernel(q_ref, k_ref, v_ref, gh_ref, o_ref, *, tq, tk, sub):
    i = pl.program_id(2)
    S = k_ref.shape[0]
    n_heads = LANES // HEAD_DIM
    n_sub = tk // sub
    qt = q_ref[...].astype(F32).T
    zeros_t = jnp.zeros((HEAD_DIM, tq), F32)
    qz = [jnp.concatenate([qt[hh * HEAD_DIM:(hh + 1) * HEAD_DIM] if h2 == hh else zeros_t
                           for h2 in range(n_heads)], axis=0).astype(BF16)
          for hh in range(n_heads)]
    srow = lax.broadcasted_iota(jnp.int32, (sub, sub), 0)
    scol = lax.broadcasted_iota(jnp.int32, (sub, sub), 1)
    later = jnp.where(scol > srow, 1.0, 0.0).astype(BF16)
    krow = lax.broadcasted_iota(jnp.int32, (tk, tq), 0)
    qcol = lax.broadcasted_iota(jnp.int32, (tk, tq), 1)
    last = (i * tq + tq - 1) // tk
    causal = krow - qcol < i * tq - last * tk

    def scores(t):
        return [_dot(k_ref[t * tk:(t + 1) * tk, :], qz[hh]) for hh in range(n_heads)]

    def weights_pv(zs, t, carry, diag):
        wts, runs = [], []
        for hh in range(n_heads):
            z = zs[hh]
            nz = -z
            lr = jnp.minimum(nz, 0.0) - jnp.log2(1.0 + jnp.exp2(jnp.minimum(z, nz)))
            if diag:
                lr = jnp.where(causal, lr, 0.0)
            lrb = lr.astype(BF16)
            tail = carry[hh][0]
            afters = [None] * n_sub
            for sb in reversed(range(n_sub)):
                afters[sb] = _dot(later, lrb[sb * sub:(sb + 1) * sub]) + tail
                tail = afters[sb][0:1] + lr[sb * sub:sb * sub + 1]
            w = jnp.exp2(z + lr + jnp.concatenate(afters, axis=0))
            if diag:
                w = jnp.where(causal, w, 0.0)
            wts.append(w.astype(BF16))
            runs.append(tail)
        new = []
        for hh in range(n_heads):
            pv = lax.dot_general(v_ref[t * tk:(t + 1) * tk, :], wts[hh],
                                 (((0,), (0,)), ((), ())), preferred_element_type=F32)
            new.append((runs[hh], carry[hh][1] + pv[hh * HEAD_DIM:(hh + 1) * HEAD_DIM]))
        return tuple(new)

    def attend(n_past):
        order = list(range(n_past, -1, -1))
        carry = tuple((jnp.zeros((1, tq), F32), jnp.zeros((HEAD_DIM, tq), F32))
                      for _ in range(n_heads))
        zs = scores(order[0])
        for n, t in enumerate(order):
            nxt = scores(order[n + 1]) if n + 1 < len(order) else None
            carry = weights_pv(zs, t, carry, diag=(n == 0))
            zs = nxt
        outs = []
        for hh in range(n_heads):
            o = carry[hh][1]
            outs.append(o * lax.rsqrt(jnp.mean(o * o, axis=0, keepdims=True) + NORM_EPS))
        o_ref[...] = (jnp.concatenate(outs, axis=0).T * gh_ref[...]).astype(BF16)

    for n_past in range(S // tk):
        pl.when(last == n_past)(functools.partial(attend, n_past))


def _sb_attention(proj, gh, B, S):
    tq, sub = 512, 256
    tk = 2 * sub
    assert S % tk == 0
    nq = S // tq
    n_pairs = W_SB // LANES
    return pl.pallas_call(
        functools.partial(_sb_kernel, tq=tq, tk=tk, sub=sub),
        grid=(B, n_pairs, nq),
        in_specs=[pl.BlockSpec((tq, LANES), lambda b, p, i: (b * nq + i, C_QA // LANES + p)),
                  pl.BlockSpec((S, LANES), lambda b, p, i: (b, C_KA // LANES + p)),
                  pl.BlockSpec((S, LANES), lambda b, p, i: (b, C_VA // LANES + p)),
                  pl.BlockSpec((1, LANES), lambda b, p, i: (0, p))],
        out_specs=pl.BlockSpec((tq, LANES), lambda b, p, i: (b * nq + i, p)),
        out_shape=jax.ShapeDtypeStruct((B * S, W_SB), BF16),
        compiler_params=pltpu.CompilerParams(
            dimension_semantics=("arbitrary", "arbitrary", "arbitrary"),
            vmem_limit_bytes=VMEM_LIMIT),
        name="sb_attn",
    )(proj, proj, proj, gh)


def _moba_kernel(q_ref, k_ref, v_ref, gh_ref, o_ref, kaug_ref, vaug_ref, rhi_ref, rlo_ref,
                 *, tq, tk, topk):
    i = pl.program_id(2)
    S = k_ref.shape[0]
    n_heads = LANES // HEAD_DIM

    @pl.when(i == 0)
    def _prepare_keys():
        kp = k_ref[...]
        vp = v_ref[...]
        srow = lax.broadcasted_iota(jnp.int32, (S, HEAD_DIM), 0)
        scol = lax.broadcasted_iota(jnp.int32, (S, HEAD_DIM), 1)
        onehot = jnp.where(srow // MOBA_BLOCK == scol, 1.0, 0.0).astype(BF16)
        arow = lax.broadcasted_iota(jnp.int32, (LANES, S), 0) - HEAD_DIM
        acol = lax.broadcasted_iota(jnp.int32, (LANES, S), 1) // MOBA_BLOCK
        avg = jnp.where(arow == acol, 1.0 / MOBA_BLOCK, 0.0).astype(BF16)
        zeros = jnp.zeros((S, HEAD_DIM), BF16)
        ones = jnp.ones((S, HEAD_DIM), BF16)
        for hh in range(n_heads):
            kh = kp[:, hh * HEAD_DIM:(hh + 1) * HEAD_DIM]
            kaug_ref[hh] = jnp.concatenate([kh, onehot], axis=1)
            vaug_ref[hh] = jnp.concatenate([vp[:, hh * HEAD_DIM:(hh + 1) * HEAD_DIM], ones], axis=1)
            kmean = _dot(avg, jnp.concatenate([kh, zeros], axis=1))
            hi, lo = _split_bf16(kmean)
            rhi_ref[hh] = hi
            rlo_ref[hh] = lo

    qt = q_ref[...].astype(F32).T
    group = 8
    bidx = lax.broadcasted_iota(jnp.int32, (group, tq), 0)
    zeros_t = jnp.zeros((HEAD_DIM, tq), F32)
    own = (i * tq + lax.broadcasted_iota(jnp.int32, (group, tq), 1)) // MOBA_BLOCK
    qaugs = []
    for hh in range(n_heads):
        qh = qt[hh * HEAD_DIM:(hh + 1) * HEAD_DIM]
        qz = jnp.concatenate([qh, zeros_t], axis=0).astype(BF16)
        gate = (_dot(rhi_ref[hh], qz) + _dot(rlo_ref[hh], qz))[HEAD_DIM:HEAD_DIM + group]
        valid = bidx < own
        gm = jnp.where(valid, gate, NEG)
        rank = jnp.zeros((group, tq), F32)
        for d in range(1, group):
            nb = pltpu.roll(gm, d, axis=0)
            rank = rank + jnp.where(bidx >= d, jnp.where(nb >= gm, 1.0, 0.0),
                                    jnp.where(nb > gm, 1.0, 0.0))
        allowed = (valid & (rank < topk)) | (bidx == own)
        bias = jnp.where(allowed, 0.0, NEG)
        qaugs.append(jnp.concatenate(
            [qh, bias, jnp.zeros((HEAD_DIM - group, tq), F32)], axis=0).astype(BF16))

    krow = lax.broadcasted_iota(jnp.int32, (tk, tq), 0)
    qcol = lax.broadcasted_iota(jnp.int32, (tk, tq), 1)
    last = (i * tq + tq - 1) // tk
    causal = krow - qcol <= i * tq - last * tk

    def scores(t):
        return [_dot(kaug_ref[hh, t * tk:(t + 1) * tk, :], qaugs[hh]) for hh in range(n_heads)]

    def softmax_pv(sts, t, carry, diag):
        pts, stats = [], []
        for hh in range(n_heads):
            m, l, acc = carry[hh]
            st = jnp.where(causal, sts[hh], NEG) if diag else sts[hh]
            m_new = jnp.maximum(m, jnp.max(st, axis=0, keepdims=True))
            pts.append(jnp.exp2(st - m_new).astype(BF16))
            stats.append((m_new, jnp.exp2(m - m_new)))
        new = []
        for hh in range(n_heads):
            _, l, acc = carry[hh]
            m_new, alpha = stats[hh]
            pv = lax.dot_general(vaug_ref[hh, t * tk:(t + 1) * tk, :], pts[hh],
                                 (((0,), (0,)), ((), ())), preferred_element_type=F32)
            new.append((m_new, alpha * l + pv[HEAD_DIM:HEAD_DIM + 1],
                        alpha * acc + pv[:HEAD_DIM]))
        return tuple(new)

    def attend(n_past):
        order = [n_past] + list(range(n_past))
        carry = tuple((jnp.full((1, tq), NEG, F32), jnp.zeros((1, tq), F32),
                       jnp.zeros((HEAD_DIM, tq), F32)) for _ in range(n_heads))
        sts = scores(order[0])
        for n, t in enumerate(order):
            nxt = scores(order[n + 1]) if n + 1 < len(order) else None
            carry = softmax_pv(sts, t, carry, diag=(n == 0))
            sts = nxt
        outs = []
        for hh in range(n_heads):
            _, l, acc = carry[hh]
            o = acc / l
            outs.append(o * lax.rsqrt(jnp.mean(o * o, axis=0, keepdims=True) + NORM_EPS))
        o_ref[...] = (jnp.concatenate(outs, axis=0).T * gh_ref[...]).astype(BF16)

    for n_past in range(S // tk):
        pl.when(last == n_past)(functools.partial(attend, n_past))


def _moba_attention(proj, gh, B, S):
    tq = 2 * MOBA_BLOCK
    tk = 2 * MOBA_BLOCK
    assert S % tk == 0
    nq = S // tq
    n_blk = S // MOBA_BLOCK
    assert n_blk <= 8
    topk = min(MOBA_TOPK, max(n_blk - 1, 1))
    n_pairs = W_MOBA // LANES
    n_heads = LANES // HEAD_DIM
    return pl.pallas_call(
        functools.partial(_moba_kernel, tq=tq, tk=tk, topk=topk),
        grid=(B, n_pairs, nq),
        in_specs=[pl.BlockSpec((tq, LANES), lambda b, p, i: (b * nq + i, C_QB // LANES + p)),
                  pl.BlockSpec((S, LANES), lambda b, p, i: (b, C_KB // LANES + p)),
                  pl.BlockSpec((S, LANES), lambda b, p, i: (b, C_VB // LANES + p)),
                  pl.BlockSpec((1, LANES), lambda b, p, i: (0, p))],
        out_specs=pl.BlockSpec((tq, LANES), lambda b, p, i: (b * nq + i, p)),
        out_shape=jax.ShapeDtypeStruct((B * S, W_MOBA), BF16),
        scratch_shapes=[pltpu.VMEM((n_heads, S, LANES), BF16),
                        pltpu.VMEM((n_heads, S, LANES), BF16),
                        pltpu.VMEM((n_heads, LANES, LANES), BF16),
                        pltpu.VMEM((n_heads, LANES, LANES), BF16)],
        compiler_params=pltpu.CompilerParams(
            dimension_semantics=("arbitrary", "arbitrary", "arbitrary"),
            vmem_limit_bytes=VMEM_LIMIT),
        name="moba_attn",
    )(proj, proj, proj, gh)


def _gmlp_kernel(u_ref, v_ref, gv_ref, ws_ref, b_ref, gh_ref, o_ref, *, tm):
    gu = jax.nn.gelu(u_ref[...].astype(F32))
    gv = jax.nn.gelu(v_ref[...].astype(F32))
    row = lax.broadcasted_iota(jnp.int32, (GMLP_CHUNK, GMLP_CHUNK), 0)
    col = lax.broadcasted_iota(jnp.int32, (GMLP_CHUNK, GMLP_CHUNK), 1)
    outs = []
    for g in range(N_GROUPS_GMLP):
        c0, c1 = g * HEAD_DIM, (g + 1) * HEAD_DIM
        vn = _rms(gv[:, c0:c1], gv_ref[:, c0:c1]).astype(BF16)
        wm = jnp.where(col <= row, ws_ref[g], 0.0).astype(BF16)
        bias = b_ref[:, c0:c1]
        mixed = jnp.concatenate(
            [_dot(wm, vn[c * GMLP_CHUNK:(c + 1) * GMLP_CHUNK]) + bias
             for c in range(tm // GMLP_CHUNK)], axis=0)
        outs.append(_rms(gu[:, c0:c1] * mixed, gh_ref[:, c0:c1]))
    o_ref[...] = jnp.concatenate(outs, axis=1).astype(BF16)


def _gmlp(proj, gv, ws, b_exp, gh, tm):
    T = proj.shape[0]
    nu, nv = C_UC // W_GMLP, C_VC // W_GMLP
    return pl.pallas_call(
        functools.partial(_gmlp_kernel, tm=tm),
        grid=(T // tm,),
        in_specs=[pl.BlockSpec((tm, W_GMLP), lambda i: (i, nu)),
                  pl.BlockSpec((tm, W_GMLP), lambda i: (i, nv)),
                  pl.BlockSpec((1, W_GMLP), lambda i: (0, 0)),
                  pl.BlockSpec((N_GROUPS_GMLP, GMLP_CHUNK, GMLP_CHUNK), lambda i: (0, 0, 0)),
                  pl.BlockSpec((GMLP_CHUNK, W_GMLP), lambda i: (0, 0)),
                  pl.BlockSpec((1, W_GMLP), lambda i: (0, 0))],
        out_specs=pl.BlockSpec((tm, W_GMLP), lambda i: (i, 0)),
        out_shape=jax.ShapeDtypeStruct((T, W_GMLP), BF16),
        compiler_params=pltpu.CompilerParams(dimension_semantics=("arbitrary",),
                                             vmem_limit_bytes=VMEM_LIMIT),
        name="gmlp",
    )(proj, proj, gv, ws, b_exp, gh)


def _out_kernel(osb_ref, omoba_ref, ogmlp_ref, x_ref, w_ref, g_ref, rhi_ref, rlo_ref,
                x1_ref, h_ref, lg_ref):
    x1 = (x_ref[...]
          + _dot(osb_ref[...], w_ref[0:W_SB, :])
          + _dot(omoba_ref[...], w_ref[W_SB:W_SB + W_MOBA, :])
          + _dot(ogmlp_ref[...], w_ref[W_SB + W_MOBA:, :]))
    x1_ref[...] = x1
    hn = _rms(x1, g_ref[...])
    hi, lo = _split_bf16(hn)
    h_ref[...] = _pack_halves(hn)
    lg_ref[...] = (_nt_dot(rhi_ref[...], hi) + _nt_dot(rhi_ref[...], lo)
                   + _nt_dot(rlo_ref[...], hi))


def _out_proj(o_sb, o_moba, o_gmlp, x2d, w_bf16, g, r_hi, r_lo, tm):
    T, D = x2d.shape
    row = lambda i: (i, 0)
    const = lambda i: (0, 0)
    return pl.pallas_call(
        _out_kernel,
        grid=(T // tm,),
        in_specs=[pl.BlockSpec((tm, W_SB), row), pl.BlockSpec((tm, W_MOBA), row),
                  pl.BlockSpec((tm, W_GMLP), row), pl.BlockSpec((tm, D), row),
                  pl.BlockSpec((D, D), const), pl.BlockSpec((1, D), const),
                  pl.BlockSpec((ROUTER_ROWS, D), const), pl.BlockSpec((ROUTER_ROWS, D), const)],
        out_specs=[pl.BlockSpec((tm, D), row), pl.BlockSpec((tm, D // 2), row),
                   pl.BlockSpec((ROUTER_ROWS, tm), lambda i: (0, i))],
        out_shape=[jax.ShapeDtypeStruct((T, D), F32), jax.ShapeDtypeStruct((T, D // 2), jnp.uint32),
                   jax.ShapeDtypeStruct((ROUTER_ROWS, T), F32)],
        compiler_params=pltpu.CompilerParams(dimension_semantics=("arbitrary",),
                                             vmem_limit_bytes=VMEM_LIMIT),
        name="outproj",
    )(o_sb, o_moba, o_gmlp, x2d, w_bf16, g, r_hi, r_lo)


ROUTER_EXPERT_ROW = 8


def _first_max(p):
    rows = lax.broadcasted_iota(jnp.int32, p.shape, 0).astype(F32)
    top = jnp.max(p, axis=0, keepdims=True)
    idx = jnp.min(jnp.where(p == top, rows, float(p.shape[0])), axis=0, keepdims=True)
    return top, idx, rows


def _router_kernel(lg_ref, pos_ref, gw_ref, cnt_ref, cnt_acc, base_ref, *, tm):
    phase = pl.program_id(0)
    i = pl.program_id(1)
    lg = lg_ref[...]
    gl = lg[0:N_EXPERT_GROUPS]
    ge = jnp.exp(gl - jnp.max(gl, axis=0, keepdims=True))
    p_group = ge / jnp.sum(ge, axis=0, keepdims=True)
    p_g, g_sel, _ = _first_max(p_group)
    le = jnp.zeros((EXPERTS_PER_GROUP, tm), F32)
    for g in range(N_EXPERT_GROUPS):
        r0 = ROUTER_EXPERT_ROW + g * EXPERTS_PER_GROUP
        le = jnp.where(g_sel == g, lg[r0:r0 + EXPERTS_PER_GROUP], le)
    ee = jnp.exp(le - jnp.max(le, axis=0, keepdims=True))
    p = ee / jnp.sum(ee, axis=0, keepdims=True)
    p0, i0, rows = _first_max(p)
    p1, i1, _ = _first_max(jnp.where(rows == i0, -1.0, p))
    e0 = g_sel * EXPERTS_PER_GROUP + i0
    e1 = g_sel * EXPERTS_PER_GROUP + i1
    xrow = lax.broadcasted_iota(jnp.int32, (N_EXPERTS, tm), 0).astype(F32)
    oh0 = xrow == e0
    oh1 = xrow == e1
    slots = jnp.where(oh0, 1.0, 0.0) + jnp.where(oh1, 1.0, 0.0)
    tile_cnt = slots[:, 0:LANES]
    for c in range(1, tm // LANES):
        tile_cnt = tile_cnt + slots[:, c * LANES:(c + 1) * LANES]

    @pl.when((phase == 0) & (i == 0))
    def _init():
        cnt_acc[...] = jnp.zeros_like(cnt_acc)

    @pl.when(phase == 0)
    def _count():
        cnt_acc[...] += tile_cnt

    @pl.when((phase == 1) & (i == 0))
    def _starts():
        counts = jnp.sum(cnt_acc[...], axis=1, keepdims=True)
        n_blk = jnp.floor((counts + (MOE_BLOCK - 1)) * (1.0 / MOE_BLOCK))
        er = lax.broadcasted_iota(jnp.int32, (N_EXPERTS, N_EXPERTS), 0)
        ec = lax.broadcasted_iota(jnp.int32, (N_EXPERTS, N_EXPERTS), 1)
        before = jnp.where(ec < er, 1.0, 0.0).astype(BF16)
        start_blk = _dot(before, jnp.broadcast_to(n_blk, (N_EXPERTS, LANES)).astype(BF16))
        base_ref[...] = start_blk * MOE_BLOCK
        cnt_ref[...] = jnp.broadcast_to(counts, (N_EXPERTS, LANES)).astype(jnp.int32)

    @pl.when(phase == 1)
    def _assign():
        tr = lax.broadcasted_iota(jnp.int32, (tm, tm), 0)
        tc = lax.broadcasted_iota(jnp.int32, (tm, tm), 1)
        earlier = jnp.where(tr < tc, 1.0, 0.0).astype(BF16)
        row_of = _dot(slots.astype(BF16), earlier) + base_ref[:, 0:1]
        pos0 = jnp.sum(jnp.where(oh0, row_of, 0.0), axis=0, keepdims=True)
        pos1 = jnp.sum(jnp.where(oh1, row_of, 0.0), axis=0, keepdims=True)
        pos_ref[...] = jnp.concatenate([pos0, pos1], axis=0).astype(jnp.int32)
        scale = p_g / (p0 + p1)
        gw_ref[...] = jnp.concatenate([p0 * scale, p1 * scale], axis=0)
        base_ref[...] += jnp.sum(tile_cnt, axis=1, keepdims=True)


def _router(logits_t, tm):
    T = logits_t.shape[1]
    tok = lambda p, i: (0, i * p)
    return pl.pallas_call(
        functools.partial(_router_kernel, tm=tm),
        grid=(2, T // tm),
        in_specs=[pl.BlockSpec((ROUTER_ROWS, tm), lambda p, i: (0, i))],
        out_specs=[pl.BlockSpec((MOE_TOP_K, tm), tok), pl.BlockSpec((MOE_TOP_K, tm), tok),
                   pl.BlockSpec((N_EXPERTS, LANES), lambda p, i: (0, 0))],
        out_shape=[jax.ShapeDtypeStruct((MOE_TOP_K, T), jnp.int32),
                   jax.ShapeDtypeStruct((MOE_TOP_K, T), F32),
                   jax.ShapeDtypeStruct((N_EXPERTS, LANES), jnp.int32)],
        scratch_shapes=[pltpu.VMEM((N_EXPERTS, LANES), F32), pltpu.VMEM((N_EXPERTS, LANES), F32)],
        compiler_params=pltpu.CompilerParams(dimension_semantics=("arbitrary", "arbitrary")),
        name="router",
    )(logits_t)


def _moe_kernel(be_ref, first_ref, slot_ref, next_ref, rows_ref, na_ref, xs_ref, wg_hbm, wu_hbm,
                wd_hbm, ys_ref, wgf, wuf, wdf, wgb, wub, wdb, sem, *, layer):
    b = pl.program_id(0)
    e = be_ref[b]
    slot = slot_ref[b]
    active = b < na_ref[0]

    def fetch(expert, s):
        return (pltpu.make_async_copy(wg_hbm.at[layer, expert], wgf.at[s], sem.at[s, 0]),
                pltpu.make_async_copy(wu_hbm.at[layer, expert], wuf.at[s], sem.at[s, 1]),
                pltpu.make_async_copy(wd_hbm.at[layer, expert], wdf.at[s], sem.at[s, 2]))

    @pl.when(b == 0)
    def _first_fetch():
        for c in fetch(e, slot):
            c.start()

    @pl.when(active & (first_ref[b] == 1))
    def _load_expert():
        for c in fetch(e, slot):
            c.wait()
        wgb[...] = wgf[slot].astype(BF16)
        wub[...] = wuf[slot].astype(BF16)
        wdb[...] = wdf[slot].astype(BF16)
        nxt = next_ref[b]

        @pl.when(nxt >= 0)
        def _prefetch():
            for c in fetch(nxt, 1 - slot):
                c.start()

    @pl.when(active)
    def _compute():
        row = lax.broadcasted_iota(jnp.int32, xs_ref.shape, 0)
        words = jnp.where(row < rows_ref[b], xs_ref[...], jnp.uint32(0))
        xb = _unpack_halves(words).astype(BF16)
        a = jax.nn.silu(_dot(xb, wgb[...])) * _dot(xb, wub[...])
        ys_ref[...] = _pack_halves(_dot(a.astype(BF16), wdb[...]))

    @pl.when(jnp.logical_not(active))
    def _pad():
        ys_ref[...] = jnp.zeros_like(ys_ref)


def _moe_experts(plan, xs, w_gate, w_up, w_down, layer):
    n_rows = xs.shape[0]
    D = 2 * xs.shape[1]
    n_blocks = n_rows // MOE_BLOCK
    DE = w_gate.shape[-1]
    rows = lambda b, *_: (b, 0)
    grid_spec = pltpu.PrefetchScalarGridSpec(
        num_scalar_prefetch=len(plan),
        grid=(n_blocks,),
        in_specs=[pl.BlockSpec((MOE_BLOCK, D // 2), rows),
                  pl.BlockSpec(memory_space=pl.ANY), pl.BlockSpec(memory_space=pl.ANY),
                  pl.BlockSpec(memory_space=pl.ANY)],
        out_specs=pl.BlockSpec((MOE_BLOCK, D // 2), rows),
        scratch_shapes=[pltpu.VMEM((2, D, DE), F32), pltpu.VMEM((2, D, DE), F32),
                        pltpu.VMEM((2, DE, D), F32),
                        pltpu.VMEM((D, DE), BF16), pltpu.VMEM((D, DE), BF16),
                        pltpu.VMEM((DE, D), BF16),
                        pltpu.SemaphoreType.DMA((2, 3))])
    return pl.pallas_call(
        functools.partial(_moe_kernel, layer=layer),
        grid_spec=grid_spec,
        out_shape=jax.ShapeDtypeStruct((n_rows, D // 2), jnp.uint32),
        compiler_params=pltpu.CompilerParams(dimension_semantics=("arbitrary",),
                                             vmem_limit_bytes=VMEM_LIMIT),
        name="moe_experts",
    )(*plan, xs, w_gate, w_up, w_down)


def _final_kernel(x1_ref, yg_ref, gw_ref, g_ref, o_ref):
    o_ref[...] = _rms(_combine(x1_ref, yg_ref, gw_ref), g_ref[...])


def _final_norm(x2d, yg, gw, g, tm):
    T, D = x2d.shape
    row = lambda i: (i, 0)
    return pl.pallas_call(
        _final_kernel,
        grid=(T // tm,),
        in_specs=[pl.BlockSpec((tm, D), row),
                  pl.BlockSpec((MOE_TOP_K, tm, D // 2), lambda i: (0, i, 0)),
                  pl.BlockSpec((tm, MOE_TOP_K), row), pl.BlockSpec((1, D), lambda i: (0, 0))],
        out_specs=pl.BlockSpec((tm, D), row),
        out_shape=jax.ShapeDtypeStruct((T, D), F32),
        compiler_params=pltpu.CompilerParams(dimension_semantics=("arbitrary",)),
        name="final_norm",
    )(x2d, yg, gw, g)


SC_GATHER_ROWS = 64


def _sc_gather(table, idx):
    info = plsc.get_sparse_core_info()
    n_cores, n_workers = info.num_cores, info.num_cores * info.num_subcores
    N, W = idx.shape[0], table.shape[1]
    per_w = N // n_workers
    n_ch = per_w // SC_GATHER_ROWS
    assert per_w * n_workers == N and n_ch * SC_GATHER_ROWS == per_w
    mesh = plsc.VectorSubcoreMesh(core_axis_name="c", subcore_axis_name="s")

    @functools.partial(
        pl.kernel, mesh=mesh,
        out_type=jax.ShapeDtypeStruct((N, W), table.dtype),
        scratch_types=[pltpu.VMEM((n_ch, SC_GATHER_ROWS), jnp.int32),
                       pltpu.VMEM((2, SC_GATHER_ROWS, W), table.dtype),
                       pltpu.SemaphoreType.DMA((2,)),
                       pltpu.SemaphoreType.DMA((2,))])
    def gather_kernel(table_hbm, idx_hbm, out_hbm, idx_v, rows_v, gsem, ssem):
        wid = lax.axis_index("s") * n_cores + lax.axis_index("c")
        base = wid * per_w
        pltpu.sync_copy(idx_hbm.at[wid], idx_v)

        def fetch(c):
            return pltpu.make_async_copy(table_hbm.at[idx_v.at[c]], rows_v.at[c % 2],
                                         gsem.at[c % 2])

        def write(c):
            return pltpu.make_async_copy(
                rows_v.at[c % 2], out_hbm.at[pl.ds(base + c * SC_GATHER_ROWS, SC_GATHER_ROWS)],
                ssem.at[c % 2])

        fetch(0).start()
        for c in range(n_ch):
            if c + 1 < n_ch:
                if c >= 1:
                    write(c - 1).wait()
                fetch(c + 1).start()
            fetch(c).wait()
            write(c).start()
        if n_ch >= 2:
            write(n_ch - 2).wait()
        write(n_ch - 1).wait()

    return gather_kernel(table, idx.reshape(n_workers, n_ch, SC_GATHER_ROWS))


def _sc_scatter(rows, pos, n_out):
    info = plsc.get_sparse_core_info()
    n_cores, n_workers = info.num_cores, info.num_cores * info.num_subcores
    K, T = pos.shape
    W = rows.shape[1]
    per_w = T // n_workers
    n_ch = per_w // SC_GATHER_ROWS
    assert per_w * n_workers == T and n_ch * SC_GATHER_ROWS == per_w
    mesh = plsc.VectorSubcoreMesh(core_axis_name="c", subcore_axis_name="s")

    @functools.partial(
        pl.kernel, mesh=mesh,
        out_type=jax.ShapeDtypeStruct((n_out, W), rows.dtype),
        scratch_types=[pltpu.VMEM((K, n_ch, SC_GATHER_ROWS), jnp.int32),
                       pltpu.VMEM((2, SC_GATHER_ROWS, W), rows.dtype),
                       pltpu.SemaphoreType.DMA((2,)),
                       pltpu.SemaphoreType.DMA((2, K))])
    def scatter_kernel(rows_hbm, idx_hbm, out_hbm, idx_v, buf_v, lsem, ssem):
        wid = lax.axis_index("s") * n_cores + lax.axis_index("c")
        base = wid * per_w
        pltpu.sync_copy(idx_hbm.at[wid], idx_v)

        def load(c):
            return pltpu.make_async_copy(
                rows_hbm.at[pl.ds(base + c * SC_GATHER_ROWS, SC_GATHER_ROWS)], buf_v.at[c % 2],
                lsem.at[c % 2])

        def store(c, k):
            return pltpu.make_async_copy(buf_v.at[c % 2], out_hbm.at[idx_v.at[k, c]],
                                         ssem.at[c % 2, k])

        load(0).start()
        for c in range(n_ch):
            if c + 1 < n_ch:
                if c >= 1:
                    for k in range(K):
                        store(c - 1, k).wait()
                load(c + 1).start()
            load(c).wait()
            for k in range(K):
                store(c, k).start()
        for c in range(max(n_ch - 2, 0), n_ch):
            for k in range(K):
                store(c, k).wait()

    idx = pos.reshape(K, n_workers, n_ch, SC_GATHER_ROWS).transpose(1, 0, 2, 3)
    return scatter_kernel(rows, idx)


def _rope_tables(positions):
    half = ROT_DIM // 2
    inv_freq = jnp.power(ROPE_THETA, -jnp.arange(half, dtype=F32) / half)
    ang = positions.astype(F32).reshape(-1)[:, None] * inv_freq
    cos, sin = jnp.cos(ang), jnp.sin(ang)
    T = ang.shape[0]
    ones = jnp.ones((T, HEAD_DIM - ROT_DIM), F32)
    zeros = jnp.zeros((T, HEAD_DIM - ROT_DIM), F32)
    z8 = jnp.zeros((T, half), F32)
    rc = jnp.concatenate([cos, cos, ones], axis=1)
    rs1 = jnp.concatenate([-sin, z8, zeros], axis=1)
    rs2 = jnp.concatenate([z8, sin, zeros], axis=1)
    rep = LANES // HEAD_DIM
    return jnp.tile(rc, (1, rep)), jnp.tile(rs1, (1, rep)), jnp.tile(rs2, (1, rep))


def _block_plan(counts, T):
    i32 = jnp.int32
    n_blocks = -(-(T * MOE_TOP_K) // MOE_BLOCK) + N_EXPERTS
    experts = jnp.arange(N_EXPERTS, dtype=i32)
    blocks = jnp.arange(n_blocks, dtype=i32)
    n_blk = (counts + MOE_BLOCK - 1) // MOE_BLOCK
    blocks_end = jnp.cumsum(n_blk)
    blk_expert = jnp.minimum(jnp.sum(blocks_end[None, :] <= blocks[:, None], axis=1),
                             N_EXPERTS - 1).astype(i32)
    is_active = n_blk > 0
    ordinal = jnp.cumsum(is_active.astype(i32)) - 1
    later_active = is_active[None, :] & (experts[None, :] > experts[:, None])
    next_active = jnp.min(jnp.where(later_active, experts[None, :], N_EXPERTS), axis=1)
    next_active = jnp.where(next_active < N_EXPERTS, next_active, -1).astype(i32)
    blk_in_expert = blocks - (blocks_end - n_blk)[blk_expert]
    blk_first = (blk_in_expert == 0).astype(i32)
    blk_rows = jnp.clip(counts[blk_expert] - blk_in_expert * MOE_BLOCK, 0, MOE_BLOCK).astype(i32)
    plan = (blk_expert, blk_first, (ordinal[blk_expert] % 2).astype(i32),
            next_active[blk_expert], blk_rows, blocks_end[-1:].astype(i32))
    return n_blocks, plan


def kernel(x, positions, w_in, w_out, g_mix_norm, g_head_norm, g_gmlp_vnorm, w_spatial, b_spatial,
           g_ffn_norm, w_router_group, w_router_expert, w_expert_gate, w_expert_up, w_expert_down,
           g_final):
    B, S, D = x.shape
    T = B * S
    depth = w_in.shape[0]
    tm = min(512, T)
    rc, rs1, rs2 = _rope_tables(positions)
    xc = x.reshape(T, D)
    moe_out = None
    for l in range(depth):
        gh = g_head_norm[l].reshape(1, -1)
        proj, xc = _inproj(xc, moe_out, g_mix_norm[l].reshape(1, D), w_in[l].astype(BF16),
                           rc, rs1, rs2, tm)
        o_sb = _sb_attention(proj, gh[:, :W_SB], B, S)
        o_moba = _moba_attention(proj, gh[:, W_SB:W_SB + W_MOBA], B, S)
        b_exp = jnp.repeat(b_spatial[l].T, HEAD_DIM, axis=1)
        o_gmlp = _gmlp(proj, g_gmlp_vnorm[l].reshape(1, -1), w_spatial[l], b_exp,
                       gh[:, W_SB + W_MOBA:], tm)
        w_r = jnp.concatenate(
            [w_router_group[l].T, jnp.zeros((ROUTER_EXPERT_ROW - N_EXPERT_GROUPS, D), F32),
             w_router_expert[l].T,
             jnp.zeros((ROUTER_ROWS - ROUTER_EXPERT_ROW - N_EXPERTS, D), F32)], axis=0)
        r_hi, r_lo = _split_bf16(w_r)
        xc, h, logits_t = _out_proj(o_sb, o_moba, o_gmlp, xc, w_out[l].astype(BF16),
                                    g_ffn_norm[l].reshape(1, D), r_hi, r_lo, tm)
        pos, gate_w, counts = _router(logits_t, tm)
        n_blocks, plan = _block_plan(counts[:, 0], T)
        xs = _sc_scatter(h, pos, n_blocks * MOE_BLOCK)
        ys = _moe_experts(plan, xs, w_expert_gate, w_expert_up, w_expert_down, l)
        yg = _sc_gather(ys, pos.reshape(-1)).reshape(MOE_TOP_K, T, D // 2)
        moe_out = (yg, gate_w.T)
    return _final_norm(xc, moe_out[0], moe_out[1], g_final.reshape(1, D), tm).reshape(B, S, D)
```

```python
import collections
import functools

import jax
import jax.numpy as jnp
from jax import lax
from jax.experimental import pallas as pl
from jax.experimental.pallas import tpu as pltpu
from jax.experimental.pallas import tpu_sc as plsc

F32 = jnp.float32
BF16 = jnp.bfloat16

HEAD_DIM = 64
LANES = 128
MXU_COLS = 256
N_HEADS_SB = 4
N_HEADS_MOBA = 8
N_GROUPS_GMLP = 4
W_SB = N_HEADS_SB * HEAD_DIM
W_MOBA = N_HEADS_MOBA * HEAD_DIM
W_GMLP = N_GROUPS_GMLP * HEAD_DIM
MOBA_BLOCK = 256
MOBA_TOPK = 3
GMLP_CHUNK = 128
ROPE_THETA = 500000.0
ROT_DIM = HEAD_DIM // 4
N_EXPERT_GROUPS = 4
EXPERTS_PER_GROUP = 8
N_EXPERTS = N_EXPERT_GROUPS * EXPERTS_PER_GROUP
MOE_TOP_K = 2
MOE_BLOCK = 256
NORM_EPS = 1e-6
ATTN_SCALE = HEAD_DIM ** -0.5
NEG = -1e30
LOG2E = 1.4426950408889634
ROUTER_ROWS = 64
VMEM_LIMIT = 48 * 1024 * 1024

C_QA, C_KA, C_VA = 0, W_SB, 2 * W_SB
C_QB = 3 * W_SB
C_KB = C_QB + W_MOBA
C_VB = C_KB + W_MOBA
C_UC = C_VB + W_MOBA
C_VC = C_UC + W_GMLP
IN_COLS = C_VC + W_GMLP


def _nt_dot(a, b):
    return lax.dot_general(a, b, (((1,), (1,)), ((), ())), preferred_element_type=F32)


def _dot(a, b):
    return jnp.dot(a, b, preferred_element_type=F32)


def _rms(x, g):
    return x * lax.rsqrt(jnp.mean(x * x, axis=-1, keepdims=True) + NORM_EPS) * g


def _pack_halves(x):
    w = x.shape[1] // 2
    lo = lax.bitcast_convert_type(x[:, :w].astype(BF16).astype(F32), jnp.uint32)
    hi = lax.bitcast_convert_type(x[:, w:].astype(BF16).astype(F32), jnp.uint32)
    return (lo >> 16) | hi


def _unpack_halves(words):
    lo = lax.bitcast_convert_type(words << 16, F32)
    hi = lax.bitcast_convert_type(words & jnp.uint32(0xFFFF0000), F32)
    return jnp.concatenate([lo, hi], axis=1)


def _split_bf16(x):
    hi = x.astype(BF16)
    lo = (x - hi.astype(F32)).astype(BF16)
    return hi, lo


def _combine(x1_ref, yg_ref, gw_ref):
    gw = gw_ref[...]
    return (x1_ref[...] + _unpack_halves(yg_ref[0]) * gw[:, 0:1]
            + _unpack_halves(yg_ref[1]) * gw[:, 1:2])


def _inproj_kernel(*refs, combine):
    if combine:
        x1_ref, yg_ref, gw_ref, g_ref, w_ref, rc_ref, rs1_ref, rs2_ref, o_ref, x_ref = refs
        x = _combine(x1_ref, yg_ref, gw_ref)
        x_ref[...] = x
    else:
        x_ref, g_ref, w_ref, rc_ref, rs1_ref, rs2_ref, o_ref = refs
        x = x_ref[...]
    y = _rms(x, g_ref[...]).astype(BF16)
    wide = lambda t_ref: jnp.concatenate([t_ref[...]] * (MXU_COLS // LANES), axis=1)
    rc, rs1, rs2 = wide(rc_ref), wide(rs1_ref), wide(rs2_ref)
    half = ROT_DIM // 2
    for c0 in range(0, IN_COLS, MXU_COLS):
        p = _dot(y, w_ref[:, c0:c0 + MXU_COLS])
        if C_QB <= c0 < C_VB:
            p = (p * rc + pltpu.roll(p, MXU_COLS - half, axis=1) * rs1
                 + pltpu.roll(p, half, axis=1) * rs2)
        if c0 < C_KA or C_QB <= c0 < C_KB:
            p = p * (ATTN_SCALE * LOG2E)
        o_ref[:, c0:c0 + MXU_COLS] = p.astype(BF16)


def _inproj(x2d, moe_out, g, w_bf16, rc, rs1, rs2, tm):
    T, D = x2d.shape
    row = lambda i: (i, 0)
    const = lambda i: (0, 0)
    combine = moe_out is not None
    x_specs = [pl.BlockSpec((tm, D), row)]
    x_args = [x2d]
    out_specs = [pl.BlockSpec((tm, IN_COLS), row)]
    out_shape = [jax.ShapeDtypeStruct((T, IN_COLS), BF16)]
    if combine:
        yg, gw = moe_out
        x_specs += [pl.BlockSpec((MOE_TOP_K, tm, D // 2), lambda i: (0, i, 0)),
                    pl.BlockSpec((tm, MOE_TOP_K), row)]
        x_args += [yg, gw]
        out_specs.append(pl.BlockSpec((tm, D), row))
        out_shape.append(jax.ShapeDtypeStruct((T, D), F32))
    outs = pl.pallas_call(
        functools.partial(_inproj_kernel, combine=combine),
        grid=(T // tm,),
        in_specs=x_specs + [pl.BlockSpec((1, D), const), pl.BlockSpec((D, IN_COLS), const),
                            pl.BlockSpec((tm, LANES), row), pl.BlockSpec((tm, LANES), row),
                            pl.BlockSpec((tm, LANES), row)],
        out_specs=out_specs,
        out_shape=out_shape,
        compiler_params=pltpu.CompilerParams(dimension_semantics=("arbitrary",),
                                             vmem_limit_bytes=VMEM_LIMIT),
        name="inproj",
    )(*x_args, g, w_bf16, rc, rs1, rs2)
    return (outs[0], outs[1]) if combine else (outs[0], x2d)


_TileUnit = collections.namedtuple("_TileUnit", "order scores step init final")


def _run_tiles(units, n_past):
    orders = [u.order(n_past) for u in units]
    carries = [u.init() for u in units]
    zs = [u.scores(o[0], True) for u, o in zip(units, orders)]
    for n in range(n_past + 1):
        nxt = [u.scores(o[n + 1], False) if n < n_past else None
               for u, o in zip(units, orders)]
        carries = [u.step(z, o[n], c, n == 0) for u, o, z, c in zip(units, orders, zs, carries)]
        zs = nxt
    return carries


def _sb_unit(q_ref, k_ref, v_ref, gh_ref, i, *, tq, tk, sub):
    n_heads = LANES // HEAD_DIM
    n_sub = tk // sub
    qt = q_ref[...].astype(F32).T
    zeros_t = jnp.zeros((HEAD_DIM, tq), F32)
    qz = [jnp.concatenate([qt[hh * HEAD_DIM:(hh + 1) * HEAD_DIM] if h2 == hh else zeros_t
                           for h2 in range(n_heads)], axis=0).astype(BF16)
          for hh in range(n_heads)]
    srow = lax.broadcasted_iota(jnp.int32, (sub, sub), 0)
    scol = lax.broadcasted_iota(jnp.int32, (sub, sub), 1)
    later = jnp.where(scol > srow, 1.0, 0.0).astype(BF16)
    assert tq == tk

    def log_rest(z):
        nz = -z
        return jnp.minimum(nz, 0.0) - jnp.log2(1.0 + jnp.exp2(jnp.minimum(z, nz)))

    def scores(t, diag):
        if not diag:
            return [_dot(k_ref[t * tk:(t + 1) * tk, :], qz[hh]) for hh in range(n_heads)]
        return [[_dot(k_ref[t * tk + sb * sub:t * tk + (sb + 1) * sub, :], qz[hh][:, sb * sub:])
                 for sb in range(n_sub)] for hh in range(n_heads)]

    def past_step(zs, t, carry):
        wts, runs = [], []
        for hh in range(n_heads):
            z = zs[hh]
            lr = log_rest(z)
            lrb = lr.astype(BF16)
            tail = carry[hh][0]
            afters = [None] * n_sub
            for sb in reversed(range(n_sub)):
                afters[sb] = _dot(later, lrb[sb * sub:(sb + 1) * sub]) + tail
                tail = afters[sb][0:1] + lr[sb * sub:sb * sub + 1]
            wts.append(jnp.exp2(z + lr + jnp.concatenate(afters, axis=0)).astype(BF16))
            runs.append(tail)
        new = []
        for hh in range(n_heads):
            pv = lax.dot_general(v_ref[t * tk:(t + 1) * tk, :], wts[hh],
                                 (((0,), (0,)), ((), ())), preferred_element_type=F32)
            new.append((runs[hh], carry[hh][1] + pv[hh * HEAD_DIM:(hh + 1) * HEAD_DIM]))
        return tuple(new)

    def diag_step(zs, t, carry):
        new = []
        for hh in range(n_heads):
            tail, acc = carry[hh]
            for sb in reversed(range(n_sub)):
                off = sb * sub
                z = zs[hh][sb]
                krow = lax.broadcasted_iota(jnp.int32, z.shape, 0)
                qcol = lax.broadcasted_iota(jnp.int32, z.shape, 1)
                causal = krow < qcol
                lr = jnp.where(causal, log_rest(z), 0.0)
                after = _dot(later, lr.astype(BF16)) + tail[:, off:]
                w = jnp.where(causal, jnp.exp2(z + lr + after), 0.0).astype(BF16)
                run = after[0:1] + lr[0:1]
                pv = lax.dot_general(v_ref[t * tk + off:t * tk + off + sub, :], w,
                                     (((0,), (0,)), ((), ())), preferred_element_type=F32)
                pv = pv[hh * HEAD_DIM:(hh + 1) * HEAD_DIM]
                if off:
                    run = jnp.concatenate([tail[:, :off], run], axis=1)
                    pv = jnp.concatenate([jnp.zeros((HEAD_DIM, off), F32), pv], axis=1)
                tail, acc = run, acc + pv
            new.append((tail, acc))
        return tuple(new)

    def weights_pv(zs, t, carry, diag):
        return diag_step(zs, t, carry) if diag else past_step(zs, t, carry)

    def init():
        return tuple((jnp.zeros((1, tq), F32), jnp.zeros((HEAD_DIM, tq), F32))
                     for _ in range(n_heads))

    def final(carry):
        outs = []
        for hh in range(n_heads):
            o = carry[hh][1]
            outs.append(o * lax.rsqrt(jnp.mean(o * o, axis=0, keepdims=True) + NORM_EPS))
        return (jnp.concatenate(outs, axis=0).T * gh_ref[...]).astype(BF16)

    return _TileUnit(order=lambda n_past: list(range(n_past, -1, -1)), scores=scores,
                     step=weights_pv, init=init, final=final)


def _moba_unit(q_ref, k_ref, v_ref, gh_ref, kaug_ref, vaug_ref, rhi_ref, rlo_ref, i,
               *, tq, tk, topk):
    S = k_ref.shape[0]
    n_heads = q_ref.shape[1] // HEAD_DIM

    @pl.when(i == 0)
    def _prepare_keys():
        kp = k_ref[...]
        vp = v_ref[...]
        srow = lax.broadcasted_iota(jnp.int32, (S, HEAD_DIM), 0)
        scol = lax.broadcasted_iota(jnp.int32, (S, HEAD_DIM), 1)
        onehot = jnp.where(srow // MOBA_BLOCK == scol, 1.0, 0.0).astype(BF16)
        arow = lax.broadcasted_iota(jnp.int32, (LANES, S), 0) - HEAD_DIM
        acol = lax.broadcasted_iota(jnp.int32, (LANES, S), 1) // MOBA_BLOCK
        avg = jnp.where(arow == acol, 1.0 / MOBA_BLOCK, 0.0).astype(BF16)
        zeros = jnp.zeros((S, HEAD_DIM), BF16)
        ones = jnp.ones((S, HEAD_DIM), BF16)
        for hh in range(n_heads):
            kh = kp[:, hh * HEAD_DIM:(hh + 1) * HEAD_DIM]
            kaug_ref[hh] = jnp.concatenate([kh, onehot], axis=1)
            vaug_ref[hh] = jnp.concatenate([vp[:, hh * HEAD_DIM:(hh + 1) * HEAD_DIM], ones], axis=1)
            kmean = _dot(avg, jnp.concatenate([kh, zeros], axis=1))
            hi, lo = _split_bf16(kmean)
            rhi_ref[hh] = hi
            rlo_ref[hh] = lo

    qt = q_ref[...].astype(F32).T
    group = 8
    bidx = lax.broadcasted_iota(jnp.int32, (group, tq), 0)
    zeros_t = jnp.zeros((HEAD_DIM, tq), F32)
    own = (i * tq + lax.broadcasted_iota(jnp.int32, (group, tq), 1)) // MOBA_BLOCK
    qaugs = []
    for hh in range(n_heads):
        qh = qt[hh * HEAD_DIM:(hh + 1) * HEAD_DIM]
        qz = jnp.concatenate([qh, zeros_t], axis=0).astype(BF16)
        gate = (_dot(rhi_ref[hh], qz) + _dot(rlo_ref[hh], qz))[HEAD_DIM:HEAD_DIM + group]
        valid = bidx < own
        gm = jnp.where(valid, gate, NEG)
        rank = jnp.zeros((group, tq), F32)
        for d in range(1, group):
            nb = pltpu.roll(gm, d, axis=0)
            rank = rank + jnp.where(bidx >= d, jnp.where(nb >= gm, 1.0, 0.0),
                                    jnp.where(nb > gm, 1.0, 0.0))
        allowed = (valid & (rank < topk)) | (bidx == own)
        bias = jnp.where(allowed, 0.0, NEG)
        qaugs.append(jnp.concatenate(
            [qh, bias, jnp.zeros((HEAD_DIM - group, tq), F32)], axis=0).astype(BF16))

    assert tq == tk
    n_sub = tk // MOBA_BLOCK

    def scores(t, diag):
        if not diag:
            return [_dot(kaug_ref[hh, t * tk:(t + 1) * tk, :], qaugs[hh])
                    for hh in range(n_heads)]
        return [[_dot(kaug_ref[hh, t * tk + sb * MOBA_BLOCK:t * tk + (sb + 1) * MOBA_BLOCK, :],
                      qaugs[hh][:, sb * MOBA_BLOCK:]) for sb in range(n_sub)]
                for hh in range(n_heads)]

    def past_step(sts, t, carry):
        pts, stats = [], []
        for hh in range(n_heads):
            m = carry[hh][0]
            m_new = jnp.maximum(m, jnp.max(sts[hh], axis=0, keepdims=True))
            pts.append(jnp.exp2(sts[hh] - m_new).astype(BF16))
            stats.append((m_new, jnp.exp2(m - m_new)))
        new = []
        for hh in range(n_heads):
            _, l, acc = carry[hh]
            m_new, alpha = stats[hh]
            pv = lax.dot_general(vaug_ref[hh, t * tk:(t + 1) * tk, :], pts[hh],
                                 (((0,), (0,)), ((), ())), preferred_element_type=F32)
            new.append((m_new, alpha * l + pv[HEAD_DIM:HEAD_DIM + 1],
                        alpha * acc + pv[:HEAD_DIM]))
        return tuple(new)

    def diag_step(sts, t, carry):
        new = []
        for hh in range(n_heads):
            m, l, acc = carry[hh]
            masked = []
            m_new = m
            for sb in range(n_sub):
                off = sb * MOBA_BLOCK
                st = sts[hh][sb]
                krow = lax.broadcasted_iota(jnp.int32, st.shape, 0)
                qcol = lax.broadcasted_iota(jnp.int32, st.shape, 1)
                st = jnp.where(krow <= qcol, st, NEG)
                masked.append(st)
                top = jnp.max(st, axis=0, keepdims=True)
                if off:
                    top = jnp.concatenate([jnp.full((1, off), NEG, F32), top], axis=1)
                m_new = jnp.maximum(m_new, top)
            alpha = jnp.exp2(m - m_new)
            pv = None
            for sb in range(n_sub):
                off = sb * MOBA_BLOCK
                pt = jnp.exp2(masked[sb] - m_new[:, off:]).astype(BF16)
                part = lax.dot_general(
                    vaug_ref[hh, t * tk + off:t * tk + off + MOBA_BLOCK, :], pt,
                    (((0,), (0,)), ((), ())), preferred_element_type=F32)
                if off:
                    part = jnp.concatenate([jnp.zeros((LANES, off), F32), part], axis=1)
                pv = part if pv is None else pv + part
            new.append((m_new, alpha * l + pv[HEAD_DIM:HEAD_DIM + 1],
                        alpha * acc + pv[:HEAD_DIM]))
        return tuple(new)

    def softmax_pv(sts, t, carry, diag):
        return diag_step(sts, t, carry) if diag else past_step(sts, t, carry)

    def init():
        return tuple((jnp.full((1, tq), NEG, F32), jnp.zeros((1, tq), F32),
                      jnp.zeros((HEAD_DIM, tq), F32)) for _ in range(n_heads))

    def final(carry):
        outs = []
        for hh in range(n_heads):
            _, l, acc = carry[hh]
            o = acc / l
            outs.append(o * lax.rsqrt(jnp.mean(o * o, axis=0, keepdims=True) + NORM_EPS))
        return (jnp.concatenate(outs, axis=0).T * gh_ref[...]).astype(BF16)

    return _TileUnit(order=lambda n_past: [n_past] + list(range(n_past)), scores=scores,
                     step=softmax_pv, init=init, final=final)


ATTN_TQ = 2 * MOBA_BLOCK
MOBA_COLS = 2 * LANES


def _attn_kernel(qa_ref, ka_ref, va_ref, qb_ref, kb_ref, vb_ref, gha_ref, ghb_ref,
                 oa_ref, ob_ref, kaug_ref, vaug_ref, rhi_ref, rlo_ref, *, tq, topk):
    i = pl.program_id(2)
    S = ka_ref.shape[0]
    sb = _sb_unit(qa_ref, ka_ref, va_ref, gha_ref, i, tq=tq, tk=tq, sub=MOBA_BLOCK)
    mb = _moba_unit(qb_ref, kb_ref, vb_ref, ghb_ref, kaug_ref, vaug_ref, rhi_ref, rlo_ref, i,
                    tq=tq, tk=tq, topk=topk)

    def attend(n_past):
        ca, cb = _run_tiles([sb, mb], n_past)
        oa_ref[...] = sb.final(ca)
        ob_ref[...] = mb.final(cb)

    for n_past in range(S // tq):
        pl.when(i == n_past)(functools.partial(attend, n_past))


def _attention(proj, gh_sb, gh_moba, B, S):
    tq = ATTN_TQ
    assert S % tq == 0 and W_MOBA // MOBA_COLS == W_SB // LANES
    nq = S // tq
    n_blk = S // MOBA_BLOCK
    assert n_blk <= 8
    topk = min(MOBA_TOPK, max(n_blk - 1, 1))
    n_mb_heads = MOBA_COLS // HEAD_DIM
    q_row = lambda b, p, i: b * nq + i
    return pl.pallas_call(
        functools.partial(_attn_kernel, tq=tq, topk=topk),
        grid=(B, W_SB // LANES, nq),
        in_specs=[pl.BlockSpec((tq, LANES), lambda b, p, i: (q_row(b, p, i), C_QA // LANES + p)),
                  pl.BlockSpec((S, LANES), lambda b, p, i: (b, C_KA // LANES + p)),
                  pl.BlockSpec((S, LANES), lambda b, p, i: (b, C_VA // LANES + p)),
                  pl.BlockSpec((tq, MOBA_COLS),
                               lambda b, p, i: (q_row(b, p, i), C_QB // MOBA_COLS + p)),
                  pl.BlockSpec((S, MOBA_COLS), lambda b, p, i: (b, C_KB // MOBA_COLS + p)),
                  pl.BlockSpec((S, MOBA_COLS), lambda b, p, i: (b, C_VB // MOBA_COLS + p)),
                  pl.BlockSpec((1, LANES), lambda b, p, i: (0, p)),
                  pl.BlockSpec((1, MOBA_COLS), lambda b, p, i: (0, p))],
        out_specs=[pl.BlockSpec((tq, LANES), lambda b, p, i: (q_row(b, p, i), p)),
                   pl.BlockSpec((tq, MOBA_COLS), lambda b, p, i: (q_row(b, p, i), p))],
        out_shape=[jax.ShapeDtypeStruct((B * S, W_SB), BF16),
                   jax.ShapeDtypeStruct((B * S, W_MOBA), BF16)],
        scratch_shapes=[pltpu.VMEM((n_mb_heads, S, LANES), BF16),
                        pltpu.VMEM((n_mb_heads, S, LANES), BF16),
                        pltpu.VMEM((n_mb_heads, LANES, LANES), BF16),
                        pltpu.VMEM((n_mb_heads, LANES, LANES), BF16)],
        compiler_params=pltpu.CompilerParams(
            dimension_semantics=("arbitrary", "arbitrary", "arbitrary"),
            vmem_limit_bytes=VMEM_LIMIT),
        name="attention",
    )(proj, proj, proj, proj, proj, proj, gh_sb, gh_moba)


def _gmlp_kernel(u_ref, v_ref, gv_ref, ws_ref, b_ref, gh_ref, o_ref, *, tm):
    gu = jax.nn.gelu(u_ref[...].astype(F32))
    gv = jax.nn.gelu(v_ref[...].astype(F32))
    row = lax.broadcasted_iota(jnp.int32, (GMLP_CHUNK, GMLP_CHUNK), 0)
    col = lax.broadcasted_iota(jnp.int32, (GMLP_CHUNK, GMLP_CHUNK), 1)
    outs = []
    for g in range(N_GROUPS_GMLP):
        c0, c1 = g * HEAD_DIM, (g + 1) * HEAD_DIM
        vn = _rms(gv[:, c0:c1], gv_ref[:, c0:c1]).astype(BF16)
        wm = jnp.where(col <= row, ws_ref[g], 0.0).astype(BF16)
        bias = b_ref[:, c0:c1]
        mixed = jnp.concatenate(
            [_dot(wm, vn[c * GMLP_CHUNK:(c + 1) * GMLP_CHUNK]) + bias
             for c in range(tm // GMLP_CHUNK)], axis=0)
        outs.append(_rms(gu[:, c0:c1] * mixed, gh_ref[:, c0:c1]))
    o_ref[...] = jnp.concatenate(outs, axis=1).astype(BF16)


def _gmlp(proj, gv, ws, b_exp, gh, tm):
    T = proj.shape[0]
    nu, nv = C_UC // W_GMLP, C_VC // W_GMLP
    return pl.pallas_call(
        functools.partial(_gmlp_kernel, tm=tm),
        grid=(T // tm,),
        in_specs=[pl.BlockSpec((tm, W_GMLP), lambda i: (i, nu)),
                  pl.BlockSpec((tm, W_GMLP), lambda i: (i, nv)),
                  pl.BlockSpec((1, W_GMLP), lambda i: (0, 0)),
                  pl.BlockSpec((N_GROUPS_GMLP, GMLP_CHUNK, GMLP_CHUNK), lambda i: (0, 0, 0)),
                  pl.BlockSpec((GMLP_CHUNK, W_GMLP), lambda i: (0, 0)),
                  pl.BlockSpec((1, W_GMLP), lambda i: (0, 0))],
        out_specs=pl.BlockSpec((tm, W_GMLP), lambda i: (i, 0)),
        out_shape=jax.ShapeDtypeStruct((T, W_GMLP), BF16),
        compiler_params=pltpu.CompilerParams(dimension_semantics=("arbitrary",),
                                             vmem_limit_bytes=VMEM_LIMIT),
        name="gmlp",
    )(proj, proj, gv, ws, b_exp, gh)


def _out_kernel(osb_ref, omoba_ref, ogmlp_ref, x_ref, w_ref, g_ref, rhi_ref, rlo_ref,
                x1_ref, h_ref, lg_ref):
    x1 = (x_ref[...]
          + _dot(osb_ref[...], w_ref[0:W_SB, :])
          + _dot(omoba_ref[...], w_ref[W_SB:W_SB + W_MOBA, :])
          + _dot(ogmlp_ref[...], w_ref[W_SB + W_MOBA:, :]))
    x1_ref[...] = x1
    hn = _rms(x1, g_ref[...])
    hi, lo = _split_bf16(hn)
    h_ref[...] = _pack_halves(hn)
    lg_ref[...] = (_nt_dot(rhi_ref[...], hi) + _nt_dot(rhi_ref[...], lo)
                   + _nt_dot(rlo_ref[...], hi))


def _out_proj(o_sb, o_moba, o_gmlp, x2d, w_bf16, g, r_hi, r_lo, tm):
    T, D = x2d.shape
    row = lambda i: (i, 0)
    const = lambda i: (0, 0)
    return pl.pallas_call(
        _out_kernel,
        grid=(T // tm,),
        in_specs=[pl.BlockSpec((tm, W_SB), row), pl.BlockSpec((tm, W_MOBA), row),
                  pl.BlockSpec((tm, W_GMLP), row), pl.BlockSpec((tm, D), row),
                  pl.BlockSpec((D, D), const), pl.BlockSpec((1, D), const),
                  pl.BlockSpec((ROUTER_ROWS, D), const), pl.BlockSpec((ROUTER_ROWS, D), const)],
        out_specs=[pl.BlockSpec((tm, D), row), pl.BlockSpec((tm, D // 2), row),
                   pl.BlockSpec((ROUTER_ROWS, tm), lambda i: (0, i))],
        out_shape=[jax.ShapeDtypeStruct((T, D), F32), jax.ShapeDtypeStruct((T, D // 2), jnp.uint32),
                   jax.ShapeDtypeStruct((ROUTER_ROWS, T), F32)],
        compiler_params=pltpu.CompilerParams(dimension_semantics=("arbitrary",),
                                             vmem_limit_bytes=VMEM_LIMIT),
        name="outproj",
    )(o_sb, o_moba, o_gmlp, x2d, w_bf16, g, r_hi, r_lo)


ROUTER_EXPERT_ROW = 8


def _first_max(p):
    rows = lax.broadcasted_iota(jnp.int32, p.shape, 0).astype(F32)
    top = jnp.max(p, axis=0, keepdims=True)
    idx = jnp.min(jnp.where(p == top, rows, float(p.shape[0])), axis=0, keepdims=True)
    return top, idx, rows


def _router_kernel(lg_ref, pos_ref, gw_ref, cnt_ref, cnt_acc, base_ref, *, tm):
    phase = pl.program_id(0)
    i = pl.program_id(1)
    lg = lg_ref[...]
    gl = lg[0:N_EXPERT_GROUPS]
    ge = jnp.exp(gl - jnp.max(gl, axis=0, keepdims=True))
    p_group = ge / jnp.sum(ge, axis=0, keepdims=True)
    p_g, g_sel, _ = _first_max(p_group)
    le = jnp.zeros((EXPERTS_PER_GROUP, tm), F32)
    for g in range(N_EXPERT_GROUPS):
        r0 = ROUTER_EXPERT_ROW + g * EXPERTS_PER_GROUP
        le = jnp.where(g_sel == g, lg[r0:r0 + EXPERTS_PER_GROUP], le)
    ee = jnp.exp(le - jnp.max(le, axis=0, keepdims=True))
    p = ee / jnp.sum(ee, axis=0, keepdims=True)
    p0, i0, rows = _first_max(p)
    p1, i1, _ = _first_max(jnp.where(rows == i0, -1.0, p))
    e0 = g_sel * EXPERTS_PER_GROUP + i0
    e1 = g_sel * EXPERTS_PER_GROUP + i1
    xrow = lax.broadcasted_iota(jnp.int32, (N_EXPERTS, tm), 0).astype(F32)
    oh0 = xrow == e0
    oh1 = xrow == e1
    slots = jnp.where(oh0, 1.0, 0.0) + jnp.where(oh1, 1.0, 0.0)
    tile_cnt = slots[:, 0:LANES]
    for c in range(1, tm // LANES):
        tile_cnt = tile_cnt + slots[:, c * LANES:(c + 1) * LANES]

    @pl.when((phase == 0) & (i == 0))
    def _init():
        cnt_acc[...] = jnp.zeros_like(cnt_acc)

    @pl.when(phase == 0)
    def _count():
        cnt_acc[...] += tile_cnt

    @pl.when((phase == 1) & (i == 0))
    def _starts():
        counts = jnp.sum(cnt_acc[...], axis=1, keepdims=True)
        n_blk = jnp.floor((counts + (MOE_BLOCK - 1)) * (1.0 / MOE_BLOCK))
        er = lax.broadcasted_iota(jnp.int32, (N_EXPERTS, N_EXPERTS), 0)
        ec = lax.broadcasted_iota(jnp.int32, (N_EXPERTS, N_EXPERTS), 1)
        before = jnp.where(ec < er, 1.0, 0.0).astype(BF16)
        start_blk = _dot(before, jnp.broadcast_to(n_blk, (N_EXPERTS, LANES)).astype(BF16))
        base_ref[...] = start_blk * MOE_BLOCK
        cnt_ref[...] = jnp.broadcast_to(counts, (N_EXPERTS, LANES)).astype(jnp.int32)

    @pl.when(phase == 1)
    def _assign():
        tr = lax.broadcasted_iota(jnp.int32, (tm, tm), 0)
        tc = lax.broadcasted_iota(jnp.int32, (tm, tm), 1)
        earlier = jnp.where(tr < tc, 1.0, 0.0).astype(BF16)
        row_of = _dot(slots.astype(BF16), earlier) + base_ref[:, 0:1]
        pos0 = jnp.sum(jnp.where(oh0, row_of, 0.0), axis=0, keepdims=True)
        pos1 = jnp.sum(jnp.where(oh1, row_of, 0.0), axis=0, keepdims=True)
        pos_ref[...] = jnp.concatenate([pos0, pos1], axis=0).astype(jnp.int32)
        scale = p_g / (p0 + p1)
        gw_ref[...] = jnp.concatenate([p0 * scale, p1 * scale], axis=0)
        base_ref[...] += jnp.sum(tile_cnt, axis=1, keepdims=True)


def _router(logits_t, tm):
    T = logits_t.shape[1]
    tok = lambda p, i: (0, i * p)
    return pl.pallas_call(
        functools.partial(_router_kernel, tm=tm),
        grid=(2, T // tm),
        in_specs=[pl.BlockSpec((ROUTER_ROWS, tm), lambda p, i: (0, i))],
        out_specs=[pl.BlockSpec((MOE_TOP_K, tm), tok), pl.BlockSpec((MOE_TOP_K, tm), tok),
                   pl.BlockSpec((N_EXPERTS, LANES), lambda p, i: (0, 0))],
        out_shape=[jax.ShapeDtypeStruct((MOE_TOP_K, T), jnp.int32),
                   jax.ShapeDtypeStruct((MOE_TOP_K, T), F32),
                   jax.ShapeDtypeStruct((N_EXPERTS, LANES), jnp.int32)],
        scratch_shapes=[pltpu.VMEM((N_EXPERTS, LANES), F32), pltpu.VMEM((N_EXPERTS, LANES), F32)],
        compiler_params=pltpu.CompilerParams(dimension_semantics=("arbitrary", "arbitrary")),
        name="router",
    )(logits_t)


def _moe_kernel(be_ref, first_ref, slot_ref, next_ref, rows_ref, na_ref, xs_ref, wg_hbm, wu_hbm,
                wd_hbm, ys_ref, wgf, wuf, wdf, wgb, wub, wdb, sem, *, layer):
    b = pl.program_id(0)
    e = be_ref[b]
    slot = slot_ref[b]
    active = b < na_ref[0]

    def fetch(expert, s):
        return (pltpu.make_async_copy(wg_hbm.at[layer, expert], wgf.at[s], sem.at[s, 0]),
                pltpu.make_async_copy(wu_hbm.at[layer, expert], wuf.at[s], sem.at[s, 1]),
                pltpu.make_async_copy(wd_hbm.at[layer, expert], wdf.at[s], sem.at[s, 2]))

    @pl.when(b == 0)
    def _first_fetch():
        for c in fetch(e, slot):
            c.start()

    @pl.when(active & (first_ref[b] == 1))
    def _load_expert():
        for c in fetch(e, slot):
            c.wait()
        wgb[...] = wgf[slot].astype(BF16)
        wub[...] = wuf[slot].astype(BF16)
        wdb[...] = wdf[slot].astype(BF16)
        nxt = next_ref[b]

        @pl.when(nxt >= 0)
        def _prefetch():
            for c in fetch(nxt, 1 - slot):
                c.start()

    @pl.when(active)
    def _compute():
        row = lax.broadcasted_iota(jnp.int32, xs_ref.shape, 0)
        words = jnp.where(row < rows_ref[b], xs_ref[...], jnp.uint32(0))
        xb = _unpack_halves(words).astype(BF16)
        a = jax.nn.silu(_dot(xb, wgb[...])) * _dot(xb, wub[...])
        ys_ref[...] = _pack_halves(_dot(a.astype(BF16), wdb[...]))

    @pl.when(jnp.logical_not(active))
    def _pad():
        ys_ref[...] = jnp.zeros_like(ys_ref)


def _moe_experts(plan, xs, w_gate, w_up, w_down, layer):
    n_rows = xs.shape[0]
    D = 2 * xs.shape[1]
    n_blocks = n_rows // MOE_BLOCK
    DE = w_gate.shape[-1]
    rows = lambda b, *_: (b, 0)
    grid_spec = pltpu.PrefetchScalarGridSpec(
        num_scalar_prefetch=len(plan),
        grid=(n_blocks,),
        in_specs=[pl.BlockSpec((MOE_BLOCK, D // 2), rows),
                  pl.BlockSpec(memory_space=pl.ANY), pl.BlockSpec(memory_space=pl.ANY),
                  pl.BlockSpec(memory_space=pl.ANY)],
        out_specs=pl.BlockSpec((MOE_BLOCK, D // 2), rows),
        scratch_shapes=[pltpu.VMEM((2, D, DE), F32), pltpu.VMEM((2, D, DE), F32),
                        pltpu.VMEM((2, DE, D), F32),
                        pltpu.VMEM((D, DE), BF16), pltpu.VMEM((D, DE), BF16),
                        pltpu.VMEM((DE, D), BF16),
                        pltpu.SemaphoreType.DMA((2, 3))])
    return pl.pallas_call(
        functools.partial(_moe_kernel, layer=layer),
        grid_spec=grid_spec,
        out_shape=jax.ShapeDtypeStruct((n_rows, D // 2), jnp.uint32),
        compiler_params=pltpu.CompilerParams(dimension_semantics=("arbitrary",),
                                             vmem_limit_bytes=VMEM_LIMIT),
        name="moe_experts",
    )(*plan, xs, w_gate, w_up, w_down)


def _final_kernel(x1_ref, yg_ref, gw_ref, g_ref, o_ref):
    o_ref[...] = _rms(_combine(x1_ref, yg_ref, gw_ref), g_ref[...])


def _final_norm(x2d, yg, gw, g, tm):
    T, D = x2d.shape
    row = lambda i: (i, 0)
    return pl.pallas_call(
        _final_kernel,
        grid=(T // tm,),
        in_specs=[pl.BlockSpec((tm, D), row),
                  pl.BlockSpec((MOE_TOP_K, tm, D // 2), lambda i: (0, i, 0)),
                  pl.BlockSpec((tm, MOE_TOP_K), row), pl.BlockSpec((1, D), lambda i: (0, 0))],
        out_specs=pl.BlockSpec((tm, D), row),
        out_shape=jax.ShapeDtypeStruct((T, D), F32),
        compiler_params=pltpu.CompilerParams(dimension_semantics=("arbitrary",)),
        name="final_norm",
    )(x2d, yg, gw, g)


SC_GATHER_ROWS = 64


def _sc_gather(table, idx):
    info = plsc.get_sparse_core_info()
    n_cores, n_workers = info.num_cores, info.num_cores * info.num_subcores
    N, W = idx.shape[0], table.shape[1]
    per_w = N // n_workers
    n_ch = per_w // SC_GATHER_ROWS
    assert per_w * n_workers == N and n_ch * SC_GATHER_ROWS == per_w
    mesh = plsc.VectorSubcoreMesh(core_axis_name="c", subcore_axis_name="s")

    @functools.partial(
        pl.kernel, mesh=mesh,
        out_type=jax.ShapeDtypeStruct((N, W), table.dtype),
        scratch_types=[pltpu.VMEM((n_ch, SC_GATHER_ROWS), jnp.int32),
                       pltpu.VMEM((2, SC_GATHER_ROWS, W), table.dtype),
                       pltpu.SemaphoreType.DMA((2,)),
                       pltpu.SemaphoreType.DMA((2,))])
    def gather_kernel(table_hbm, idx_hbm, out_hbm, idx_v, rows_v, gsem, ssem):
        wid = lax.axis_index("s") * n_cores + lax.axis_index("c")
        base = wid * per_w
        pltpu.sync_copy(idx_hbm.at[wid], idx_v)

        def fetch(c):
            return pltpu.make_async_copy(table_hbm.at[idx_v.at[c]], rows_v.at[c % 2],
                                         gsem.at[c % 2])

        def write(c):
            return pltpu.make_async_copy(
                rows_v.at[c % 2], out_hbm.at[pl.ds(base + c * SC_GATHER_ROWS, SC_GATHER_ROWS)],
                ssem.at[c % 2])

        fetch(0).start()
        for c in range(n_ch):
            if c + 1 < n_ch:
                if c >= 1:
                    write(c - 1).wait()
                fetch(c + 1).start()
            fetch(c).wait()
            write(c).start()
        if n_ch >= 2:
            write(n_ch - 2).wait()
        write(n_ch - 1).wait()

    return gather_kernel(table, idx.reshape(n_workers, n_ch, SC_GATHER_ROWS))


def _sc_scatter(rows, pos, n_out):
    info = plsc.get_sparse_core_info()
    n_cores, n_workers = info.num_cores, info.num_cores * info.num_subcores
    K, T = pos.shape
    W = rows.shape[1]
    per_w = T // n_workers
    n_ch = per_w // SC_GATHER_ROWS
    assert per_w * n_workers == T and n_ch * SC_GATHER_ROWS == per_w
    mesh = plsc.VectorSubcoreMesh(core_axis_name="c", subcore_axis_name="s")

    @functools.partial(
        pl.kernel, mesh=mesh,
        out_type=jax.ShapeDtypeStruct((n_out, W), rows.dtype),
        scratch_types=[pltpu.VMEM((K, n_ch, SC_GATHER_ROWS), jnp.int32),
                       pltpu.VMEM((2, SC_GATHER_ROWS, W), rows.dtype),
                       pltpu.SemaphoreType.DMA((2,)),
                       pltpu.SemaphoreType.DMA((2, K))])
    def scatter_kernel(rows_hbm, idx_hbm, out_hbm, idx_v, buf_v, lsem, ssem):
        wid = lax.axis_index("s") * n_cores + lax.axis_index("c")
        base = wid * per_w
        pltpu.sync_copy(idx_hbm.at[wid], idx_v)

        def load(c):
            return pltpu.make_async_copy(
                rows_hbm.at[pl.ds(base + c * SC_GATHER_ROWS, SC_GATHER_ROWS)], buf_v.at[c % 2],
                lsem.at[c % 2])

        def store(c, k):
            return pltpu.make_async_copy(buf_v.at[c % 2], out_hbm.at[idx_v.at[k, c]],
                                         ssem.at[c % 2, k])

        load(0).start()
        for c in range(n_ch):
            if c + 1 < n_ch:
                if c >= 1:
                    for k in range(K):
                        store(c - 1, k).wait()
                load(c + 1).start()
            load(c).wait()
            for k in range(K):
                store(c, k).start()
        for c in range(max(n_ch - 2, 0), n_ch):
            for k in range(K):
                store(c, k).wait()

    idx = pos.reshape(K, n_workers, n_ch, SC_GATHER_ROWS).transpose(1, 0, 2, 3)
    return scatter_kernel(rows, idx)


def _rope_tables(positions):
    half = ROT_DIM // 2
    inv_freq = jnp.power(ROPE_THETA, -jnp.arange(half, dtype=F32) / half)
    ang = positions.astype(F32).reshape(-1)[:, None] * inv_freq
    cos, sin = jnp.cos(ang), jnp.sin(ang)
    T = ang.shape[0]
    ones = jnp.ones((T, HEAD_DIM - ROT_DIM), F32)
    zeros = jnp.zeros((T, HEAD_DIM - ROT_DIM), F32)
    z8 = jnp.zeros((T, half), F32)
    rc = jnp.concatenate([cos, cos, ones], axis=1)
    rs1 = jnp.concatenate([-sin, z8, zeros], axis=1)
    rs2 = jnp.concatenate([z8, sin, zeros], axis=1)
    rep = LANES // HEAD_DIM
    return jnp.tile(rc, (1, rep)), jnp.tile(rs1, (1, rep)), jnp.tile(rs2, (1, rep))


def _block_plan(counts, T):
    i32 = jnp.int32
    n_blocks = -(-(T * MOE_TOP_K) // MOE_BLOCK) + N_EXPERTS
    experts = jnp.arange(N_EXPERTS, dtype=i32)
    blocks = jnp.arange(n_blocks, dtype=i32)
    n_blk = (counts + MOE_BLOCK - 1) // MOE_BLOCK
    blocks_end = jnp.cumsum(n_blk)
    blk_expert = jnp.minimum(jnp.sum(blocks_end[None, :] <= blocks[:, None], axis=1),
                             N_EXPERTS - 1).astype(i32)
    is_active = n_blk > 0
    ordinal = jnp.cumsum(is_active.astype(i32)) - 1
    later_active = is_active[None, :] & (experts[None, :] > experts[:, None])
    next_active = jnp.min(jnp.where(later_active, experts[None, :], N_EXPERTS), axis=1)
    next_active = jnp.where(next_active < N_EXPERTS, next_active, -1).astype(i32)
    blk_in_expert = blocks - (blocks_end - n_blk)[blk_expert]
    blk_first = (blk_in_expert == 0).astype(i32)
    blk_rows = jnp.clip(counts[blk_expert] - blk_in_expert * MOE_BLOCK, 0, MOE_BLOCK).astype(i32)
    plan = (blk_expert, blk_first, (ordinal[blk_expert] % 2).astype(i32),
            next_active[blk_expert], blk_rows, blocks_end[-1:].astype(i32))
    return n_blocks, plan


def kernel(x, positions, w_in, w_out, g_mix_norm, g_head_norm, g_gmlp_vnorm, w_spatial, b_spatial,
           g_ffn_norm, w_router_group, w_router_expert, w_expert_gate, w_expert_up, w_expert_down,
           g_final):
    B, S, D = x.shape
    T = B * S
    depth = w_in.shape[0]
    tm = min(512, T)
    rc, rs1, rs2 = _rope_tables(positions)
    xc = x.reshape(T, D)
    moe_out = None
    for l in range(depth):
        gh = g_head_norm[l].reshape(1, -1)
        proj, xc = _inproj(xc, moe_out, g_mix_norm[l].reshape(1, D), w_in[l].astype(BF16),
                           rc, rs1, rs2, tm)
        o_sb, o_moba = _attention(proj, gh[:, :W_SB], gh[:, W_SB:W_SB + W_MOBA], B, S)
        b_exp = jnp.repeat(b_spatial[l].T, HEAD_DIM, axis=1)
        o_gmlp = _gmlp(proj, g_gmlp_vnorm[l].reshape(1, -1), w_spatial[l], b_exp,
                       gh[:, W_SB + W_MOBA:], tm)
        w_r = jnp.concatenate(
            [w_router_group[l].T, jnp.zeros((ROUTER_EXPERT_ROW - N_EXPERT_GROUPS, D), F32),
             w_router_expert[l].T,
             jnp.zeros((ROUTER_ROWS - ROUTER_EXPERT_ROW - N_EXPERTS, D), F32)], axis=0)
        r_hi, r_lo = _split_bf16(w_r)
        xc, h, logits_t = _out_proj(o_sb, o_moba, o_gmlp, xc, w_out[l].astype(BF16),
                                    g_ffn_norm[l].reshape(1, D), r_hi, r_lo, tm)
        pos, gate_w, counts = _router(logits_t, tm)
        n_blocks, plan = _block_plan(counts[:, 0], T)
        xs = _sc_scatter(h, pos, n_blocks * MOE_BLOCK)
        ys = _moe_experts(plan, xs, w_expert_gate, w_expert_up, w_expert_down, l)
        yg = _sc_gather(ys, pos.reshape(-1)).reshape(MOE_TOP_K, T, D // 2)
        moe_out = (yg, gate_w.T)
    return _final_norm(xc, moe_out[0], moe_out[1], g_final.reshape(1, D), tm).reshape(B, S, D)
```

```python
import collections
import functools

import jax
import jax.numpy as jnp
from jax import lax
from jax.experimental import pallas as pl
from jax.experimental.pallas import tpu as pltpu
from jax.experimental.pallas import tpu_sc as plsc

F32 = jnp.float32
BF16 = jnp.bfloat16

HEAD_DIM = 64
LANES = 128
MXU_COLS = 256
N_HEADS_SB = 4
N_HEADS_MOBA = 8
N_GROUPS_GMLP = 4
W_SB = N_HEADS_SB * HEAD_DIM
W_MOBA = N_HEADS_MOBA * HEAD_DIM
W_GMLP = N_GROUPS_GMLP * HEAD_DIM
MOBA_BLOCK = 256
MOBA_TOPK = 3
GMLP_CHUNK = 128
ROPE_THETA = 500000.0
ROT_DIM = HEAD_DIM // 4
N_EXPERT_GROUPS = 4
EXPERTS_PER_GROUP = 8
N_EXPERTS = N_EXPERT_GROUPS * EXPERTS_PER_GROUP
MOE_TOP_K = 2
MOE_BLOCK = 256
NORM_EPS = 1e-6
ATTN_SCALE = HEAD_DIM ** -0.5
NEG = -1e30
LOG2E = 1.4426950408889634
ROUTER_ROWS = 64
VMEM_LIMIT = 48 * 1024 * 1024

C_QA, C_KA, C_VA = 0, W_SB, 2 * W_SB
C_QB = 3 * W_SB
C_KB = C_QB + W_MOBA
C_VB = C_KB + W_MOBA
C_UC = C_VB + W_MOBA
C_VC = C_UC + W_GMLP
IN_COLS = C_VC + W_GMLP


def _nt_dot(a, b):
    return lax.dot_general(a, b, (((1,), (1,)), ((), ())), preferred_element_type=F32)


def _dot(a, b):
    return jnp.dot(a, b, preferred_element_type=F32)


def _rms(x, g):
    return x * lax.rsqrt(jnp.mean(x * x, axis=-1, keepdims=True) + NORM_EPS) * g


def _pack_halves(x):
    w = x.shape[1] // 2
    lo = lax.bitcast_convert_type(x[:, :w].astype(BF16).astype(F32), jnp.uint32)
    hi = lax.bitcast_convert_type(x[:, w:].astype(BF16).astype(F32), jnp.uint32)
    return (lo >> 16) | hi


def _unpack_halves(words):
    lo = lax.bitcast_convert_type(words << 16, F32)
    hi = lax.bitcast_convert_type(words & jnp.uint32(0xFFFF0000), F32)
    return jnp.concatenate([lo, hi], axis=1)


def _split_bf16(x):
    hi = x.astype(BF16)
    lo = (x - hi.astype(F32)).astype(BF16)
    return hi, lo


def _combine(x1_ref, yg_ref, gw_ref):
    rows = gw_ref.shape[0]
    eye = jnp.where(lax.broadcasted_iota(jnp.int32, (rows, LANES), 0)
                    == lax.broadcasted_iota(jnp.int32, (rows, LANES), 1), 1.0, 0.0).astype(BF16)
    tn = (((0,), (0,)), ((), ()))
    hi, lo = _split_bf16(gw_ref[...])
    gw = (lax.dot_general(hi, eye, tn, preferred_element_type=F32)
          + lax.dot_general(lo, eye, tn, preferred_element_type=F32))
    return (x1_ref[...] + _unpack_halves(yg_ref[0]) * gw[:, 0:1]
            + _unpack_halves(yg_ref[1]) * gw[:, 1:2])


def _inproj_kernel(*refs, combine):
    if combine:
        x1_ref, yg_ref, gw_ref, g_ref, w_ref, rc_ref, rs1_ref, rs2_ref, o_ref, x_ref = refs
        x = _combine(x1_ref, yg_ref, gw_ref)
        x_ref[...] = x
    else:
        x_ref, g_ref, w_ref, rc_ref, rs1_ref, rs2_ref, o_ref = refs
        x = x_ref[...]
    y = _rms(x, g_ref[...]).astype(BF16)
    wide = lambda t_ref: jnp.concatenate([t_ref[...]] * (MXU_COLS // LANES), axis=1)
    rc, rs1, rs2 = wide(rc_ref), wide(rs1_ref), wide(rs2_ref)
    half = ROT_DIM // 2
    for c0 in range(0, IN_COLS, MXU_COLS):
        p = _dot(y, w_ref[:, c0:c0 + MXU_COLS])
        if C_QB <= c0 < C_VB:
            p = (p * rc + pltpu.roll(p, MXU_COLS - half, axis=1) * rs1
                 + pltpu.roll(p, half, axis=1) * rs2)
        if c0 < C_KA or C_QB <= c0 < C_KB:
            p = p * (ATTN_SCALE * LOG2E)
        o_ref[:, c0:c0 + MXU_COLS] = p.astype(BF16)


def _inproj(x2d, moe_out, g, w_bf16, rc, rs1, rs2, tm):
    T, D = x2d.shape
    row = lambda i: (i, 0)
    const = lambda i: (0, 0)
    combine = moe_out is not None
    x_specs = [pl.BlockSpec((tm, D), row)]
    x_args = [x2d]
    out_specs = [pl.BlockSpec((tm, IN_COLS), row)]
    out_shape = [jax.ShapeDtypeStruct((T, IN_COLS), BF16)]
    if combine:
        yg, gw = moe_out
        x_specs += [pl.BlockSpec((MOE_TOP_K, tm, D // 2), lambda i: (0, i, 0)),
                    pl.BlockSpec((GW_ROWS, tm), lambda i: (0, i))]
        x_args += [yg, gw]
        out_specs.append(pl.BlockSpec((tm, D), row))
        out_shape.append(jax.ShapeDtypeStruct((T, D), F32))
    outs = pl.pallas_call(
        functools.partial(_inproj_kernel, combine=combine),
        grid=(T // tm,),
        in_specs=x_specs + [pl.BlockSpec((1, D), const), pl.BlockSpec((D, IN_COLS), const),
                            pl.BlockSpec((tm, LANES), row), pl.BlockSpec((tm, LANES), row),
                            pl.BlockSpec((tm, LANES), row)],
        out_specs=out_specs,
        out_shape=out_shape,
        compiler_params=pltpu.CompilerParams(dimension_semantics=("arbitrary",),
                                             vmem_limit_bytes=VMEM_LIMIT),
        name="inproj",
    )(*x_args, g, w_bf16, rc, rs1, rs2)
    return (outs[0], outs[1]) if combine else (outs[0], x2d)


_TileUnit = collections.namedtuple("_TileUnit", "order scores step init final")


def _run_tiles(units, n_past):
    orders = [u.order(n_past) for u in units]
    carries = [u.init() for u in units]
    zs = [u.scores(o[0], True) for u, o in zip(units, orders)]
    for n in range(n_past + 1):
        nxt = [u.scores(o[n + 1], False) if n < n_past else None
               for u, o in zip(units, orders)]
        carries = [u.step(z, o[n], c, n == 0) for u, o, z, c in zip(units, orders, zs, carries)]
        zs = nxt
    return carries


def _sb_unit(q_ref, k_ref, v_ref, gh_ref, i, *, tq, tk, sub):
    n_heads = LANES // HEAD_DIM
    n_sub = tk // sub
    qt = q_ref[...].astype(F32).T
    zeros_t = jnp.zeros((HEAD_DIM, tq), F32)
    qz = [jnp.concatenate([qt[hh * HEAD_DIM:(hh + 1) * HEAD_DIM] if h2 == hh else zeros_t
                           for h2 in range(n_heads)], axis=0).astype(BF16)
          for hh in range(n_heads)]
    srow = lax.broadcasted_iota(jnp.int32, (sub, sub), 0)
    scol = lax.broadcasted_iota(jnp.int32, (sub, sub), 1)
    later = jnp.where(scol > srow, 1.0, 0.0).astype(BF16)
    assert tq == tk

    def log_rest(z):
        nz = -z
        return jnp.minimum(nz, 0.0) - jnp.log2(1.0 + jnp.exp2(jnp.minimum(z, nz)))

    def scores(t, diag):
        if not diag:
            return [_dot(k_ref[t * tk:(t + 1) * tk, :], qz[hh]) for hh in range(n_heads)]
        return [[_dot(k_ref[t * tk + sb * sub:t * tk + (sb + 1) * sub, :], qz[hh][:, sb * sub:])
                 for sb in range(n_sub)] for hh in range(n_heads)]

    def past_step(zs, t, carry):
        wts, runs = [], []
        for hh in range(n_heads):
            z = zs[hh]
            lr = log_rest(z)
            lrb = lr.astype(BF16)
            tail = carry[hh][0]
            afters = [None] * n_sub
            for sb in reversed(range(n_sub)):
                afters[sb] = _dot(later, lrb[sb * sub:(sb + 1) * sub]) + tail
                tail = afters[sb][0:1] + lr[sb * sub:sb * sub + 1]
            wts.append(jnp.exp2(z + lr + jnp.concatenate(afters, axis=0)).astype(BF16))
            runs.append(tail)
        new = []
        for hh in range(n_heads):
            pv = lax.dot_general(v_ref[t * tk:(t + 1) * tk, :], wts[hh],
                                 (((0,), (0,)), ((), ())), preferred_element_type=F32)
            new.append((runs[hh], carry[hh][1] + pv[hh * HEAD_DIM:(hh + 1) * HEAD_DIM]))
        return tuple(new)

    def diag_step(zs, t, carry):
        new = []
        for hh in range(n_heads):
            tail, acc = carry[hh]
            for sb in reversed(range(n_sub)):
                off = sb * sub
                z = zs[hh][sb]
                krow = lax.broadcasted_iota(jnp.int32, z.shape, 0)
                qcol = lax.broadcasted_iota(jnp.int32, z.shape, 1)
                causal = krow < qcol
                lr = jnp.where(causal, log_rest(z), 0.0)
                after = _dot(later, lr.astype(BF16)) + tail[:, off:]
                w = jnp.where(causal, jnp.exp2(z + lr + after), 0.0).astype(BF16)
                run = after[0:1] + lr[0:1]
                pv = lax.dot_general(v_ref[t * tk + off:t * tk + off + sub, :], w,
                                     (((0,), (0,)), ((), ())), preferred_element_type=F32)
                pv = pv[hh * HEAD_DIM:(hh + 1) * HEAD_DIM]
                if off:
                    run = jnp.concatenate([tail[:, :off], run], axis=1)
                    pv = jnp.concatenate([jnp.zeros((HEAD_DIM, off), F32), pv], axis=1)
                tail, acc = run, acc + pv
            new.append((tail, acc))
        return tuple(new)

    def weights_pv(zs, t, carry, diag):
        return diag_step(zs, t, carry) if diag else past_step(zs, t, carry)

    def init():
        return tuple((jnp.zeros((1, tq), F32), jnp.zeros((HEAD_DIM, tq), F32))
                     for _ in range(n_heads))

    def final(carry):
        outs = []
        for hh in range(n_heads):
            o = carry[hh][1]
            outs.append(o * lax.rsqrt(jnp.mean(o * o, axis=0, keepdims=True) + NORM_EPS))
        return (jnp.concatenate(outs, axis=0).T * gh_ref[...]).astype(BF16)

    return _TileUnit(order=lambda n_past: list(range(n_past, -1, -1)), scores=scores,
                     step=weights_pv, init=init, final=final)


def _moba_unit(q_ref, k_ref, v_ref, gh_ref, kaug_ref, vaug_ref, rhi_ref, rlo_ref, i,
               *, tq, tk, topk):
    S = k_ref.shape[0]
    n_heads = q_ref.shape[1] // HEAD_DIM

    @pl.when(i == 0)
    def _prepare_keys():
        kp = k_ref[...]
        vp = v_ref[...]
        srow = lax.broadcasted_iota(jnp.int32, (S, HEAD_DIM), 0)
        scol = lax.broadcasted_iota(jnp.int32, (S, HEAD_DIM), 1)
        onehot = jnp.where(srow // MOBA_BLOCK == scol, 1.0, 0.0).astype(BF16)
        arow = lax.broadcasted_iota(jnp.int32, (LANES, S), 0) - HEAD_DIM
        acol = lax.broadcasted_iota(jnp.int32, (LANES, S), 1) // MOBA_BLOCK
        avg = jnp.where(arow == acol, 1.0 / MOBA_BLOCK, 0.0).astype(BF16)
        zeros = jnp.zeros((S, HEAD_DIM), BF16)
        ones = jnp.ones((S, HEAD_DIM), BF16)
        for hh in range(n_heads):
            kh = kp[:, hh * HEAD_DIM:(hh + 1) * HEAD_DIM]
            kaug_ref[hh] = jnp.concatenate([kh, onehot], axis=1)
            vaug_ref[hh] = jnp.concatenate([vp[:, hh * HEAD_DIM:(hh + 1) * HEAD_DIM], ones], axis=1)
            kmean = _dot(avg, jnp.concatenate([kh, zeros], axis=1))
            hi, lo = _split_bf16(kmean)
            rhi_ref[hh] = hi
            rlo_ref[hh] = lo

    qt = q_ref[...].astype(F32).T
    group = 8
    bidx = lax.broadcasted_iota(jnp.int32, (group, tq), 0)
    zeros_t = jnp.zeros((HEAD_DIM, tq), F32)
    own = (i * tq + lax.broadcasted_iota(jnp.int32, (group, tq), 1)) // MOBA_BLOCK
    qaugs = []
    for hh in range(n_heads):
        qh = qt[hh * HEAD_DIM:(hh + 1) * HEAD_DIM]
        qz = jnp.concatenate([qh, zeros_t], axis=0).astype(BF16)
        gate = (_dot(rhi_ref[hh], qz) + _dot(rlo_ref[hh], qz))[HEAD_DIM:HEAD_DIM + group]
        valid = bidx < own
        gm = jnp.where(valid, gate, NEG)
        rank = jnp.zeros((group, tq), F32)
        for d in range(1, group):
            nb = pltpu.roll(gm, d, axis=0)
            rank = rank + jnp.where(bidx >= d, jnp.where(nb >= gm, 1.0, 0.0),
                                    jnp.where(nb > gm, 1.0, 0.0))
        allowed = (valid & (rank < topk)) | (bidx == own)
        bias = jnp.where(allowed, 0.0, NEG)
        qaugs.append(jnp.concatenate(
            [qh, bias, jnp.zeros((HEAD_DIM - group, tq), F32)], axis=0).astype(BF16))

    assert tq == tk
    n_sub = tk // MOBA_BLOCK

    def scores(t, diag):
        if not diag:
            return [_dot(kaug_ref[hh, t * tk:(t + 1) * tk, :], qaugs[hh])
                    for hh in range(n_heads)]
        return [[_dot(kaug_ref[hh, t * tk + sb * MOBA_BLOCK:t * tk + (sb + 1) * MOBA_BLOCK, :],
                      qaugs[hh][:, sb * MOBA_BLOCK:]) for sb in range(n_sub)]
                for hh in range(n_heads)]

    def past_step(sts, t, carry):
        pts, stats = [], []
        for hh in range(n_heads):
            m = carry[hh][0]
            m_new = jnp.maximum(m, jnp.max(sts[hh], axis=0, keepdims=True))
            pts.append(jnp.exp2(sts[hh] - m_new).astype(BF16))
            stats.append((m_new, jnp.exp2(m - m_new)))
        new = []
        for hh in range(n_heads):
            _, l, acc = carry[hh]
            m_new, alpha = stats[hh]
            pv = lax.dot_general(vaug_ref[hh, t * tk:(t + 1) * tk, :], pts[hh],
                                 (((0,), (0,)), ((), ())), preferred_element_type=F32)
            new.append((m_new, alpha * l + pv[HEAD_DIM:HEAD_DIM + 1],
                        alpha * acc + pv[:HEAD_DIM]))
        return tuple(new)

    def diag_step(sts, t, carry):
        new = []
        for hh in range(n_heads):
            m, l, acc = carry[hh]
            masked = []
            m_new = m
            for sb in range(n_sub):
                off = sb * MOBA_BLOCK
                st = sts[hh][sb]
                krow = lax.broadcasted_iota(jnp.int32, st.shape, 0)
                qcol = lax.broadcasted_iota(jnp.int32, st.shape, 1)
                st = jnp.where(krow <= qcol, st, NEG)
                masked.append(st)
                top = jnp.max(st, axis=0, keepdims=True)
                if off:
                    top = jnp.concatenate([jnp.full((1, off), NEG, F32), top], axis=1)
                m_new = jnp.maximum(m_new, top)
            alpha = jnp.exp2(m - m_new)
            pv = None
            for sb in range(n_sub):
                off = sb * MOBA_BLOCK
                pt = jnp.exp2(masked[sb] - m_new[:, off:]).astype(BF16)
                part = lax.dot_general(
                    vaug_ref[hh, t * tk + off:t * tk + off + MOBA_BLOCK, :], pt,
                    (((0,), (0,)), ((), ())), preferred_element_type=F32)
                if off:
                    part = jnp.concatenate([jnp.zeros((LANES, off), F32), part], axis=1)
                pv = part if pv is None else pv + part
            new.append((m_new, alpha * l + pv[HEAD_DIM:HEAD_DIM + 1],
                        alpha * acc + pv[:HEAD_DIM]))
        return tuple(new)

    def softmax_pv(sts, t, carry, diag):
        return diag_step(sts, t, carry) if diag else past_step(sts, t, carry)

    def init():
        return tuple((jnp.full((1, tq), NEG, F32), jnp.zeros((1, tq), F32),
                      jnp.zeros((HEAD_DIM, tq), F32)) for _ in range(n_heads))

    def final(carry):
        outs = []
        for hh in range(n_heads):
            _, l, acc = carry[hh]
            o = acc / l
            outs.append(o * lax.rsqrt(jnp.mean(o * o, axis=0, keepdims=True) + NORM_EPS))
        return (jnp.concatenate(outs, axis=0).T * gh_ref[...]).astype(BF16)

    return _TileUnit(order=lambda n_past: [n_past] + list(range(n_past)), scores=scores,
                     step=softmax_pv, init=init, final=final)


ATTN_TQ = 2 * MOBA_BLOCK
MOBA_COLS = 2 * LANES


def _attn_kernel(qa_ref, ka_ref, va_ref, qb_ref, kb_ref, vb_ref, gha_ref, ghb_ref,
                 oa_ref, ob_ref, kaug_ref, vaug_ref, rhi_ref, rlo_ref, *, tq, topk):
    i = pl.program_id(2)
    S = ka_ref.shape[0]
    sb = _sb_unit(qa_ref, ka_ref, va_ref, gha_ref, i, tq=tq, tk=tq, sub=MOBA_BLOCK)
    mb = _moba_unit(qb_ref, kb_ref, vb_ref, ghb_ref, kaug_ref, vaug_ref, rhi_ref, rlo_ref, i,
                    tq=tq, tk=tq, topk=topk)

    def attend(n_past):
        ca, cb = _run_tiles([sb, mb], n_past)
        oa_ref[...] = sb.final(ca)
        ob_ref[...] = mb.final(cb)

    for n_past in range(S // tq):
        pl.when(i == n_past)(functools.partial(attend, n_past))


def _attention(proj, gh_sb, gh_moba, B, S):
    tq = ATTN_TQ
    assert S % tq == 0 and W_MOBA // MOBA_COLS == W_SB // LANES
    nq = S // tq
    n_blk = S // MOBA_BLOCK
    assert n_blk <= 8
    topk = min(MOBA_TOPK, max(n_blk - 1, 1))
    n_mb_heads = MOBA_COLS // HEAD_DIM
    q_row = lambda b, p, i: b * nq + i
    return pl.pallas_call(
        functools.partial(_attn_kernel, tq=tq, topk=topk),
        grid=(B, W_SB // LANES, nq),
        in_specs=[pl.BlockSpec((tq, LANES), lambda b, p, i: (q_row(b, p, i), C_QA // LANES + p)),
                  pl.BlockSpec((S, LANES), lambda b, p, i: (b, C_KA // LANES + p)),
                  pl.BlockSpec((S, LANES), lambda b, p, i: (b, C_VA // LANES + p)),
                  pl.BlockSpec((tq, MOBA_COLS),
                               lambda b, p, i: (q_row(b, p, i), C_QB // MOBA_COLS + p)),
                  pl.BlockSpec((S, MOBA_COLS), lambda b, p, i: (b, C_KB // MOBA_COLS + p)),
                  pl.BlockSpec((S, MOBA_COLS), lambda b, p, i: (b, C_VB // MOBA_COLS + p)),
                  pl.BlockSpec((1, LANES), lambda b, p, i: (0, p)),
                  pl.BlockSpec((1, MOBA_COLS), lambda b, p, i: (0, p))],
        out_specs=[pl.BlockSpec((tq, LANES), lambda b, p, i: (q_row(b, p, i), p)),
                   pl.BlockSpec((tq, MOBA_COLS), lambda b, p, i: (q_row(b, p, i), p))],
        out_shape=[jax.ShapeDtypeStruct((B * S, W_SB), BF16),
                   jax.ShapeDtypeStruct((B * S, W_MOBA), BF16)],
        scratch_shapes=[pltpu.VMEM((n_mb_heads, S, LANES), BF16),
                        pltpu.VMEM((n_mb_heads, S, LANES), BF16),
                        pltpu.VMEM((n_mb_heads, LANES, LANES), BF16),
                        pltpu.VMEM((n_mb_heads, LANES, LANES), BF16)],
        compiler_params=pltpu.CompilerParams(
            dimension_semantics=("arbitrary", "arbitrary", "arbitrary"),
            vmem_limit_bytes=VMEM_LIMIT),
        name="attention",
    )(proj, proj, proj, proj, proj, proj, gh_sb, gh_moba)


def _gmlp_kernel(u_ref, v_ref, gv_ref, ws_ref, b_ref, gh_ref, o_ref, *, tm):
    row = lax.broadcasted_iota(jnp.int32, (GMLP_CHUNK, GMLP_CHUNK), 0)
    col = lax.broadcasted_iota(jnp.int32, (GMLP_CHUNK, GMLP_CHUNK), 1)
    seg_mean = jnp.where(row // HEAD_DIM == col // HEAD_DIM, 1.0 / HEAD_DIM, 0.0).astype(BF16)
    first_group = lax.broadcasted_iota(jnp.int32, (GMLP_CHUNK, LANES), 1) < HEAD_DIM

    def group_rms(x, g):
        hi, lo = _split_bf16(x * x)
        ms = _dot(hi, seg_mean) + _dot(lo, seg_mean)
        return x * lax.rsqrt(ms + NORM_EPS) * g

    for p in range(W_GMLP // LANES):
        lanes = slice(p * LANES, (p + 1) * LANES)
        gu = jax.nn.gelu(u_ref[:, lanes].astype(F32))
        vn = group_rms(jax.nn.gelu(v_ref[:, lanes].astype(F32)), gv_ref[:, lanes]).astype(BF16)
        wm = [jnp.where(col <= row, ws_ref[2 * p + j], 0.0).astype(BF16) for j in range(2)]
        bias = b_ref[:, lanes]
        mixed = jnp.concatenate(
            [jnp.where(first_group, _dot(wm[0], vn[c * GMLP_CHUNK:(c + 1) * GMLP_CHUNK]),
                       _dot(wm[1], vn[c * GMLP_CHUNK:(c + 1) * GMLP_CHUNK])) + bias
             for c in range(tm // GMLP_CHUNK)], axis=0)
        o_ref[:, lanes] = group_rms(gu * mixed, gh_ref[:, lanes]).astype(BF16)


def _gmlp(proj, gv, ws, b_exp, gh, tm):
    T = proj.shape[0]
    nu, nv = C_UC // W_GMLP, C_VC // W_GMLP
    return pl.pallas_call(
        functools.partial(_gmlp_kernel, tm=tm),
        grid=(T // tm,),
        in_specs=[pl.BlockSpec((tm, W_GMLP), lambda i: (i, nu)),
                  pl.BlockSpec((tm, W_GMLP), lambda i: (i, nv)),
                  pl.BlockSpec((1, W_GMLP), lambda i: (0, 0)),
                  pl.BlockSpec((N_GROUPS_GMLP, GMLP_CHUNK, GMLP_CHUNK), lambda i: (0, 0, 0)),
                  pl.BlockSpec((GMLP_CHUNK, W_GMLP), lambda i: (0, 0)),
                  pl.BlockSpec((1, W_GMLP), lambda i: (0, 0))],
        out_specs=pl.BlockSpec((tm, W_GMLP), lambda i: (i, 0)),
        out_shape=jax.ShapeDtypeStruct((T, W_GMLP), BF16),
        compiler_params=pltpu.CompilerParams(dimension_semantics=("arbitrary",),
                                             vmem_limit_bytes=VMEM_LIMIT),
        name="gmlp",
    )(proj, proj, gv, ws, b_exp, gh)


def _out_kernel(osb_ref, omoba_ref, ogmlp_ref, x_ref, w_ref, g_ref, rhi_ref, rlo_ref,
                x1_ref, h_ref, lg_ref):
    x1 = (x_ref[...]
          + _dot(osb_ref[...], w_ref[0:W_SB, :])
          + _dot(omoba_ref[...], w_ref[W_SB:W_SB + W_MOBA, :])
          + _dot(ogmlp_ref[...], w_ref[W_SB + W_MOBA:, :]))
    x1_ref[...] = x1
    hn = _rms(x1, g_ref[...])
    hi, lo = _split_bf16(hn)
    h_ref[...] = _pack_halves(hn)
    lg_ref[...] = (_nt_dot(rhi_ref[...], hi) + _nt_dot(rhi_ref[...], lo)
                   + _nt_dot(rlo_ref[...], hi))


def _out_proj(o_sb, o_moba, o_gmlp, x2d, w_bf16, g, r_hi, r_lo, tm):
    T, D = x2d.shape
    row = lambda i: (i, 0)
    const = lambda i: (0, 0)
    return pl.pallas_call(
        _out_kernel,
        grid=(T // tm,),
        in_specs=[pl.BlockSpec((tm, W_SB), row), pl.BlockSpec((tm, W_MOBA), row),
                  pl.BlockSpec((tm, W_GMLP), row), pl.BlockSpec((tm, D), row),
                  pl.BlockSpec((D, D), const), pl.BlockSpec((1, D), const),
                  pl.BlockSpec((ROUTER_ROWS, D), const), pl.BlockSpec((ROUTER_ROWS, D), const)],
        out_specs=[pl.BlockSpec((tm, D), row), pl.BlockSpec((tm, D // 2), row),
                   pl.BlockSpec((ROUTER_ROWS, tm), lambda i: (0, i))],
        out_shape=[jax.ShapeDtypeStruct((T, D), F32), jax.ShapeDtypeStruct((T, D // 2), jnp.uint32),
                   jax.ShapeDtypeStruct((ROUTER_ROWS, T), F32)],
        compiler_params=pltpu.CompilerParams(dimension_semantics=("arbitrary",),
                                             vmem_limit_bytes=VMEM_LIMIT),
        name="outproj",
    )(o_sb, o_moba, o_gmlp, x2d, w_bf16, g, r_hi, r_lo)


ROUTER_EXPERT_ROW = 8
GW_ROWS = 8


def _first_max(p):
    rows = lax.broadcasted_iota(jnp.int32, p.shape, 0).astype(F32)
    top = jnp.max(p, axis=0, keepdims=True)
    idx = jnp.min(jnp.where(p == top, rows, float(p.shape[0])), axis=0, keepdims=True)
    return top, idx, rows


def _router_kernel(lg_ref, pos_ref, gw_ref, cnt_ref, cnt_acc, base_ref, *, tm):
    phase = pl.program_id(0)
    i = pl.program_id(1)
    lg = lg_ref[...]
    gl = lg[0:N_EXPERT_GROUPS]
    ge = jnp.exp(gl - jnp.max(gl, axis=0, keepdims=True))
    p_group = ge / jnp.sum(ge, axis=0, keepdims=True)
    p_g, g_sel, _ = _first_max(p_group)
    le = jnp.zeros((EXPERTS_PER_GROUP, tm), F32)
    for g in range(N_EXPERT_GROUPS):
        r0 = ROUTER_EXPERT_ROW + g * EXPERTS_PER_GROUP
        le = jnp.where(g_sel == g, lg[r0:r0 + EXPERTS_PER_GROUP], le)
    ee = jnp.exp(le - jnp.max(le, axis=0, keepdims=True))
    p = ee / jnp.sum(ee, axis=0, keepdims=True)
    p0, i0, rows = _first_max(p)
    p1, i1, _ = _first_max(jnp.where(rows == i0, -1.0, p))
    e0 = g_sel * EXPERTS_PER_GROUP + i0
    e1 = g_sel * EXPERTS_PER_GROUP + i1
    xrow = lax.broadcasted_iota(jnp.int32, (N_EXPERTS, tm), 0).astype(F32)
    oh0 = xrow == e0
    oh1 = xrow == e1
    slots = jnp.where(oh0, 1.0, 0.0) + jnp.where(oh1, 1.0, 0.0)
    tile_cnt = slots[:, 0:LANES]
    for c in range(1, tm // LANES):
        tile_cnt = tile_cnt + slots[:, c * LANES:(c + 1) * LANES]

    @pl.when((phase == 0) & (i == 0))
    def _init():
        cnt_acc[...] = jnp.zeros_like(cnt_acc)

    @pl.when(phase == 0)
    def _count():
        cnt_acc[...] += tile_cnt

    @pl.when((phase == 1) & (i == 0))
    def _starts():
        counts = jnp.sum(cnt_acc[...], axis=1, keepdims=True)
        n_blk = jnp.floor((counts + (MOE_BLOCK - 1)) * (1.0 / MOE_BLOCK))
        er = lax.broadcasted_iota(jnp.int32, (N_EXPERTS, N_EXPERTS), 0)
        ec = lax.broadcasted_iota(jnp.int32, (N_EXPERTS, N_EXPERTS), 1)
        before = jnp.where(ec < er, 1.0, 0.0).astype(BF16)
        start_blk = _dot(before, jnp.broadcast_to(n_blk, (N_EXPERTS, LANES)).astype(BF16))
        base_ref[...] = start_blk * MOE_BLOCK
        cnt_ref[...] = jnp.broadcast_to(counts, (N_EXPERTS, LANES)).astype(jnp.int32)

    @pl.when(phase == 1)
    def _assign():
        tr = lax.broadcasted_iota(jnp.int32, (tm, tm), 0)
        tc = lax.broadcasted_iota(jnp.int32, (tm, tm), 1)
        earlier = jnp.where(tr < tc, 1.0, 0.0).astype(BF16)
        row_of = _dot(slots.astype(BF16), earlier) + base_ref[:, 0:1]
        pos0 = jnp.sum(jnp.where(oh0, row_of, 0.0), axis=0, keepdims=True)
        pos1 = jnp.sum(jnp.where(oh1, row_of, 0.0), axis=0, keepdims=True)
        pos_ref[...] = jnp.concatenate([pos0, pos1], axis=0).astype(jnp.int32)
        scale = p_g / (p0 + p1)
        gw_ref[...] = jnp.concatenate(
            [p0 * scale, p1 * scale, jnp.zeros((GW_ROWS - MOE_TOP_K, tm), F32)], axis=0)
        base_ref[...] += jnp.sum(tile_cnt, axis=1, keepdims=True)


def _router(logits_t, tm):
    T = logits_t.shape[1]
    tok = lambda p, i: (0, i * p)
    return pl.pallas_call(
        functools.partial(_router_kernel, tm=tm),
        grid=(2, T // tm),
        in_specs=[pl.BlockSpec((ROUTER_ROWS, tm), lambda p, i: (0, i))],
        out_specs=[pl.BlockSpec((MOE_TOP_K, tm), tok), pl.BlockSpec((GW_ROWS, tm), tok),
                   pl.BlockSpec((N_EXPERTS, LANES), lambda p, i: (0, 0))],
        out_shape=[jax.ShapeDtypeStruct((MOE_TOP_K, T), jnp.int32),
                   jax.ShapeDtypeStruct((GW_ROWS, T), F32),
                   jax.ShapeDtypeStruct((N_EXPERTS, LANES), jnp.int32)],
        scratch_shapes=[pltpu.VMEM((N_EXPERTS, LANES), F32), pltpu.VMEM((N_EXPERTS, LANES), F32)],
        compiler_params=pltpu.CompilerParams(dimension_semantics=("arbitrary", "arbitrary")),
        name="router",
    )(logits_t)


def _moe_kernel(be_ref, first_ref, slot_ref, next_ref, rows_ref, na_ref, xs_ref, wg_hbm, wu_hbm,
                wd_hbm, ys_ref, wgf, wuf, wdf, wgb, wub, wdb, sem, *, layer):
    b = pl.program_id(0)
    e = be_ref[b]
    slot = slot_ref[b]
    active = b < na_ref[0]

    def fetch(expert, s):
        return (pltpu.make_async_copy(wg_hbm.at[layer, expert], wgf.at[s], sem.at[s, 0]),
                pltpu.make_async_copy(wu_hbm.at[layer, expert], wuf.at[s], sem.at[s, 1]),
                pltpu.make_async_copy(wd_hbm.at[layer, expert], wdf.at[s], sem.at[s, 2]))

    @pl.when(b == 0)
    def _first_fetch():
        for c in fetch(e, slot):
            c.start()

    @pl.when(active & (first_ref[b] == 1))
    def _load_expert():
        for c in fetch(e, slot):
            c.wait()
        wgb[...] = wgf[slot].astype(BF16)
        wub[...] = wuf[slot].astype(BF16)
        wdb[...] = wdf[slot].astype(BF16)
        nxt = next_ref[b]

        @pl.when(nxt >= 0)
        def _prefetch():
            for c in fetch(nxt, 1 - slot):
                c.start()

    @pl.when(active)
    def _compute():
        row = lax.broadcasted_iota(jnp.int32, xs_ref.shape, 0)
        words = jnp.where(row < rows_ref[b], xs_ref[...], jnp.uint32(0))
        xb = _unpack_halves(words).astype(BF16)
        a = jax.nn.silu(_dot(xb, wgb[...])) * _dot(xb, wub[...])
        ys_ref[...] = _pack_halves(_dot(a.astype(BF16), wdb[...]))

    @pl.when(jnp.logical_not(active))
    def _pad():
        ys_ref[...] = jnp.zeros_like(ys_ref)


def _moe_experts(plan, xs, w_gate, w_up, w_down, layer):
    n_rows = xs.shape[0]
    D = 2 * xs.shape[1]
    n_blocks = n_rows // MOE_BLOCK
    DE = w_gate.shape[-1]
    rows = lambda b, *_: (b, 0)
    grid_spec = pltpu.PrefetchScalarGridSpec(
        num_scalar_prefetch=len(plan),
        grid=(n_blocks,),
        in_specs=[pl.BlockSpec((MOE_BLOCK, D // 2), rows),
                  pl.BlockSpec(memory_space=pl.ANY), pl.BlockSpec(memory_space=pl.ANY),
                  pl.BlockSpec(memory_space=pl.ANY)],
        out_specs=pl.BlockSpec((MOE_BLOCK, D // 2), rows),
        scratch_shapes=[pltpu.VMEM((2, D, DE), F32), pltpu.VMEM((2, D, DE), F32),
                        pltpu.VMEM((2, DE, D), F32),
                        pltpu.VMEM((D, DE), BF16), pltpu.VMEM((D, DE), BF16),
                        pltpu.VMEM((DE, D), BF16),
                        pltpu.SemaphoreType.DMA((2, 3))])
    return pl.pallas_call(
        functools.partial(_moe_kernel, layer=layer),
        grid_spec=grid_spec,
        out_shape=jax.ShapeDtypeStruct((n_rows, D // 2), jnp.uint32),
        compiler_params=pltpu.CompilerParams(dimension_semantics=("arbitrary",),
                                             vmem_limit_bytes=VMEM_LIMIT),
        name="moe_experts",
    )(*plan, xs, w_gate, w_up, w_down)


def _final_kernel(x1_ref, yg_ref, gw_ref, g_ref, o_ref):
    o_ref[...] = _rms(_combine(x1_ref, yg_ref, gw_ref), g_ref[...])


def _final_norm(x2d, yg, gw, g, tm):
    T, D = x2d.shape
    row = lambda i: (i, 0)
    return pl.pallas_call(
        _final_kernel,
        grid=(T // tm,),
        in_specs=[pl.BlockSpec((tm, D), row),
                  pl.BlockSpec((MOE_TOP_K, tm, D // 2), lambda i: (0, i, 0)),
                  pl.BlockSpec((GW_ROWS, tm), lambda i: (0, i)),
                  pl.BlockSpec((1, D), lambda i: (0, 0))],
        out_specs=pl.BlockSpec((tm, D), row),
        out_shape=jax.ShapeDtypeStruct((T, D), F32),
        compiler_params=pltpu.CompilerParams(dimension_semantics=("arbitrary",)),
        name="final_norm",
    )(x2d, yg, gw, g)


SC_GATHER_ROWS = 64


def _sc_gather(table, idx):
    info = plsc.get_sparse_core_info()
    n_cores, n_workers = info.num_cores, info.num_cores * info.num_subcores
    N, W = idx.shape[0], table.shape[1]
    per_w = N // n_workers
    n_ch = per_w // SC_GATHER_ROWS
    assert per_w * n_workers == N and n_ch * SC_GATHER_ROWS == per_w
    mesh = plsc.VectorSubcoreMesh(core_axis_name="c", subcore_axis_name="s")

    @functools.partial(
        pl.kernel, mesh=mesh,
        out_type=jax.ShapeDtypeStruct((N, W), table.dtype),
        scratch_types=[pltpu.VMEM((n_ch, SC_GATHER_ROWS), jnp.int32),
                       pltpu.VMEM((2, SC_GATHER_ROWS, W), table.dtype),
                       pltpu.SemaphoreType.DMA((2,)),
                       pltpu.SemaphoreType.DMA((2,))])
    def gather_kernel(table_hbm, idx_hbm, out_hbm, idx_v, rows_v, gsem, ssem):
        wid = lax.axis_index("s") * n_cores + lax.axis_index("c")
        base = wid * per_w
        pltpu.sync_copy(idx_hbm.at[wid], idx_v)

        def fetch(c):
            return pltpu.make_async_copy(table_hbm.at[idx_v.at[c]], rows_v.at[c % 2],
                                         gsem.at[c % 2])

        def write(c):
            return pltpu.make_async_copy(
                rows_v.at[c % 2], out_hbm.at[pl.ds(base + c * SC_GATHER_ROWS, SC_GATHER_ROWS)],
                ssem.at[c % 2])

        fetch(0).start()
        for c in range(n_ch):
            if c + 1 < n_ch:
                if c >= 1:
                    write(c - 1).wait()
                fetch(c + 1).start()
            fetch(c).wait()
            write(c).start()
        if n_ch >= 2:
            write(n_ch - 2).wait()
        write(n_ch - 1).wait()

    return gather_kernel(table, idx.reshape(n_workers, n_ch, SC_GATHER_ROWS))


def _sc_scatter(rows, pos, n_out):
    info = plsc.get_sparse_core_info()
    n_cores, n_workers = info.num_cores, info.num_cores * info.num_subcores
    K, T = pos.shape
    W = rows.shape[1]
    per_w = T // n_workers
    n_ch = per_w // SC_GATHER_ROWS
    assert per_w * n_workers == T and n_ch * SC_GATHER_ROWS == per_w
    mesh = plsc.VectorSubcoreMesh(core_axis_name="c", subcore_axis_name="s")

    @functools.partial(
        pl.kernel, mesh=mesh,
        out_type=jax.ShapeDtypeStruct((n_out, W), rows.dtype),
        scratch_types=[pltpu.VMEM((K, n_ch, SC_GATHER_ROWS), jnp.int32),
                       pltpu.VMEM((2, SC_GATHER_ROWS, W), rows.dtype),
                       pltpu.SemaphoreType.DMA((2,)),
                       pltpu.SemaphoreType.DMA((2, K))])
    def scatter_kernel(rows_hbm, idx_hbm, out_hbm, idx_v, buf_v, lsem, ssem):
        wid = lax.axis_index("s") * n_cores + lax.axis_index("c")
        base = wid * per_w
        for k in range(K):
            pltpu.sync_copy(idx_hbm.at[k, wid], idx_v.at[k])

        def load(c):
            return pltpu.make_async_copy(
                rows_hbm.at[pl.ds(base + c * SC_GATHER_ROWS, SC_GATHER_ROWS)], buf_v.at[c % 2],
                lsem.at[c % 2])

        def store(c, k):
            return pltpu.make_async_copy(buf_v.at[c % 2], out_hbm.at[idx_v.at[k, c]],
                                         ssem.at[c % 2, k])

        load(0).start()
        for c in range(n_ch):
            if c + 1 < n_ch:
                if c >= 1:
                    for k in range(K):
                        store(c - 1, k).wait()
                load(c + 1).start()
            load(c).wait()
            for k in range(K):
                store(c, k).start()
        for c in range(max(n_ch - 2, 0), n_ch):
            for k in range(K):
                store(c, k).wait()

    return scatter_kernel(rows, pos.reshape(K, n_workers, n_ch, SC_GATHER_ROWS))


def _rope_tables(positions):
    half = ROT_DIM // 2
    inv_freq = jnp.power(ROPE_THETA, -jnp.arange(half, dtype=F32) / half)
    ang = positions.astype(F32).reshape(-1)[:, None] * inv_freq
    cos, sin = jnp.cos(ang), jnp.sin(ang)
    T = ang.shape[0]
    ones = jnp.ones((T, HEAD_DIM - ROT_DIM), F32)
    zeros = jnp.zeros((T, HEAD_DIM - ROT_DIM), F32)
    z8 = jnp.zeros((T, half), F32)
    rc = jnp.concatenate([cos, cos, ones], axis=1)
    rs1 = jnp.concatenate([-sin, z8, zeros], axis=1)
    rs2 = jnp.concatenate([z8, sin, zeros], axis=1)
    rep = LANES // HEAD_DIM
    return jnp.tile(rc, (1, rep)), jnp.tile(rs1, (1, rep)), jnp.tile(rs2, (1, rep))


def _block_plan(counts, T):
    i32 = jnp.int32
    n_blocks = -(-(T * MOE_TOP_K) // MOE_BLOCK) + N_EXPERTS
    experts = jnp.arange(N_EXPERTS, dtype=i32)
    blocks = jnp.arange(n_blocks, dtype=i32)
    n_blk = (counts + MOE_BLOCK - 1) // MOE_BLOCK
    blocks_end = jnp.cumsum(n_blk)
    blk_expert = jnp.minimum(jnp.sum(blocks_end[None, :] <= blocks[:, None], axis=1),
                             N_EXPERTS - 1).astype(i32)
    is_active = n_blk > 0
    ordinal = jnp.cumsum(is_active.astype(i32)) - 1
    later_active = is_active[None, :] & (experts[None, :] > experts[:, None])
    next_active = jnp.min(jnp.where(later_active, experts[None, :], N_EXPERTS), axis=1)
    next_active = jnp.where(next_active < N_EXPERTS, next_active, -1).astype(i32)
    tables = jnp.stack([blocks_end - n_blk, counts, ordinal, next_active], axis=1).astype(i32)
    onehot = (blk_expert[:, None] == experts[None, :]).astype(i32)
    looked = jnp.sum(onehot[:, :, None] * tables[None, :, :], axis=1)
    blk_in_expert = blocks - looked[:, 0]
    blk_first = (blk_in_expert == 0).astype(i32)
    blk_rows = jnp.clip(looked[:, 1] - blk_in_expert * MOE_BLOCK, 0, MOE_BLOCK).astype(i32)
    plan = (blk_expert, blk_first, (looked[:, 2] % 2).astype(i32), looked[:, 3], blk_rows,
            blocks_end[-1:].astype(i32))
    return n_blocks, plan


def kernel(x, positions, w_in, w_out, g_mix_norm, g_head_norm, g_gmlp_vnorm, w_spatial, b_spatial,
           g_ffn_norm, w_router_group, w_router_expert, w_expert_gate, w_expert_up, w_expert_down,
           g_final):
    B, S, D = x.shape
    T = B * S
    depth = w_in.shape[0]
    tm = min(512, T)
    rc, rs1, rs2 = _rope_tables(positions)
    xc = x.reshape(T, D)
    moe_out = None
    for l in range(depth):
        gh = g_head_norm[l].reshape(1, -1)
        proj, xc = _inproj(xc, moe_out, g_mix_norm[l].reshape(1, D), w_in[l].astype(BF16),
                           rc, rs1, rs2, tm)
        o_sb, o_moba = _attention(proj, gh[:, :W_SB], gh[:, W_SB:W_SB + W_MOBA], B, S)
        b_exp = jnp.repeat(b_spatial[l].T, HEAD_DIM, axis=1)
        o_gmlp = _gmlp(proj, g_gmlp_vnorm[l].reshape(1, -1), w_spatial[l], b_exp,
                       gh[:, W_SB + W_MOBA:], tm)
        w_r = jnp.concatenate(
            [w_router_group[l].T, jnp.zeros((ROUTER_EXPERT_ROW - N_EXPERT_GROUPS, D), F32),
             w_router_expert[l].T,
             jnp.zeros((ROUTER_ROWS - ROUTER_EXPERT_ROW - N_EXPERTS, D), F32)], axis=0)
        r_hi, r_lo = _split_bf16(w_r)
        xc, h, logits_t = _out_proj(o_sb, o_moba, o_gmlp, xc, w_out[l].astype(BF16),
                                    g_ffn_norm[l].reshape(1, D), r_hi, r_lo, tm)
        pos, gate_w, counts = _router(logits_t, tm)
        n_blocks, plan = _block_plan(counts[:, 0], T)
        xs = _sc_scatter(h, pos, n_blocks * MOE_BLOCK)
        ys = _moe_experts(plan, xs, w_expert_gate, w_expert_up, w_expert_down, l)
        yg = _sc_gather(ys, pos.reshape(-1)).reshape(MOE_TOP_K, T, D // 2)
        moe_out = (yg, gate_w)
    return _final_norm(xc, moe_out[0], moe_out[1], g_final.reshape(1, D), tm).reshape(B, S, D)
```

```python
import collections
import functools

import jax
import jax.numpy as jnp
from jax import lax
from jax.experimental import pallas as pl
from jax.experimental.pallas import tpu as pltpu
from jax.experimental.pallas import tpu_sc as plsc

F32 = jnp.float32
BF16 = jnp.bfloat16

HEAD_DIM = 64
LANES = 128
MXU_COLS = 256
N_HEADS_SB = 4
N_HEADS_MOBA = 8
N_GROUPS_GMLP = 4
W_SB = N_HEADS_SB * HEAD_DIM
W_MOBA = N_HEADS_MOBA * HEAD_DIM
W_GMLP = N_GROUPS_GMLP * HEAD_DIM
MOBA_BLOCK = 256
MOBA_TOPK = 3
GMLP_CHUNK = 128
ROPE_THETA = 500000.0
ROT_DIM = HEAD_DIM // 4
N_EXPERT_GROUPS = 4
EXPERTS_PER_GROUP = 8
N_EXPERTS = N_EXPERT_GROUPS * EXPERTS_PER_GROUP
MOE_TOP_K = 2
MOE_BLOCK = 256
NORM_EPS = 1e-6
ATTN_SCALE = HEAD_DIM ** -0.5
NEG = -1e30
LOG2E = 1.4426950408889634
ROUTER_ROWS = 64
VMEM_LIMIT = 48 * 1024 * 1024

C_QA, C_KA, C_VA = 0, W_SB, 2 * W_SB
C_QB = 3 * W_SB
C_KB = C_QB + W_MOBA
C_VB = C_KB + W_MOBA
C_UC = C_VB + W_MOBA
C_VC = C_UC + W_GMLP
IN_COLS = C_VC + W_GMLP


def _nt_dot(a, b):
    return lax.dot_general(a, b, (((1,), (1,)), ((), ())), preferred_element_type=F32)


def _dot(a, b):
    return jnp.dot(a, b, preferred_element_type=F32)


def _rms(x, g):
    return x * lax.rsqrt(jnp.mean(x * x, axis=-1, keepdims=True) + NORM_EPS) * g


def _pack_halves(x):
    w = x.shape[1] // 2
    lo = lax.bitcast_convert_type(x[:, :w].astype(BF16).astype(F32), jnp.uint32)
    hi = lax.bitcast_convert_type(x[:, w:].astype(BF16).astype(F32), jnp.uint32)
    return (lo >> 16) | hi


def _unpack_halves(words):
    lo = lax.bitcast_convert_type(words << 16, F32)
    hi = lax.bitcast_convert_type(words & jnp.uint32(0xFFFF0000), F32)
    return jnp.concatenate([lo, hi], axis=1)


def _split_bf16(x):
    hi = x.astype(BF16)
    lo = (x - hi.astype(F32)).astype(BF16)
    return hi, lo


def _combine(x1_ref, yg_ref, gw_ref):
    rows = gw_ref.shape[0]
    eye = jnp.where(lax.broadcasted_iota(jnp.int32, (rows, LANES), 0)
                    == lax.broadcasted_iota(jnp.int32, (rows, LANES), 1), 1.0, 0.0).astype(BF16)
    tn = (((0,), (0,)), ((), ()))
    hi, lo = _split_bf16(gw_ref[...])
    gw = (lax.dot_general(hi, eye, tn, preferred_element_type=F32)
          + lax.dot_general(lo, eye, tn, preferred_element_type=F32))
    return (x1_ref[...] + _unpack_halves(yg_ref[0]) * gw[:, 0:1]
            + _unpack_halves(yg_ref[1]) * gw[:, 1:2])


def _inproj_kernel(*refs, combine):
    if combine:
        x1_ref, yg_ref, gw_ref, g_ref, w_ref, rc_ref, rs1_ref, rs2_ref, o_ref, x_ref = refs
        x = _combine(x1_ref, yg_ref, gw_ref)
        x_ref[...] = x
    else:
        x_ref, g_ref, w_ref, rc_ref, rs1_ref, rs2_ref, o_ref = refs
        x = x_ref[...]
    y = _rms(x, g_ref[...]).astype(BF16)
    wide = lambda t_ref: jnp.concatenate([t_ref[...]] * (MXU_COLS // LANES), axis=1)
    rc, rs1, rs2 = wide(rc_ref), wide(rs1_ref), wide(rs2_ref)
    half = ROT_DIM // 2
    for c0 in range(0, IN_COLS, MXU_COLS):
        p = _dot(y, w_ref[:, c0:c0 + MXU_COLS])
        if C_QB <= c0 < C_VB:
            p = (p * rc + pltpu.roll(p, MXU_COLS - half, axis=1) * rs1
                 + pltpu.roll(p, half, axis=1) * rs2)
        if c0 < C_KA or C_QB <= c0 < C_KB:
            p = p * (ATTN_SCALE * LOG2E)
        o_ref[:, c0:c0 + MXU_COLS] = p.astype(BF16)


def _inproj(x2d, moe_out, g, w_bf16, rc, rs1, rs2, tm):
    T, D = x2d.shape
    row = lambda i: (i, 0)
    const = lambda i: (0, 0)
    combine = moe_out is not None
    x_specs = [pl.BlockSpec((tm, D), row)]
    x_args = [x2d]
    out_specs = [pl.BlockSpec((tm, IN_COLS), row)]
    out_shape = [jax.ShapeDtypeStruct((T, IN_COLS), BF16)]
    if combine:
        yg, gw = moe_out
        x_specs += [pl.BlockSpec((MOE_TOP_K, tm, D // 2), lambda i: (0, i, 0)),
                    pl.BlockSpec((GW_ROWS, tm), lambda i: (0, i))]
        x_args += [yg, gw]
        out_specs.append(pl.BlockSpec((tm, D), row))
        out_shape.append(jax.ShapeDtypeStruct((T, D), F32))
    outs = pl.pallas_call(
        functools.partial(_inproj_kernel, combine=combine),
        grid=(T // tm,),
        in_specs=x_specs + [pl.BlockSpec((1, D), const), pl.BlockSpec((D, IN_COLS), const),
                            pl.BlockSpec((tm, LANES), row), pl.BlockSpec((tm, LANES), row),
                            pl.BlockSpec((tm, LANES), row)],
        out_specs=out_specs,
        out_shape=out_shape,
        compiler_params=pltpu.CompilerParams(dimension_semantics=("arbitrary",),
                                             vmem_limit_bytes=VMEM_LIMIT),
        name="inproj",
    )(*x_args, g, w_bf16, rc, rs1, rs2)
    return (outs[0], outs[1]) if combine else (outs[0], x2d)


_TileUnit = collections.namedtuple("_TileUnit", "order scores step init final")


def _run_tiles(units, n_past):
    orders = [u.order(n_past) for u in units]
    carries = [u.init() for u in units]
    zs = [u.scores(o[0], True) for u, o in zip(units, orders)]
    for n in range(n_past + 1):
        nxt = [u.scores(o[n + 1], False) if n < n_past else None
               for u, o in zip(units, orders)]
        carries = [u.step(z, o[n], c, n == 0) for u, o, z, c in zip(units, orders, zs, carries)]
        zs = nxt
    return carries


def _sb_unit(q_ref, k_ref, v_ref, gh_ref, i, *, tq, tk, sub):
    n_heads = LANES // HEAD_DIM
    n_sub = tk // sub
    qt = q_ref[...].astype(F32).T
    zeros_t = jnp.zeros((HEAD_DIM, tq), F32)
    qz = [jnp.concatenate([qt[hh * HEAD_DIM:(hh + 1) * HEAD_DIM] if h2 == hh else zeros_t
                           for h2 in range(n_heads)], axis=0).astype(BF16)
          for hh in range(n_heads)]
    srow = lax.broadcasted_iota(jnp.int32, (sub, sub), 0)
    scol = lax.broadcasted_iota(jnp.int32, (sub, sub), 1)
    later = jnp.where(scol > srow, 1.0, 0.0).astype(BF16)
    assert tq == tk

    def log_rest(z):
        nz = -z
        return jnp.minimum(nz, 0.0) - jnp.log2(1.0 + jnp.exp2(jnp.minimum(z, nz)))

    def scores(t, diag):
        if not diag:
            return [_dot(k_ref[t * tk:(t + 1) * tk, :], qz[hh]) for hh in range(n_heads)]
        return [[_dot(k_ref[t * tk + sb * sub:t * tk + (sb + 1) * sub, :], qz[hh][:, sb * sub:])
                 for sb in range(n_sub)] for hh in range(n_heads)]

    def past_step(zs, t, carry):
        wts, runs = [], []
        for hh in range(n_heads):
            z = zs[hh]
            lr = log_rest(z)
            lrb = lr.astype(BF16)
            tail = carry[hh][0]
            afters = [None] * n_sub
            for sb in reversed(range(n_sub)):
                afters[sb] = _dot(later, lrb[sb * sub:(sb + 1) * sub]) + tail
                tail = afters[sb][0:1] + lr[sb * sub:sb * sub + 1]
            wts.append(jnp.exp2(z + lr + jnp.concatenate(afters, axis=0)).astype(BF16))
            runs.append(tail)
        new = []
        for hh in range(n_heads):
            pv = lax.dot_general(v_ref[t * tk:(t + 1) * tk, :], wts[hh],
                                 (((0,), (0,)), ((), ())), preferred_element_type=F32)
            new.append((runs[hh], carry[hh][1] + pv[hh * HEAD_DIM:(hh + 1) * HEAD_DIM]))
        return tuple(new)

    def diag_step(zs, t, carry):
        new = []
        for hh in range(n_heads):
            tail, acc = carry[hh]
            for sb in reversed(range(n_sub)):
                off = sb * sub
                z = zs[hh][sb]
                krow = lax.broadcasted_iota(jnp.int32, z.shape, 0)
                qcol = lax.broadcasted_iota(jnp.int32, z.shape, 1)
                causal = krow < qcol
                lr = jnp.where(causal, log_rest(z), 0.0)
                after = _dot(later, lr.astype(BF16)) + tail[:, off:]
                w = jnp.where(causal, jnp.exp2(z + lr + after), 0.0).astype(BF16)
                run = after[0:1] + lr[0:1]
                pv = lax.dot_general(v_ref[t * tk + off:t * tk + off + sub, :], w,
                                     (((0,), (0,)), ((), ())), preferred_element_type=F32)
                pv = pv[hh * HEAD_DIM:(hh + 1) * HEAD_DIM]
                if off:
                    run = jnp.concatenate([tail[:, :off], run], axis=1)
                    pv = jnp.concatenate([jnp.zeros((HEAD_DIM, off), F32), pv], axis=1)
                tail, acc = run, acc + pv
            new.append((tail, acc))
        return tuple(new)

    def weights_pv(zs, t, carry, diag):
        return diag_step(zs, t, carry) if diag else past_step(zs, t, carry)

    def init():
        return tuple((jnp.zeros((1, tq), F32), jnp.zeros((HEAD_DIM, tq), F32))
                     for _ in range(n_heads))

    def final(carry):
        outs = []
        for hh in range(n_heads):
            o = carry[hh][1]
            outs.append(o * lax.rsqrt(jnp.mean(o * o, axis=0, keepdims=True) + NORM_EPS))
        return (jnp.concatenate(outs, axis=0).T * gh_ref[...]).astype(BF16)

    return _TileUnit(order=lambda n_past: list(range(n_past, -1, -1)), scores=scores,
                     step=weights_pv, init=init, final=final)


def _moba_unit(q_ref, k_ref, v_ref, gh_ref, kaug_ref, vaug_ref, rhi_ref, rlo_ref, i,
               *, tq, tk, topk):
    S = k_ref.shape[0]
    n_heads = q_ref.shape[1] // HEAD_DIM

    @pl.when(i == 0)
    def _prepare_keys():
        kp = k_ref[...]
        vp = v_ref[...]
        srow = lax.broadcasted_iota(jnp.int32, (S, HEAD_DIM), 0)
        scol = lax.broadcasted_iota(jnp.int32, (S, HEAD_DIM), 1)
        onehot = jnp.where(srow // MOBA_BLOCK == scol, 1.0, 0.0).astype(BF16)
        arow = lax.broadcasted_iota(jnp.int32, (LANES, S), 0) - HEAD_DIM
        acol = lax.broadcasted_iota(jnp.int32, (LANES, S), 1) // MOBA_BLOCK
        avg = jnp.where(arow == acol, 1.0 / MOBA_BLOCK, 0.0).astype(BF16)
        zeros = jnp.zeros((S, HEAD_DIM), BF16)
        ones = jnp.ones((S, HEAD_DIM), BF16)
        for hh in range(n_heads):
            kh = kp[:, hh * HEAD_DIM:(hh + 1) * HEAD_DIM]
            kaug_ref[hh] = jnp.concatenate([kh, onehot], axis=1)
            vaug_ref[hh] = jnp.concatenate([vp[:, hh * HEAD_DIM:(hh + 1) * HEAD_DIM], ones], axis=1)
            kmean = _dot(avg, jnp.concatenate([kh, zeros], axis=1))
            hi, lo = _split_bf16(kmean)
            rhi_ref[hh] = hi
            rlo_ref[hh] = lo

    qt = q_ref[...].astype(F32).T
    group = 8
    bidx = lax.broadcasted_iota(jnp.int32, (group, tq), 0)
    zeros_t = jnp.zeros((HEAD_DIM, tq), F32)
    own = (i * tq + lax.broadcasted_iota(jnp.int32, (group, tq), 1)) // MOBA_BLOCK
    qaugs = []
    for hh in range(n_heads):
        qh = qt[hh * HEAD_DIM:(hh + 1) * HEAD_DIM]
        qz = jnp.concatenate([qh, zeros_t], axis=0).astype(BF16)
        gate = (_dot(rhi_ref[hh], qz) + _dot(rlo_ref[hh], qz))[HEAD_DIM:HEAD_DIM + group]
        valid = bidx < own
        gm = jnp.where(valid, gate, NEG)
        rank = jnp.zeros((group, tq), F32)
        for d in range(1, group):
            nb = pltpu.roll(gm, d, axis=0)
            rank = rank + jnp.where(bidx >= d, jnp.where(nb >= gm, 1.0, 0.0),
                                    jnp.where(nb > gm, 1.0, 0.0))
        allowed = (valid & (rank < topk)) | (bidx == own)
        bias = jnp.where(allowed, 0.0, NEG)
        qaugs.append(jnp.concatenate(
            [qh, bias, jnp.zeros((HEAD_DIM - group, tq), F32)], axis=0).astype(BF16))

    assert tq == tk
    n_sub = tk // MOBA_BLOCK

    def scores(t, diag):
        if not diag:
            return [_dot(kaug_ref[hh, t * tk:(t + 1) * tk, :], qaugs[hh])
                    for hh in range(n_heads)]
        return [[_dot(kaug_ref[hh, t * tk + sb * MOBA_BLOCK:t * tk + (sb + 1) * MOBA_BLOCK, :],
                      qaugs[hh][:, sb * MOBA_BLOCK:]) for sb in range(n_sub)]
                for hh in range(n_heads)]

    def past_step(sts, t, carry):
        pts, stats = [], []
        for hh in range(n_heads):
            m = carry[hh][0]
            m_new = jnp.maximum(m, jnp.max(sts[hh], axis=0, keepdims=True))
            pts.append(jnp.exp2(sts[hh] - m_new).astype(BF16))
            stats.append((m_new, jnp.exp2(m - m_new)))
        new = []
        for hh in range(n_heads):
            _, l, acc = carry[hh]
            m_new, alpha = stats[hh]
            pv = lax.dot_general(vaug_ref[hh, t * tk:(t + 1) * tk, :], pts[hh],
                                 (((0,), (0,)), ((), ())), preferred_element_type=F32)
            new.append((m_new, alpha * l + pv[HEAD_DIM:HEAD_DIM + 1],
                        alpha * acc + pv[:HEAD_DIM]))
        return tuple(new)

    def diag_step(sts, t, carry):
        new = []
        for hh in range(n_heads):
            m, l, acc = carry[hh]
            masked = []
            m_new = m
            for sb in range(n_sub):
                off = sb * MOBA_BLOCK
                st = sts[hh][sb]
                krow = lax.broadcasted_iota(jnp.int32, st.shape, 0)
                qcol = lax.broadcasted_iota(jnp.int32, st.shape, 1)
                st = jnp.where(krow <= qcol, st, NEG)
                masked.append(st)
                top = jnp.max(st, axis=0, keepdims=True)
                if off:
                    top = jnp.concatenate([jnp.full((1, off), NEG, F32), top], axis=1)
                m_new = jnp.maximum(m_new, top)
            alpha = jnp.exp2(m - m_new)
            pv = None
            for sb in range(n_sub):
                off = sb * MOBA_BLOCK
                pt = jnp.exp2(masked[sb] - m_new[:, off:]).astype(BF16)
                part = lax.dot_general(
                    vaug_ref[hh, t * tk + off:t * tk + off + MOBA_BLOCK, :], pt,
                    (((0,), (0,)), ((), ())), preferred_element_type=F32)
                if off:
                    part = jnp.concatenate([jnp.zeros((LANES, off), F32), part], axis=1)
                pv = part if pv is None else pv + part
            new.append((m_new, alpha * l + pv[HEAD_DIM:HEAD_DIM + 1],
                        alpha * acc + pv[:HEAD_DIM]))
        return tuple(new)

    def softmax_pv(sts, t, carry, diag):
        return diag_step(sts, t, carry) if diag else past_step(sts, t, carry)

    def init():
        return tuple((jnp.full((1, tq), NEG, F32), jnp.zeros((1, tq), F32),
                      jnp.zeros((HEAD_DIM, tq), F32)) for _ in range(n_heads))

    def final(carry):
        outs = []
        for hh in range(n_heads):
            _, l, acc = carry[hh]
            o = acc / l
            outs.append(o * lax.rsqrt(jnp.mean(o * o, axis=0, keepdims=True) + NORM_EPS))
        return (jnp.concatenate(outs, axis=0).T * gh_ref[...]).astype(BF16)

    return _TileUnit(order=lambda n_past: [n_past] + list(range(n_past)), scores=scores,
                     step=softmax_pv, init=init, final=final)


ATTN_TQ = 2 * MOBA_BLOCK
MOBA_COLS = 2 * LANES


def _attn_kernel(qa_ref, ka_ref, va_ref, qb_ref, kb_ref, vb_ref, gha_ref, ghb_ref,
                 oa_ref, ob_ref, kaug_ref, vaug_ref, rhi_ref, rlo_ref, *, tq, topk):
    i = pl.program_id(2)
    S = ka_ref.shape[0]
    sb = _sb_unit(qa_ref, ka_ref, va_ref, gha_ref, i, tq=tq, tk=tq, sub=MOBA_BLOCK)
    mb = _moba_unit(qb_ref, kb_ref, vb_ref, ghb_ref, kaug_ref, vaug_ref, rhi_ref, rlo_ref, i,
                    tq=tq, tk=tq, topk=topk)

    def attend(n_past):
        ca, cb = _run_tiles([sb, mb], n_past)
        oa_ref[...] = sb.final(ca)
        ob_ref[...] = mb.final(cb)

    for n_past in range(S // tq):
        pl.when(i == n_past)(functools.partial(attend, n_past))


def _attention(proj, gh_sb, gh_moba, B, S):
    tq = ATTN_TQ
    assert S % tq == 0 and W_MOBA // MOBA_COLS == W_SB // LANES
    nq = S // tq
    n_blk = S // MOBA_BLOCK
    assert n_blk <= 8
    topk = min(MOBA_TOPK, max(n_blk - 1, 1))
    n_mb_heads = MOBA_COLS // HEAD_DIM
    q_row = lambda b, p, i: b * nq + i
    return pl.pallas_call(
        functools.partial(_attn_kernel, tq=tq, topk=topk),
        grid=(B, W_SB // LANES, nq),
        in_specs=[pl.BlockSpec((tq, LANES), lambda b, p, i: (q_row(b, p, i), C_QA // LANES + p)),
                  pl.BlockSpec((S, LANES), lambda b, p, i: (b, C_KA // LANES + p)),
                  pl.BlockSpec((S, LANES), lambda b, p, i: (b, C_VA // LANES + p)),
                  pl.BlockSpec((tq, MOBA_COLS),
                               lambda b, p, i: (q_row(b, p, i), C_QB // MOBA_COLS + p)),
                  pl.BlockSpec((S, MOBA_COLS), lambda b, p, i: (b, C_KB // MOBA_COLS + p)),
                  pl.BlockSpec((S, MOBA_COLS), lambda b, p, i: (b, C_VB // MOBA_COLS + p)),
                  pl.BlockSpec((1, LANES), lambda b, p, i: (0, p)),
                  pl.BlockSpec((1, MOBA_COLS), lambda b, p, i: (0, p))],
        out_specs=[pl.BlockSpec((tq, LANES), lambda b, p, i: (q_row(b, p, i), p)),
                   pl.BlockSpec((tq, MOBA_COLS), lambda b, p, i: (q_row(b, p, i), p))],
        out_shape=[jax.ShapeDtypeStruct((B * S, W_SB), BF16),
                   jax.ShapeDtypeStruct((B * S, W_MOBA), BF16)],
        scratch_shapes=[pltpu.VMEM((n_mb_heads, S, LANES), BF16),
                        pltpu.VMEM((n_mb_heads, S, LANES), BF16),
                        pltpu.VMEM((n_mb_heads, LANES, LANES), BF16),
                        pltpu.VMEM((n_mb_heads, LANES, LANES), BF16)],
        compiler_params=pltpu.CompilerParams(
            dimension_semantics=("arbitrary", "arbitrary", "arbitrary"),
            vmem_limit_bytes=VMEM_LIMIT),
        name="attention",
    )(proj, proj, proj, proj, proj, proj, gh_sb, gh_moba)


def _gmlp_kernel(u_ref, v_ref, gv_ref, ws_ref, b_ref, gh_ref, o_ref, *, tm):
    row = lax.broadcasted_iota(jnp.int32, (GMLP_CHUNK, GMLP_CHUNK), 0)
    col = lax.broadcasted_iota(jnp.int32, (GMLP_CHUNK, GMLP_CHUNK), 1)
    seg_mean = jnp.where(row // HEAD_DIM == col // HEAD_DIM, 1.0 / HEAD_DIM, 0.0).astype(BF16)
    first_group = lax.broadcasted_iota(jnp.int32, (GMLP_CHUNK, LANES), 1) < HEAD_DIM

    def group_rms(x, g):
        hi, lo = _split_bf16(x * x)
        ms = _dot(hi, seg_mean) + _dot(lo, seg_mean)
        return x * lax.rsqrt(ms + NORM_EPS) * g

    for p in range(W_GMLP // LANES):
        lanes = slice(p * LANES, (p + 1) * LANES)
        gu = jax.nn.gelu(u_ref[:, lanes].astype(F32))
        vn = group_rms(jax.nn.gelu(v_ref[:, lanes].astype(F32)), gv_ref[:, lanes]).astype(BF16)
        wm = [jnp.where(col <= row, ws_ref[2 * p + j], 0.0).astype(BF16) for j in range(2)]
        bias = b_ref[:, lanes]
        mixed = jnp.concatenate(
            [jnp.where(first_group, _dot(wm[0], vn[c * GMLP_CHUNK:(c + 1) * GMLP_CHUNK]),
                       _dot(wm[1], vn[c * GMLP_CHUNK:(c + 1) * GMLP_CHUNK])) + bias
             for c in range(tm // GMLP_CHUNK)], axis=0)
        o_ref[:, lanes] = group_rms(gu * mixed, gh_ref[:, lanes]).astype(BF16)


def _gmlp(proj, gv, ws, b_exp, gh, tm):
    T = proj.shape[0]
    nu, nv = C_UC // W_GMLP, C_VC // W_GMLP
    return pl.pallas_call(
        functools.partial(_gmlp_kernel, tm=tm),
        grid=(T // tm,),
        in_specs=[pl.BlockSpec((tm, W_GMLP), lambda i: (i, nu)),
                  pl.BlockSpec((tm, W_GMLP), lambda i: (i, nv)),
                  pl.BlockSpec((1, W_GMLP), lambda i: (0, 0)),
                  pl.BlockSpec((N_GROUPS_GMLP, GMLP_CHUNK, GMLP_CHUNK), lambda i: (0, 0, 0)),
                  pl.BlockSpec((GMLP_CHUNK, W_GMLP), lambda i: (0, 0)),
                  pl.BlockSpec((1, W_GMLP), lambda i: (0, 0))],
        out_specs=pl.BlockSpec((tm, W_GMLP), lambda i: (i, 0)),
        out_shape=jax.ShapeDtypeStruct((T, W_GMLP), BF16),
        compiler_params=pltpu.CompilerParams(dimension_semantics=("arbitrary",),
                                             vmem_limit_bytes=VMEM_LIMIT),
        name="gmlp",
    )(proj, proj, gv, ws, b_exp, gh)


def _out_kernel(osb_ref, omoba_ref, ogmlp_ref, x_ref, w_ref, g_ref, rhi_ref, rlo_ref,
                x1_ref, h_ref, lg_ref):
    o = jnp.concatenate([osb_ref[...], omoba_ref[...], ogmlp_ref[...]], axis=1)
    x1 = x_ref[...] + _dot(o, w_ref[...])
    x1_ref[...] = x1
    hn = _rms(x1, g_ref[...])
    hi, lo = _split_bf16(hn)
    h_ref[...] = _pack_halves(hn)
    stacked = _nt_dot(jnp.concatenate([rhi_ref[...], rlo_ref[...]], axis=0), hi)
    lg_ref[...] = stacked[:ROUTER_ROWS] + stacked[ROUTER_ROWS:] + _nt_dot(rhi_ref[...], lo)


def _out_proj(o_sb, o_moba, o_gmlp, x2d, w_bf16, g, r_hi, r_lo, tm):
    T, D = x2d.shape
    row = lambda i: (i, 0)
    const = lambda i: (0, 0)
    return pl.pallas_call(
        _out_kernel,
        grid=(T // tm,),
        in_specs=[pl.BlockSpec((tm, W_SB), row), pl.BlockSpec((tm, W_MOBA), row),
                  pl.BlockSpec((tm, W_GMLP), row), pl.BlockSpec((tm, D), row),
                  pl.BlockSpec((D, D), const), pl.BlockSpec((1, D), const),
                  pl.BlockSpec((ROUTER_ROWS, D), const), pl.BlockSpec((ROUTER_ROWS, D), const)],
        out_specs=[pl.BlockSpec((tm, D), row), pl.BlockSpec((tm, D // 2), row),
                   pl.BlockSpec((ROUTER_ROWS, tm), lambda i: (0, i))],
        out_shape=[jax.ShapeDtypeStruct((T, D), F32), jax.ShapeDtypeStruct((T, D // 2), jnp.uint32),
                   jax.ShapeDtypeStruct((ROUTER_ROWS, T), F32)],
        compiler_params=pltpu.CompilerParams(dimension_semantics=("arbitrary",),
                                             vmem_limit_bytes=VMEM_LIMIT),
        name="outproj",
    )(o_sb, o_moba, o_gmlp, x2d, w_bf16, g, r_hi, r_lo)


ROUTER_EXPERT_ROW = 8
GW_ROWS = 8
MOE_DMA_PARTS = 4


def _first_max(p):
    rows = lax.broadcasted_iota(jnp.int32, p.shape, 0).astype(F32)
    top = jnp.max(p, axis=0, keepdims=True)
    idx = jnp.min(jnp.where(p == top, rows, float(p.shape[0])), axis=0, keepdims=True)
    return top, idx, rows


def _router_kernel(lg_ref, pos_ref, gw_ref, cnt_ref, cnt_acc, base_ref, *, tm):
    phase = pl.program_id(0)
    i = pl.program_id(1)
    lg = lg_ref[...]
    gl = lg[0:N_EXPERT_GROUPS]
    ge = jnp.exp(gl - jnp.max(gl, axis=0, keepdims=True))
    p_group = ge / jnp.sum(ge, axis=0, keepdims=True)
    p_g, g_sel, _ = _first_max(p_group)
    le = jnp.zeros((EXPERTS_PER_GROUP, tm), F32)
    for g in range(N_EXPERT_GROUPS):
        r0 = ROUTER_EXPERT_ROW + g * EXPERTS_PER_GROUP
        le = jnp.where(g_sel == g, lg[r0:r0 + EXPERTS_PER_GROUP], le)
    ee = jnp.exp(le - jnp.max(le, axis=0, keepdims=True))
    p = ee / jnp.sum(ee, axis=0, keepdims=True)
    p0, i0, rows = _first_max(p)
    p1, i1, _ = _first_max(jnp.where(rows == i0, -1.0, p))
    e0 = g_sel * EXPERTS_PER_GROUP + i0
    e1 = g_sel * EXPERTS_PER_GROUP + i1
    xrow = lax.broadcasted_iota(jnp.int32, (N_EXPERTS, tm), 0).astype(F32)
    oh0 = xrow == e0
    oh1 = xrow == e1
    slots = jnp.where(oh0, 1.0, 0.0) + jnp.where(oh1, 1.0, 0.0)
    tile_cnt = slots[:, 0:LANES]
    for c in range(1, tm // LANES):
        tile_cnt = tile_cnt + slots[:, c * LANES:(c + 1) * LANES]

    @pl.when((phase == 0) & (i == 0))
    def _init():
        cnt_acc[...] = jnp.zeros_like(cnt_acc)

    @pl.when(phase == 0)
    def _count():
        cnt_acc[...] += tile_cnt

    @pl.when((phase == 1) & (i == 0))
    def _starts():
        counts = jnp.sum(cnt_acc[...], axis=1, keepdims=True)
        n_blk = jnp.floor((counts + (MOE_BLOCK - 1)) * (1.0 / MOE_BLOCK))
        er = lax.broadcasted_iota(jnp.int32, (N_EXPERTS, N_EXPERTS), 0)
        ec = lax.broadcasted_iota(jnp.int32, (N_EXPERTS, N_EXPERTS), 1)
        before = jnp.where(ec < er, 1.0, 0.0).astype(BF16)
        start_blk = _dot(before, jnp.broadcast_to(n_blk, (N_EXPERTS, LANES)).astype(BF16))
        base_ref[...] = start_blk * MOE_BLOCK
        cnt_ref[...] = jnp.broadcast_to(counts, (N_EXPERTS, LANES)).astype(jnp.int32)

    @pl.when(phase == 1)
    def _assign():
        tr = lax.broadcasted_iota(jnp.int32, (tm, tm), 0)
        tc = lax.broadcasted_iota(jnp.int32, (tm, tm), 1)
        earlier = jnp.where(tr < tc, 1.0, 0.0).astype(BF16)
        row_of = _dot(slots.astype(BF16), earlier) + base_ref[:, 0:1]
        pos0 = jnp.sum(jnp.where(oh0, row_of, 0.0), axis=0, keepdims=True)
        pos1 = jnp.sum(jnp.where(oh1, row_of, 0.0), axis=0, keepdims=True)
        pos_ref[...] = jnp.concatenate([pos0, pos1], axis=0).astype(jnp.int32)
        scale = p_g / (p0 + p1)
        gw_ref[...] = jnp.concatenate(
            [p0 * scale, p1 * scale, jnp.zeros((GW_ROWS - MOE_TOP_K, tm), F32)], axis=0)
        base_ref[...] += jnp.sum(tile_cnt, axis=1, keepdims=True)


def _router(logits_t, tm):
    T = logits_t.shape[1]
    tok = lambda p, i: (0, i * p)
    return pl.pallas_call(
        functools.partial(_router_kernel, tm=tm),
        grid=(2, T // tm),
        in_specs=[pl.BlockSpec((ROUTER_ROWS, tm), lambda p, i: (0, i))],
        out_specs=[pl.BlockSpec((MOE_TOP_K, tm), tok), pl.BlockSpec((GW_ROWS, tm), tok),
                   pl.BlockSpec((N_EXPERTS, LANES), lambda p, i: (0, 0))],
        out_shape=[jax.ShapeDtypeStruct((MOE_TOP_K, T), jnp.int32),
                   jax.ShapeDtypeStruct((GW_ROWS, T), F32),
                   jax.ShapeDtypeStruct((N_EXPERTS, LANES), jnp.int32)],
        scratch_shapes=[pltpu.VMEM((N_EXPERTS, LANES), F32), pltpu.VMEM((N_EXPERTS, LANES), F32)],
        compiler_params=pltpu.CompilerParams(dimension_semantics=("arbitrary", "arbitrary")),
        name="router",
    )(logits_t)


def _moe_kernel(be_ref, first_ref, slot_ref, next_ref, rows_ref, na_ref, xs_ref, wg_hbm, wu_hbm,
                wd_hbm, ys_ref, wgf, wuf, wdf, wgb, wub, wdb, sem, *, layer):
    b = pl.program_id(0)
    e = be_ref[b]
    slot = slot_ref[b]
    active = b < na_ref[0]

    def fetch(expert, s):
        copies = []
        for m, (hbm, stage) in enumerate(((wg_hbm, wgf), (wu_hbm, wuf), (wd_hbm, wdf))):
            rows = stage.shape[1] // MOE_DMA_PARTS
            for c in range(MOE_DMA_PARTS):
                part = pl.ds(c * rows, rows)
                copies.append(pltpu.make_async_copy(
                    hbm.at[layer, expert, part], stage.at[s, part],
                    sem.at[s, m * MOE_DMA_PARTS + c]))
        return copies

    @pl.when(b == 0)
    def _first_fetch():
        for c in fetch(e, slot):
            c.start()

    @pl.when(active & (first_ref[b] == 1))
    def _load_expert():
        for c in fetch(e, slot):
            c.wait()
        wgb[...] = wgf[slot].astype(BF16)
        wub[...] = wuf[slot].astype(BF16)
        wdb[...] = wdf[slot].astype(BF16)
        nxt = next_ref[b]

        @pl.when(nxt >= 0)
        def _prefetch():
            for c in fetch(nxt, 1 - slot):
                c.start()

    @pl.when(active)
    def _compute():
        row = lax.broadcasted_iota(jnp.int32, xs_ref.shape, 0)
        words = jnp.where(row < rows_ref[b], xs_ref[...], jnp.uint32(0))
        xb = _unpack_halves(words).astype(BF16)
        a = jax.nn.silu(_dot(xb, wgb[...])) * _dot(xb, wub[...])
        ys_ref[...] = _pack_halves(_dot(a.astype(BF16), wdb[...]))

    @pl.when(jnp.logical_not(active))
    def _pad():
        ys_ref[...] = jnp.zeros_like(ys_ref)


def _moe_experts(plan, xs, w_gate, w_up, w_down, layer):
    n_rows = xs.shape[0]
    D = 2 * xs.shape[1]
    n_blocks = n_rows // MOE_BLOCK
    DE = w_gate.shape[-1]
    rows = lambda b, *_: (b, 0)
    grid_spec = pltpu.PrefetchScalarGridSpec(
        num_scalar_prefetch=len(plan),
        grid=(n_blocks,),
        in_specs=[pl.BlockSpec((MOE_BLOCK, D // 2), rows),
                  pl.BlockSpec(memory_space=pl.ANY), pl.BlockSpec(memory_space=pl.ANY),
                  pl.BlockSpec(memory_space=pl.ANY)],
        out_specs=pl.BlockSpec((MOE_BLOCK, D // 2), rows),
        scratch_shapes=[pltpu.VMEM((2, D, DE), F32), pltpu.VMEM((2, D, DE), F32),
                        pltpu.VMEM((2, DE, D), F32),
                        pltpu.VMEM((D, DE), BF16), pltpu.VMEM((D, DE), BF16),
                        pltpu.VMEM((DE, D), BF16),
                        pltpu.SemaphoreType.DMA((2, 3 * MOE_DMA_PARTS))])
    return pl.pallas_call(
        functools.partial(_moe_kernel, layer=layer),
        grid_spec=grid_spec,
        out_shape=jax.ShapeDtypeStruct((n_rows, D // 2), jnp.uint32),
        compiler_params=pltpu.CompilerParams(dimension_semantics=("arbitrary",),
                                             vmem_limit_bytes=VMEM_LIMIT),
        name="moe_experts",
    )(*plan, xs, w_gate, w_up, w_down)


def _final_kernel(x1_ref, yg_ref, gw_ref, g_ref, o_ref):
    o_ref[...] = _rms(_combine(x1_ref, yg_ref, gw_ref), g_ref[...])


def _final_norm(x2d, yg, gw, g, tm):
    T, D = x2d.shape
    row = lambda i: (i, 0)
    return pl.pallas_call(
        _final_kernel,
        grid=(T // tm,),
        in_specs=[pl.BlockSpec((tm, D), row),
                  pl.BlockSpec((MOE_TOP_K, tm, D // 2), lambda i: (0, i, 0)),
                  pl.BlockSpec((GW_ROWS, tm), lambda i: (0, i)),
                  pl.BlockSpec((1, D), lambda i: (0, 0))],
        out_specs=pl.BlockSpec((tm, D), row),
        out_shape=jax.ShapeDtypeStruct((T, D), F32),
        compiler_params=pltpu.CompilerParams(dimension_semantics=("arbitrary",)),
        name="final_norm",
    )(x2d, yg, gw, g)


SC_GATHER_ROWS = 64


def _sc_gather(table, idx):
    info = plsc.get_sparse_core_info()
    n_cores, n_workers = info.num_cores, info.num_cores * info.num_subcores
    N, W = idx.shape[0], table.shape[1]
    per_w = N // n_workers
    n_ch = per_w // SC_GATHER_ROWS
    assert per_w * n_workers == N and n_ch * SC_GATHER_ROWS == per_w
    mesh = plsc.VectorSubcoreMesh(core_axis_name="c", subcore_axis_name="s")

    @functools.partial(
        pl.kernel, mesh=mesh,
        out_type=jax.ShapeDtypeStruct((N, W), table.dtype),
        scratch_types=[pltpu.VMEM((n_ch, SC_GATHER_ROWS), jnp.int32),
                       pltpu.VMEM((2, SC_GATHER_ROWS, W), table.dtype),
                       pltpu.SemaphoreType.DMA((2,)),
                       pltpu.SemaphoreType.DMA((2,))])
    def gather_kernel(table_hbm, idx_hbm, out_hbm, idx_v, rows_v, gsem, ssem):
        wid = lax.axis_index("s") * n_cores + lax.axis_index("c")
        base = wid * per_w
        pltpu.sync_copy(idx_hbm.at[wid], idx_v)

        def fetch(c):
            return pltpu.make_async_copy(table_hbm.at[idx_v.at[c]], rows_v.at[c % 2],
                                         gsem.at[c % 2])

        def write(c):
            return pltpu.make_async_copy(
                rows_v.at[c % 2], out_hbm.at[pl.ds(base + c * SC_GATHER_ROWS, SC_GATHER_ROWS)],
                ssem.at[c % 2])

        fetch(0).start()
        for c in range(n_ch):
            if c + 1 < n_ch:
                if c >= 1:
                    write(c - 1).wait()
                fetch(c + 1).start()
            fetch(c).wait()
            write(c).start()
        if n_ch >= 2:
            write(n_ch - 2).wait()
        write(n_ch - 1).wait()

    return gather_kernel(table, idx.reshape(n_workers, n_ch, SC_GATHER_ROWS))


def _sc_scatter(rows, pos, n_out):
    info = plsc.get_sparse_core_info()
    n_cores, n_workers = info.num_cores, info.num_cores * info.num_subcores
    K, T = pos.shape
    W = rows.shape[1]
    per_w = T // n_workers
    n_ch = per_w // SC_GATHER_ROWS
    assert per_w * n_workers == T and n_ch * SC_GATHER_ROWS == per_w
    mesh = plsc.VectorSubcoreMesh(core_axis_name="c", subcore_axis_name="s")

    @functools.partial(
        pl.kernel, mesh=mesh,
        out_type=jax.ShapeDtypeStruct((n_out, W), rows.dtype),
        scratch_types=[pltpu.VMEM((K, n_ch, SC_GATHER_ROWS), jnp.int32),
                       pltpu.VMEM((2, SC_GATHER_ROWS, W), rows.dtype),
                       pltpu.SemaphoreType.DMA((2,)),
                       pltpu.SemaphoreType.DMA((2, K))])
    def scatter_kernel(rows_hbm, idx_hbm, out_hbm, idx_v, buf_v, lsem, ssem):
        wid = lax.axis_index("s") * n_cores + lax.axis_index("c")
        base = wid * per_w
        for k in range(K):
            pltpu.sync_copy(idx_hbm.at[k, wid], idx_v.at[k])

        def load(c):
            return pltpu.make_async_copy(
                rows_hbm.at[pl.ds(base + c * SC_GATHER_ROWS, SC_GATHER_ROWS)], buf_v.at[c % 2],
                lsem.at[c % 2])

        def store(c, k):
            return pltpu.make_async_copy(buf_v.at[c % 2], out_hbm.at[idx_v.at[k, c]],
                                         ssem.at[c % 2, k])

        load(0).start()
        for c in range(n_ch):
            if c + 1 < n_ch:
                if c >= 1:
                    for k in range(K):
                        store(c - 1, k).wait()
                load(c + 1).start()
            load(c).wait()
            for k in range(K):
                store(c, k).start()
        for c in range(max(n_ch - 2, 0), n_ch):
            for k in range(K):
                store(c, k).wait()

    return scatter_kernel(rows, pos.reshape(K, n_workers, n_ch, SC_GATHER_ROWS))


def _rope_tables(positions):
    half = ROT_DIM // 2
    inv_freq = jnp.power(ROPE_THETA, -jnp.arange(half, dtype=F32) / half)
    ang = positions.astype(F32).reshape(-1)[:, None] * inv_freq
    cos, sin = jnp.cos(ang), jnp.sin(ang)
    T = ang.shape[0]
    ones = jnp.ones((T, HEAD_DIM - ROT_DIM), F32)
    zeros = jnp.zeros((T, HEAD_DIM - ROT_DIM), F32)
    z8 = jnp.zeros((T, half), F32)
    rc = jnp.concatenate([cos, cos, ones], axis=1)
    rs1 = jnp.concatenate([-sin, z8, zeros], axis=1)
    rs2 = jnp.concatenate([z8, sin, zeros], axis=1)
    rep = LANES // HEAD_DIM
    return jnp.tile(rc, (1, rep)), jnp.tile(rs1, (1, rep)), jnp.tile(rs2, (1, rep))


def _block_plan(counts, T):
    i32 = jnp.int32
    n_blocks = -(-(T * MOE_TOP_K) // MOE_BLOCK) + N_EXPERTS
    experts = jnp.arange(N_EXPERTS, dtype=i32)
    blocks = jnp.arange(n_blocks, dtype=i32)
    n_blk = (counts + MOE_BLOCK - 1) // MOE_BLOCK
    blocks_end = jnp.cumsum(n_blk)
    blk_expert = jnp.minimum(jnp.sum(blocks_end[None, :] <= blocks[:, None], axis=1),
                             N_EXPERTS - 1).astype(i32)
    is_active = n_blk > 0
    ordinal = jnp.cumsum(is_active.astype(i32)) - 1
    later_active = is_active[None, :] & (experts[None, :] > experts[:, None])
    next_active = jnp.min(jnp.where(later_active, experts[None, :], N_EXPERTS), axis=1)
    next_active = jnp.where(next_active < N_EXPERTS, next_active, -1).astype(i32)
    tables = jnp.stack([blocks_end - n_blk, counts, ordinal, next_active], axis=1).astype(i32)
    onehot = (blk_expert[:, None] == experts[None, :]).astype(i32)
    looked = jnp.sum(onehot[:, :, None] * tables[None, :, :], axis=1)
    blk_in_expert = blocks - looked[:, 0]
    blk_first = (blk_in_expert == 0).astype(i32)
    blk_rows = jnp.clip(looked[:, 1] - blk_in_expert * MOE_BLOCK, 0, MOE_BLOCK).astype(i32)
    plan = (blk_expert, blk_first, (looked[:, 2] % 2).astype(i32), looked[:, 3], blk_rows,
            blocks_end[-1:].astype(i32))
    return n_blocks, plan


def kernel(x, positions, w_in, w_out, g_mix_norm, g_head_norm, g_gmlp_vnorm, w_spatial, b_spatial,
           g_ffn_norm, w_router_group, w_router_expert, w_expert_gate, w_expert_up, w_expert_down,
           g_final):
    B, S, D = x.shape
    T = B * S
    depth = w_in.shape[0]
    tm = min(512, T)
    rc, rs1, rs2 = _rope_tables(positions)
    xc = x.reshape(T, D)
    moe_out = None
    for l in range(depth):
        gh = g_head_norm[l].reshape(1, -1)
        proj, xc = _inproj(xc, moe_out, g_mix_norm[l].reshape(1, D), w_in[l].astype(BF16),
                           rc, rs1, rs2, tm)
        o_sb, o_moba = _attention(proj, gh[:, :W_SB], gh[:, W_SB:W_SB + W_MOBA], B, S)
        b_exp = jnp.repeat(b_spatial[l].T, HEAD_DIM, axis=1)
        o_gmlp = _gmlp(proj, g_gmlp_vnorm[l].reshape(1, -1), w_spatial[l], b_exp,
                       gh[:, W_SB + W_MOBA:], tm)
        w_r = jnp.concatenate(
            [w_router_group[l].T, jnp.zeros((ROUTER_EXPERT_ROW - N_EXPERT_GROUPS, D), F32),
             w_router_expert[l].T,
             jnp.zeros((ROUTER_ROWS - ROUTER_EXPERT_ROW - N_EXPERTS, D), F32)], axis=0)
        r_hi, r_lo = _split_bf16(w_r)
        xc, h, logits_t = _out_proj(o_sb, o_moba, o_gmlp, xc, w_out[l].astype(BF16),
                                    g_ffn_norm[l].reshape(1, D), r_hi, r_lo, tm)
        pos, gate_w, counts = _router(logits_t, tm)
        n_blocks, plan = _block_plan(counts[:, 0], T)
        xs = _sc_scatter(h, pos, n_blocks * MOE_BLOCK)
        ys = _moe_experts(plan, xs, w_expert_gate, w_expert_up, w_expert_down, l)
        yg = _sc_gather(ys, pos.reshape(-1)).reshape(MOE_TOP_K, T, D // 2)
        moe_out = (yg, gate_w)
    return _final_norm(xc, moe_out[0], moe_out[1], g_final.reshape(1, D), tm).reshape(B, S, D)
```

```python
import collections
import functools

import jax
import jax.numpy as jnp
from jax import lax
from jax.experimental import pallas as pl
from jax.experimental.pallas import tpu as pltpu
from jax.experimental.pallas import tpu_sc as plsc

F32 = jnp.float32
BF16 = jnp.bfloat16

HEAD_DIM = 64
LANES = 128
MXU_COLS = 256
N_HEADS_SB = 4
N_HEADS_MOBA = 8
N_GROUPS_GMLP = 4
W_SB = N_HEADS_SB * HEAD_DIM
W_MOBA = N_HEADS_MOBA * HEAD_DIM
W_GMLP = N_GROUPS_GMLP * HEAD_DIM
MOBA_BLOCK = 256
MOBA_TOPK = 3
GMLP_CHUNK = 128
ROPE_THETA = 500000.0
ROT_DIM = HEAD_DIM // 4
N_EXPERT_GROUPS = 4
EXPERTS_PER_GROUP = 8
N_EXPERTS = N_EXPERT_GROUPS * EXPERTS_PER_GROUP
MOE_TOP_K = 2
MOE_BLOCK = 256
NORM_EPS = 1e-6
ATTN_SCALE = HEAD_DIM ** -0.5
NEG = -1e30
LOG2E = 1.4426950408889634
ROUTER_ROWS = 64
VMEM_LIMIT = 48 * 1024 * 1024

C_QA, C_KA, C_VA = 0, W_SB, 2 * W_SB
C_QB = 3 * W_SB
C_KB = C_QB + W_MOBA
C_VB = C_KB + W_MOBA
C_UC = C_VB + W_MOBA
C_VC = C_UC + W_GMLP
IN_COLS = C_VC + W_GMLP


def _nt_dot(a, b):
    return lax.dot_general(a, b, (((1,), (1,)), ((), ())), preferred_element_type=F32)


def _dot(a, b):
    return jnp.dot(a, b, preferred_element_type=F32)


def _rms(x, g):
    return x * lax.rsqrt(jnp.mean(x * x, axis=-1, keepdims=True) + NORM_EPS) * g


def _pack_halves(x):
    w = x.shape[1] // 2
    lo = lax.bitcast_convert_type(x[:, :w].astype(BF16).astype(F32), jnp.uint32)
    hi = lax.bitcast_convert_type(x[:, w:].astype(BF16).astype(F32), jnp.uint32)
    return (lo >> 16) | hi


def _unpack_halves(words):
    lo = lax.bitcast_convert_type(words << 16, F32)
    hi = lax.bitcast_convert_type(words & jnp.uint32(0xFFFF0000), F32)
    return jnp.concatenate([lo, hi], axis=1)


def _split_bf16(x):
    hi = x.astype(BF16)
    lo = (x - hi.astype(F32)).astype(BF16)
    return hi, lo


def _combine(x1_ref, yg_ref, gw_ref):
    rows = gw_ref.shape[0]
    eye = jnp.where(lax.broadcasted_iota(jnp.int32, (rows, LANES), 0)
                    == lax.broadcasted_iota(jnp.int32, (rows, LANES), 1), 1.0, 0.0).astype(BF16)
    tn = (((0,), (0,)), ((), ()))
    hi, lo = _split_bf16(gw_ref[...])
    gw = (lax.dot_general(hi, eye, tn, preferred_element_type=F32)
          + lax.dot_general(lo, eye, tn, preferred_element_type=F32))
    return (x1_ref[...] + _unpack_halves(yg_ref[0]) * gw[:, 0:1]
            + _unpack_halves(yg_ref[1]) * gw[:, 1:2])


def _inproj_kernel(*refs, combine):
    if combine:
        x1_ref, yg_ref, gw_ref, g_ref, w_ref, rc_ref, rs1_ref, rs2_ref, o_ref, x_ref = refs
        x = _combine(x1_ref, yg_ref, gw_ref)
        x_ref[...] = x
    else:
        x_ref, g_ref, w_ref, rc_ref, rs1_ref, rs2_ref, o_ref = refs
        x = x_ref[...]
    y = _rms(x, g_ref[...]).astype(BF16)
    wide = lambda t_ref: jnp.concatenate([t_ref[...]] * (MXU_COLS // LANES), axis=1)
    rc, rs1, rs2 = wide(rc_ref), wide(rs1_ref), wide(rs2_ref)
    half = ROT_DIM // 2
    for c0 in range(0, IN_COLS, MXU_COLS):
        p = _dot(y, w_ref[:, c0:c0 + MXU_COLS])
        if C_QB <= c0 < C_VB:
            p = (p * rc + pltpu.roll(p, MXU_COLS - half, axis=1) * rs1
                 + pltpu.roll(p, half, axis=1) * rs2)
        if c0 < C_KA or C_QB <= c0 < C_KB:
            p = p * (ATTN_SCALE * LOG2E)
        o_ref[:, c0:c0 + MXU_COLS] = p.astype(BF16)


def _inproj(x2d, moe_out, g, w_bf16, rc, rs1, rs2, tm):
    T, D = x2d.shape
    row = lambda i: (i, 0)
    const = lambda i: (0, 0)
    combine = moe_out is not None
    x_specs = [pl.BlockSpec((tm, D), row)]
    x_args = [x2d]
    out_specs = [pl.BlockSpec((tm, IN_COLS), row)]
    out_shape = [jax.ShapeDtypeStruct((T, IN_COLS), BF16)]
    if combine:
        yg, gw = moe_out
        x_specs += [pl.BlockSpec((MOE_TOP_K, tm, D // 2), lambda i: (0, i, 0)),
                    pl.BlockSpec((GW_ROWS, tm), lambda i: (0, i))]
        x_args += [yg, gw]
        out_specs.append(pl.BlockSpec((tm, D), row))
        out_shape.append(jax.ShapeDtypeStruct((T, D), F32))
    outs = pl.pallas_call(
        functools.partial(_inproj_kernel, combine=combine),
        grid=(T // tm,),
        in_specs=x_specs + [pl.BlockSpec((1, D), const), pl.BlockSpec((D, IN_COLS), const),
                            pl.BlockSpec((tm, LANES), row), pl.BlockSpec((tm, LANES), row),
                            pl.BlockSpec((tm, LANES), row)],
        out_specs=out_specs,
        out_shape=out_shape,
        compiler_params=pltpu.CompilerParams(dimension_semantics=("arbitrary",),
                                             vmem_limit_bytes=VMEM_LIMIT),
        name="inproj",
    )(*x_args, g, w_bf16, rc, rs1, rs2)
    return (outs[0], outs[1]) if combine else (outs[0], x2d)


_TileUnit = collections.namedtuple("_TileUnit", "order scores step init final")


def _run_tiles(units, n_past):
    orders = [u.order(n_past) for u in units]
    carries = [u.init() for u in units]
    zs = [u.scores(o[0], True) for u, o in zip(units, orders)]
    for n in range(n_past + 1):
        nxt = [u.scores(o[n + 1], False) if n < n_past else None
               for u, o in zip(units, orders)]
        carries = [u.step(z, o[n], c, n == 0) for u, o, z, c in zip(units, orders, zs, carries)]
        zs = nxt
    return carries


def _sb_unit(q_ref, k_ref, v_ref, gh_ref, i, *, tq, tk, sub):
    n_heads = LANES // HEAD_DIM
    n_sub = tk // sub
    qt = q_ref[...].astype(F32).T
    zeros_t = jnp.zeros((HEAD_DIM, tq), F32)
    qz = [jnp.concatenate([qt[hh * HEAD_DIM:(hh + 1) * HEAD_DIM] if h2 == hh else zeros_t
                           for h2 in range(n_heads)], axis=0).astype(BF16)
          for hh in range(n_heads)]
    srow = lax.broadcasted_iota(jnp.int32, (sub, sub), 0)
    scol = lax.broadcasted_iota(jnp.int32, (sub, sub), 1)
    later = jnp.where(scol > srow, 1.0, 0.0).astype(BF16)
    assert tq == tk

    def log_rest(z):
        nz = -z
        return jnp.minimum(nz, 0.0) - jnp.log2(1.0 + jnp.exp2(jnp.minimum(z, nz)))

    def scores(t, diag):
        if not diag:
            return [_dot(k_ref[t * tk:(t + 1) * tk, :], qz[hh]) for hh in range(n_heads)]
        return [[_dot(k_ref[t * tk + sb * sub:t * tk + (sb + 1) * sub, :], qz[hh][:, sb * sub:])
                 for sb in range(n_sub)] for hh in range(n_heads)]

    def past_step(zs, t, carry):
        wts, runs = [], []
        for hh in range(n_heads):
            z = zs[hh]
            lr = log_rest(z)
            lrb = lr.astype(BF16)
            tail = carry[hh][0]
            afters = [None] * n_sub
            for sb in reversed(range(n_sub)):
                afters[sb] = _dot(later, lrb[sb * sub:(sb + 1) * sub]) + tail
                tail = afters[sb][0:1] + lr[sb * sub:sb * sub + 1]
            wts.append(jnp.exp2(z + lr + jnp.concatenate(afters, axis=0)).astype(BF16))
            runs.append(tail)
        new = []
        for hh in range(n_heads):
            pv = lax.dot_general(v_ref[t * tk:(t + 1) * tk, :], wts[hh],
                                 (((0,), (0,)), ((), ())), preferred_element_type=F32)
            new.append((runs[hh], carry[hh][1] + pv[hh * HEAD_DIM:(hh + 1) * HEAD_DIM]))
        return tuple(new)

    def diag_step(zs, t, carry):
        new = []
        for hh in range(n_heads):
            tail, acc = carry[hh]
            for sb in reversed(range(n_sub)):
                off = sb * sub
                z = zs[hh][sb]
                krow = lax.broadcasted_iota(jnp.int32, z.shape, 0)
                qcol = lax.broadcasted_iota(jnp.int32, z.shape, 1)
                causal = krow < qcol
                lr = jnp.where(causal, log_rest(z), 0.0)
                after = _dot(later, lr.astype(BF16)) + tail[:, off:]
                w = jnp.where(causal, jnp.exp2(z + lr + after), 0.0).astype(BF16)
                run = after[0:1] + lr[0:1]
                pv = lax.dot_general(v_ref[t * tk + off:t * tk + off + sub, :], w,
                                     (((0,), (0,)), ((), ())), preferred_element_type=F32)
                pv = pv[hh * HEAD_DIM:(hh + 1) * HEAD_DIM]
                if off:
                    run = jnp.concatenate([tail[:, :off], run], axis=1)
                    pv = jnp.concatenate([jnp.zeros((HEAD_DIM, off), F32), pv], axis=1)
                tail, acc = run, acc + pv
            new.append((tail, acc))
        return tuple(new)

    def weights_pv(zs, t, carry, diag):
        return diag_step(zs, t, carry) if diag else past_step(zs, t, carry)

    def init():
        return tuple((jnp.zeros((1, tq), F32), jnp.zeros((HEAD_DIM, tq), F32))
                     for _ in range(n_heads))

    def final(carry):
        outs = []
        for hh in range(n_heads):
            o = carry[hh][1]
            outs.append(o * lax.rsqrt(jnp.mean(o * o, axis=0, keepdims=True) + NORM_EPS))
        return (jnp.concatenate(outs, axis=0).T * gh_ref[...]).astype(BF16)

    return _TileUnit(order=lambda n_past: list(range(n_past, -1, -1)), scores=scores,
                     step=weights_pv, init=init, final=final)


def _moba_unit(q_ref, k_ref, v_ref, gh_ref, kaug_ref, vaug_ref, rhi_ref, rlo_ref, i,
               *, tq, tk, topk):
    S = k_ref.shape[0]
    n_heads = q_ref.shape[1] // HEAD_DIM

    @pl.when(i == 0)
    def _prepare_keys():
        kp = k_ref[...]
        vp = v_ref[...]
        srow = lax.broadcasted_iota(jnp.int32, (S, HEAD_DIM), 0)
        scol = lax.broadcasted_iota(jnp.int32, (S, HEAD_DIM), 1)
        onehot = jnp.where(srow // MOBA_BLOCK == scol, 1.0, 0.0).astype(BF16)
        arow = lax.broadcasted_iota(jnp.int32, (LANES, S), 0) - HEAD_DIM
        acol = lax.broadcasted_iota(jnp.int32, (LANES, S), 1) // MOBA_BLOCK
        avg = jnp.where(arow == acol, 1.0 / MOBA_BLOCK, 0.0).astype(BF16)
        zeros = jnp.zeros((S, HEAD_DIM), BF16)
        ones = jnp.ones((S, HEAD_DIM), BF16)
        for hh in range(n_heads):
            kh = kp[:, hh * HEAD_DIM:(hh + 1) * HEAD_DIM]
            kaug_ref[hh] = jnp.concatenate([kh, onehot], axis=1)
            vaug_ref[hh] = jnp.concatenate([vp[:, hh * HEAD_DIM:(hh + 1) * HEAD_DIM], ones], axis=1)
            kmean = _dot(avg, jnp.concatenate([kh, zeros], axis=1))
            hi, lo = _split_bf16(kmean)
            rhi_ref[hh] = hi
            rlo_ref[hh] = lo

    qt = q_ref[...].astype(F32).T
    group = 8
    bidx = lax.broadcasted_iota(jnp.int32, (group, tq), 0)
    zeros_t = jnp.zeros((HEAD_DIM, tq), F32)
    own = (i * tq + lax.broadcasted_iota(jnp.int32, (group, tq), 1)) // MOBA_BLOCK
    qaugs = []
    for hh in range(n_heads):
        qh = qt[hh * HEAD_DIM:(hh + 1) * HEAD_DIM]
        qz = jnp.concatenate([qh, zeros_t], axis=0).astype(BF16)
        gate = (_dot(rhi_ref[hh], qz) + _dot(rlo_ref[hh], qz))[HEAD_DIM:HEAD_DIM + group]
        valid = bidx < own
        gm = jnp.where(valid, gate, NEG)
        rank = jnp.zeros((group, tq), F32)
        for d in range(1, group):
            nb = pltpu.roll(gm, d, axis=0)
            rank = rank + jnp.where(bidx >= d, jnp.where(nb >= gm, 1.0, 0.0),
                                    jnp.where(nb > gm, 1.0, 0.0))
        allowed = (valid & (rank < topk)) | (bidx == own)
        bias = jnp.where(allowed, 0.0, NEG)
        qaugs.append(jnp.concatenate(
            [qh, bias, jnp.zeros((HEAD_DIM - group, tq), F32)], axis=0).astype(BF16))

    assert tq == tk
    n_sub = tk // MOBA_BLOCK

    def scores(t, diag):
        if not diag:
            return [_dot(kaug_ref[hh, t * tk:(t + 1) * tk, :], qaugs[hh])
                    for hh in range(n_heads)]
        return [[_dot(kaug_ref[hh, t * tk + sb * MOBA_BLOCK:t * tk + (sb + 1) * MOBA_BLOCK, :],
                      qaugs[hh][:, sb * MOBA_BLOCK:]) for sb in range(n_sub)]
                for hh in range(n_heads)]

    def past_step(sts, t, carry):
        pts, stats = [], []
        for hh in range(n_heads):
            m = carry[hh][0]
            m_new = jnp.maximum(m, jnp.max(sts[hh], axis=0, keepdims=True))
            pts.append(jnp.exp2(sts[hh] - m_new).astype(BF16))
            stats.append((m_new, jnp.exp2(m - m_new)))
        new = []
        for hh in range(n_heads):
            _, l, acc = carry[hh]
            m_new, alpha = stats[hh]
            pv = lax.dot_general(vaug_ref[hh, t * tk:(t + 1) * tk, :], pts[hh],
                                 (((0,), (0,)), ((), ())), preferred_element_type=F32)
            new.append((m_new, alpha * l + pv[HEAD_DIM:HEAD_DIM + 1],
                        alpha * acc + pv[:HEAD_DIM]))
        return tuple(new)

    def diag_step(sts, t, carry):
        new = []
        for hh in range(n_heads):
            m, l, acc = carry[hh]
            masked = []
            m_new = m
            for sb in range(n_sub):
                off = sb * MOBA_BLOCK
                st = sts[hh][sb]
                krow = lax.broadcasted_iota(jnp.int32, st.shape, 0)
                qcol = lax.broadcasted_iota(jnp.int32, st.shape, 1)
                st = jnp.where(krow <= qcol, st, NEG)
                masked.append(st)
                top = jnp.max(st, axis=0, keepdims=True)
                if off:
                    top = jnp.concatenate([jnp.full((1, off), NEG, F32), top], axis=1)
                m_new = jnp.maximum(m_new, top)
            alpha = jnp.exp2(m - m_new)
            pv = None
            for sb in range(n_sub):
                off = sb * MOBA_BLOCK
                pt = jnp.exp2(masked[sb] - m_new[:, off:]).astype(BF16)
                part = lax.dot_general(
                    vaug_ref[hh, t * tk + off:t * tk + off + MOBA_BLOCK, :], pt,
                    (((0,), (0,)), ((), ())), preferred_element_type=F32)
                if off:
                    part = jnp.concatenate([jnp.zeros((LANES, off), F32), part], axis=1)
                pv = part if pv is None else pv + part
            new.append((m_new, alpha * l + pv[HEAD_DIM:HEAD_DIM + 1],
                        alpha * acc + pv[:HEAD_DIM]))
        return tuple(new)

    def softmax_pv(sts, t, carry, diag):
        return diag_step(sts, t, carry) if diag else past_step(sts, t, carry)

    def init():
        return tuple((jnp.full((1, tq), NEG, F32), jnp.zeros((1, tq), F32),
                      jnp.zeros((HEAD_DIM, tq), F32)) for _ in range(n_heads))

    def final(carry):
        outs = []
        for hh in range(n_heads):
            _, l, acc = carry[hh]
            o = acc / l
            outs.append(o * lax.rsqrt(jnp.mean(o * o, axis=0, keepdims=True) + NORM_EPS))
        return (jnp.concatenate(outs, axis=0).T * gh_ref[...]).astype(BF16)

    return _TileUnit(order=lambda n_past: [n_past] + list(range(n_past)), scores=scores,
                     step=softmax_pv, init=init, final=final)


ATTN_TQ = 2 * MOBA_BLOCK
MOBA_COLS = 2 * LANES


def _attn_kernel(qa_ref, ka_ref, va_ref, qb_ref, kb_ref, vb_ref, gha_ref, ghb_ref,
                 oa_ref, ob_ref, kaug_ref, vaug_ref, rhi_ref, rlo_ref, *, tq, topk):
    i = pl.program_id(2)
    S = ka_ref.shape[0]
    sb = _sb_unit(qa_ref, ka_ref, va_ref, gha_ref, i, tq=tq, tk=tq, sub=MOBA_BLOCK)
    mb = _moba_unit(qb_ref, kb_ref, vb_ref, ghb_ref, kaug_ref, vaug_ref, rhi_ref, rlo_ref, i,
                    tq=tq, tk=tq, topk=topk)

    def attend(n_past):
        ca, cb = _run_tiles([sb, mb], n_past)
        oa_ref[...] = sb.final(ca)
        ob_ref[...] = mb.final(cb)

    for n_past in range(S // tq):
        pl.when(i == n_past)(functools.partial(attend, n_past))


def _attention(proj, gh_sb, gh_moba, B, S):
    tq = ATTN_TQ
    assert S % tq == 0 and W_MOBA // MOBA_COLS == W_SB // LANES
    nq = S // tq
    n_blk = S // MOBA_BLOCK
    assert n_blk <= 8
    topk = min(MOBA_TOPK, max(n_blk - 1, 1))
    n_mb_heads = MOBA_COLS // HEAD_DIM
    q_row = lambda b, p, i: b * nq + i
    return pl.pallas_call(
        functools.partial(_attn_kernel, tq=tq, topk=topk),
        grid=(B, W_SB // LANES, nq),
        in_specs=[pl.BlockSpec((tq, LANES), lambda b, p, i: (q_row(b, p, i), C_QA // LANES + p)),
                  pl.BlockSpec((S, LANES), lambda b, p, i: (b, C_KA // LANES + p)),
                  pl.BlockSpec((S, LANES), lambda b, p, i: (b, C_VA // LANES + p)),
                  pl.BlockSpec((tq, MOBA_COLS),
                               lambda b, p, i: (q_row(b, p, i), C_QB // MOBA_COLS + p)),
                  pl.BlockSpec((S, MOBA_COLS), lambda b, p, i: (b, C_KB // MOBA_COLS + p)),
                  pl.BlockSpec((S, MOBA_COLS), lambda b, p, i: (b, C_VB // MOBA_COLS + p)),
                  pl.BlockSpec((1, LANES), lambda b, p, i: (0, p)),
                  pl.BlockSpec((1, MOBA_COLS), lambda b, p, i: (0, p))],
        out_specs=[pl.BlockSpec((tq, LANES), lambda b, p, i: (q_row(b, p, i), p)),
                   pl.BlockSpec((tq, MOBA_COLS), lambda b, p, i: (q_row(b, p, i), p))],
        out_shape=[jax.ShapeDtypeStruct((B * S, W_SB), BF16),
                   jax.ShapeDtypeStruct((B * S, W_MOBA), BF16)],
        scratch_shapes=[pltpu.VMEM((n_mb_heads, S, LANES), BF16),
                        pltpu.VMEM((n_mb_heads, S, LANES), BF16),
                        pltpu.VMEM((n_mb_heads, LANES, LANES), BF16),
                        pltpu.VMEM((n_mb_heads, LANES, LANES), BF16)],
        compiler_params=pltpu.CompilerParams(
            dimension_semantics=("arbitrary", "arbitrary", "arbitrary"),
            vmem_limit_bytes=VMEM_LIMIT),
        name="attention",
    )(proj, proj, proj, proj, proj, proj, gh_sb, gh_moba)


def _gmlp_kernel(u_ref, v_ref, gv_ref, ws_ref, b_ref, gh_ref, o_ref, *, tm):
    row = lax.broadcasted_iota(jnp.int32, (GMLP_CHUNK, GMLP_CHUNK), 0)
    col = lax.broadcasted_iota(jnp.int32, (GMLP_CHUNK, GMLP_CHUNK), 1)
    seg_mean = jnp.where(row // HEAD_DIM == col // HEAD_DIM, 1.0 / HEAD_DIM, 0.0).astype(BF16)
    first_group = lax.broadcasted_iota(jnp.int32, (GMLP_CHUNK, LANES), 1) < HEAD_DIM

    def group_rms(x, g):
        hi, lo = _split_bf16(x * x)
        ms = _dot(hi, seg_mean) + _dot(lo, seg_mean)
        return x * lax.rsqrt(ms + NORM_EPS) * g

    for p in range(W_GMLP // LANES):
        lanes = slice(p * LANES, (p + 1) * LANES)
        gu = jax.nn.gelu(u_ref[:, lanes].astype(F32))
        vn = group_rms(jax.nn.gelu(v_ref[:, lanes].astype(F32)), gv_ref[:, lanes]).astype(BF16)
        wm = [jnp.where(col <= row, ws_ref[2 * p + j], 0.0).astype(BF16) for j in range(2)]
        bias = b_ref[:, lanes]
        mixed = jnp.concatenate(
            [jnp.where(first_group, _dot(wm[0], vn[c * GMLP_CHUNK:(c + 1) * GMLP_CHUNK]),
                       _dot(wm[1], vn[c * GMLP_CHUNK:(c + 1) * GMLP_CHUNK])) + bias
             for c in range(tm // GMLP_CHUNK)], axis=0)
        o_ref[:, lanes] = group_rms(gu * mixed, gh_ref[:, lanes]).astype(BF16)


def _gmlp(proj, gv, ws, b_exp, gh, tm):
    T = proj.shape[0]
    nu, nv = C_UC // W_GMLP, C_VC // W_GMLP
    return pl.pallas_call(
        functools.partial(_gmlp_kernel, tm=tm),
        grid=(T // tm,),
        in_specs=[pl.BlockSpec((tm, W_GMLP), lambda i: (i, nu)),
                  pl.BlockSpec((tm, W_GMLP), lambda i: (i, nv)),
                  pl.BlockSpec((1, W_GMLP), lambda i: (0, 0)),
                  pl.BlockSpec((N_GROUPS_GMLP, GMLP_CHUNK, GMLP_CHUNK), lambda i: (0, 0, 0)),
                  pl.BlockSpec((GMLP_CHUNK, W_GMLP), lambda i: (0, 0)),
                  pl.BlockSpec((1, W_GMLP), lambda i: (0, 0))],
        out_specs=pl.BlockSpec((tm, W_GMLP), lambda i: (i, 0)),
        out_shape=jax.ShapeDtypeStruct((T, W_GMLP), BF16),
        compiler_params=pltpu.CompilerParams(dimension_semantics=("arbitrary",),
                                             vmem_limit_bytes=VMEM_LIMIT),
        name="gmlp",
    )(proj, proj, gv, ws, b_exp, gh)


def _out_kernel(osb_ref, omoba_ref, ogmlp_ref, x_ref, w_ref, g_ref, rhi_ref, rlo_ref,
                x1_ref, h_ref, lg_ref):
    o = jnp.concatenate([osb_ref[...], omoba_ref[...], ogmlp_ref[...]], axis=1)
    x1 = x_ref[...] + _dot(o, w_ref[...])
    x1_ref[...] = x1
    hn = _rms(x1, g_ref[...])
    hi, lo = _split_bf16(hn)
    h_ref[...] = _pack_halves(hn)
    stacked = _nt_dot(jnp.concatenate([rhi_ref[...], rlo_ref[...]], axis=0), hi)
    lg_ref[...] = stacked[:ROUTER_ROWS] + stacked[ROUTER_ROWS:] + _nt_dot(rhi_ref[...], lo)


def _out_proj(o_sb, o_moba, o_gmlp, x2d, w_bf16, g, r_hi, r_lo, tm):
    T, D = x2d.shape
    row = lambda i: (i, 0)
    const = lambda i: (0, 0)
    return pl.pallas_call(
        _out_kernel,
        grid=(T // tm,),
        in_specs=[pl.BlockSpec((tm, W_SB), row), pl.BlockSpec((tm, W_MOBA), row),
                  pl.BlockSpec((tm, W_GMLP), row), pl.BlockSpec((tm, D), row),
                  pl.BlockSpec((D, D), const), pl.BlockSpec((1, D), const),
                  pl.BlockSpec((ROUTER_ROWS, D), const), pl.BlockSpec((ROUTER_ROWS, D), const)],
        out_specs=[pl.BlockSpec((tm, D), row), pl.BlockSpec((tm, D // 2), row),
                   pl.BlockSpec((ROUTER_ROWS, tm), lambda i: (0, i))],
        out_shape=[jax.ShapeDtypeStruct((T, D), F32), jax.ShapeDtypeStruct((T, D // 2), jnp.uint32),
                   jax.ShapeDtypeStruct((ROUTER_ROWS, T), F32)],
        compiler_params=pltpu.CompilerParams(dimension_semantics=("arbitrary",),
                                             vmem_limit_bytes=VMEM_LIMIT),
        name="outproj",
    )(o_sb, o_moba, o_gmlp, x2d, w_bf16, g, r_hi, r_lo)


ROUTER_EXPERT_ROW = 8
GW_ROWS = 8
ROUTER_TM = 2048
ROUTER_CHUNK = 512


def _first_max(p):
    rows = lax.broadcasted_iota(jnp.int32, p.shape, 0).astype(F32)
    top = jnp.max(p, axis=0, keepdims=True)
    idx = jnp.min(jnp.where(p == top, rows, float(p.shape[0])), axis=0, keepdims=True)
    return top, idx, rows


def _router_kernel(lg_ref, pos_ref, gw_ref, cnt_ref, cnt_acc, base_ref, *, tm):
    phase = pl.program_id(0)
    i = pl.program_id(1)
    lg = lg_ref[...]
    gl = lg[0:N_EXPERT_GROUPS]
    ge = jnp.exp(gl - jnp.max(gl, axis=0, keepdims=True))
    p_group = ge / jnp.sum(ge, axis=0, keepdims=True)
    p_g, g_sel, _ = _first_max(p_group)
    le = jnp.zeros((EXPERTS_PER_GROUP, tm), F32)
    for g in range(N_EXPERT_GROUPS):
        r0 = ROUTER_EXPERT_ROW + g * EXPERTS_PER_GROUP
        le = jnp.where(g_sel == g, lg[r0:r0 + EXPERTS_PER_GROUP], le)
    ee = jnp.exp(le - jnp.max(le, axis=0, keepdims=True))
    p = ee / jnp.sum(ee, axis=0, keepdims=True)
    p0, i0, rows = _first_max(p)
    p1, i1, _ = _first_max(jnp.where(rows == i0, -1.0, p))
    e0 = g_sel * EXPERTS_PER_GROUP + i0
    e1 = g_sel * EXPERTS_PER_GROUP + i1
    xrow = lax.broadcasted_iota(jnp.int32, (N_EXPERTS, tm), 0).astype(F32)
    oh0 = xrow == e0
    oh1 = xrow == e1
    slots = jnp.where(oh0, 1.0, 0.0) + jnp.where(oh1, 1.0, 0.0)
    tile_cnt = slots[:, 0:LANES]
    for c in range(1, tm // LANES):
        tile_cnt = tile_cnt + slots[:, c * LANES:(c + 1) * LANES]

    @pl.when((phase == 0) & (i == 0))
    def _init():
        cnt_acc[...] = jnp.zeros_like(cnt_acc)

    @pl.when(phase == 0)
    def _count():
        cnt_acc[...] += tile_cnt

    @pl.when((phase == 1) & (i == 0))
    def _starts():
        counts = jnp.sum(cnt_acc[...], axis=1, keepdims=True)
        n_blk = jnp.floor((counts + (MOE_BLOCK - 1)) * (1.0 / MOE_BLOCK))
        er = lax.broadcasted_iota(jnp.int32, (N_EXPERTS, N_EXPERTS), 0)
        ec = lax.broadcasted_iota(jnp.int32, (N_EXPERTS, N_EXPERTS), 1)
        before = jnp.where(ec < er, 1.0, 0.0).astype(BF16)
        start_blk = _dot(before, jnp.broadcast_to(n_blk, (N_EXPERTS, LANES)).astype(BF16))
        base_ref[...] = start_blk * MOE_BLOCK
        cnt_ref[...] = jnp.broadcast_to(counts, (N_EXPERTS, LANES)).astype(jnp.int32)

    @pl.when(phase == 1)
    def _assign():
        chunk = min(ROUTER_CHUNK, tm)
        tr = lax.broadcasted_iota(jnp.int32, (chunk, chunk), 0)
        tc = lax.broadcasted_iota(jnp.int32, (chunk, chunk), 1)
        earlier = jnp.where(tr < tc, 1.0, 0.0).astype(BF16)
        run = base_ref[:, 0:1]
        parts = []
        for c in range(tm // chunk):
            part = slots[:, c * chunk:(c + 1) * chunk]
            parts.append(_dot(part.astype(BF16), earlier) + run)
            run = run + jnp.sum(part, axis=1, keepdims=True)
        row_of = jnp.concatenate(parts, axis=1)
        pos0 = jnp.sum(jnp.where(oh0, row_of, 0.0), axis=0, keepdims=True)
        pos1 = jnp.sum(jnp.where(oh1, row_of, 0.0), axis=0, keepdims=True)
        pos_ref[...] = jnp.concatenate([pos0, pos1], axis=0).astype(jnp.int32)
        scale = p_g / (p0 + p1)
        gw_ref[...] = jnp.concatenate(
            [p0 * scale, p1 * scale, jnp.zeros((GW_ROWS - MOE_TOP_K, tm), F32)], axis=0)
        base_ref[...] += jnp.sum(tile_cnt, axis=1, keepdims=True)


def _router(logits_t, tm):
    T = logits_t.shape[1]
    tok = lambda p, i: (0, i * p)
    return pl.pallas_call(
        functools.partial(_router_kernel, tm=tm),
        grid=(2, T // tm),
        in_specs=[pl.BlockSpec((ROUTER_ROWS, tm), lambda p, i: (0, i))],
        out_specs=[pl.BlockSpec((MOE_TOP_K, tm), tok), pl.BlockSpec((GW_ROWS, tm), tok),
                   pl.BlockSpec((N_EXPERTS, LANES), lambda p, i: (0, 0))],
        out_shape=[jax.ShapeDtypeStruct((MOE_TOP_K, T), jnp.int32),
                   jax.ShapeDtypeStruct((GW_ROWS, T), F32),
                   jax.ShapeDtypeStruct((N_EXPERTS, LANES), jnp.int32)],
        scratch_shapes=[pltpu.VMEM((N_EXPERTS, LANES), F32), pltpu.VMEM((N_EXPERTS, LANES), F32)],
        compiler_params=pltpu.CompilerParams(dimension_semantics=("arbitrary", "arbitrary")),
        name="router",
    )(logits_t)


def _moe_kernel(be_ref, first_ref, slot_ref, next_ref, rows_ref, na_ref, xs_ref, wg_hbm, wu_hbm,
                wd_hbm, ys_ref, wgf, wuf, wdf, wgb, wub, wdb, sem, *, layer):
    b = pl.program_id(0)
    e = be_ref[b]
    slot = slot_ref[b]
    active = b < na_ref[0]

    def fetch(expert, s):
        return (pltpu.make_async_copy(wg_hbm.at[layer, expert], wgf.at[s], sem.at[s, 0]),
                pltpu.make_async_copy(wu_hbm.at[layer, expert], wuf.at[s], sem.at[s, 1]),
                pltpu.make_async_copy(wd_hbm.at[layer, expert], wdf.at[s], sem.at[s, 2]))

    @pl.when(b == 0)
    def _first_fetch():
        for c in fetch(e, slot):
            c.start()

    @pl.when(active & (first_ref[b] == 1))
    def _load_expert():
        for c in fetch(e, slot):
            c.wait()
        wgb[...] = wgf[slot].astype(BF16)
        wub[...] = wuf[slot].astype(BF16)
        wdb[...] = wdf[slot].astype(BF16)
        nxt = next_ref[b]

        @pl.when(nxt >= 0)
        def _prefetch():
            for c in fetch(nxt, 1 - slot):
                c.start(priority=1)

    @pl.when(active)
    def _compute():
        row = lax.broadcasted_iota(jnp.int32, xs_ref.shape, 0)
        words = jnp.where(row < rows_ref[b], xs_ref[...], jnp.uint32(0))
        xb = _unpack_halves(words).astype(BF16)
        a = jax.nn.silu(_dot(xb, wgb[...])) * _dot(xb, wub[...])
        ys_ref[...] = _pack_halves(_dot(a.astype(BF16), wdb[...]))


def _moe_experts(plan, xs, w_gate, w_up, w_down, layer):
    n_rows = xs.shape[0]
    D = 2 * xs.shape[1]
    n_blocks = n_rows // MOE_BLOCK
    DE = w_gate.shape[-1]
    rows = lambda b, *plan_refs: (jnp.minimum(b, plan_refs[-1][0] - 1), 0)
    grid_spec = pltpu.PrefetchScalarGridSpec(
        num_scalar_prefetch=len(plan),
        grid=(n_blocks,),
        in_specs=[pl.BlockSpec((MOE_BLOCK, D // 2), rows),
                  pl.BlockSpec(memory_space=pl.ANY), pl.BlockSpec(memory_space=pl.ANY),
                  pl.BlockSpec(memory_space=pl.ANY)],
        out_specs=pl.BlockSpec((MOE_BLOCK, D // 2), rows),
        scratch_shapes=[pltpu.VMEM((2, D, DE), F32), pltpu.VMEM((2, D, DE), F32),
                        pltpu.VMEM((2, DE, D), F32),
                        pltpu.VMEM((D, DE), BF16), pltpu.VMEM((D, DE), BF16),
                        pltpu.VMEM((DE, D), BF16),
                        pltpu.SemaphoreType.DMA((2, 3))])
    return pl.pallas_call(
        functools.partial(_moe_kernel, layer=layer),
        grid_spec=grid_spec,
        out_shape=jax.ShapeDtypeStruct((n_rows, D // 2), jnp.uint32),
        compiler_params=pltpu.CompilerParams(dimension_semantics=("arbitrary",),
                                             vmem_limit_bytes=VMEM_LIMIT),
        name="moe_experts",
    )(*plan, xs, w_gate, w_up, w_down)


def _final_kernel(x1_ref, yg_ref, gw_ref, g_ref, o_ref):
    o_ref[...] = _rms(_combine(x1_ref, yg_ref, gw_ref), g_ref[...])


def _final_norm(x2d, yg, gw, g, tm):
    T, D = x2d.shape
    row = lambda i: (i, 0)
    return pl.pallas_call(
        _final_kernel,
        grid=(T // tm,),
        in_specs=[pl.BlockSpec((tm, D), row),
                  pl.BlockSpec((MOE_TOP_K, tm, D // 2), lambda i: (0, i, 0)),
                  pl.BlockSpec((GW_ROWS, tm), lambda i: (0, i)),
                  pl.BlockSpec((1, D), lambda i: (0, 0))],
        out_specs=pl.BlockSpec((tm, D), row),
        out_shape=jax.ShapeDtypeStruct((T, D), F32),
        compiler_params=pltpu.CompilerParams(dimension_semantics=("arbitrary",)),
        name="final_norm",
    )(x2d, yg, gw, g)


SC_GATHER_ROWS = 64


def _sc_gather(table, idx):
    info = plsc.get_sparse_core_info()
    n_cores, n_workers = info.num_cores, info.num_cores * info.num_subcores
    N, W = idx.shape[0], table.shape[1]
    per_w = N // n_workers
    n_ch = per_w // SC_GATHER_ROWS
    assert per_w * n_workers == N and n_ch * SC_GATHER_ROWS == per_w
    mesh = plsc.VectorSubcoreMesh(core_axis_name="c", subcore_axis_name="s")

    @functools.partial(
        pl.kernel, mesh=mesh,
        out_type=jax.ShapeDtypeStruct((N, W), table.dtype),
        scratch_types=[pltpu.VMEM((n_ch, SC_GATHER_ROWS), jnp.int32),
                       pltpu.VMEM((2, SC_GATHER_ROWS, W), table.dtype),
                       pltpu.SemaphoreType.DMA((2,)),
                       pltpu.SemaphoreType.DMA((2,))])
    def gather_kernel(table_hbm, idx_hbm, out_hbm, idx_v, rows_v, gsem, ssem):
        wid = lax.axis_index("s") * n_cores + lax.axis_index("c")
        base = wid * per_w
        pltpu.sync_copy(idx_hbm.at[wid], idx_v)

        def fetch(c):
            return pltpu.make_async_copy(table_hbm.at[idx_v.at[c]], rows_v.at[c % 2],
                                         gsem.at[c % 2])

        def write(c):
            return pltpu.make_async_copy(
                rows_v.at[c % 2], out_hbm.at[pl.ds(base + c * SC_GATHER_ROWS, SC_GATHER_ROWS)],
                ssem.at[c % 2])

        fetch(0).start()
        for c in range(n_ch):
            if c + 1 < n_ch:
                if c >= 1:
                    write(c - 1).wait()
                fetch(c + 1).start()
            fetch(c).wait()
            write(c).start()
        if n_ch >= 2:
            write(n_ch - 2).wait()
        write(n_ch - 1).wait()

    return gather_kernel(table, idx.reshape(n_workers, n_ch, SC_GATHER_ROWS))


def _sc_scatter(rows, pos, n_out):
    info = plsc.get_sparse_core_info()
    n_cores, n_workers = info.num_cores, info.num_cores * info.num_subcores
    K, T = pos.shape
    W = rows.shape[1]
    per_w = T // n_workers
    n_ch = per_w // SC_GATHER_ROWS
    assert per_w * n_workers == T and n_ch * SC_GATHER_ROWS == per_w
    mesh = plsc.VectorSubcoreMesh(core_axis_name="c", subcore_axis_name="s")

    @functools.partial(
        pl.kernel, mesh=mesh,
        out_type=jax.ShapeDtypeStruct((n_out, W), rows.dtype),
        scratch_types=[pltpu.VMEM((K, n_ch, SC_GATHER_ROWS), jnp.int32),
                       pltpu.VMEM((2, SC_GATHER_ROWS, W), rows.dtype),
                       pltpu.SemaphoreType.DMA((2,)),
                       pltpu.SemaphoreType.DMA((2, K))])
    def scatter_kernel(rows_hbm, idx_hbm, out_hbm, idx_v, buf_v, lsem, ssem):
        wid = lax.axis_index("s") * n_cores + lax.axis_index("c")
        base = wid * per_w
        for k in range(K):
            pltpu.sync_copy(idx_hbm.at[k, wid], idx_v.at[k])

        def load(c):
            return pltpu.make_async_copy(
                rows_hbm.at[pl.ds(base + c * SC_GATHER_ROWS, SC_GATHER_ROWS)], buf_v.at[c % 2],
                lsem.at[c % 2])

        def store(c, k):
            return pltpu.make_async_copy(buf_v.at[c % 2], out_hbm.at[idx_v.at[k, c]],
                                         ssem.at[c % 2, k])

        load(0).start()
        for c in range(n_ch):
            if c + 1 < n_ch:
                if c >= 1:
                    for k in range(K):
                        store(c - 1, k).wait()
                load(c + 1).start()
            load(c).wait()
            for k in range(K):
                store(c, k).start()
        for c in range(max(n_ch - 2, 0), n_ch):
            for k in range(K):
                store(c, k).wait()

    return scatter_kernel(rows, pos.reshape(K, n_workers, n_ch, SC_GATHER_ROWS))


def _rope_tables(positions):
    half = ROT_DIM // 2
    inv_freq = jnp.power(ROPE_THETA, -jnp.arange(half, dtype=F32) / half)
    ang = positions.astype(F32).reshape(-1)[:, None] * inv_freq
    cos, sin = jnp.cos(ang), jnp.sin(ang)
    T = ang.shape[0]
    ones = jnp.ones((T, HEAD_DIM - ROT_DIM), F32)
    zeros = jnp.zeros((T, HEAD_DIM - ROT_DIM), F32)
    z8 = jnp.zeros((T, half), F32)
    rc = jnp.concatenate([cos, cos, ones], axis=1)
    rs1 = jnp.concatenate([-sin, z8, zeros], axis=1)
    rs2 = jnp.concatenate([z8, sin, zeros], axis=1)
    rep = LANES // HEAD_DIM
    return jnp.tile(rc, (1, rep)), jnp.tile(rs1, (1, rep)), jnp.tile(rs2, (1, rep))


def _block_plan(counts, T):
    i32 = jnp.int32
    n_blocks = -(-(T * MOE_TOP_K) // MOE_BLOCK) + N_EXPERTS
    experts = jnp.arange(N_EXPERTS, dtype=i32)
    blocks = jnp.arange(n_blocks, dtype=i32)
    n_blk = (counts + MOE_BLOCK - 1) // MOE_BLOCK
    blocks_end = jnp.cumsum(n_blk)
    blk_expert = jnp.minimum(jnp.sum(blocks_end[None, :] <= blocks[:, None], axis=1),
                             N_EXPERTS - 1).astype(i32)
    is_active = n_blk > 0
    ordinal = jnp.cumsum(is_active.astype(i32)) - 1
    later_active = is_active[None, :] & (experts[None, :] > experts[:, None])
    next_active = jnp.min(jnp.where(later_active, experts[None, :], N_EXPERTS), axis=1)
    next_active = jnp.where(next_active < N_EXPERTS, next_active, -1).astype(i32)
    tables = jnp.stack([blocks_end - n_blk, counts, ordinal, next_active], axis=1).astype(i32)
    onehot = (blk_expert[:, None] == experts[None, :]).astype(i32)
    looked = jnp.sum(onehot[:, :, None] * tables[None, :, :], axis=1)
    blk_in_expert = blocks - looked[:, 0]
    blk_first = (blk_in_expert == 0).astype(i32)
    blk_rows = jnp.clip(looked[:, 1] - blk_in_expert * MOE_BLOCK, 0, MOE_BLOCK).astype(i32)
    plan = (blk_expert, blk_first, (looked[:, 2] % 2).astype(i32), looked[:, 3], blk_rows,
            blocks_end[-1:].astype(i32))
    return n_blocks, plan


def kernel(x, positions, w_in, w_out, g_mix_norm, g_head_norm, g_gmlp_vnorm, w_spatial, b_spatial,
           g_ffn_norm, w_router_group, w_router_expert, w_expert_gate, w_expert_up, w_expert_down,
           g_final):
    B, S, D = x.shape
    T = B * S
    depth = w_in.shape[0]
    tm = min(512, T)
    rc, rs1, rs2 = _rope_tables(positions)
    xc = x.reshape(T, D)
    moe_out = None
    for l in range(depth):
        gh = g_head_norm[l].reshape(1, -1)
        proj, xc = _inproj(xc, moe_out, g_mix_norm[l].reshape(1, D), w_in[l].astype(BF16),
                           rc, rs1, rs2, tm)
        o_sb, o_moba = _attention(proj, gh[:, :W_SB], gh[:, W_SB:W_SB + W_MOBA], B, S)
        b_exp = jnp.repeat(b_spatial[l].T, HEAD_DIM, axis=1)
        o_gmlp = _gmlp(proj, g_gmlp_vnorm[l].reshape(1, -1), w_spatial[l], b_exp,
                       gh[:, W_SB + W_MOBA:], tm)
        w_r = jnp.concatenate(
            [w_router_group[l].T, jnp.zeros((ROUTER_EXPERT_ROW - N_EXPERT_GROUPS, D), F32),
             w_router_expert[l].T,
             jnp.zeros((ROUTER_ROWS - ROUTER_EXPERT_ROW - N_EXPERTS, D), F32)], axis=0)
        r_hi, r_lo = _split_bf16(w_r)
        xc, h, logits_t = _out_proj(o_sb, o_moba, o_gmlp, xc, w_out[l].astype(BF16),
                                    g_ffn_norm[l].reshape(1, D), r_hi, r_lo, tm)
        pos, gate_w, counts = _router(logits_t, min(ROUTER_TM, T))
        n_blocks, plan = _block_plan(counts[:, 0], T)
        xs = _sc_scatter(h, pos, n_blocks * MOE_BLOCK)
        ys = _moe_experts(plan, xs, w_expert_gate, w_expert_up, w_expert_down, l)
        yg = _sc_gather(ys, pos.reshape(-1)).reshape(MOE_TOP_K, T, D // 2)
        moe_out = (yg, gate_w)
    return _final_norm(xc, moe_out[0], moe_out[1], g_final.reshape(1, D), tm).reshape(B, S, D)
```

```python
import collections
import functools

import jax
import jax.numpy as jnp
from jax import lax
from jax.experimental import pallas as pl
from jax.experimental.pallas import tpu as pltpu
from jax.experimental.pallas import tpu_sc as plsc

F32 = jnp.float32
BF16 = jnp.bfloat16

HEAD_DIM = 64
LANES = 128
MXU_COLS = 256
N_HEADS_SB = 4
N_HEADS_MOBA = 8
N_GROUPS_GMLP = 4
W_SB = N_HEADS_SB * HEAD_DIM
W_MOBA = N_HEADS_MOBA * HEAD_DIM
W_GMLP = N_GROUPS_GMLP * HEAD_DIM
MOBA_BLOCK = 256
MOBA_TOPK = 3
GMLP_CHUNK = 128
ROPE_THETA = 500000.0
ROT_DIM = HEAD_DIM // 4
N_EXPERT_GROUPS = 4
EXPERTS_PER_GROUP = 8
N_EXPERTS = N_EXPERT_GROUPS * EXPERTS_PER_GROUP
MOE_TOP_K = 2
MOE_BLOCK = 256
NORM_EPS = 1e-6
ATTN_SCALE = HEAD_DIM ** -0.5
NEG = -1e30
LOG2E = 1.4426950408889634
ROUTER_ROWS = 64
VMEM_LIMIT = 48 * 1024 * 1024

C_QA, C_KA, C_VA = 0, W_SB, 2 * W_SB
C_QB = 3 * W_SB
C_KB = C_QB + W_MOBA
C_VB = C_KB + W_MOBA
C_UC = C_VB + W_MOBA
C_VC = C_UC + W_GMLP
IN_COLS = C_VC + W_GMLP


def _nt_dot(a, b):
    return lax.dot_general(a, b, (((1,), (1,)), ((), ())), preferred_element_type=F32)


def _dot(a, b):
    return jnp.dot(a, b, preferred_element_type=F32)


def _rms(x, g):
    return x * lax.rsqrt(jnp.mean(x * x, axis=-1, keepdims=True) + NORM_EPS) * g


def _pack_halves(x):
    w = x.shape[1] // 2
    lo = lax.bitcast_convert_type(x[:, :w].astype(BF16).astype(F32), jnp.uint32)
    hi = lax.bitcast_convert_type(x[:, w:].astype(BF16).astype(F32), jnp.uint32)
    return (lo >> 16) | hi


def _unpack_halves(words):
    lo = lax.bitcast_convert_type(words << 16, F32)
    hi = lax.bitcast_convert_type(words & jnp.uint32(0xFFFF0000), F32)
    return jnp.concatenate([lo, hi], axis=1)


def _split_bf16(x):
    hi = x.astype(BF16)
    lo = (x - hi.astype(F32)).astype(BF16)
    return hi, lo


def _combine(x1_ref, yg_ref, gw_ref):
    rows = gw_ref.shape[0]
    eye = jnp.where(lax.broadcasted_iota(jnp.int32, (rows, LANES), 0)
                    == lax.broadcasted_iota(jnp.int32, (rows, LANES), 1), 1.0, 0.0).astype(BF16)
    tn = (((0,), (0,)), ((), ()))
    hi, lo = _split_bf16(gw_ref[...])
    gw = (lax.dot_general(hi, eye, tn, preferred_element_type=F32)
          + lax.dot_general(lo, eye, tn, preferred_element_type=F32))
    return (x1_ref[...] + _unpack_halves(yg_ref[0]) * gw[:, 0:1]
            + _unpack_halves(yg_ref[1]) * gw[:, 1:2])


def _inproj_kernel(*refs, combine):
    if combine:
        x1_ref, yg_ref, gw_ref, g_ref, w_ref, rc_ref, rs1_ref, rs2_ref, o_ref, x_ref = refs
        x = _combine(x1_ref, yg_ref, gw_ref)
        x_ref[...] = x
    else:
        x_ref, g_ref, w_ref, rc_ref, rs1_ref, rs2_ref, o_ref = refs
        x = x_ref[...]
    y = _rms(x, g_ref[...]).astype(BF16)
    wide = lambda t_ref: jnp.concatenate([t_ref[...]] * (MXU_COLS // LANES), axis=1)
    rc, rs1, rs2 = wide(rc_ref), wide(rs1_ref), wide(rs2_ref)
    half = ROT_DIM // 2
    for c0 in range(0, IN_COLS, MXU_COLS):
        p = _dot(y, w_ref[:, c0:c0 + MXU_COLS])
        if C_QB <= c0 < C_VB:
            p = (p * rc + pltpu.roll(p, MXU_COLS - half, axis=1) * rs1
                 + pltpu.roll(p, half, axis=1) * rs2)
        if c0 < C_KA or C_QB <= c0 < C_KB:
            p = p * (ATTN_SCALE * LOG2E)
        o_ref[:, c0:c0 + MXU_COLS] = p.astype(BF16)


def _inproj(x2d, moe_out, g, w_bf16, rc, rs1, rs2, tm):
    T, D = x2d.shape
    row = lambda i: (i, 0)
    const = lambda i: (0, 0)
    combine = moe_out is not None
    x_specs = [pl.BlockSpec((tm, D), row)]
    x_args = [x2d]
    out_specs = [pl.BlockSpec((tm, IN_COLS), row)]
    out_shape = [jax.ShapeDtypeStruct((T, IN_COLS), BF16)]
    if combine:
        yg, gw = moe_out
        x_specs += [pl.BlockSpec((MOE_TOP_K, tm, D // 2), lambda i: (0, i, 0)),
                    pl.BlockSpec((GW_ROWS, tm), lambda i: (0, i))]
        x_args += [yg, gw]
        out_specs.append(pl.BlockSpec((tm, D), row))
        out_shape.append(jax.ShapeDtypeStruct((T, D), F32))
    outs = pl.pallas_call(
        functools.partial(_inproj_kernel, combine=combine),
        grid=(T // tm,),
        in_specs=x_specs + [pl.BlockSpec((1, D), const), pl.BlockSpec((D, IN_COLS), const),
                            pl.BlockSpec((tm, LANES), row), pl.BlockSpec((tm, LANES), row),
                            pl.BlockSpec((tm, LANES), row)],
        out_specs=out_specs,
        out_shape=out_shape,
        compiler_params=pltpu.CompilerParams(dimension_semantics=("arbitrary",),
                                             vmem_limit_bytes=VMEM_LIMIT),
        name="inproj",
    )(*x_args, g, w_bf16, rc, rs1, rs2)
    return (outs[0], outs[1]) if combine else (outs[0], x2d)


_TileUnit = collections.namedtuple("_TileUnit", "order scores step init final")


def _run_tiles(units, n_past):
    orders = [u.order(n_past) for u in units]
    carries = [u.init() for u in units]
    zs = [u.scores(o[0], True) for u, o in zip(units, orders)]
    for n in range(n_past + 1):
        nxt = [u.scores(o[n + 1], False) if n < n_past else None
               for u, o in zip(units, orders)]
        carries = [u.step(z, o[n], c, n == 0) for u, o, z, c in zip(units, orders, zs, carries)]
        zs = nxt
    return carries


def _sb_unit(q_ref, k_ref, v_ref, gh_ref, i, *, tq, tk, sub):
    n_heads = LANES // HEAD_DIM
    n_sub = tk // sub
    qt = q_ref[...].astype(F32).T
    zeros_t = jnp.zeros((HEAD_DIM, tq), F32)
    qz = [jnp.concatenate([qt[hh * HEAD_DIM:(hh + 1) * HEAD_DIM] if h2 == hh else zeros_t
                           for h2 in range(n_heads)], axis=0).astype(BF16)
          for hh in range(n_heads)]
    srow = lax.broadcasted_iota(jnp.int32, (sub, sub), 0)
    scol = lax.broadcasted_iota(jnp.int32, (sub, sub), 1)
    later = jnp.where(scol > srow, 1.0, 0.0).astype(BF16)
    assert tq == tk

    def log_rest(z):
        nz = -z
        return jnp.minimum(nz, 0.0) - jnp.log2(1.0 + jnp.exp2(jnp.minimum(z, nz)))

    def scores(t, diag):
        if not diag:
            return [_dot(k_ref[t * tk:(t + 1) * tk, :], qz[hh]) for hh in range(n_heads)]
        return [[_dot(k_ref[t * tk + sb * sub:t * tk + (sb + 1) * sub, :], qz[hh][:, sb * sub:])
                 for sb in range(n_sub)] for hh in range(n_heads)]

    def past_step(zs, t, carry):
        wts, runs = [], []
        for hh in range(n_heads):
            z = zs[hh]
            lr = log_rest(z)
            lrb = lr.astype(BF16)
            tail = carry[hh][0]
            afters = [None] * n_sub
            for sb in reversed(range(n_sub)):
                afters[sb] = _dot(later, lrb[sb * sub:(sb + 1) * sub]) + tail
                tail = afters[sb][0:1] + lr[sb * sub:sb * sub + 1]
            wts.append(jnp.exp2(z + lr + jnp.concatenate(afters, axis=0)).astype(BF16))
            runs.append(tail)
        new = []
        for hh in range(n_heads):
            pv = lax.dot_general(v_ref[t * tk:(t + 1) * tk, :], wts[hh],
                                 (((0,), (0,)), ((), ())), preferred_element_type=F32)
            new.append((runs[hh], carry[hh][1] + pv[hh * HEAD_DIM:(hh + 1) * HEAD_DIM]))
        return tuple(new)

    def diag_step(zs, t, carry):
        new = []
        for hh in range(n_heads):
            tail, acc = carry[hh]
            for sb in reversed(range(n_sub)):
                off = sb * sub
                z = zs[hh][sb]
                krow = lax.broadcasted_iota(jnp.int32, z.shape, 0)
                qcol = lax.broadcasted_iota(jnp.int32, z.shape, 1)
                causal = krow < qcol
                lr = jnp.where(causal, log_rest(z), 0.0)
                after = _dot(later, lr.astype(BF16)) + tail[:, off:]
                w = jnp.where(causal, jnp.exp2(z + lr + after), 0.0).astype(BF16)
                run = after[0:1] + lr[0:1]
                pv = lax.dot_general(v_ref[t * tk + off:t * tk + off + sub, :], w,
                                     (((0,), (0,)), ((), ())), preferred_element_type=F32)
                pv = pv[hh * HEAD_DIM:(hh + 1) * HEAD_DIM]
                if off:
                    run = jnp.concatenate([tail[:, :off], run], axis=1)
                    pv = jnp.concatenate([jnp.zeros((HEAD_DIM, off), F32), pv], axis=1)
                tail, acc = run, acc + pv
            new.append((tail, acc))
        return tuple(new)

    def weights_pv(zs, t, carry, diag):
        return diag_step(zs, t, carry) if diag else past_step(zs, t, carry)

    def init():
        return tuple((jnp.zeros((1, tq), F32), jnp.zeros((HEAD_DIM, tq), F32))
                     for _ in range(n_heads))

    def final(carry):
        outs = []
        for hh in range(n_heads):
            o = carry[hh][1]
            outs.append(o * lax.rsqrt(jnp.mean(o * o, axis=0, keepdims=True) + NORM_EPS))
        return (jnp.concatenate(outs, axis=0).T * gh_ref[...]).astype(BF16)

    return _TileUnit(order=lambda n_past: list(range(n_past, -1, -1)), scores=scores,
                     step=weights_pv, init=init, final=final)


def _moba_unit(q_ref, k_ref, v_ref, gh_ref, kaug_ref, vaug_ref, rhi_ref, rlo_ref, i,
               *, tq, tk, topk):
    S = k_ref.shape[0]
    n_heads = q_ref.shape[1] // HEAD_DIM

    @pl.when(i == 0)
    def _prepare_keys():
        kp = k_ref[...]
        vp = v_ref[...]
        srow = lax.broadcasted_iota(jnp.int32, (S, HEAD_DIM), 0)
        scol = lax.broadcasted_iota(jnp.int32, (S, HEAD_DIM), 1)
        onehot = jnp.where(srow // MOBA_BLOCK == scol, 1.0, 0.0).astype(BF16)
        arow = lax.broadcasted_iota(jnp.int32, (LANES, S), 0) - HEAD_DIM
        acol = lax.broadcasted_iota(jnp.int32, (LANES, S), 1) // MOBA_BLOCK
        avg = jnp.where(arow == acol, 1.0 / MOBA_BLOCK, 0.0).astype(BF16)
        zeros = jnp.zeros((S, HEAD_DIM), BF16)
        ones = jnp.ones((S, HEAD_DIM), BF16)
        for hh in range(n_heads):
            kh = kp[:, hh * HEAD_DIM:(hh + 1) * HEAD_DIM]
            kaug_ref[hh] = jnp.concatenate([kh, onehot], axis=1)
            vaug_ref[hh] = jnp.concatenate([vp[:, hh * HEAD_DIM:(hh + 1) * HEAD_DIM], ones], axis=1)
            kmean = _dot(avg, jnp.concatenate([kh, zeros], axis=1))
            hi, lo = _split_bf16(kmean)
            rhi_ref[hh] = hi
            rlo_ref[hh] = lo

    qt = q_ref[...].astype(F32).T
    group = 8
    bidx = lax.broadcasted_iota(jnp.int32, (group, tq), 0)
    zeros_t = jnp.zeros((HEAD_DIM, tq), F32)
    own = (i * tq + lax.broadcasted_iota(jnp.int32, (group, tq), 1)) // MOBA_BLOCK
    qaugs = []
    for hh in range(n_heads):
        qh = qt[hh * HEAD_DIM:(hh + 1) * HEAD_DIM]
        qz = jnp.concatenate([qh, zeros_t], axis=0).astype(BF16)
        gate = (_dot(rhi_ref[hh], qz) + _dot(rlo_ref[hh], qz))[HEAD_DIM:HEAD_DIM + group]
        valid = bidx < own
        gm = jnp.where(valid, gate, NEG)
        rank = jnp.zeros((group, tq), F32)
        for d in range(1, group):
            nb = pltpu.roll(gm, d, axis=0)
            rank = rank + jnp.where(bidx >= d, jnp.where(nb >= gm, 1.0, 0.0),
                                    jnp.where(nb > gm, 1.0, 0.0))
        allowed = (valid & (rank < topk)) | (bidx == own)
        bias = jnp.where(allowed, 0.0, NEG)
        qaugs.append(jnp.concatenate(
            [qh, bias, jnp.zeros((HEAD_DIM - group, tq), F32)], axis=0).astype(BF16))

    assert tq == tk
    n_sub = tk // MOBA_BLOCK

    def scores(t, diag):
        if not diag:
            return [_dot(kaug_ref[hh, t * tk:(t + 1) * tk, :], qaugs[hh])
                    for hh in range(n_heads)]
        return [[_dot(kaug_ref[hh, t * tk + sb * MOBA_BLOCK:t * tk + (sb + 1) * MOBA_BLOCK, :],
                      qaugs[hh][:, sb * MOBA_BLOCK:]) for sb in range(n_sub)]
                for hh in range(n_heads)]

    def past_step(sts, t, carry):
        pts, stats = [], []
        for hh in range(n_heads):
            m = carry[hh][0]
            m_new = jnp.maximum(m, jnp.max(sts[hh], axis=0, keepdims=True))
            pts.append(jnp.exp2(sts[hh] - m_new).astype(BF16))
            stats.append((m_new, jnp.exp2(m - m_new)))
        new = []
        for hh in range(n_heads):
            _, l, acc = carry[hh]
            m_new, alpha = stats[hh]
            pv = lax.dot_general(vaug_ref[hh, t * tk:(t + 1) * tk, :], pts[hh],
                                 (((0,), (0,)), ((), ())), preferred_element_type=F32)
            new.append((m_new, alpha * l + pv[HEAD_DIM:HEAD_DIM + 1],
                        alpha * acc + pv[:HEAD_DIM]))
        return tuple(new)

    def diag_step(sts, t, carry):
        new = []
        for hh in range(n_heads):
            m, l, acc = carry[hh]
            masked = []
            m_new = m
            for sb in range(n_sub):
                off = sb * MOBA_BLOCK
                st = sts[hh][sb]
                krow = lax.broadcasted_iota(jnp.int32, st.shape, 0)
                qcol = lax.broadcasted_iota(jnp.int32, st.shape, 1)
                st = jnp.where(krow <= qcol, st, NEG)
                masked.append(st)
                top = jnp.max(st, axis=0, keepdims=True)
                if off:
                    top = jnp.concatenate([jnp.full((1, off), NEG, F32), top], axis=1)
                m_new = jnp.maximum(m_new, top)
            alpha = jnp.exp2(m - m_new)
            pv = None
            for sb in range(n_sub):
                off = sb * MOBA_BLOCK
                pt = jnp.exp2(masked[sb] - m_new[:, off:]).astype(BF16)
                part = lax.dot_general(
                    vaug_ref[hh, t * tk + off:t * tk + off + MOBA_BLOCK, :], pt,
                    (((0,), (0,)), ((), ())), preferred_element_type=F32)
                if off:
                    part = jnp.concatenate([jnp.zeros((LANES, off), F32), part], axis=1)
                pv = part if pv is None else pv + part
            new.append((m_new, alpha * l + pv[HEAD_DIM:HEAD_DIM + 1],
                        alpha * acc + pv[:HEAD_DIM]))
        return tuple(new)

    def softmax_pv(sts, t, carry, diag):
        return diag_step(sts, t, carry) if diag else past_step(sts, t, carry)

    def init():
        return tuple((jnp.full((1, tq), NEG, F32), jnp.zeros((1, tq), F32),
                      jnp.zeros((HEAD_DIM, tq), F32)) for _ in range(n_heads))

    def final(carry):
        outs = []
        for hh in range(n_heads):
            _, l, acc = carry[hh]
            o = acc / l
            outs.append(o * lax.rsqrt(jnp.mean(o * o, axis=0, keepdims=True) + NORM_EPS))
        return (jnp.concatenate(outs, axis=0).T * gh_ref[...]).astype(BF16)

    return _TileUnit(order=lambda n_past: [n_past] + list(range(n_past)), scores=scores,
                     step=softmax_pv, init=init, final=final)


ATTN_TQ = 2 * MOBA_BLOCK
MOBA_COLS = 2 * LANES


def _attn_kernel(qa_ref, ka_ref, va_ref, qb_ref, kb_ref, vb_ref, gha_ref, ghb_ref,
                 oa_ref, ob_ref, kaug_ref, vaug_ref, rhi_ref, rlo_ref, *, tq, topk):
    i = pl.program_id(2)
    S = ka_ref.shape[0]
    sb = _sb_unit(qa_ref, ka_ref, va_ref, gha_ref, i, tq=tq, tk=tq, sub=MOBA_BLOCK)
    mb = _moba_unit(qb_ref, kb_ref, vb_ref, ghb_ref, kaug_ref, vaug_ref, rhi_ref, rlo_ref, i,
                    tq=tq, tk=tq, topk=topk)

    def attend(n_past):
        ca, cb = _run_tiles([sb, mb], n_past)
        oa_ref[...] = sb.final(ca)
        ob_ref[...] = mb.final(cb)

    for n_past in range(S // tq):
        pl.when(i == n_past)(functools.partial(attend, n_past))


def _attention(proj, gh_sb, gh_moba, B, S):
    tq = ATTN_TQ
    assert S % tq == 0 and W_MOBA // MOBA_COLS == W_SB // LANES
    nq = S // tq
    n_blk = S // MOBA_BLOCK
    assert n_blk <= 8
    topk = min(MOBA_TOPK, max(n_blk - 1, 1))
    n_mb_heads = MOBA_COLS // HEAD_DIM
    q_row = lambda b, p, i: b * nq + i
    return pl.pallas_call(
        functools.partial(_attn_kernel, tq=tq, topk=topk),
        grid=(B, W_SB // LANES, nq),
        in_specs=[pl.BlockSpec((tq, LANES), lambda b, p, i: (q_row(b, p, i), C_QA // LANES + p)),
                  pl.BlockSpec((S, LANES), lambda b, p, i: (b, C_KA // LANES + p)),
                  pl.BlockSpec((S, LANES), lambda b, p, i: (b, C_VA // LANES + p)),
                  pl.BlockSpec((tq, MOBA_COLS),
                               lambda b, p, i: (q_row(b, p, i), C_QB // MOBA_COLS + p)),
                  pl.BlockSpec((S, MOBA_COLS), lambda b, p, i: (b, C_KB // MOBA_COLS + p)),
                  pl.BlockSpec((S, MOBA_COLS), lambda b, p, i: (b, C_VB // MOBA_COLS + p)),
                  pl.BlockSpec((1, LANES), lambda b, p, i: (0, p)),
                  pl.BlockSpec((1, MOBA_COLS), lambda b, p, i: (0, p))],
        out_specs=[pl.BlockSpec((tq, LANES), lambda b, p, i: (q_row(b, p, i), p)),
                   pl.BlockSpec((tq, MOBA_COLS), lambda b, p, i: (q_row(b, p, i), p))],
        out_shape=[jax.ShapeDtypeStruct((B * S, W_SB), BF16),
                   jax.ShapeDtypeStruct((B * S, W_MOBA), BF16)],
        scratch_shapes=[pltpu.VMEM((n_mb_heads, S, LANES), BF16),
                        pltpu.VMEM((n_mb_heads, S, LANES), BF16),
                        pltpu.VMEM((n_mb_heads, LANES, LANES), BF16),
                        pltpu.VMEM((n_mb_heads, LANES, LANES), BF16)],
        compiler_params=pltpu.CompilerParams(
            dimension_semantics=("arbitrary", "arbitrary", "arbitrary"),
            vmem_limit_bytes=VMEM_LIMIT),
        name="attention",
    )(proj, proj, proj, proj, proj, proj, gh_sb, gh_moba)


def _gmlp_kernel(u_ref, v_ref, gv_ref, ws_ref, b_ref, gh_ref, o_ref, *, tm):
    row = lax.broadcasted_iota(jnp.int32, (GMLP_CHUNK, GMLP_CHUNK), 0)
    col = lax.broadcasted_iota(jnp.int32, (GMLP_CHUNK, GMLP_CHUNK), 1)
    seg_mean = jnp.where(row // HEAD_DIM == col // HEAD_DIM, 1.0 / HEAD_DIM, 0.0).astype(BF16)
    first_group = lax.broadcasted_iota(jnp.int32, (GMLP_CHUNK, LANES), 1) < HEAD_DIM

    def group_rms(x, g):
        hi, lo = _split_bf16(x * x)
        ms = _dot(hi, seg_mean) + _dot(lo, seg_mean)
        return x * lax.rsqrt(ms + NORM_EPS) * g

    for p in range(W_GMLP // LANES):
        lanes = slice(p * LANES, (p + 1) * LANES)
        gu = jax.nn.gelu(u_ref[:, lanes].astype(F32))
        vn = group_rms(jax.nn.gelu(v_ref[:, lanes].astype(F32)), gv_ref[:, lanes]).astype(BF16)
        wm = [jnp.where(col <= row, ws_ref[2 * p + j], 0.0).astype(BF16) for j in range(2)]
        bias = b_ref[:, lanes]
        mixed = jnp.concatenate(
            [jnp.where(first_group, _dot(wm[0], vn[c * GMLP_CHUNK:(c + 1) * GMLP_CHUNK]),
                       _dot(wm[1], vn[c * GMLP_CHUNK:(c + 1) * GMLP_CHUNK])) + bias
             for c in range(tm // GMLP_CHUNK)], axis=0)
        o_ref[:, lanes] = group_rms(gu * mixed, gh_ref[:, lanes]).astype(BF16)


def _gmlp(proj, gv, ws, b_exp, gh, tm):
    T = proj.shape[0]
    nu, nv = C_UC // W_GMLP, C_VC // W_GMLP
    return pl.pallas_call(
        functools.partial(_gmlp_kernel, tm=tm),
        grid=(T // tm,),
        in_specs=[pl.BlockSpec((tm, W_GMLP), lambda i: (i, nu)),
                  pl.BlockSpec((tm, W_GMLP), lambda i: (i, nv)),
                  pl.BlockSpec((1, W_GMLP), lambda i: (0, 0)),
                  pl.BlockSpec((N_GROUPS_GMLP, GMLP_CHUNK, GMLP_CHUNK), lambda i: (0, 0, 0)),
                  pl.BlockSpec((GMLP_CHUNK, W_GMLP), lambda i: (0, 0)),
                  pl.BlockSpec((1, W_GMLP), lambda i: (0, 0))],
        out_specs=pl.BlockSpec((tm, W_GMLP), lambda i: (i, 0)),
        out_shape=jax.ShapeDtypeStruct((T, W_GMLP), BF16),
        compiler_params=pltpu.CompilerParams(dimension_semantics=("arbitrary",),
                                             vmem_limit_bytes=VMEM_LIMIT),
        name="gmlp",
    )(proj, proj, gv, ws, b_exp, gh)


def _out_kernel(osb_ref, omoba_ref, ogmlp_ref, x_ref, w_ref, g_ref, rhi_ref, rlo_ref,
                x1_ref, h_ref, lg_ref):
    o = jnp.concatenate([osb_ref[...], omoba_ref[...], ogmlp_ref[...]], axis=1)
    x1 = x_ref[...] + _dot(o, w_ref[...])
    x1_ref[...] = x1
    hn = _rms(x1, g_ref[...])
    hi, lo = _split_bf16(hn)
    h_ref[...] = _pack_halves(hn)
    stacked = _nt_dot(jnp.concatenate([rhi_ref[...], rlo_ref[...]], axis=0), hi)
    lg_ref[...] = stacked[:ROUTER_ROWS] + stacked[ROUTER_ROWS:] + _nt_dot(rhi_ref[...], lo)


def _out_proj(o_sb, o_moba, o_gmlp, x2d, w_bf16, g, r_hi, r_lo, tm):
    T, D = x2d.shape
    row = lambda i: (i, 0)
    const = lambda i: (0, 0)
    return pl.pallas_call(
        _out_kernel,
        grid=(T // tm,),
        in_specs=[pl.BlockSpec((tm, W_SB), row), pl.BlockSpec((tm, W_MOBA), row),
                  pl.BlockSpec((tm, W_GMLP), row), pl.BlockSpec((tm, D), row),
                  pl.BlockSpec((D, D), const), pl.BlockSpec((1, D), const),
                  pl.BlockSpec((ROUTER_ROWS, D), const), pl.BlockSpec((ROUTER_ROWS, D), const)],
        out_specs=[pl.BlockSpec((tm, D), row), pl.BlockSpec((tm, D // 2), row),
                   pl.BlockSpec((ROUTER_ROWS, tm), lambda i: (0, i))],
        out_shape=[jax.ShapeDtypeStruct((T, D), F32), jax.ShapeDtypeStruct((T, D // 2), jnp.uint32),
                   jax.ShapeDtypeStruct((ROUTER_ROWS, T), F32)],
        compiler_params=pltpu.CompilerParams(dimension_semantics=("arbitrary",),
                                             vmem_limit_bytes=VMEM_LIMIT),
        name="outproj",
    )(o_sb, o_moba, o_gmlp, x2d, w_bf16, g, r_hi, r_lo)


ROUTER_EXPERT_ROW = 8
GW_ROWS = 8
MOE_STEP_BLOCKS = 2
ROUTER_TM = 2048
ROUTER_CHUNK = 512


def _first_max(p):
    rows = lax.broadcasted_iota(jnp.int32, p.shape, 0).astype(F32)
    top = jnp.max(p, axis=0, keepdims=True)
    idx = jnp.min(jnp.where(p == top, rows, float(p.shape[0])), axis=0, keepdims=True)
    return top, idx, rows


def _router_kernel(lg_ref, pos_ref, gw_ref, cnt_ref, cnt_acc, base_ref, *, tm):
    phase = pl.program_id(0)
    i = pl.program_id(1)
    lg = lg_ref[...]
    gl = lg[0:N_EXPERT_GROUPS]
    ge = jnp.exp(gl - jnp.max(gl, axis=0, keepdims=True))
    p_group = ge / jnp.sum(ge, axis=0, keepdims=True)
    p_g, g_sel, _ = _first_max(p_group)
    le = jnp.zeros((EXPERTS_PER_GROUP, tm), F32)
    for g in range(N_EXPERT_GROUPS):
        r0 = ROUTER_EXPERT_ROW + g * EXPERTS_PER_GROUP
        le = jnp.where(g_sel == g, lg[r0:r0 + EXPERTS_PER_GROUP], le)
    ee = jnp.exp(le - jnp.max(le, axis=0, keepdims=True))
    p = ee / jnp.sum(ee, axis=0, keepdims=True)
    p0, i0, rows = _first_max(p)
    p1, i1, _ = _first_max(jnp.where(rows == i0, -1.0, p))
    e0 = g_sel * EXPERTS_PER_GROUP + i0
    e1 = g_sel * EXPERTS_PER_GROUP + i1
    xrow = lax.broadcasted_iota(jnp.int32, (N_EXPERTS, tm), 0).astype(F32)
    oh0 = xrow == e0
    oh1 = xrow == e1
    slots = jnp.where(oh0, 1.0, 0.0) + jnp.where(oh1, 1.0, 0.0)
    tile_cnt = slots[:, 0:LANES]
    for c in range(1, tm // LANES):
        tile_cnt = tile_cnt + slots[:, c * LANES:(c + 1) * LANES]

    @pl.when((phase == 0) & (i == 0))
    def _init():
        cnt_acc[...] = jnp.zeros_like(cnt_acc)

    @pl.when(phase == 0)
    def _count():
        cnt_acc[...] += tile_cnt

    @pl.when((phase == 1) & (i == 0))
    def _starts():
        counts = jnp.sum(cnt_acc[...], axis=1, keepdims=True)
        n_blk = jnp.floor((counts + (MOE_BLOCK - 1)) * (1.0 / MOE_BLOCK))
        er = lax.broadcasted_iota(jnp.int32, (N_EXPERTS, N_EXPERTS), 0)
        ec = lax.broadcasted_iota(jnp.int32, (N_EXPERTS, N_EXPERTS), 1)
        before = jnp.where(ec < er, 1.0, 0.0).astype(BF16)
        start_blk = _dot(before, jnp.broadcast_to(n_blk, (N_EXPERTS, LANES)).astype(BF16))
        base_ref[...] = start_blk * MOE_BLOCK
        cnt_ref[...] = jnp.broadcast_to(counts, (N_EXPERTS, LANES)).astype(jnp.int32)

    @pl.when(phase == 1)
    def _assign():
        chunk = min(ROUTER_CHUNK, tm)
        tr = lax.broadcasted_iota(jnp.int32, (chunk, chunk), 0)
        tc = lax.broadcasted_iota(jnp.int32, (chunk, chunk), 1)
        earlier = jnp.where(tr < tc, 1.0, 0.0).astype(BF16)
        run = base_ref[:, 0:1]
        parts = []
        for c in range(tm // chunk):
            part = slots[:, c * chunk:(c + 1) * chunk]
            parts.append(_dot(part.astype(BF16), earlier) + run)
            run = run + jnp.sum(part, axis=1, keepdims=True)
        row_of = jnp.concatenate(parts, axis=1)
        pos0 = jnp.sum(jnp.where(oh0, row_of, 0.0), axis=0, keepdims=True)
        pos1 = jnp.sum(jnp.where(oh1, row_of, 0.0), axis=0, keepdims=True)
        pos_ref[...] = jnp.concatenate([pos0, pos1], axis=0).astype(jnp.int32)
        scale = p_g / (p0 + p1)
        gw_ref[...] = jnp.concatenate(
            [p0 * scale, p1 * scale, jnp.zeros((GW_ROWS - MOE_TOP_K, tm), F32)], axis=0)
        base_ref[...] += jnp.sum(tile_cnt, axis=1, keepdims=True)


def _router(logits_t, tm):
    T = logits_t.shape[1]
    tok = lambda p, i: (0, i * p)
    return pl.pallas_call(
        functools.partial(_router_kernel, tm=tm),
        grid=(2, T // tm),
        in_specs=[pl.BlockSpec((ROUTER_ROWS, tm), lambda p, i: (0, i))],
        out_specs=[pl.BlockSpec((MOE_TOP_K, tm), tok), pl.BlockSpec((GW_ROWS, tm), tok),
                   pl.BlockSpec((N_EXPERTS, LANES), lambda p, i: (0, 0))],
        out_shape=[jax.ShapeDtypeStruct((MOE_TOP_K, T), jnp.int32),
                   jax.ShapeDtypeStruct((GW_ROWS, T), F32),
                   jax.ShapeDtypeStruct((N_EXPERTS, LANES), jnp.int32)],
        scratch_shapes=[pltpu.VMEM((N_EXPERTS, LANES), F32), pltpu.VMEM((N_EXPERTS, LANES), F32)],
        compiler_params=pltpu.CompilerParams(dimension_semantics=("arbitrary", "arbitrary")),
        name="router",
    )(logits_t)


def _moe_kernel(be_ref, first_ref, slot_ref, next_ref, rows_ref, na_ref, xs_ref, wg_hbm, wu_hbm,
                wd_hbm, ys_ref, wgf, wuf, wdf, wgb, wub, wdb, sem, *, layer):
    def fetch(expert, s):
        return (pltpu.make_async_copy(wg_hbm.at[layer, expert], wgf.at[s], sem.at[s, 0]),
                pltpu.make_async_copy(wu_hbm.at[layer, expert], wuf.at[s], sem.at[s, 1]),
                pltpu.make_async_copy(wd_hbm.at[layer, expert], wdf.at[s], sem.at[s, 2]))

    @pl.when(pl.program_id(0) == 0)
    def _first_fetch():
        for c in fetch(be_ref[0], slot_ref[0]):
            c.start()

    def one_block(b, rows):
        e = be_ref[b]
        slot = slot_ref[b]
        active = b < na_ref[0]

        @pl.when(active & (first_ref[b] == 1))
        def _load_expert():
            for c in fetch(e, slot):
                c.wait()
            wgb[...] = wgf[slot].astype(BF16)
            wub[...] = wuf[slot].astype(BF16)
            wdb[...] = wdf[slot].astype(BF16)
            nxt = next_ref[b]

            @pl.when(nxt >= 0)
            def _prefetch():
                for c in fetch(nxt, 1 - slot):
                    c.start(priority=1)

        @pl.when(active)
        def _compute():
            row = lax.broadcasted_iota(jnp.int32, (MOE_BLOCK, xs_ref.shape[1]), 0)
            words = jnp.where(row < rows_ref[b], xs_ref[rows, :], jnp.uint32(0))
            xb = _unpack_halves(words).astype(BF16)
            a = jax.nn.silu(_dot(xb, wgb[...])) * _dot(xb, wub[...])
            ys_ref[rows, :] = _pack_halves(_dot(a.astype(BF16), wdb[...]))

    for j in range(MOE_STEP_BLOCKS):
        one_block(pl.program_id(0) * MOE_STEP_BLOCKS + j, slice(j * MOE_BLOCK, (j + 1) * MOE_BLOCK))


def _moe_experts(plan, xs, w_gate, w_up, w_down, layer):
    n_rows = xs.shape[0]
    D = 2 * xs.shape[1]
    step_rows = MOE_STEP_BLOCKS * MOE_BLOCK
    assert n_rows % step_rows == 0
    DE = w_gate.shape[-1]
    rows = lambda g, *plan_refs: (jnp.minimum(g, (plan_refs[-1][0] - 1) // MOE_STEP_BLOCKS), 0)
    grid_spec = pltpu.PrefetchScalarGridSpec(
        num_scalar_prefetch=len(plan),
        grid=(n_rows // step_rows,),
        in_specs=[pl.BlockSpec((step_rows, D // 2), rows),
                  pl.BlockSpec(memory_space=pl.ANY), pl.BlockSpec(memory_space=pl.ANY),
                  pl.BlockSpec(memory_space=pl.ANY)],
        out_specs=pl.BlockSpec((step_rows, D // 2), rows),
        scratch_shapes=[pltpu.VMEM((2, D, DE), F32), pltpu.VMEM((2, D, DE), F32),
                        pltpu.VMEM((2, DE, D), F32),
                        pltpu.VMEM((D, DE), BF16), pltpu.VMEM((D, DE), BF16),
                        pltpu.VMEM((DE, D), BF16),
                        pltpu.SemaphoreType.DMA((2, 3))])
    return pl.pallas_call(
        functools.partial(_moe_kernel, layer=layer),
        grid_spec=grid_spec,
        out_shape=jax.ShapeDtypeStruct((n_rows, D // 2), jnp.uint32),
        compiler_params=pltpu.CompilerParams(dimension_semantics=("arbitrary",),
                                             vmem_limit_bytes=VMEM_LIMIT),
        name="moe_experts",
    )(*plan, xs, w_gate, w_up, w_down)


def _final_kernel(x1_ref, yg_ref, gw_ref, g_ref, o_ref):
    o_ref[...] = _rms(_combine(x1_ref, yg_ref, gw_ref), g_ref[...])


def _final_norm(x2d, yg, gw, g, tm):
    T, D = x2d.shape
    row = lambda i: (i, 0)
    return pl.pallas_call(
        _final_kernel,
        grid=(T // tm,),
        in_specs=[pl.BlockSpec((tm, D), row),
                  pl.BlockSpec((MOE_TOP_K, tm, D // 2), lambda i: (0, i, 0)),
                  pl.BlockSpec((GW_ROWS, tm), lambda i: (0, i)),
                  pl.BlockSpec((1, D), lambda i: (0, 0))],
        out_specs=pl.BlockSpec((tm, D), row),
        out_shape=jax.ShapeDtypeStruct((T, D), F32),
        compiler_params=pltpu.CompilerParams(dimension_semantics=("arbitrary",)),
        name="final_norm",
    )(x2d, yg, gw, g)


SC_GATHER_ROWS = 64


def _sc_gather(table, idx):
    info = plsc.get_sparse_core_info()
    n_cores, n_workers = info.num_cores, info.num_cores * info.num_subcores
    N, W = idx.shape[0], table.shape[1]
    per_w = N // n_workers
    n_ch = per_w // SC_GATHER_ROWS
    assert per_w * n_workers == N and n_ch * SC_GATHER_ROWS == per_w
    mesh = plsc.VectorSubcoreMesh(core_axis_name="c", subcore_axis_name="s")

    @functools.partial(
        pl.kernel, mesh=mesh,
        out_type=jax.ShapeDtypeStruct((N, W), table.dtype),
        scratch_types=[pltpu.VMEM((n_ch, SC_GATHER_ROWS), jnp.int32),
                       pltpu.VMEM((2, SC_GATHER_ROWS, W), table.dtype),
                       pltpu.SemaphoreType.DMA((2,)),
                       pltpu.SemaphoreType.DMA((2,))])
    def gather_kernel(table_hbm, idx_hbm, out_hbm, idx_v, rows_v, gsem, ssem):
        wid = lax.axis_index("s") * n_cores + lax.axis_index("c")
        base = wid * per_w
        pltpu.sync_copy(idx_hbm.at[wid], idx_v)

        def fetch(c):
            return pltpu.make_async_copy(table_hbm.at[idx_v.at[c]], rows_v.at[c % 2],
                                         gsem.at[c % 2])

        def write(c):
            return pltpu.make_async_copy(
                rows_v.at[c % 2], out_hbm.at[pl.ds(base + c * SC_GATHER_ROWS, SC_GATHER_ROWS)],
                ssem.at[c % 2])

        fetch(0).start()
        for c in range(n_ch):
            if c + 1 < n_ch:
                if c >= 1:
                    write(c - 1).wait()
                fetch(c + 1).start()
            fetch(c).wait()
            write(c).start()
        if n_ch >= 2:
            write(n_ch - 2).wait()
        write(n_ch - 1).wait()

    return gather_kernel(table, idx.reshape(n_workers, n_ch, SC_GATHER_ROWS))


def _sc_scatter(rows, pos, n_out):
    info = plsc.get_sparse_core_info()
    n_cores, n_workers = info.num_cores, info.num_cores * info.num_subcores
    K, T = pos.shape
    W = rows.shape[1]
    per_w = T // n_workers
    n_ch = per_w // SC_GATHER_ROWS
    assert per_w * n_workers == T and n_ch * SC_GATHER_ROWS == per_w
    mesh = plsc.VectorSubcoreMesh(core_axis_name="c", subcore_axis_name="s")

    @functools.partial(
        pl.kernel, mesh=mesh,
        out_type=jax.ShapeDtypeStruct((n_out, W), rows.dtype),
        scratch_types=[pltpu.VMEM((K, n_ch, SC_GATHER_ROWS), jnp.int32),
                       pltpu.VMEM((2, SC_GATHER_ROWS, W), rows.dtype),
                       pltpu.SemaphoreType.DMA((2,)),
                       pltpu.SemaphoreType.DMA((2, K))])
    def scatter_kernel(rows_hbm, idx_hbm, out_hbm, idx_v, buf_v, lsem, ssem):
        wid = lax.axis_index("s") * n_cores + lax.axis_index("c")
        base = wid * per_w
        for k in range(K):
            pltpu.sync_copy(idx_hbm.at[k, wid], idx_v.at[k])

        def load(c):
            return pltpu.make_async_copy(
                rows_hbm.at[pl.ds(base + c * SC_GATHER_ROWS, SC_GATHER_ROWS)], buf_v.at[c % 2],
                lsem.at[c % 2])

        def store(c, k):
            return pltpu.make_async_copy(buf_v.at[c % 2], out_hbm.at[idx_v.at[k, c]],
                                         ssem.at[c % 2, k])

        load(0).start()
        for c in range(n_ch):
            if c + 1 < n_ch:
                if c >= 1:
                    for k in range(K):
                        store(c - 1, k).wait()
                load(c + 1).start()
            load(c).wait()
            for k in range(K):
                store(c, k).start()
        for c in range(max(n_ch - 2, 0), n_ch):
            for k in range(K):
                store(c, k).wait()

    return scatter_kernel(rows, pos.reshape(K, n_workers, n_ch, SC_GATHER_ROWS))


def _rope_tables(positions):
    half = ROT_DIM // 2
    inv_freq = jnp.power(ROPE_THETA, -jnp.arange(half, dtype=F32) / half)
    ang = positions.astype(F32).reshape(-1)[:, None] * inv_freq
    cos, sin = jnp.cos(ang), jnp.sin(ang)
    T = ang.shape[0]
    ones = jnp.ones((T, HEAD_DIM - ROT_DIM), F32)
    zeros = jnp.zeros((T, HEAD_DIM - ROT_DIM), F32)
    z8 = jnp.zeros((T, half), F32)
    rc = jnp.concatenate([cos, cos, ones], axis=1)
    rs1 = jnp.concatenate([-sin, z8, zeros], axis=1)
    rs2 = jnp.concatenate([z8, sin, zeros], axis=1)
    rep = LANES // HEAD_DIM
    return jnp.tile(rc, (1, rep)), jnp.tile(rs1, (1, rep)), jnp.tile(rs2, (1, rep))


def _block_plan(counts, T):
    i32 = jnp.int32
    n_blocks = -(-(T * MOE_TOP_K) // MOE_BLOCK) + N_EXPERTS
    experts = jnp.arange(N_EXPERTS, dtype=i32)
    blocks = jnp.arange(n_blocks, dtype=i32)
    n_blk = (counts + MOE_BLOCK - 1) // MOE_BLOCK
    blocks_end = jnp.cumsum(n_blk)
    blk_expert = jnp.minimum(jnp.sum(blocks_end[None, :] <= blocks[:, None], axis=1),
                             N_EXPERTS - 1).astype(i32)
    is_active = n_blk > 0
    ordinal = jnp.cumsum(is_active.astype(i32)) - 1
    later_active = is_active[None, :] & (experts[None, :] > experts[:, None])
    next_active = jnp.min(jnp.where(later_active, experts[None, :], N_EXPERTS), axis=1)
    next_active = jnp.where(next_active < N_EXPERTS, next_active, -1).astype(i32)
    tables = jnp.stack([blocks_end - n_blk, counts, ordinal, next_active], axis=1).astype(i32)
    onehot = (blk_expert[:, None] == experts[None, :]).astype(i32)
    looked = jnp.sum(onehot[:, :, None] * tables[None, :, :], axis=1)
    blk_in_expert = blocks - looked[:, 0]
    blk_first = (blk_in_expert == 0).astype(i32)
    blk_rows = jnp.clip(looked[:, 1] - blk_in_expert * MOE_BLOCK, 0, MOE_BLOCK).astype(i32)
    plan = (blk_expert, blk_first, (looked[:, 2] % 2).astype(i32), looked[:, 3], blk_rows,
            blocks_end[-1:].astype(i32))
    return n_blocks, plan


def kernel(x, positions, w_in, w_out, g_mix_norm, g_head_norm, g_gmlp_vnorm, w_spatial, b_spatial,
           g_ffn_norm, w_router_group, w_router_expert, w_expert_gate, w_expert_up, w_expert_down,
           g_final):
    B, S, D = x.shape
    T = B * S
    depth = w_in.shape[0]
    tm = min(512, T)
    rc, rs1, rs2 = _rope_tables(positions)
    xc = x.reshape(T, D)
    moe_out = None
    for l in range(depth):
        gh = g_head_norm[l].reshape(1, -1)
        proj, xc = _inproj(xc, moe_out, g_mix_norm[l].reshape(1, D), w_in[l].astype(BF16),
                           rc, rs1, rs2, tm)
        o_sb, o_moba = _attention(proj, gh[:, :W_SB], gh[:, W_SB:W_SB + W_MOBA], B, S)
        b_exp = jnp.repeat(b_spatial[l].T, HEAD_DIM, axis=1)
        o_gmlp = _gmlp(proj, g_gmlp_vnorm[l].reshape(1, -1), w_spatial[l], b_exp,
                       gh[:, W_SB + W_MOBA:], tm)
        w_r = jnp.concatenate(
            [w_router_group[l].T, jnp.zeros((ROUTER_EXPERT_ROW - N_EXPERT_GROUPS, D), F32),
             w_router_expert[l].T,
             jnp.zeros((ROUTER_ROWS - ROUTER_EXPERT_ROW - N_EXPERTS, D), F32)], axis=0)
        r_hi, r_lo = _split_bf16(w_r)
        xc, h, logits_t = _out_proj(o_sb, o_moba, o_gmlp, xc, w_out[l].astype(BF16),
                                    g_ffn_norm[l].reshape(1, D), r_hi, r_lo, min(2 * tm, T))
        pos, gate_w, counts = _router(logits_t, min(ROUTER_TM, T))
        n_blocks, plan = _block_plan(counts[:, 0], T)
        xs = _sc_scatter(h, pos, n_blocks * MOE_BLOCK)
        ys = _moe_experts(plan, xs, w_expert_gate, w_expert_up, w_expert_down, l)
        yg = _sc_gather(ys, pos.reshape(-1)).reshape(MOE_TOP_K, T, D // 2)
        moe_out = (yg, gate_w)
    return _final_norm(xc, moe_out[0], moe_out[1], g_final.reshape(1, D), tm).reshape(B, S, D)
```

```python
import collections
import functools

import jax
import jax.numpy as jnp
from jax import lax
from jax.experimental import pallas as pl
from jax.experimental.pallas import tpu as pltpu
from jax.experimental.pallas import tpu_sc as plsc

F32 = jnp.float32
BF16 = jnp.bfloat16

HEAD_DIM = 64
LANES = 128
MXU_COLS = 256
N_HEADS_SB = 4
N_HEADS_MOBA = 8
N_GROUPS_GMLP = 4
W_SB = N_HEADS_SB * HEAD_DIM
W_MOBA = N_HEADS_MOBA * HEAD_DIM
W_GMLP = N_GROUPS_GMLP * HEAD_DIM
MOBA_BLOCK = 256
MOBA_TOPK = 3
GMLP_CHUNK = 128
ROPE_THETA = 500000.0
ROT_DIM = HEAD_DIM // 4
N_EXPERT_GROUPS = 4
EXPERTS_PER_GROUP = 8
N_EXPERTS = N_EXPERT_GROUPS * EXPERTS_PER_GROUP
MOE_TOP_K = 2
MOE_BLOCK = 256
NORM_EPS = 1e-6
ATTN_SCALE = HEAD_DIM ** -0.5
NEG = -1e30
LOG2E = 1.4426950408889634
ROUTER_ROWS = 64
VMEM_LIMIT = 48 * 1024 * 1024

C_QA, C_KA, C_VA = 0, W_SB, 2 * W_SB
C_QB = 3 * W_SB
C_KB = C_QB + W_MOBA
C_VB = C_KB + W_MOBA
C_UC = C_VB + W_MOBA
C_VC = C_UC + W_GMLP
IN_COLS = C_VC + W_GMLP


def _nt_dot(a, b):
    return lax.dot_general(a, b, (((1,), (1,)), ((), ())), preferred_element_type=F32)


def _dot(a, b):
    return jnp.dot(a, b, preferred_element_type=F32)


def _rms(x, g):
    return x * lax.rsqrt(jnp.mean(x * x, axis=-1, keepdims=True) + NORM_EPS) * g


def _pack_halves(x):
    w = x.shape[1] // 2
    lo = lax.bitcast_convert_type(x[:, :w].astype(BF16).astype(F32), jnp.uint32)
    hi = lax.bitcast_convert_type(x[:, w:].astype(BF16).astype(F32), jnp.uint32)
    return (lo >> 16) | hi


def _unpack_halves(words):
    lo = lax.bitcast_convert_type(words << 16, F32)
    hi = lax.bitcast_convert_type(words & jnp.uint32(0xFFFF0000), F32)
    return jnp.concatenate([lo, hi], axis=1)


def _split_bf16(x):
    hi = x.astype(BF16)
    lo = (x - hi.astype(F32)).astype(BF16)
    return hi, lo


def _combine(x1_ref, yg_ref, gw_ref):
    rows = gw_ref.shape[0]
    eye = jnp.where(lax.broadcasted_iota(jnp.int32, (rows, LANES), 0)
                    == lax.broadcasted_iota(jnp.int32, (rows, LANES), 1), 1.0, 0.0).astype(BF16)
    tn = (((0,), (0,)), ((), ()))
    hi, lo = _split_bf16(gw_ref[...])
    gw = (lax.dot_general(hi, eye, tn, preferred_element_type=F32)
          + lax.dot_general(lo, eye, tn, preferred_element_type=F32))
    return (x1_ref[...] + _unpack_halves(yg_ref[0]) * gw[:, 0:1]
            + _unpack_halves(yg_ref[1]) * gw[:, 1:2])


def _inproj_kernel(*refs, combine):
    if combine:
        x1_ref, yg_ref, gw_ref, g_ref, w_ref, rt_ref, o_ref, x_ref = refs
        x = _combine(x1_ref, yg_ref, gw_ref)
        x_ref[...] = x
    else:
        x_ref, g_ref, w_ref, rt_ref, o_ref = refs
        x = x_ref[...]
    y = _rms(x, g_ref[...]).astype(BF16)
    half = ROT_DIM // 2
    rt = rt_ref[...]
    in_head = lax.broadcasted_iota(jnp.int32, rt.shape, 1) % HEAD_DIM
    rc = jnp.where(in_head < ROT_DIM, rt, 1.0)
    rs1 = jnp.where(in_head < half, -pltpu.roll(rt, LANES - ROT_DIM, axis=1), 0.0)
    rs2 = jnp.where((in_head >= half) & (in_head < ROT_DIM),
                    pltpu.roll(rt, LANES - half, axis=1), 0.0)
    wide = lambda t: jnp.concatenate([t] * (MXU_COLS // LANES), axis=1)
    rc, rs1, rs2 = wide(rc), wide(rs1), wide(rs2)
    for c0 in range(0, IN_COLS, MXU_COLS):
        p = _dot(y, w_ref[:, c0:c0 + MXU_COLS])
        if C_QB <= c0 < C_VB:
            p = (p * rc + pltpu.roll(p, MXU_COLS - half, axis=1) * rs1
                 + pltpu.roll(p, half, axis=1) * rs2)
        if c0 < C_KA or C_QB <= c0 < C_KB:
            p = p * (ATTN_SCALE * LOG2E)
        o_ref[:, c0:c0 + MXU_COLS] = p.astype(BF16)


def _inproj(x2d, moe_out, g, w_bf16, rope, tm):
    T, D = x2d.shape
    row = lambda i: (i, 0)
    const = lambda i: (0, 0)
    combine = moe_out is not None
    x_specs = [pl.BlockSpec((tm, D), row)]
    x_args = [x2d]
    out_specs = [pl.BlockSpec((tm, IN_COLS), row)]
    out_shape = [jax.ShapeDtypeStruct((T, IN_COLS), BF16)]
    if combine:
        yg, gw = moe_out
        x_specs += [pl.BlockSpec((MOE_TOP_K, tm, D // 2), lambda i: (0, i, 0)),
                    pl.BlockSpec((GW_ROWS, tm), lambda i: (0, i))]
        x_args += [yg, gw]
        out_specs.append(pl.BlockSpec((tm, D), row))
        out_shape.append(jax.ShapeDtypeStruct((T, D), F32))
    outs = pl.pallas_call(
        functools.partial(_inproj_kernel, combine=combine),
        grid=(T // tm,),
        in_specs=x_specs + [pl.BlockSpec((1, D), const), pl.BlockSpec((D, IN_COLS), const),
                            pl.BlockSpec((tm, LANES), row)],
        out_specs=out_specs,
        out_shape=out_shape,
        compiler_params=pltpu.CompilerParams(dimension_semantics=("arbitrary",),
                                             vmem_limit_bytes=VMEM_LIMIT),
        name="inproj",
    )(*x_args, g, w_bf16, rope)
    return (outs[0], outs[1]) if combine else (outs[0], x2d)


_TileUnit = collections.namedtuple("_TileUnit", "order scores step init final")


def _run_tiles(units, n_past):
    orders = [u.order(n_past) for u in units]
    carries = [u.init() for u in units]
    zs = [u.scores(o[0], True) for u, o in zip(units, orders)]
    for n in range(n_past + 1):
        nxt = [u.scores(o[n + 1], False) if n < n_past else None
               for u, o in zip(units, orders)]
        carries = [u.step(z, o[n], c, n == 0) for u, o, z, c in zip(units, orders, zs, carries)]
        zs = nxt
    return carries


def _sb_unit(q_ref, k_ref, v_ref, gh_ref, i, *, tq, tk, sub):
    n_heads = LANES // HEAD_DIM
    n_sub = tk // sub
    qt = q_ref[...].astype(F32).T
    zeros_t = jnp.zeros((HEAD_DIM, tq), F32)
    qz = [jnp.concatenate([qt[hh * HEAD_DIM:(hh + 1) * HEAD_DIM] if h2 == hh else zeros_t
                           for h2 in range(n_heads)], axis=0).astype(BF16)
          for hh in range(n_heads)]
    srow = lax.broadcasted_iota(jnp.int32, (sub, sub), 0)
    scol = lax.broadcasted_iota(jnp.int32, (sub, sub), 1)
    later = jnp.where(scol > srow, 1.0, 0.0).astype(BF16)
    assert tq == tk

    def log_rest(z):
        nz = -z
        return jnp.minimum(nz, 0.0) - jnp.log2(1.0 + jnp.exp2(jnp.minimum(z, nz)))

    def scores(t, diag):
        if not diag:
            return [_dot(k_ref[t * tk:(t + 1) * tk, :], qz[hh]) for hh in range(n_heads)]
        return [[_dot(k_ref[t * tk + sb * sub:t * tk + (sb + 1) * sub, :], qz[hh][:, sb * sub:])
                 for sb in range(n_sub)] for hh in range(n_heads)]

    def past_step(zs, t, carry):
        wts, runs = [], []
        for hh in range(n_heads):
            z = zs[hh]
            lr = log_rest(z)
            lrb = lr.astype(BF16)
            tail = carry[hh][0]
            afters = [None] * n_sub
            for sb in reversed(range(n_sub)):
                afters[sb] = _dot(later, lrb[sb * sub:(sb + 1) * sub]) + tail
                tail = afters[sb][0:1] + lr[sb * sub:sb * sub + 1]
            wts.append(jnp.exp2(z + lr + jnp.concatenate(afters, axis=0)).astype(BF16))
            runs.append(tail)
        new = []
        for hh in range(n_heads):
            pv = lax.dot_general(v_ref[t * tk:(t + 1) * tk, :], wts[hh],
                                 (((0,), (0,)), ((), ())), preferred_element_type=F32)
            new.append((runs[hh], carry[hh][1] + pv[hh * HEAD_DIM:(hh + 1) * HEAD_DIM]))
        return tuple(new)

    def diag_step(zs, t, carry):
        new = []
        for hh in range(n_heads):
            tail, acc = carry[hh]
            for sb in reversed(range(n_sub)):
                off = sb * sub
                z = zs[hh][sb]
                krow = lax.broadcasted_iota(jnp.int32, z.shape, 0)
                qcol = lax.broadcasted_iota(jnp.int32, z.shape, 1)
                causal = krow < qcol
                lr = jnp.where(causal, log_rest(z), 0.0)
                after = _dot(later, lr.astype(BF16)) + tail[:, off:]
                w = jnp.where(causal, jnp.exp2(z + lr + after), 0.0).astype(BF16)
                run = after[0:1] + lr[0:1]
                pv = lax.dot_general(v_ref[t * tk + off:t * tk + off + sub, :], w,
                                     (((0,), (0,)), ((), ())), preferred_element_type=F32)
                pv = pv[hh * HEAD_DIM:(hh + 1) * HEAD_DIM]
                if off:
                    run = jnp.concatenate([tail[:, :off], run], axis=1)
                    pv = jnp.concatenate([jnp.zeros((HEAD_DIM, off), F32), pv], axis=1)
                tail, acc = run, acc + pv
            new.append((tail, acc))
        return tuple(new)

    def weights_pv(zs, t, carry, diag):
        return diag_step(zs, t, carry) if diag else past_step(zs, t, carry)

    def init():
        return tuple((jnp.zeros((1, tq), F32), jnp.zeros((HEAD_DIM, tq), F32))
                     for _ in range(n_heads))

    def final(carry):
        outs = []
        for hh in range(n_heads):
            o = carry[hh][1]
            outs.append(o * lax.rsqrt(jnp.mean(o * o, axis=0, keepdims=True) + NORM_EPS))
        return (jnp.concatenate(outs, axis=0).T * gh_ref[...]).astype(BF16)

    return _TileUnit(order=lambda n_past: list(range(n_past, -1, -1)), scores=scores,
                     step=weights_pv, init=init, final=final)


def _moba_unit(q_ref, k_ref, v_ref, gh_ref, kaug_ref, vaug_ref, rhi_ref, rlo_ref, i,
               *, tq, tk, topk):
    S = k_ref.shape[0]
    n_heads = q_ref.shape[1] // HEAD_DIM

    @pl.when(i == 0)
    def _prepare_keys():
        kp = k_ref[...]
        vp = v_ref[...]
        srow = lax.broadcasted_iota(jnp.int32, (S, HEAD_DIM), 0)
        scol = lax.broadcasted_iota(jnp.int32, (S, HEAD_DIM), 1)
        onehot = jnp.where(srow // MOBA_BLOCK == scol, 1.0, 0.0).astype(BF16)
        arow = lax.broadcasted_iota(jnp.int32, (LANES, S), 0) - HEAD_DIM
        acol = lax.broadcasted_iota(jnp.int32, (LANES, S), 1) // MOBA_BLOCK
        avg = jnp.where(arow == acol, 1.0 / MOBA_BLOCK, 0.0).astype(BF16)
        zeros = jnp.zeros((S, HEAD_DIM), BF16)
        ones = jnp.ones((S, HEAD_DIM), BF16)
        for hh in range(n_heads):
            kh = kp[:, hh * HEAD_DIM:(hh + 1) * HEAD_DIM]
            kaug_ref[hh] = jnp.concatenate([kh, onehot], axis=1)
            vaug_ref[hh] = jnp.concatenate([vp[:, hh * HEAD_DIM:(hh + 1) * HEAD_DIM], ones], axis=1)
            kmean = _dot(avg, jnp.concatenate([kh, zeros], axis=1))
            hi, lo = _split_bf16(kmean)
            rhi_ref[hh] = hi
            rlo_ref[hh] = lo

    qt = q_ref[...].astype(F32).T
    group = 8
    bidx = lax.broadcasted_iota(jnp.int32, (group, tq), 0)
    zeros_t = jnp.zeros((HEAD_DIM, tq), F32)
    own = (i * tq + lax.broadcasted_iota(jnp.int32, (group, tq), 1)) // MOBA_BLOCK
    qaugs = []
    for hh in range(n_heads):
        qh = qt[hh * HEAD_DIM:(hh + 1) * HEAD_DIM]
        qz = jnp.concatenate([qh, zeros_t], axis=0).astype(BF16)
        gate = (_dot(rhi_ref[hh], qz) + _dot(rlo_ref[hh], qz))[HEAD_DIM:HEAD_DIM + group]
        valid = bidx < own
        gm = jnp.where(valid, gate, NEG)
        rank = jnp.zeros((group, tq), F32)
        for d in range(1, group):
            nb = pltpu.roll(gm, d, axis=0)
            rank = rank + jnp.where(bidx >= d, jnp.where(nb >= gm, 1.0, 0.0),
                                    jnp.where(nb > gm, 1.0, 0.0))
        allowed = (valid & (rank < topk)) | (bidx == own)
        bias = jnp.where(allowed, 0.0, NEG)
        qaugs.append(jnp.concatenate(
            [qh, bias, jnp.zeros((HEAD_DIM - group, tq), F32)], axis=0).astype(BF16))

    assert tq == tk
    n_sub = tk // MOBA_BLOCK

    def scores(t, diag):
        if not diag:
            return [_dot(kaug_ref[hh, t * tk:(t + 1) * tk, :], qaugs[hh])
                    for hh in range(n_heads)]
        return [[_dot(kaug_ref[hh, t * tk + sb * MOBA_BLOCK:t * tk + (sb + 1) * MOBA_BLOCK, :],
                      qaugs[hh][:, sb * MOBA_BLOCK:]) for sb in range(n_sub)]
                for hh in range(n_heads)]

    def past_step(sts, t, carry):
        pts, stats = [], []
        for hh in range(n_heads):
            m = carry[hh][0]
            m_new = jnp.maximum(m, jnp.max(sts[hh], axis=0, keepdims=True))
            pts.append(jnp.exp2(sts[hh] - m_new).astype(BF16))
            stats.append((m_new, jnp.exp2(m - m_new)))
        new = []
        for hh in range(n_heads):
            _, l, acc = carry[hh]
            m_new, alpha = stats[hh]
            pv = lax.dot_general(vaug_ref[hh, t * tk:(t + 1) * tk, :], pts[hh],
                                 (((0,), (0,)), ((), ())), preferred_element_type=F32)
            new.append((m_new, alpha * l + pv[HEAD_DIM:HEAD_DIM + 1],
                        alpha * acc + pv[:HEAD_DIM]))
        return tuple(new)

    def diag_step(sts, t, carry):
        new = []
        for hh in range(n_heads):
            m, l, acc = carry[hh]
            masked = []
            m_new = m
            for sb in range(n_sub):
                off = sb * MOBA_BLOCK
                st = sts[hh][sb]
                krow = lax.broadcasted_iota(jnp.int32, st.shape, 0)
                qcol = lax.broadcasted_iota(jnp.int32, st.shape, 1)
                st = jnp.where(krow <= qcol, st, NEG)
                masked.append(st)
                top = jnp.max(st, axis=0, keepdims=True)
                if off:
                    top = jnp.concatenate([jnp.full((1, off), NEG, F32), top], axis=1)
                m_new = jnp.maximum(m_new, top)
            alpha = jnp.exp2(m - m_new)
            pv = None
            for sb in range(n_sub):
                off = sb * MOBA_BLOCK
                pt = jnp.exp2(masked[sb] - m_new[:, off:]).astype(BF16)
                part = lax.dot_general(
                    vaug_ref[hh, t * tk + off:t * tk + off + MOBA_BLOCK, :], pt,
                    (((0,), (0,)), ((), ())), preferred_element_type=F32)
                if off:
                    part = jnp.concatenate([jnp.zeros((LANES, off), F32), part], axis=1)
                pv = part if pv is None else pv + part
            new.append((m_new, alpha * l + pv[HEAD_DIM:HEAD_DIM + 1],
                        alpha * acc + pv[:HEAD_DIM]))
        return tuple(new)

    def softmax_pv(sts, t, carry, diag):
        return diag_step(sts, t, carry) if diag else past_step(sts, t, carry)

    def init():
        return tuple((jnp.full((1, tq), NEG, F32), jnp.zeros((1, tq), F32),
                      jnp.zeros((HEAD_DIM, tq), F32)) for _ in range(n_heads))

    def final(carry):
        outs = []
        for hh in range(n_heads):
            _, l, acc = carry[hh]
            o = acc / l
            outs.append(o * lax.rsqrt(jnp.mean(o * o, axis=0, keepdims=True) + NORM_EPS))
        return (jnp.concatenate(outs, axis=0).T * gh_ref[...]).astype(BF16)

    return _TileUnit(order=lambda n_past: [n_past] + list(range(n_past)), scores=scores,
                     step=softmax_pv, init=init, final=final)


ATTN_TQ = 2 * MOBA_BLOCK
MOBA_COLS = 2 * LANES


def _attn_kernel(qa_ref, ka_ref, va_ref, qb_ref, kb_ref, vb_ref, gha_ref, ghb_ref,
                 oa_ref, ob_ref, kaug_ref, vaug_ref, rhi_ref, rlo_ref, *, tq, topk):
    i = pl.program_id(2)
    S = ka_ref.shape[0]
    sb = _sb_unit(qa_ref, ka_ref, va_ref, gha_ref, i, tq=tq, tk=tq, sub=MOBA_BLOCK)
    mb = _moba_unit(qb_ref, kb_ref, vb_ref, ghb_ref, kaug_ref, vaug_ref, rhi_ref, rlo_ref, i,
                    tq=tq, tk=tq, topk=topk)

    def attend(n_past):
        ca, cb = _run_tiles([sb, mb], n_past)
        oa_ref[...] = sb.final(ca)
        ob_ref[...] = mb.final(cb)

    for n_past in range(S // tq):
        pl.when(i == n_past)(functools.partial(attend, n_past))


def _attention(proj, gh_sb, gh_moba, B, S):
    tq = ATTN_TQ
    assert S % tq == 0 and W_MOBA // MOBA_COLS == W_SB // LANES
    nq = S // tq
    n_blk = S // MOBA_BLOCK
    assert n_blk <= 8
    topk = min(MOBA_TOPK, max(n_blk - 1, 1))
    n_mb_heads = MOBA_COLS // HEAD_DIM
    q_row = lambda b, p, i: b * nq + i
    return pl.pallas_call(
        functools.partial(_attn_kernel, tq=tq, topk=topk),
        grid=(B, W_SB // LANES, nq),
        in_specs=[pl.BlockSpec((tq, LANES), lambda b, p, i: (q_row(b, p, i), C_QA // LANES + p)),
                  pl.BlockSpec((S, LANES), lambda b, p, i: (b, C_KA // LANES + p)),
                  pl.BlockSpec((S, LANES), lambda b, p, i: (b, C_VA // LANES + p)),
                  pl.BlockSpec((tq, MOBA_COLS),
                               lambda b, p, i: (q_row(b, p, i), C_QB // MOBA_COLS + p)),
                  pl.BlockSpec((S, MOBA_COLS), lambda b, p, i: (b, C_KB // MOBA_COLS + p)),
                  pl.BlockSpec((S, MOBA_COLS), lambda b, p, i: (b, C_VB // MOBA_COLS + p)),
                  pl.BlockSpec((1, LANES), lambda b, p, i: (0, p)),
                  pl.BlockSpec((1, MOBA_COLS), lambda b, p, i: (0, p))],
        out_specs=[pl.BlockSpec((tq, LANES), lambda b, p, i: (q_row(b, p, i), p)),
                   pl.BlockSpec((tq, MOBA_COLS), lambda b, p, i: (q_row(b, p, i), p))],
        out_shape=[jax.ShapeDtypeStruct((B * S, W_SB), BF16),
                   jax.ShapeDtypeStruct((B * S, W_MOBA), BF16)],
        scratch_shapes=[pltpu.VMEM((n_mb_heads, S, LANES), BF16),
                        pltpu.VMEM((n_mb_heads, S, LANES), BF16),
                        pltpu.VMEM((n_mb_heads, LANES, LANES), BF16),
                        pltpu.VMEM((n_mb_heads, LANES, LANES), BF16)],
        compiler_params=pltpu.CompilerParams(
            dimension_semantics=("arbitrary", "arbitrary", "arbitrary"),
            vmem_limit_bytes=VMEM_LIMIT),
        name="attention",
    )(proj, proj, proj, proj, proj, proj, gh_sb, gh_moba)


def _gmlp_kernel(u_ref, v_ref, gv_ref, ws_ref, b_ref, gh_ref, o_ref, *, tm):
    row = lax.broadcasted_iota(jnp.int32, (GMLP_CHUNK, GMLP_CHUNK), 0)
    col = lax.broadcasted_iota(jnp.int32, (GMLP_CHUNK, GMLP_CHUNK), 1)
    seg_mean = jnp.where(row // HEAD_DIM == col // HEAD_DIM, 1.0 / HEAD_DIM, 0.0).astype(BF16)
    first_group = lax.broadcasted_iota(jnp.int32, (GMLP_CHUNK, LANES), 1) < HEAD_DIM

    def group_rms(x, g):
        hi, lo = _split_bf16(x * x)
        ms = _dot(hi, seg_mean) + _dot(lo, seg_mean)
        return x * lax.rsqrt(ms + NORM_EPS) * g

    for p in range(W_GMLP // LANES):
        lanes = slice(p * LANES, (p + 1) * LANES)
        gu = jax.nn.gelu(u_ref[:, lanes].astype(F32))
        vn = group_rms(jax.nn.gelu(v_ref[:, lanes].astype(F32)), gv_ref[:, lanes]).astype(BF16)
        wm = [jnp.where(col <= row, ws_ref[2 * p + j], 0.0).astype(BF16) for j in range(2)]
        bias = b_ref[:, lanes]
        mixed = jnp.concatenate(
            [jnp.where(first_group, _dot(wm[0], vn[c * GMLP_CHUNK:(c + 1) * GMLP_CHUNK]),
                       _dot(wm[1], vn[c * GMLP_CHUNK:(c + 1) * GMLP_CHUNK])) + bias
             for c in range(tm // GMLP_CHUNK)], axis=0)
        o_ref[:, lanes] = group_rms(gu * mixed, gh_ref[:, lanes]).astype(BF16)


def _gmlp(proj, gv, ws, b_exp, gh, tm):
    T = proj.shape[0]
    nu, nv = C_UC // W_GMLP, C_VC // W_GMLP
    return pl.pallas_call(
        functools.partial(_gmlp_kernel, tm=tm),
        grid=(T // tm,),
        in_specs=[pl.BlockSpec((tm, W_GMLP), lambda i: (i, nu)),
                  pl.BlockSpec((tm, W_GMLP), lambda i: (i, nv)),
                  pl.BlockSpec((1, W_GMLP), lambda i: (0, 0)),
                  pl.BlockSpec((N_GROUPS_GMLP, GMLP_CHUNK, GMLP_CHUNK), lambda i: (0, 0, 0)),
                  pl.BlockSpec((GMLP_CHUNK, W_GMLP), lambda i: (0, 0)),
                  pl.BlockSpec((1, W_GMLP), lambda i: (0, 0))],
        out_specs=pl.BlockSpec((tm, W_GMLP), lambda i: (i, 0)),
        out_shape=jax.ShapeDtypeStruct((T, W_GMLP), BF16),
        compiler_params=pltpu.CompilerParams(dimension_semantics=("arbitrary",),
                                             vmem_limit_bytes=VMEM_LIMIT),
        name="gmlp",
    )(proj, proj, gv, ws, b_exp, gh)


def _out_kernel(osb_ref, omoba_ref, ogmlp_ref, x_ref, w_ref, g_ref, rhi_ref, rlo_ref,
                x1_ref, h_ref, lg_ref):
    o = jnp.concatenate([osb_ref[...], omoba_ref[...], ogmlp_ref[...]], axis=1)
    x1 = x_ref[...] + _dot(o, w_ref[...])
    x1_ref[...] = x1
    hn = _rms(x1, g_ref[...])
    hi, lo = _split_bf16(hn)
    h_ref[...] = _pack_halves(hn)
    stacked = _nt_dot(jnp.concatenate([rhi_ref[...], rlo_ref[...]], axis=0), hi)
    lg_ref[...] = stacked[:ROUTER_ROWS] + stacked[ROUTER_ROWS:] + _nt_dot(rhi_ref[...], lo)


def _out_proj(o_sb, o_moba, o_gmlp, x2d, w_bf16, g, r_hi, r_lo, tm):
    T, D = x2d.shape
    row = lambda i: (i, 0)
    const = lambda i: (0, 0)
    return pl.pallas_call(
        _out_kernel,
        grid=(T // tm,),
        in_specs=[pl.BlockSpec((tm, W_SB), row), pl.BlockSpec((tm, W_MOBA), row),
                  pl.BlockSpec((tm, W_GMLP), row), pl.BlockSpec((tm, D), row),
                  pl.BlockSpec((D, D), const), pl.BlockSpec((1, D), const),
                  pl.BlockSpec((ROUTER_ROWS, D), const), pl.BlockSpec((ROUTER_ROWS, D), const)],
        out_specs=[pl.BlockSpec((tm, D), row), pl.BlockSpec((tm, D // 2), row),
                   pl.BlockSpec((ROUTER_ROWS, tm), lambda i: (0, i))],
        out_shape=[jax.ShapeDtypeStruct((T, D), F32), jax.ShapeDtypeStruct((T, D // 2), jnp.uint32),
                   jax.ShapeDtypeStruct((ROUTER_ROWS, T), F32)],
        compiler_params=pltpu.CompilerParams(dimension_semantics=("arbitrary",),
                                             vmem_limit_bytes=VMEM_LIMIT),
        name="outproj",
    )(o_sb, o_moba, o_gmlp, x2d, w_bf16, g, r_hi, r_lo)


ROUTER_EXPERT_ROW = 8
GW_ROWS = 8
MOE_STEP_BLOCKS = 4
ROUTER_TM = 2048
ROUTER_CHUNK = 512


def _first_max(p):
    rows = lax.broadcasted_iota(jnp.int32, p.shape, 0).astype(F32)
    top = jnp.max(p, axis=0, keepdims=True)
    idx = jnp.min(jnp.where(p == top, rows, float(p.shape[0])), axis=0, keepdims=True)
    return top, idx, rows


def _router_kernel(lg_ref, pos_ref, gw_ref, cnt_ref, cnt_acc, base_ref, *, tm):
    phase = pl.program_id(0)
    i = pl.program_id(1)
    lg = lg_ref[...]
    gl = lg[0:N_EXPERT_GROUPS]
    ge = jnp.exp(gl - jnp.max(gl, axis=0, keepdims=True))
    p_group = ge / jnp.sum(ge, axis=0, keepdims=True)
    p_g, g_sel, _ = _first_max(p_group)
    le = jnp.zeros((EXPERTS_PER_GROUP, tm), F32)
    for g in range(N_EXPERT_GROUPS):
        r0 = ROUTER_EXPERT_ROW + g * EXPERTS_PER_GROUP
        le = jnp.where(g_sel == g, lg[r0:r0 + EXPERTS_PER_GROUP], le)
    ee = jnp.exp(le - jnp.max(le, axis=0, keepdims=True))
    p = ee / jnp.sum(ee, axis=0, keepdims=True)
    p0, i0, rows = _first_max(p)
    p1, i1, _ = _first_max(jnp.where(rows == i0, -1.0, p))
    e0 = g_sel * EXPERTS_PER_GROUP + i0
    e1 = g_sel * EXPERTS_PER_GROUP + i1
    xrow = lax.broadcasted_iota(jnp.int32, (N_EXPERTS, tm), 0).astype(F32)
    oh0 = xrow == e0
    oh1 = xrow == e1
    slots = jnp.where(oh0, 1.0, 0.0) + jnp.where(oh1, 1.0, 0.0)
    tile_cnt = slots[:, 0:LANES]
    for c in range(1, tm // LANES):
        tile_cnt = tile_cnt + slots[:, c * LANES:(c + 1) * LANES]

    @pl.when((phase == 0) & (i == 0))
    def _init():
        cnt_acc[...] = jnp.zeros_like(cnt_acc)

    @pl.when(phase == 0)
    def _count():
        cnt_acc[...] += tile_cnt

    @pl.when((phase == 1) & (i == 0))
    def _starts():
        counts = jnp.sum(cnt_acc[...], axis=1, keepdims=True)
        n_blk = jnp.floor((counts + (MOE_BLOCK - 1)) * (1.0 / MOE_BLOCK))
        er = lax.broadcasted_iota(jnp.int32, (N_EXPERTS, N_EXPERTS), 0)
        ec = lax.broadcasted_iota(jnp.int32, (N_EXPERTS, N_EXPERTS), 1)
        before = jnp.where(ec < er, 1.0, 0.0).astype(BF16)
        start_blk = _dot(before, jnp.broadcast_to(n_blk, (N_EXPERTS, LANES)).astype(BF16))
        base_ref[...] = start_blk * MOE_BLOCK
        cnt_ref[...] = jnp.broadcast_to(counts, (N_EXPERTS, LANES)).astype(jnp.int32)

    @pl.when(phase == 1)
    def _assign():
        chunk = min(ROUTER_CHUNK, tm)
        tr = lax.broadcasted_iota(jnp.int32, (chunk, chunk), 0)
        tc = lax.broadcasted_iota(jnp.int32, (chunk, chunk), 1)
        earlier = jnp.where(tr < tc, 1.0, 0.0).astype(BF16)
        run = base_ref[:, 0:1]
        parts = []
        for c in range(tm // chunk):
            part = slots[:, c * chunk:(c + 1) * chunk]
            parts.append(_dot(part.astype(BF16), earlier) + run)
            run = run + jnp.sum(part, axis=1, keepdims=True)
        row_of = jnp.concatenate(parts, axis=1)
        pos0 = jnp.sum(jnp.where(oh0, row_of, 0.0), axis=0, keepdims=True)
        pos1 = jnp.sum(jnp.where(oh1, row_of, 0.0), axis=0, keepdims=True)
        pos_ref[...] = jnp.concatenate([pos0, pos1], axis=0).astype(jnp.int32)
        scale = p_g / (p0 + p1)
        gw_ref[...] = jnp.concatenate(
            [p0 * scale, p1 * scale, jnp.zeros((GW_ROWS - MOE_TOP_K, tm), F32)], axis=0)
        base_ref[...] += jnp.sum(tile_cnt, axis=1, keepdims=True)


def _router(logits_t, tm):
    T = logits_t.shape[1]
    tok = lambda p, i: (0, i * p)
    return pl.pallas_call(
        functools.partial(_router_kernel, tm=tm),
        grid=(2, T // tm),
        in_specs=[pl.BlockSpec((ROUTER_ROWS, tm), lambda p, i: (0, i))],
        out_specs=[pl.BlockSpec((MOE_TOP_K, tm), tok), pl.BlockSpec((GW_ROWS, tm), tok),
                   pl.BlockSpec((N_EXPERTS, LANES), lambda p, i: (0, 0))],
        out_shape=[jax.ShapeDtypeStruct((MOE_TOP_K, T), jnp.int32),
                   jax.ShapeDtypeStruct((GW_ROWS, T), F32),
                   jax.ShapeDtypeStruct((N_EXPERTS, LANES), jnp.int32)],
        scratch_shapes=[pltpu.VMEM((N_EXPERTS, LANES), F32), pltpu.VMEM((N_EXPERTS, LANES), F32)],
        compiler_params=pltpu.CompilerParams(dimension_semantics=("arbitrary", "arbitrary")),
        name="router",
    )(logits_t)


def _moe_kernel(be_ref, first_ref, slot_ref, next_ref, rows_ref, na_ref, xs_ref, wg_hbm, wu_hbm,
                wd_hbm, ys_ref, wgf, wuf, wdf, wgb, wub, wdb, sem, *, layer):
    def fetch(expert, s):
        return (pltpu.make_async_copy(wg_hbm.at[layer, expert], wgf.at[s], sem.at[s, 0]),
                pltpu.make_async_copy(wu_hbm.at[layer, expert], wuf.at[s], sem.at[s, 1]),
                pltpu.make_async_copy(wd_hbm.at[layer, expert], wdf.at[s], sem.at[s, 2]))

    @pl.when(pl.program_id(0) == 0)
    def _first_fetch():
        for c in fetch(be_ref[0], slot_ref[0]):
            c.start()

    def one_block(b, rows):
        e = be_ref[b]
        slot = slot_ref[b]
        active = b < na_ref[0]

        @pl.when(active & (first_ref[b] == 1))
        def _load_expert():
            for c in fetch(e, slot):
                c.wait()
            wgb[...] = wgf[slot].astype(BF16)
            wub[...] = wuf[slot].astype(BF16)
            wdb[...] = wdf[slot].astype(BF16)
            nxt = next_ref[b]

            @pl.when(nxt >= 0)
            def _prefetch():
                for c in fetch(nxt, 1 - slot):
                    c.start(priority=1)

        @pl.when(active)
        def _compute():
            row = lax.broadcasted_iota(jnp.int32, (MOE_BLOCK, xs_ref.shape[1]), 0)
            words = jnp.where(row < rows_ref[b], xs_ref[rows, :], jnp.uint32(0))
            xb = _unpack_halves(words).astype(BF16)
            a = jax.nn.silu(_dot(xb, wgb[...])) * _dot(xb, wub[...])
            ys_ref[rows, :] = _pack_halves(_dot(a.astype(BF16), wdb[...]))

    for j in range(MOE_STEP_BLOCKS):
        one_block(pl.program_id(0) * MOE_STEP_BLOCKS + j, slice(j * MOE_BLOCK, (j + 1) * MOE_BLOCK))


def _moe_experts(plan, xs, w_gate, w_up, w_down, layer):
    n_rows = xs.shape[0]
    D = 2 * xs.shape[1]
    step_rows = MOE_STEP_BLOCKS * MOE_BLOCK
    assert n_rows % step_rows == 0
    DE = w_gate.shape[-1]
    rows = lambda g, *plan_refs: (jnp.minimum(g, (plan_refs[-1][0] - 1) // MOE_STEP_BLOCKS), 0)
    grid_spec = pltpu.PrefetchScalarGridSpec(
        num_scalar_prefetch=len(plan),
        grid=(n_rows // step_rows,),
        in_specs=[pl.BlockSpec((step_rows, D // 2), rows),
                  pl.BlockSpec(memory_space=pl.ANY), pl.BlockSpec(memory_space=pl.ANY),
                  pl.BlockSpec(memory_space=pl.ANY)],
        out_specs=pl.BlockSpec((step_rows, D // 2), rows),
        scratch_shapes=[pltpu.VMEM((2, D, DE), F32), pltpu.VMEM((2, D, DE), F32),
                        pltpu.VMEM((2, DE, D), F32),
                        pltpu.VMEM((D, DE), BF16), pltpu.VMEM((D, DE), BF16),
                        pltpu.VMEM((DE, D), BF16),
                        pltpu.SemaphoreType.DMA((2, 3))])
    return pl.pallas_call(
        functools.partial(_moe_kernel, layer=layer),
        grid_spec=grid_spec,
        out_shape=jax.ShapeDtypeStruct((n_rows, D // 2), jnp.uint32),
        compiler_params=pltpu.CompilerParams(dimension_semantics=("arbitrary",),
                                             vmem_limit_bytes=VMEM_LIMIT),
        name="moe_experts",
    )(*plan, xs, w_gate, w_up, w_down)


def _final_kernel(x1_ref, yg_ref, gw_ref, g_ref, o_ref):
    o_ref[...] = _rms(_combine(x1_ref, yg_ref, gw_ref), g_ref[...])


def _final_norm(x2d, yg, gw, g, tm):
    T, D = x2d.shape
    row = lambda i: (i, 0)
    return pl.pallas_call(
        _final_kernel,
        grid=(T // tm,),
        in_specs=[pl.BlockSpec((tm, D), row),
                  pl.BlockSpec((MOE_TOP_K, tm, D // 2), lambda i: (0, i, 0)),
                  pl.BlockSpec((GW_ROWS, tm), lambda i: (0, i)),
                  pl.BlockSpec((1, D), lambda i: (0, 0))],
        out_specs=pl.BlockSpec((tm, D), row),
        out_shape=jax.ShapeDtypeStruct((T, D), F32),
        compiler_params=pltpu.CompilerParams(dimension_semantics=("arbitrary",)),
        name="final_norm",
    )(x2d, yg, gw, g)


SC_GATHER_ROWS = 64


def _sc_gather(table, idx):
    info = plsc.get_sparse_core_info()
    n_cores, n_workers = info.num_cores, info.num_cores * info.num_subcores
    N, W = idx.shape[0], table.shape[1]
    per_w = N // n_workers
    n_ch = per_w // SC_GATHER_ROWS
    assert per_w * n_workers == N and n_ch * SC_GATHER_ROWS == per_w
    mesh = plsc.VectorSubcoreMesh(core_axis_name="c", subcore_axis_name="s")

    @functools.partial(
        pl.kernel, mesh=mesh,
        out_type=jax.ShapeDtypeStruct((N, W), table.dtype),
        scratch_types=[pltpu.VMEM((n_ch, SC_GATHER_ROWS), jnp.int32),
                       pltpu.VMEM((2, SC_GATHER_ROWS, W), table.dtype),
                       pltpu.SemaphoreType.DMA((2,)),
                       pltpu.SemaphoreType.DMA((2,))])
    def gather_kernel(table_hbm, idx_hbm, out_hbm, idx_v, rows_v, gsem, ssem):
        wid = lax.axis_index("s") * n_cores + lax.axis_index("c")
        base = wid * per_w
        pltpu.sync_copy(idx_hbm.at[wid], idx_v)

        def fetch(c):
            return pltpu.make_async_copy(table_hbm.at[idx_v.at[c]], rows_v.at[c % 2],
                                         gsem.at[c % 2])

        def write(c):
            return pltpu.make_async_copy(
                rows_v.at[c % 2], out_hbm.at[pl.ds(base + c * SC_GATHER_ROWS, SC_GATHER_ROWS)],
                ssem.at[c % 2])

        fetch(0).start()
        for c in range(n_ch):
            if c + 1 < n_ch:
                if c >= 1:
                    write(c - 1).wait()
                fetch(c + 1).start()
            fetch(c).wait()
            write(c).start()
        if n_ch >= 2:
            write(n_ch - 2).wait()
        write(n_ch - 1).wait()

    return gather_kernel(table, idx.reshape(n_workers, n_ch, SC_GATHER_ROWS))


def _sc_scatter(rows, pos, n_out):
    info = plsc.get_sparse_core_info()
    n_cores, n_workers = info.num_cores, info.num_cores * info.num_subcores
    K, T = pos.shape
    W = rows.shape[1]
    per_w = T // n_workers
    n_ch = per_w // SC_GATHER_ROWS
    assert per_w * n_workers == T and n_ch * SC_GATHER_ROWS == per_w
    mesh = plsc.VectorSubcoreMesh(core_axis_name="c", subcore_axis_name="s")

    @functools.partial(
        pl.kernel, mesh=mesh,
        out_type=jax.ShapeDtypeStruct((n_out, W), rows.dtype),
        scratch_types=[pltpu.VMEM((K, n_ch, SC_GATHER_ROWS), jnp.int32),
                       pltpu.VMEM((2, SC_GATHER_ROWS, W), rows.dtype),
                       pltpu.SemaphoreType.DMA((2,)),
                       pltpu.SemaphoreType.DMA((2, K))])
    def scatter_kernel(rows_hbm, idx_hbm, out_hbm, idx_v, buf_v, lsem, ssem):
        wid = lax.axis_index("s") * n_cores + lax.axis_index("c")
        base = wid * per_w
        for k in range(K):
            pltpu.sync_copy(idx_hbm.at[k, wid], idx_v.at[k])

        def load(c):
            return pltpu.make_async_copy(
                rows_hbm.at[pl.ds(base + c * SC_GATHER_ROWS, SC_GATHER_ROWS)], buf_v.at[c % 2],
                lsem.at[c % 2])

        def store(c, k):
            return pltpu.make_async_copy(buf_v.at[c % 2], out_hbm.at[idx_v.at[k, c]],
                                         ssem.at[c % 2, k])

        load(0).start()
        for c in range(n_ch):
            if c + 1 < n_ch:
                if c >= 1:
                    for k in range(K):
                        store(c - 1, k).wait()
                load(c + 1).start()
            load(c).wait()
            for k in range(K):
                store(c, k).start()
        for c in range(max(n_ch - 2, 0), n_ch):
            for k in range(K):
                store(c, k).wait()

    return scatter_kernel(rows, pos.reshape(K, n_workers, n_ch, SC_GATHER_ROWS))


def _rope_table(positions):
    half = ROT_DIM // 2
    inv_freq = jnp.power(ROPE_THETA, -jnp.arange(half, dtype=F32) / half)
    ang = positions.astype(F32).reshape(-1)[:, None] * inv_freq
    cos, sin = jnp.cos(ang), jnp.sin(ang)
    ones = jnp.ones((ang.shape[0], HEAD_DIM - ROT_DIM - half), F32)
    return jnp.tile(jnp.concatenate([cos, cos, sin, ones], axis=1), (1, LANES // HEAD_DIM))


def _block_plan(counts, T):
    i32 = jnp.int32
    n_blocks = -(-(T * MOE_TOP_K) // MOE_BLOCK) + N_EXPERTS
    experts = jnp.arange(N_EXPERTS, dtype=i32)
    blocks = jnp.arange(n_blocks, dtype=i32)
    n_blk = (counts + MOE_BLOCK - 1) // MOE_BLOCK
    blocks_end = jnp.cumsum(n_blk)
    blk_expert = jnp.minimum(jnp.sum(blocks_end[None, :] <= blocks[:, None], axis=1),
                             N_EXPERTS - 1).astype(i32)
    is_active = n_blk > 0
    ordinal = jnp.cumsum(is_active.astype(i32)) - 1
    later_active = is_active[None, :] & (experts[None, :] > experts[:, None])
    next_active = jnp.min(jnp.where(later_active, experts[None, :], N_EXPERTS), axis=1)
    next_active = jnp.where(next_active < N_EXPERTS, next_active, -1).astype(i32)
    tables = jnp.stack([blocks_end - n_blk, counts, ordinal, next_active], axis=1).astype(i32)
    onehot = (blk_expert[:, None] == experts[None, :]).astype(i32)
    looked = jnp.sum(onehot[:, :, None] * tables[None, :, :], axis=1)
    blk_in_expert = blocks - looked[:, 0]
    blk_first = (blk_in_expert == 0).astype(i32)
    blk_rows = jnp.clip(looked[:, 1] - blk_in_expert * MOE_BLOCK, 0, MOE_BLOCK).astype(i32)
    plan = (blk_expert, blk_first, (looked[:, 2] % 2).astype(i32), looked[:, 3], blk_rows,
            blocks_end[-1:].astype(i32))
    return n_blocks, plan


def kernel(x, positions, w_in, w_out, g_mix_norm, g_head_norm, g_gmlp_vnorm, w_spatial, b_spatial,
           g_ffn_norm, w_router_group, w_router_expert, w_expert_gate, w_expert_up, w_expert_down,
           g_final):
    B, S, D = x.shape
    T = B * S
    depth = w_in.shape[0]
    tm = min(512, T)
    rope = _rope_table(positions)
    xc = x.reshape(T, D)
    moe_out = None
    for l in range(depth):
        gh = g_head_norm[l].reshape(1, -1)
        proj, xc = _inproj(xc, moe_out, g_mix_norm[l].reshape(1, D), w_in[l].astype(BF16),
                           rope, tm)
        o_sb, o_moba = _attention(proj, gh[:, :W_SB], gh[:, W_SB:W_SB + W_MOBA], B, S)
        b_exp = jnp.repeat(b_spatial[l].T, HEAD_DIM, axis=1)
        o_gmlp = _gmlp(proj, g_gmlp_vnorm[l].reshape(1, -1), w_spatial[l], b_exp,
                       gh[:, W_SB + W_MOBA:], min(2 * tm, T))
        w_r = jnp.concatenate(
            [w_router_group[l].T, jnp.zeros((ROUTER_EXPERT_ROW - N_EXPERT_GROUPS, D), F32),
             w_router_expert[l].T,
             jnp.zeros((ROUTER_ROWS - ROUTER_EXPERT_ROW - N_EXPERTS, D), F32)], axis=0)
        r_hi, r_lo = _split_bf16(w_r)
        xc, h, logits_t = _out_proj(o_sb, o_moba, o_gmlp, xc, w_out[l].astype(BF16),
                                    g_ffn_norm[l].reshape(1, D), r_hi, r_lo, min(2 * tm, T))
        pos, gate_w, counts = _router(logits_t, min(ROUTER_TM, T))
        n_blocks, plan = _block_plan(counts[:, 0], T)
        xs = _sc_scatter(h, pos, n_blocks * MOE_BLOCK)
        ys = _moe_experts(plan, xs, w_expert_gate, w_expert_up, w_expert_down, l)
        yg = _sc_gather(ys, pos.reshape(-1)).reshape(MOE_TOP_K, T, D // 2)
        moe_out = (yg, gate_w)
    return _final_norm(xc, moe_out[0], moe_out[1], g_final.reshape(1, D), tm).reshape(B, S, D)
```

```python
import collections
import functools

import jax
import jax.numpy as jnp
from jax import lax
from jax.experimental import pallas as pl
from jax.experimental.pallas import tpu as pltpu
from jax.experimental.pallas import tpu_sc as plsc

F32 = jnp.float32
BF16 = jnp.bfloat16

HEAD_DIM = 64
LANES = 128
MXU_COLS = 256
N_HEADS_SB = 4
N_HEADS_MOBA = 8
N_GROUPS_GMLP = 4
W_SB = N_HEADS_SB * HEAD_DIM
W_MOBA = N_HEADS_MOBA * HEAD_DIM
W_GMLP = N_GROUPS_GMLP * HEAD_DIM
MOBA_BLOCK = 256
MOBA_TOPK = 3
GMLP_CHUNK = 128
ROPE_THETA = 500000.0
ROT_DIM = HEAD_DIM // 4
N_EXPERT_GROUPS = 4
EXPERTS_PER_GROUP = 8
N_EXPERTS = N_EXPERT_GROUPS * EXPERTS_PER_GROUP
MOE_TOP_K = 2
MOE_BLOCK = 256
NORM_EPS = 1e-6
ATTN_SCALE = HEAD_DIM ** -0.5
NEG = -1e30
LOG2E = 1.4426950408889634
ROUTER_ROWS = 64
VMEM_LIMIT = 48 * 1024 * 1024

C_QA, C_KA, C_VA = 0, W_SB, 2 * W_SB
C_QB = 3 * W_SB
C_KB = C_QB + W_MOBA
C_VB = C_KB + W_MOBA
C_UC = C_VB + W_MOBA
C_VC = C_UC + W_GMLP
IN_COLS = C_VC + W_GMLP


def _nt_dot(a, b):
    return lax.dot_general(a, b, (((1,), (1,)), ((), ())), preferred_element_type=F32)


def _dot(a, b):
    return jnp.dot(a, b, preferred_element_type=F32)


def _rms(x, g):
    return x * lax.rsqrt(jnp.mean(x * x, axis=-1, keepdims=True) + NORM_EPS) * g


def _pack_halves(x):
    w = x.shape[1] // 2
    lo = lax.bitcast_convert_type(x[:, :w].astype(BF16).astype(F32), jnp.uint32)
    hi = lax.bitcast_convert_type(x[:, w:].astype(BF16).astype(F32), jnp.uint32)
    return (lo >> 16) | hi


def _unpack_halves(words):
    lo = lax.bitcast_convert_type(words << 16, F32)
    hi = lax.bitcast_convert_type(words & jnp.uint32(0xFFFF0000), F32)
    return jnp.concatenate([lo, hi], axis=1)


def _split_bf16(x):
    hi = x.astype(BF16)
    lo = (x - hi.astype(F32)).astype(BF16)
    return hi, lo


def _combine(x1_ref, yg_ref, gw_ref):
    rows = gw_ref.shape[0]
    eye = jnp.where(lax.broadcasted_iota(jnp.int32, (rows, LANES), 0)
                    == lax.broadcasted_iota(jnp.int32, (rows, LANES), 1), 1.0, 0.0).astype(BF16)
    tn = (((0,), (0,)), ((), ()))
    hi, lo = _split_bf16(gw_ref[...])
    gw = (lax.dot_general(hi, eye, tn, preferred_element_type=F32)
          + lax.dot_general(lo, eye, tn, preferred_element_type=F32))
    return (x1_ref[...] + _unpack_halves(yg_ref[0]) * gw[:, 0:1]
            + _unpack_halves(yg_ref[1]) * gw[:, 1:2])


INPROJ_PARTS = 2


def _inproj_kernel(*refs, combine):
    if combine:
        x1_ref, yg_ref, gw_ref, g_ref, w_ref, rt_ref, o_ref, x_ref = refs
    else:
        x_ref, g_ref, w_ref, rt_ref, o_ref = refs
    tm = o_ref.shape[0]
    part_rows = tm // INPROJ_PARTS
    ys = []
    for r in range(INPROJ_PARTS):
        rows = slice(r * part_rows, (r + 1) * part_rows)
        if combine:
            x = _combine(x1_ref.at[rows], yg_ref.at[:, rows], gw_ref.at[:, rows])
            x_ref[rows, :] = x
        else:
            x = x_ref[rows, :]
        ys.append(_rms(x, g_ref[...]).astype(BF16))
    half = ROT_DIM // 2
    rt = rt_ref[...]
    in_head = lax.broadcasted_iota(jnp.int32, rt.shape, 1) % HEAD_DIM
    rc = jnp.where(in_head < ROT_DIM, rt, 1.0)
    rs1 = jnp.where(in_head < half, -pltpu.roll(rt, LANES - ROT_DIM, axis=1), 0.0)
    rs2 = jnp.where((in_head >= half) & (in_head < ROT_DIM),
                    pltpu.roll(rt, LANES - half, axis=1), 0.0)
    wide = lambda t: jnp.concatenate([t] * (MXU_COLS // LANES), axis=1)
    rc, rs1, rs2 = wide(rc), wide(rs1), wide(rs2)
    for r in range(INPROJ_PARTS):
        rows = slice(r * part_rows, (r + 1) * part_rows)
        for c0 in range(0, IN_COLS, MXU_COLS):
            p = _dot(ys[r], w_ref[:, c0:c0 + MXU_COLS])
            if C_QB <= c0 < C_VB:
                p = (p * rc[rows] + pltpu.roll(p, MXU_COLS - half, axis=1) * rs1[rows]
                     + pltpu.roll(p, half, axis=1) * rs2[rows])
            if c0 < C_KA or C_QB <= c0 < C_KB:
                p = p * (ATTN_SCALE * LOG2E)
            o_ref[rows, c0:c0 + MXU_COLS] = p.astype(BF16)


def _inproj(x2d, moe_out, g, w_bf16, rope, tm):
    T, D = x2d.shape
    row = lambda i: (i, 0)
    const = lambda i: (0, 0)
    combine = moe_out is not None
    x_specs = [pl.BlockSpec((tm, D), row)]
    x_args = [x2d]
    out_specs = [pl.BlockSpec((tm, IN_COLS), row)]
    out_shape = [jax.ShapeDtypeStruct((T, IN_COLS), BF16)]
    if combine:
        yg, gw = moe_out
        x_specs += [pl.BlockSpec((MOE_TOP_K, tm, D // 2), lambda i: (0, i, 0)),
                    pl.BlockSpec((GW_ROWS, tm), lambda i: (0, i))]
        x_args += [yg, gw]
        out_specs.append(pl.BlockSpec((tm, D), row))
        out_shape.append(jax.ShapeDtypeStruct((T, D), F32))
    outs = pl.pallas_call(
        functools.partial(_inproj_kernel, combine=combine),
        grid=(T // tm,),
        in_specs=x_specs + [pl.BlockSpec((1, D), const), pl.BlockSpec((D, IN_COLS), const),
                            pl.BlockSpec((tm, LANES), row)],
        out_specs=out_specs,
        out_shape=out_shape,
        compiler_params=pltpu.CompilerParams(dimension_semantics=("arbitrary",),
                                             vmem_limit_bytes=VMEM_LIMIT),
        name="inproj",
    )(*x_args, g, w_bf16, rope)
    return (outs[0], outs[1]) if combine else (outs[0], x2d)


_TileUnit = collections.namedtuple("_TileUnit", "order scores step init final")


def _run_tiles(units, n_past):
    orders = [u.order(n_past) for u in units]
    carries = [u.init() for u in units]
    zs = [u.scores(o[0], True) for u, o in zip(units, orders)]
    for n in range(n_past + 1):
        nxt = [u.scores(o[n + 1], False) if n < n_past else None
               for u, o in zip(units, orders)]
        carries = [u.step(z, o[n], c, n == 0) for u, o, z, c in zip(units, orders, zs, carries)]
        zs = nxt
    return carries


def _sb_unit(q_ref, k_ref, v_ref, gh_ref, i, *, tq, tk, sub):
    n_heads = LANES // HEAD_DIM
    n_sub = tk // sub
    qt = q_ref[...].astype(F32).T
    zeros_t = jnp.zeros((HEAD_DIM, tq), F32)
    qz = [jnp.concatenate([qt[hh * HEAD_DIM:(hh + 1) * HEAD_DIM] if h2 == hh else zeros_t
                           for h2 in range(n_heads)], axis=0).astype(BF16)
          for hh in range(n_heads)]
    srow = lax.broadcasted_iota(jnp.int32, (sub, sub), 0)
    scol = lax.broadcasted_iota(jnp.int32, (sub, sub), 1)
    later = jnp.where(scol > srow, 1.0, 0.0).astype(BF16)
    assert tq == tk

    def log_rest(z):
        nz = -z
        return jnp.minimum(nz, 0.0) - jnp.log2(1.0 + jnp.exp2(jnp.minimum(z, nz)))

    def scores(t, diag):
        if not diag:
            return [_dot(k_ref[t * tk:(t + 1) * tk, :], qz[hh]) for hh in range(n_heads)]
        return [[_dot(k_ref[t * tk + sb * sub:t * tk + (sb + 1) * sub, :], qz[hh][:, sb * sub:])
                 for sb in range(n_sub)] for hh in range(n_heads)]

    def past_step(zs, t, carry):
        wts, runs = [], []
        for hh in range(n_heads):
            z = zs[hh]
            lr = log_rest(z)
            lrb = lr.astype(BF16)
            tail = carry[hh][0]
            afters = [None] * n_sub
            for sb in reversed(range(n_sub)):
                afters[sb] = _dot(later, lrb[sb * sub:(sb + 1) * sub]) + tail
                tail = afters[sb][0:1] + lr[sb * sub:sb * sub + 1]
            wts.append(jnp.exp2(z + lr + jnp.concatenate(afters, axis=0)).astype(BF16))
            runs.append(tail)
        new = []
        for hh in range(n_heads):
            pv = lax.dot_general(v_ref[t * tk:(t + 1) * tk, :], wts[hh],
                                 (((0,), (0,)), ((), ())), preferred_element_type=F32)
            new.append((runs[hh], carry[hh][1] + pv[hh * HEAD_DIM:(hh + 1) * HEAD_DIM]))
        return tuple(new)

    def diag_step(zs, t, carry):
        new = []
        for hh in range(n_heads):
            tail, acc = carry[hh]
            for sb in reversed(range(n_sub)):
                off = sb * sub
                z = zs[hh][sb]
                krow = lax.broadcasted_iota(jnp.int32, z.shape, 0)
                qcol = lax.broadcasted_iota(jnp.int32, z.shape, 1)
                causal = krow < qcol
                lr = jnp.where(causal, log_rest(z), 0.0)
                after = _dot(later, lr.astype(BF16)) + tail[:, off:]
                w = jnp.where(causal, jnp.exp2(z + lr + after), 0.0).astype(BF16)
                run = after[0:1] + lr[0:1]
                pv = lax.dot_general(v_ref[t * tk + off:t * tk + off + sub, :], w,
                                     (((0,), (0,)), ((), ())), preferred_element_type=F32)
                pv = pv[hh * HEAD_DIM:(hh + 1) * HEAD_DIM]
                if off:
                    run = jnp.concatenate([tail[:, :off], run], axis=1)
                    pv = jnp.concatenate([jnp.zeros((HEAD_DIM, off), F32), pv], axis=1)
                tail, acc = run, acc + pv
            new.append((tail, acc))
        return tuple(new)

    def weights_pv(zs, t, carry, diag):
        return diag_step(zs, t, carry) if diag else past_step(zs, t, carry)

    def init():
        return tuple((jnp.zeros((1, tq), F32), jnp.zeros((HEAD_DIM, tq), F32))
                     for _ in range(n_heads))

    def final(carry):
        outs = []
        for hh in range(n_heads):
            o = carry[hh][1]
            outs.append(o * lax.rsqrt(jnp.mean(o * o, axis=0, keepdims=True) + NORM_EPS))
        return (jnp.concatenate(outs, axis=0).T * gh_ref[...]).astype(BF16)

    return _TileUnit(order=lambda n_past: list(range(n_past, -1, -1)), scores=scores,
                     step=weights_pv, init=init, final=final)


def _moba_unit(q_ref, k_ref, v_ref, gh_ref, kaug_ref, vaug_ref, rhi_ref, rlo_ref, i,
               *, tq, tk, topk):
    S = k_ref.shape[0]
    n_heads = q_ref.shape[1] // HEAD_DIM

    @pl.when(i == 0)
    def _prepare_keys():
        kp = k_ref[...]
        vp = v_ref[...]
        srow = lax.broadcasted_iota(jnp.int32, (S, HEAD_DIM), 0)
        scol = lax.broadcasted_iota(jnp.int32, (S, HEAD_DIM), 1)
        onehot = jnp.where(srow // MOBA_BLOCK == scol, 1.0, 0.0).astype(BF16)
        arow = lax.broadcasted_iota(jnp.int32, (LANES, S), 0) - HEAD_DIM
        acol = lax.broadcasted_iota(jnp.int32, (LANES, S), 1) // MOBA_BLOCK
        avg = jnp.where(arow == acol, 1.0 / MOBA_BLOCK, 0.0).astype(BF16)
        zeros = jnp.zeros((S, HEAD_DIM), BF16)
        ones = jnp.ones((S, HEAD_DIM), BF16)
        for hh in range(n_heads):
            kh = kp[:, hh * HEAD_DIM:(hh + 1) * HEAD_DIM]
            kaug_ref[hh] = jnp.concatenate([kh, onehot], axis=1)
            vaug_ref[hh] = jnp.concatenate([vp[:, hh * HEAD_DIM:(hh + 1) * HEAD_DIM], ones], axis=1)
            kmean = _dot(avg, jnp.concatenate([kh, zeros], axis=1))
            hi, lo = _split_bf16(kmean)
            rhi_ref[hh] = hi
            rlo_ref[hh] = lo

    qt = q_ref[...].astype(F32).T
    group = 8
    bidx = lax.broadcasted_iota(jnp.int32, (group, tq), 0)
    zeros_t = jnp.zeros((HEAD_DIM, tq), F32)
    own = (i * tq + lax.broadcasted_iota(jnp.int32, (group, tq), 1)) // MOBA_BLOCK
    qaugs = []
    for hh in range(n_heads):
        qh = qt[hh * HEAD_DIM:(hh + 1) * HEAD_DIM]
        qz = jnp.concatenate([qh, zeros_t], axis=0).astype(BF16)
        gate = (_dot(rhi_ref[hh], qz) + _dot(rlo_ref[hh], qz))[HEAD_DIM:HEAD_DIM + group]
        valid = bidx < own
        gm = jnp.where(valid, gate, NEG)
        rank = jnp.zeros((group, tq), F32)
        for d in range(1, group):
            nb = pltpu.roll(gm, d, axis=0)
            rank = rank + jnp.where(bidx >= d, jnp.where(nb >= gm, 1.0, 0.0),
                                    jnp.where(nb > gm, 1.0, 0.0))
        allowed = (valid & (rank < topk)) | (bidx == own)
        bias = jnp.where(allowed, 0.0, NEG)
        qaugs.append(jnp.concatenate(
            [qh, bias, jnp.zeros((HEAD_DIM - group, tq), F32)], axis=0).astype(BF16))

    assert tq == tk
    n_sub = tk // MOBA_BLOCK

    def scores(t, diag):
        if not diag:
            return [_dot(kaug_ref[hh, t * tk:(t + 1) * tk, :], qaugs[hh])
                    for hh in range(n_heads)]
        return [[_dot(kaug_ref[hh, t * tk + sb * MOBA_BLOCK:t * tk + (sb + 1) * MOBA_BLOCK, :],
                      qaugs[hh][:, sb * MOBA_BLOCK:]) for sb in range(n_sub)]
                for hh in range(n_heads)]

    def past_step(sts, t, carry):
        pts, stats = [], []
        for hh in range(n_heads):
            m = carry[hh][0]
            m_new = jnp.maximum(m, jnp.max(sts[hh], axis=0, keepdims=True))
            pts.append(jnp.exp2(sts[hh] - m_new).astype(BF16))
            stats.append((m_new, jnp.exp2(m - m_new)))
        new = []
        for hh in range(n_heads):
            _, l, acc = carry[hh]
            m_new, alpha = stats[hh]
            pv = lax.dot_general(vaug_ref[hh, t * tk:(t + 1) * tk, :], pts[hh],
                                 (((0,), (0,)), ((), ())), preferred_element_type=F32)
            new.append((m_new, alpha * l + pv[HEAD_DIM:HEAD_DIM + 1],
                        alpha * acc + pv[:HEAD_DIM]))
        return tuple(new)

    def diag_step(sts, t, carry):
        new = []
        for hh in range(n_heads):
            m, l, acc = carry[hh]
            masked = []
            m_new = m
            for sb in range(n_sub):
                off = sb * MOBA_BLOCK
                st = sts[hh][sb]
                krow = lax.broadcasted_iota(jnp.int32, st.shape, 0)
                qcol = lax.broadcasted_iota(jnp.int32, st.shape, 1)
                st = jnp.where(krow <= qcol, st, NEG)
                masked.append(st)
                top = jnp.max(st, axis=0, keepdims=True)
                if off:
                    top = jnp.concatenate([jnp.full((1, off), NEG, F32), top], axis=1)
                m_new = jnp.maximum(m_new, top)
            alpha = jnp.exp2(m - m_new)
            pv = None
            for sb in range(n_sub):
                off = sb * MOBA_BLOCK
                pt = jnp.exp2(masked[sb] - m_new[:, off:]).astype(BF16)
                part = lax.dot_general(
                    vaug_ref[hh, t * tk + off:t * tk + off + MOBA_BLOCK, :], pt,
                    (((0,), (0,)), ((), ())), preferred_element_type=F32)
                if off:
                    part = jnp.concatenate([jnp.zeros((LANES, off), F32), part], axis=1)
                pv = part if pv is None else pv + part
            new.append((m_new, alpha * l + pv[HEAD_DIM:HEAD_DIM + 1],
                        alpha * acc + pv[:HEAD_DIM]))
        return tuple(new)

    def softmax_pv(sts, t, carry, diag):
        return diag_step(sts, t, carry) if diag else past_step(sts, t, carry)

    def init():
        return tuple((jnp.full((1, tq), NEG, F32), jnp.zeros((1, tq), F32),
                      jnp.zeros((HEAD_DIM, tq), F32)) for _ in range(n_heads))

    def final(carry):
        outs = []
        for hh in range(n_heads):
            _, l, acc = carry[hh]
            o = acc / l
            outs.append(o * lax.rsqrt(jnp.mean(o * o, axis=0, keepdims=True) + NORM_EPS))
        return (jnp.concatenate(outs, axis=0).T * gh_ref[...]).astype(BF16)

    return _TileUnit(order=lambda n_past: [n_past] + list(range(n_past)), scores=scores,
                     step=softmax_pv, init=init, final=final)


ATTN_TQ = 2 * MOBA_BLOCK
MOBA_COLS = 2 * LANES


def _attn_kernel(qa_ref, ka_ref, va_ref, qb_ref, kb_ref, vb_ref, gha_ref, ghb_ref,
                 oa_ref, ob_ref, kaug_ref, vaug_ref, rhi_ref, rlo_ref, *, tq, topk):
    i = pl.program_id(2)
    S = ka_ref.shape[0]
    sb = _sb_unit(qa_ref, ka_ref, va_ref, gha_ref, i, tq=tq, tk=tq, sub=MOBA_BLOCK)
    mb = _moba_unit(qb_ref, kb_ref, vb_ref, ghb_ref, kaug_ref, vaug_ref, rhi_ref, rlo_ref, i,
                    tq=tq, tk=tq, topk=topk)

    def attend(n_past):
        ca, cb = _run_tiles([sb, mb], n_past)
        oa_ref[...] = sb.final(ca)
        ob_ref[...] = mb.final(cb)

    for n_past in range(S // tq):
        pl.when(i == n_past)(functools.partial(attend, n_past))


def _attention(proj, gh_sb, gh_moba, B, S):
    tq = ATTN_TQ
    assert S % tq == 0 and W_MOBA // MOBA_COLS == W_SB // LANES
    nq = S // tq
    n_blk = S // MOBA_BLOCK
    assert n_blk <= 8
    topk = min(MOBA_TOPK, max(n_blk - 1, 1))
    n_mb_heads = MOBA_COLS // HEAD_DIM
    q_row = lambda b, p, i: b * nq + i
    return pl.pallas_call(
        functools.partial(_attn_kernel, tq=tq, topk=topk),
        grid=(B, W_SB // LANES, nq),
        in_specs=[pl.BlockSpec((tq, LANES), lambda b, p, i: (q_row(b, p, i), C_QA // LANES + p)),
                  pl.BlockSpec((S, LANES), lambda b, p, i: (b, C_KA // LANES + p)),
                  pl.BlockSpec((S, LANES), lambda b, p, i: (b, C_VA // LANES + p)),
                  pl.BlockSpec((tq, MOBA_COLS),
                               lambda b, p, i: (q_row(b, p, i), C_QB // MOBA_COLS + p)),
                  pl.BlockSpec((S, MOBA_COLS), lambda b, p, i: (b, C_KB // MOBA_COLS + p)),
                  pl.BlockSpec((S, MOBA_COLS), lambda b, p, i: (b, C_VB // MOBA_COLS + p)),
                  pl.BlockSpec((1, LANES), lambda b, p, i: (0, p)),
                  pl.BlockSpec((1, MOBA_COLS), lambda b, p, i: (0, p))],
        out_specs=[pl.BlockSpec((tq, LANES), lambda b, p, i: (q_row(b, p, i), p)),
                   pl.BlockSpec((tq, MOBA_COLS), lambda b, p, i: (q_row(b, p, i), p))],
        out_shape=[jax.ShapeDtypeStruct((B * S, W_SB), BF16),
                   jax.ShapeDtypeStruct((B * S, W_MOBA), BF16)],
        scratch_shapes=[pltpu.VMEM((n_mb_heads, S, LANES), BF16),
                        pltpu.VMEM((n_mb_heads, S, LANES), BF16),
                        pltpu.VMEM((n_mb_heads, LANES, LANES), BF16),
                        pltpu.VMEM((n_mb_heads, LANES, LANES), BF16)],
        compiler_params=pltpu.CompilerParams(
            dimension_semantics=("arbitrary", "arbitrary", "arbitrary"),
            vmem_limit_bytes=VMEM_LIMIT),
        name="attention",
    )(proj, proj, proj, proj, proj, proj, gh_sb, gh_moba)


def _gmlp_kernel(u_ref, v_ref, gv_ref, ws_ref, b_ref, gh_ref, o_ref, *, tm):
    row = lax.broadcasted_iota(jnp.int32, (GMLP_CHUNK, GMLP_CHUNK), 0)
    col = lax.broadcasted_iota(jnp.int32, (GMLP_CHUNK, GMLP_CHUNK), 1)
    seg_mean = jnp.where(row // HEAD_DIM == col // HEAD_DIM, 1.0 / HEAD_DIM, 0.0).astype(BF16)
    first_group = lax.broadcasted_iota(jnp.int32, (GMLP_CHUNK, LANES), 1) < HEAD_DIM

    def group_rms(x, g):
        hi, lo = _split_bf16(x * x)
        ms = _dot(hi, seg_mean) + _dot(lo, seg_mean)
        return x * lax.rsqrt(ms + NORM_EPS) * g

    for p in range(W_GMLP // LANES):
        lanes = slice(p * LANES, (p + 1) * LANES)
        gu = jax.nn.gelu(u_ref[:, lanes].astype(F32))
        vn = group_rms(jax.nn.gelu(v_ref[:, lanes].astype(F32)), gv_ref[:, lanes]).astype(BF16)
        wm = [jnp.where(col <= row, ws_ref[2 * p + j], 0.0).astype(BF16) for j in range(2)]
        bias = b_ref[:, lanes]
        mixed = jnp.concatenate(
            [jnp.where(first_group, _dot(wm[0], vn[c * GMLP_CHUNK:(c + 1) * GMLP_CHUNK]),
                       _dot(wm[1], vn[c * GMLP_CHUNK:(c + 1) * GMLP_CHUNK])) + bias
             for c in range(tm // GMLP_CHUNK)], axis=0)
        o_ref[:, lanes] = group_rms(gu * mixed, gh_ref[:, lanes]).astype(BF16)


def _gmlp(proj, gv, ws, b_exp, gh, tm):
    T = proj.shape[0]
    nu, nv = C_UC // W_GMLP, C_VC // W_GMLP
    return pl.pallas_call(
        functools.partial(_gmlp_kernel, tm=tm),
        grid=(T // tm,),
        in_specs=[pl.BlockSpec((tm, W_GMLP), lambda i: (i, nu)),
                  pl.BlockSpec((tm, W_GMLP), lambda i: (i, nv)),
                  pl.BlockSpec((1, W_GMLP), lambda i: (0, 0)),
                  pl.BlockSpec((N_GROUPS_GMLP, GMLP_CHUNK, GMLP_CHUNK), lambda i: (0, 0, 0)),
                  pl.BlockSpec((GMLP_CHUNK, W_GMLP), lambda i: (0, 0)),
                  pl.BlockSpec((1, W_GMLP), lambda i: (0, 0))],
        out_specs=pl.BlockSpec((tm, W_GMLP), lambda i: (i, 0)),
        out_shape=jax.ShapeDtypeStruct((T, W_GMLP), BF16),
        compiler_params=pltpu.CompilerParams(dimension_semantics=("arbitrary",),
                                             vmem_limit_bytes=VMEM_LIMIT),
        name="gmlp",
    )(proj, proj, gv, ws, b_exp, gh)


def _out_kernel(osb_ref, omoba_ref, ogmlp_ref, x_ref, w_ref, g_ref, rhi_ref, rlo_ref,
                x1_ref, h_ref, lg_ref):
    o = jnp.concatenate([osb_ref[...], omoba_ref[...], ogmlp_ref[...]], axis=1)
    x1 = x_ref[...] + _dot(o, w_ref[...])
    x1_ref[...] = x1
    hn = _rms(x1, g_ref[...])
    hi, lo = _split_bf16(hn)
    h_ref[...] = _pack_halves(hn)
    stacked = _nt_dot(jnp.concatenate([rhi_ref[...], rlo_ref[...]], axis=0), hi)
    lg_ref[...] = stacked[:ROUTER_ROWS] + stacked[ROUTER_ROWS:] + _nt_dot(rhi_ref[...], lo)


def _out_proj(o_sb, o_moba, o_gmlp, x2d, w_bf16, g, r_hi, r_lo, tm):
    T, D = x2d.shape
    row = lambda i: (i, 0)
    const = lambda i: (0, 0)
    return pl.pallas_call(
        _out_kernel,
        grid=(T // tm,),
        in_specs=[pl.BlockSpec((tm, W_SB), row), pl.BlockSpec((tm, W_MOBA), row),
                  pl.BlockSpec((tm, W_GMLP), row), pl.BlockSpec((tm, D), row),
                  pl.BlockSpec((D, D), const), pl.BlockSpec((1, D), const),
                  pl.BlockSpec((ROUTER_ROWS, D), const), pl.BlockSpec((ROUTER_ROWS, D), const)],
        out_specs=[pl.BlockSpec((tm, D), row), pl.BlockSpec((tm, D // 2), row),
                   pl.BlockSpec((ROUTER_ROWS, tm), lambda i: (0, i))],
        out_shape=[jax.ShapeDtypeStruct((T, D), F32), jax.ShapeDtypeStruct((T, D // 2), jnp.uint32),
                   jax.ShapeDtypeStruct((ROUTER_ROWS, T), F32)],
        compiler_params=pltpu.CompilerParams(dimension_semantics=("arbitrary",),
                                             vmem_limit_bytes=VMEM_LIMIT),
        name="outproj",
    )(o_sb, o_moba, o_gmlp, x2d, w_bf16, g, r_hi, r_lo)


ROUTER_EXPERT_ROW = 8
GW_ROWS = 8
MOE_STEP_BLOCKS = 4
ROUTER_TM = 2048
ROUTER_CHUNK = 512


def _first_max(p):
    rows = lax.broadcasted_iota(jnp.int32, p.shape, 0).astype(F32)
    top = jnp.max(p, axis=0, keepdims=True)
    idx = jnp.min(jnp.where(p == top, rows, float(p.shape[0])), axis=0, keepdims=True)
    return top, idx, rows


def _router_kernel(lg_ref, pos_ref, gw_ref, cnt_ref, cnt_acc, base_ref, *, tm):
    phase = pl.program_id(0)
    i = pl.program_id(1)
    lg = lg_ref[...]
    gl = lg[0:N_EXPERT_GROUPS]
    ge = jnp.exp(gl - jnp.max(gl, axis=0, keepdims=True))
    p_group = ge / jnp.sum(ge, axis=0, keepdims=True)
    p_g, g_sel, _ = _first_max(p_group)
    le = jnp.zeros((EXPERTS_PER_GROUP, tm), F32)
    for g in range(N_EXPERT_GROUPS):
        r0 = ROUTER_EXPERT_ROW + g * EXPERTS_PER_GROUP
        le = jnp.where(g_sel == g, lg[r0:r0 + EXPERTS_PER_GROUP], le)
    ee = jnp.exp(le - jnp.max(le, axis=0, keepdims=True))
    p = ee / jnp.sum(ee, axis=0, keepdims=True)
    p0, i0, rows = _first_max(p)
    p1, i1, _ = _first_max(jnp.where(rows == i0, -1.0, p))
    e0 = g_sel * EXPERTS_PER_GROUP + i0
    e1 = g_sel * EXPERTS_PER_GROUP + i1
    xrow = lax.broadcasted_iota(jnp.int32, (N_EXPERTS, tm), 0).astype(F32)
    oh0 = xrow == e0
    oh1 = xrow == e1
    slots = jnp.where(oh0, 1.0, 0.0) + jnp.where(oh1, 1.0, 0.0)
    tile_cnt = slots[:, 0:LANES]
    for c in range(1, tm // LANES):
        tile_cnt = tile_cnt + slots[:, c * LANES:(c + 1) * LANES]

    @pl.when((phase == 0) & (i == 0))
    def _init():
        cnt_acc[...] = jnp.zeros_like(cnt_acc)

    @pl.when(phase == 0)
    def _count():
        cnt_acc[...] += tile_cnt

    @pl.when((phase == 1) & (i == 0))
    def _starts():
        counts = jnp.sum(cnt_acc[...], axis=1, keepdims=True)
        n_blk = jnp.floor((counts + (MOE_BLOCK - 1)) * (1.0 / MOE_BLOCK))
        er = lax.broadcasted_iota(jnp.int32, (N_EXPERTS, N_EXPERTS), 0)
        ec = lax.broadcasted_iota(jnp.int32, (N_EXPERTS, N_EXPERTS), 1)
        before = jnp.where(ec < er, 1.0, 0.0).astype(BF16)
        start_blk = _dot(before, jnp.broadcast_to(n_blk, (N_EXPERTS, LANES)).astype(BF16))
        base_ref[...] = start_blk * MOE_BLOCK
        cnt_ref[...] = jnp.broadcast_to(counts, (N_EXPERTS, LANES)).astype(jnp.int32)

    @pl.when(phase == 1)
    def _assign():
        chunk = min(ROUTER_CHUNK, tm)
        tr = lax.broadcasted_iota(jnp.int32, (chunk, chunk), 0)
        tc = lax.broadcasted_iota(jnp.int32, (chunk, chunk), 1)
        earlier = jnp.where(tr < tc, 1.0, 0.0).astype(BF16)
        run = base_ref[:, 0:1]
        parts = []
        for c in range(tm // chunk):
            part = slots[:, c * chunk:(c + 1) * chunk]
            parts.append(_dot(part.astype(BF16), earlier) + run)
            run = run + jnp.sum(part, axis=1, keepdims=True)
        row_of = jnp.concatenate(parts, axis=1)
        pos0 = jnp.sum(jnp.where(oh0, row_of, 0.0), axis=0, keepdims=True)
        pos1 = jnp.sum(jnp.where(oh1, row_of, 0.0), axis=0, keepdims=True)
        pos_ref[...] = jnp.concatenate([pos0, pos1], axis=0).astype(jnp.int32)
        scale = p_g / (p0 + p1)
        gw_ref[...] = jnp.concatenate(
            [p0 * scale, p1 * scale, jnp.zeros((GW_ROWS - MOE_TOP_K, tm), F32)], axis=0)
        base_ref[...] += jnp.sum(tile_cnt, axis=1, keepdims=True)


def _router(logits_t, tm):
    T = logits_t.shape[1]
    tok = lambda p, i: (0, i * p)
    return pl.pallas_call(
        functools.partial(_router_kernel, tm=tm),
        grid=(2, T // tm),
        in_specs=[pl.BlockSpec((ROUTER_ROWS, tm), lambda p, i: (0, i))],
        out_specs=[pl.BlockSpec((MOE_TOP_K, tm), tok), pl.BlockSpec((GW_ROWS, tm), tok),
                   pl.BlockSpec((N_EXPERTS, LANES), lambda p, i: (0, 0))],
        out_shape=[jax.ShapeDtypeStruct((MOE_TOP_K, T), jnp.int32),
                   jax.ShapeDtypeStruct((GW_ROWS, T), F32),
                   jax.ShapeDtypeStruct((N_EXPERTS, LANES), jnp.int32)],
        scratch_shapes=[pltpu.VMEM((N_EXPERTS, LANES), F32), pltpu.VMEM((N_EXPERTS, LANES), F32)],
        compiler_params=pltpu.CompilerParams(dimension_semantics=("arbitrary", "arbitrary")),
        name="router",
    )(logits_t)


def _moe_kernel(be_ref, first_ref, slot_ref, next_ref, rows_ref, na_ref, xs_ref, wg_hbm, wu_hbm,
                wd_hbm, ys_ref, wgf, wuf, wdf, wgb, wub, wdb, sem, *, layer):
    def fetch(expert, s):
        return (pltpu.make_async_copy(wg_hbm.at[layer, expert], wgf.at[s], sem.at[s, 0]),
                pltpu.make_async_copy(wu_hbm.at[layer, expert], wuf.at[s], sem.at[s, 1]),
                pltpu.make_async_copy(wd_hbm.at[layer, expert], wdf.at[s], sem.at[s, 2]))

    @pl.when(pl.program_id(0) == 0)
    def _first_fetch():
        for c in fetch(be_ref[0], slot_ref[0]):
            c.start()

    def one_block(b, rows):
        e = be_ref[b]
        slot = slot_ref[b]
        active = b < na_ref[0]

        @pl.when(active & (first_ref[b] == 1))
        def _load_expert():
            for c in fetch(e, slot):
                c.wait()
            wgb[...] = wgf[slot].astype(BF16)
            wub[...] = wuf[slot].astype(BF16)
            wdb[...] = wdf[slot].astype(BF16)
            nxt = next_ref[b]

            @pl.when(nxt >= 0)
            def _prefetch():
                for c in fetch(nxt, 1 - slot):
                    c.start(priority=1)

        @pl.when(active)
        def _compute():
            row = lax.broadcasted_iota(jnp.int32, (MOE_BLOCK, xs_ref.shape[1]), 0)
            words = jnp.where(row < rows_ref[b], xs_ref[rows, :], jnp.uint32(0))
            xb = _unpack_halves(words).astype(BF16)
            a = jax.nn.silu(_dot(xb, wgb[...])) * _dot(xb, wub[...])
            ys_ref[rows, :] = _pack_halves(_dot(a.astype(BF16), wdb[...]))

    for j in range(MOE_STEP_BLOCKS):
        one_block(pl.program_id(0) * MOE_STEP_BLOCKS + j, slice(j * MOE_BLOCK, (j + 1) * MOE_BLOCK))


def _moe_experts(plan, xs, w_gate, w_up, w_down, layer):
    n_rows = xs.shape[0]
    D = 2 * xs.shape[1]
    step_rows = MOE_STEP_BLOCKS * MOE_BLOCK
    assert n_rows % step_rows == 0
    DE = w_gate.shape[-1]
    rows = lambda g, *plan_refs: (jnp.minimum(g, (plan_refs[-1][0] - 1) // MOE_STEP_BLOCKS), 0)
    grid_spec = pltpu.PrefetchScalarGridSpec(
        num_scalar_prefetch=len(plan),
        grid=(n_rows // step_rows,),
        in_specs=[pl.BlockSpec((step_rows, D // 2), rows),
                  pl.BlockSpec(memory_space=pl.ANY), pl.BlockSpec(memory_space=pl.ANY),
                  pl.BlockSpec(memory_space=pl.ANY)],
        out_specs=pl.BlockSpec((step_rows, D // 2), rows),
        scratch_shapes=[pltpu.VMEM((2, D, DE), F32), pltpu.VMEM((2, D, DE), F32),
                        pltpu.VMEM((2, DE, D), F32),
                        pltpu.VMEM((D, DE), BF16), pltpu.VMEM((D, DE), BF16),
                        pltpu.VMEM((DE, D), BF16),
                        pltpu.SemaphoreType.DMA((2, 3))])
    return pl.pallas_call(
        functools.partial(_moe_kernel, layer=layer),
        grid_spec=grid_spec,
        out_shape=jax.ShapeDtypeStruct((n_rows, D // 2), jnp.uint32),
        compiler_params=pltpu.CompilerParams(dimension_semantics=("arbitrary",),
                                             vmem_limit_bytes=VMEM_LIMIT),
        name="moe_experts",
    )(*plan, xs, w_gate, w_up, w_down)


def _final_kernel(x1_ref, yg_ref, gw_ref, g_ref, o_ref):
    o_ref[...] = _rms(_combine(x1_ref, yg_ref, gw_ref), g_ref[...])


def _final_norm(x2d, yg, gw, g, tm):
    T, D = x2d.shape
    row = lambda i: (i, 0)
    return pl.pallas_call(
        _final_kernel,
        grid=(T // tm,),
        in_specs=[pl.BlockSpec((tm, D), row),
                  pl.BlockSpec((MOE_TOP_K, tm, D // 2), lambda i: (0, i, 0)),
                  pl.BlockSpec((GW_ROWS, tm), lambda i: (0, i)),
                  pl.BlockSpec((1, D), lambda i: (0, 0))],
        out_specs=pl.BlockSpec((tm, D), row),
        out_shape=jax.ShapeDtypeStruct((T, D), F32),
        compiler_params=pltpu.CompilerParams(dimension_semantics=("arbitrary",)),
        name="final_norm",
    )(x2d, yg, gw, g)


SC_GATHER_ROWS = 64


def _sc_gather(table, idx):
    info = plsc.get_sparse_core_info()
    n_cores, n_workers = info.num_cores, info.num_cores * info.num_subcores
    N, W = idx.shape[0], table.shape[1]
    per_w = N // n_workers
    n_ch = per_w // SC_GATHER_ROWS
    assert per_w * n_workers == N and n_ch * SC_GATHER_ROWS == per_w
    mesh = plsc.VectorSubcoreMesh(core_axis_name="c", subcore_axis_name="s")

    @functools.partial(
        pl.kernel, mesh=mesh,
        out_type=jax.ShapeDtypeStruct((N, W), table.dtype),
        scratch_types=[pltpu.VMEM((n_ch, SC_GATHER_ROWS), jnp.int32),
                       pltpu.VMEM((2, SC_GATHER_ROWS, W), table.dtype),
                       pltpu.SemaphoreType.DMA((2,)),
                       pltpu.SemaphoreType.DMA((2,))])
    def gather_kernel(table_hbm, idx_hbm, out_hbm, idx_v, rows_v, gsem, ssem):
        wid = lax.axis_index("s") * n_cores + lax.axis_index("c")
        base = wid * per_w
        pltpu.sync_copy(idx_hbm.at[wid], idx_v)

        def fetch(c):
            return pltpu.make_async_copy(table_hbm.at[idx_v.at[c]], rows_v.at[c % 2],
                                         gsem.at[c % 2])

        def write(c):
            return pltpu.make_async_copy(
                rows_v.at[c % 2], out_hbm.at[pl.ds(base + c * SC_GATHER_ROWS, SC_GATHER_ROWS)],
                ssem.at[c % 2])

        fetch(0).start()
        for c in range(n_ch):
            if c + 1 < n_ch:
                if c >= 1:
                    write(c - 1).wait()
                fetch(c + 1).start()
            fetch(c).wait()
            write(c).start()
        if n_ch >= 2:
            write(n_ch - 2).wait()
        write(n_ch - 1).wait()

    return gather_kernel(table, idx.reshape(n_workers, n_ch, SC_GATHER_ROWS))


def _sc_scatter(rows, pos, n_out):
    info = plsc.get_sparse_core_info()
    n_cores, n_workers = info.num_cores, info.num_cores * info.num_subcores
    K, T = pos.shape
    W = rows.shape[1]
    per_w = T // n_workers
    n_ch = per_w // SC_GATHER_ROWS
    assert per_w * n_workers == T and n_ch * SC_GATHER_ROWS == per_w
    mesh = plsc.VectorSubcoreMesh(core_axis_name="c", subcore_axis_name="s")

    @functools.partial(
        pl.kernel, mesh=mesh,
        out_type=jax.ShapeDtypeStruct((n_out, W), rows.dtype),
        scratch_types=[pltpu.VMEM((K, n_ch, SC_GATHER_ROWS), jnp.int32),
                       pltpu.VMEM((2, SC_GATHER_ROWS, W), rows.dtype),
                       pltpu.SemaphoreType.DMA((2,)),
                       pltpu.SemaphoreType.DMA((2, K))])
    def scatter_kernel(rows_hbm, idx_hbm, out_hbm, idx_v, buf_v, lsem, ssem):
        wid = lax.axis_index("s") * n_cores + lax.axis_index("c")
        base = wid * per_w
        for k in range(K):
            pltpu.sync_copy(idx_hbm.at[k, wid], idx_v.at[k])

        def load(c):
            return pltpu.make_async_copy(
                rows_hbm.at[pl.ds(base + c * SC_GATHER_ROWS, SC_GATHER_ROWS)], buf_v.at[c % 2],
                lsem.at[c % 2])

        def store(c, k):
            return pltpu.make_async_copy(buf_v.at[c % 2], out_hbm.at[idx_v.at[k, c]],
                                         ssem.at[c % 2, k])

        load(0).start()
        for c in range(n_ch):
            if c + 1 < n_ch:
                if c >= 1:
                    for k in range(K):
                        store(c - 1, k).wait()
                load(c + 1).start()
            load(c).wait()
            for k in range(K):
                store(c, k).start()
        for c in range(max(n_ch - 2, 0), n_ch):
            for k in range(K):
                store(c, k).wait()

    return scatter_kernel(rows, pos.reshape(K, n_workers, n_ch, SC_GATHER_ROWS))


def _rope_table(positions):
    half = ROT_DIM // 2
    inv_freq = jnp.power(ROPE_THETA, -jnp.arange(half, dtype=F32) / half)
    ang = positions.astype(F32).reshape(-1)[:, None] * inv_freq
    cos, sin = jnp.cos(ang), jnp.sin(ang)
    ones = jnp.ones((ang.shape[0], HEAD_DIM - ROT_DIM - half), F32)
    return jnp.tile(jnp.concatenate([cos, cos, sin, ones], axis=1), (1, LANES // HEAD_DIM))


def _block_plan(counts, T):
    i32 = jnp.int32
    n_blocks = -(-(T * MOE_TOP_K) // MOE_BLOCK) + N_EXPERTS
    experts = jnp.arange(N_EXPERTS, dtype=i32)
    blocks = jnp.arange(n_blocks, dtype=i32)
    n_blk = (counts + MOE_BLOCK - 1) // MOE_BLOCK
    blocks_end = jnp.cumsum(n_blk)
    blk_expert = jnp.minimum(jnp.sum(blocks_end[None, :] <= blocks[:, None], axis=1),
                             N_EXPERTS - 1).astype(i32)
    is_active = n_blk > 0
    ordinal = jnp.cumsum(is_active.astype(i32)) - 1
    later_active = is_active[None, :] & (experts[None, :] > experts[:, None])
    next_active = jnp.min(jnp.where(later_active, experts[None, :], N_EXPERTS), axis=1)
    next_active = jnp.where(next_active < N_EXPERTS, next_active, -1).astype(i32)
    tables = jnp.stack([blocks_end - n_blk, counts, ordinal, next_active], axis=1).astype(i32)
    onehot = (blk_expert[:, None] == experts[None, :]).astype(i32)
    looked = jnp.sum(onehot[:, :, None] * tables[None, :, :], axis=1)
    blk_in_expert = blocks - looked[:, 0]
    blk_first = (blk_in_expert == 0).astype(i32)
    blk_rows = jnp.clip(looked[:, 1] - blk_in_expert * MOE_BLOCK, 0, MOE_BLOCK).astype(i32)
    plan = (blk_expert, blk_first, (looked[:, 2] % 2).astype(i32), looked[:, 3], blk_rows,
            blocks_end[-1:].astype(i32))
    return n_blocks, plan


def kernel(x, positions, w_in, w_out, g_mix_norm, g_head_norm, g_gmlp_vnorm, w_spatial, b_spatial,
           g_ffn_norm, w_router_group, w_router_expert, w_expert_gate, w_expert_up, w_expert_down,
           g_final):
    B, S, D = x.shape
    T = B * S
    depth = w_in.shape[0]
    tm = min(512, T)
    rope = _rope_table(positions)
    xc = x.reshape(T, D)
    moe_out = None
    for l in range(depth):
        gh = g_head_norm[l].reshape(1, -1)
        proj, xc = _inproj(xc, moe_out, g_mix_norm[l].reshape(1, D), w_in[l].astype(BF16),
                           rope, tm)
        o_sb, o_moba = _attention(proj, gh[:, :W_SB], gh[:, W_SB:W_SB + W_MOBA], B, S)
        b_exp = jnp.repeat(b_spatial[l].T, HEAD_DIM, axis=1)
        o_gmlp = _gmlp(proj, g_gmlp_vnorm[l].reshape(1, -1), w_spatial[l], b_exp,
                       gh[:, W_SB + W_MOBA:], min(2 * tm, T))
        w_r = jnp.concatenate(
            [w_router_group[l].T, jnp.zeros((ROUTER_EXPERT_ROW - N_EXPERT_GROUPS, D), F32),
             w_router_expert[l].T,
             jnp.zeros((ROUTER_ROWS - ROUTER_EXPERT_ROW - N_EXPERTS, D), F32)], axis=0)
        r_hi, r_lo = _split_bf16(w_r)
        xc, h, logits_t = _out_proj(o_sb, o_moba, o_gmlp, xc, w_out[l].astype(BF16),
                                    g_ffn_norm[l].reshape(1, D), r_hi, r_lo, min(2 * tm, T))
        pos, gate_w, counts = _router(logits_t, min(ROUTER_TM, T))
        n_blocks, plan = _block_plan(counts[:, 0], T)
        xs = _sc_scatter(h, pos, n_blocks * MOE_BLOCK)
        ys = _moe_experts(plan, xs, w_expert_gate, w_expert_up, w_expert_down, l)
        yg = _sc_gather(ys, pos.reshape(-1)).reshape(MOE_TOP_K, T, D // 2)
        moe_out = (yg, gate_w)
    return _final_norm(xc, moe_out[0], moe_out[1], g_final.reshape(1, D), tm).reshape(B, S, D)
```

```python
import collections
import functools

import jax
import jax.numpy as jnp
from jax import lax
from jax.experimental import pallas as pl
from jax.experimental.pallas import tpu as pltpu
from jax.experimental.pallas import tpu_sc as plsc

F32 = jnp.float32
BF16 = jnp.bfloat16

HEAD_DIM = 64
LANES = 128
MXU_COLS = 256
N_HEADS_SB = 4
N_HEADS_MOBA = 8
N_GROUPS_GMLP = 4
W_SB = N_HEADS_SB * HEAD_DIM
W_MOBA = N_HEADS_MOBA * HEAD_DIM
W_GMLP = N_GROUPS_GMLP * HEAD_DIM
MOBA_BLOCK = 256
MOBA_TOPK = 3
GMLP_CHUNK = 128
ROPE_THETA = 500000.0
ROT_DIM = HEAD_DIM // 4
N_EXPERT_GROUPS = 4
EXPERTS_PER_GROUP = 8
N_EXPERTS = N_EXPERT_GROUPS * EXPERTS_PER_GROUP
MOE_TOP_K = 2
MOE_BLOCK = 256
NORM_EPS = 1e-6
ATTN_SCALE = HEAD_DIM ** -0.5
NEG = -1e30
LOG2E = 1.4426950408889634
ROUTER_ROWS = 64
VMEM_LIMIT = 48 * 1024 * 1024

C_QA, C_KA, C_VA = 0, W_SB, 2 * W_SB
C_QB = 3 * W_SB
C_KB = C_QB + W_MOBA
C_VB = C_KB + W_MOBA
C_UC = C_VB + W_MOBA
C_VC = C_UC + W_GMLP
IN_COLS = C_VC + W_GMLP


def _nt_dot(a, b):
    return lax.dot_general(a, b, (((1,), (1,)), ((), ())), preferred_element_type=F32)


def _dot(a, b):
    return jnp.dot(a, b, preferred_element_type=F32)


def _rms(x, g):
    return x * lax.rsqrt(jnp.mean(x * x, axis=-1, keepdims=True) + NORM_EPS) * g


def _pack_halves(x):
    w = x.shape[1] // 2
    lo = lax.bitcast_convert_type(x[:, :w].astype(BF16).astype(F32), jnp.uint32)
    hi = lax.bitcast_convert_type(x[:, w:].astype(BF16).astype(F32), jnp.uint32)
    return (lo >> 16) | hi


def _unpack_halves(words):
    lo = lax.bitcast_convert_type(words << 16, F32)
    hi = lax.bitcast_convert_type(words & jnp.uint32(0xFFFF0000), F32)
    return jnp.concatenate([lo, hi], axis=1)


def _split_bf16(x):
    hi = x.astype(BF16)
    lo = (x - hi.astype(F32)).astype(BF16)
    return hi, lo


def _combine(x1_ref, yg_ref, gw_ref):
    rows = gw_ref.shape[0]
    eye = jnp.where(lax.broadcasted_iota(jnp.int32, (rows, LANES), 0)
                    == lax.broadcasted_iota(jnp.int32, (rows, LANES), 1), 1.0, 0.0).astype(BF16)
    tn = (((0,), (0,)), ((), ()))
    hi, lo = _split_bf16(gw_ref[...])
    gw = (lax.dot_general(hi, eye, tn, preferred_element_type=F32)
          + lax.dot_general(lo, eye, tn, preferred_element_type=F32))
    return (x1_ref[...] + _unpack_halves(yg_ref[0]) * gw[:, 0:1]
            + _unpack_halves(yg_ref[1]) * gw[:, 1:2])


INPROJ_PARTS = 2


def _inproj_kernel(*refs, combine):
    if combine:
        x1_ref, yg_ref, gw_ref, g_ref, w_ref, rt_ref, o_ref, x_ref = refs
    else:
        x_ref, g_ref, w_ref, rt_ref, o_ref = refs
    tm = o_ref.shape[0]
    part_rows = tm // INPROJ_PARTS
    ys = []
    for r in range(INPROJ_PARTS):
        rows = slice(r * part_rows, (r + 1) * part_rows)
        if combine:
            x = _combine(x1_ref.at[rows], yg_ref.at[:, rows], gw_ref.at[:, rows])
            x_ref[rows, :] = x
        else:
            x = x_ref[rows, :]
        ys.append(_rms(x, g_ref[...]).astype(BF16))
    half = ROT_DIM // 2
    rt = rt_ref[...]
    in_head = lax.broadcasted_iota(jnp.int32, rt.shape, 1) % HEAD_DIM
    rc = jnp.where(in_head < ROT_DIM, rt, 1.0)
    rs1 = jnp.where(in_head < half, -pltpu.roll(rt, LANES - ROT_DIM, axis=1), 0.0)
    rs2 = jnp.where((in_head >= half) & (in_head < ROT_DIM),
                    pltpu.roll(rt, LANES - half, axis=1), 0.0)
    wide = lambda t: jnp.concatenate([t] * (MXU_COLS // LANES), axis=1)
    rc, rs1, rs2 = wide(rc), wide(rs1), wide(rs2)
    for r in range(INPROJ_PARTS):
        rows = slice(r * part_rows, (r + 1) * part_rows)
        for c0 in range(0, IN_COLS, MXU_COLS):
            p = _dot(ys[r], w_ref[:, c0:c0 + MXU_COLS])
            if C_QB <= c0 < C_VB:
                p = (p * rc[rows] + pltpu.roll(p, MXU_COLS - half, axis=1) * rs1[rows]
                     + pltpu.roll(p, half, axis=1) * rs2[rows])
            if c0 < C_KA or C_QB <= c0 < C_KB:
                p = p * (ATTN_SCALE * LOG2E)
            o_ref[rows, c0:c0 + MXU_COLS] = p.astype(BF16)


def _inproj(x2d, moe_out, g, w_bf16, rope, tm):
    T, D = x2d.shape
    row = lambda i: (i, 0)
    const = lambda i: (0, 0)
    combine = moe_out is not None
    x_specs = [pl.BlockSpec((tm, D), row)]
    x_args = [x2d]
    out_specs = [pl.BlockSpec((tm, IN_COLS), row)]
    out_shape = [jax.ShapeDtypeStruct((T, IN_COLS), BF16)]
    if combine:
        yg, gw = moe_out
        x_specs += [pl.BlockSpec((MOE_TOP_K, tm, D // 2), lambda i: (0, i, 0)),
                    pl.BlockSpec((GW_ROWS, tm), lambda i: (0, i))]
        x_args += [yg, gw]
        out_specs.append(pl.BlockSpec((tm, D), row))
        out_shape.append(jax.ShapeDtypeStruct((T, D), F32))
    outs = pl.pallas_call(
        functools.partial(_inproj_kernel, combine=combine),
        grid=(T // tm,),
        in_specs=x_specs + [pl.BlockSpec((1, D), const), pl.BlockSpec((D, IN_COLS), const),
                            pl.BlockSpec((tm, LANES), row)],
        out_specs=out_specs,
        out_shape=out_shape,
        compiler_params=pltpu.CompilerParams(dimension_semantics=("arbitrary",),
                                             vmem_limit_bytes=VMEM_LIMIT),
        name="inproj",
    )(*x_args, g, w_bf16, rope)
    return (outs[0], outs[1]) if combine else (outs[0], x2d)


_TileUnit = collections.namedtuple("_TileUnit", "order scores step init final")


def _run_tiles(units, n_past):
    orders = [u.order(n_past) for u in units]
    carries = [u.init() for u in units]
    zs = [u.scores(o[0], True) for u, o in zip(units, orders)]
    for n in range(n_past + 1):
        nxt = [u.scores(o[n + 1], False) if n < n_past else None
               for u, o in zip(units, orders)]
        carries = [u.step(z, o[n], c, n == 0) for u, o, z, c in zip(units, orders, zs, carries)]
        zs = nxt
    return carries


def _sb_unit(q_ref, k_ref, v_ref, gh_ref, i, *, tq, tk, sub):
    n_heads = LANES // HEAD_DIM
    n_sub = tk // sub
    qt = q_ref[...].astype(F32).T
    zeros_t = jnp.zeros((HEAD_DIM, tq), F32)
    qz = [jnp.concatenate([qt[hh * HEAD_DIM:(hh + 1) * HEAD_DIM] if h2 == hh else zeros_t
                           for h2 in range(n_heads)], axis=0).astype(BF16)
          for hh in range(n_heads)]
    srow = lax.broadcasted_iota(jnp.int32, (sub, sub), 0)
    scol = lax.broadcasted_iota(jnp.int32, (sub, sub), 1)
    later = jnp.where(scol > srow, 1.0, 0.0).astype(BF16)
    assert tq == tk

    def log_rest(z):
        nz = -z
        return jnp.minimum(nz, 0.0) - jnp.log2(1.0 + jnp.exp2(jnp.minimum(z, nz)))

    def scores(t, diag):
        if not diag:
            return [_dot(k_ref[t * tk:(t + 1) * tk, :], qz[hh]) for hh in range(n_heads)]
        return [[_dot(k_ref[t * tk + sb * sub:t * tk + (sb + 1) * sub, :], qz[hh][:, sb * sub:])
                 for sb in range(n_sub)] for hh in range(n_heads)]

    def past_step(zs, t, carry):
        wts, runs = [], []
        for hh in range(n_heads):
            z = zs[hh]
            lr = log_rest(z)
            lrb = lr.astype(BF16)
            tail = carry[hh][0]
            afters = [None] * n_sub
            for sb in reversed(range(n_sub)):
                afters[sb] = _dot(later, lrb[sb * sub:(sb + 1) * sub]) + tail
                tail = afters[sb][0:1] + lr[sb * sub:sb * sub + 1]
            wts.append(jnp.exp2(z + lr + jnp.concatenate(afters, axis=0)).astype(BF16))
            runs.append(tail)
        new = []
        for hh in range(n_heads):
            pv = lax.dot_general(v_ref[t * tk:(t + 1) * tk, :], wts[hh],
                                 (((0,), (0,)), ((), ())), preferred_element_type=F32)
            new.append((runs[hh], carry[hh][1] + pv[hh * HEAD_DIM:(hh + 1) * HEAD_DIM]))
        return tuple(new)

    def diag_step(zs, t, carry):
        new = []
        for hh in range(n_heads):
            tail, acc = carry[hh]
            for sb in reversed(range(n_sub)):
                off = sb * sub
                z = zs[hh][sb]
                krow = lax.broadcasted_iota(jnp.int32, z.shape, 0)
                qcol = lax.broadcasted_iota(jnp.int32, z.shape, 1)
                causal = krow < qcol
                lr = jnp.where(causal, log_rest(z), 0.0)
                after = _dot(later, lr.astype(BF16)) + tail[:, off:]
                w = jnp.where(causal, jnp.exp2(z + lr + after), 0.0).astype(BF16)
                run = after[0:1] + lr[0:1]
                pv = lax.dot_general(v_ref[t * tk + off:t * tk + off + sub, :], w,
                                     (((0,), (0,)), ((), ())), preferred_element_type=F32)
                pv = pv[hh * HEAD_DIM:(hh + 1) * HEAD_DIM]
                if off:
                    run = jnp.concatenate([tail[:, :off], run], axis=1)
                    pv = jnp.concatenate([jnp.zeros((HEAD_DIM, off), F32), pv], axis=1)
                tail, acc = run, acc + pv
            new.append((tail, acc))
        return tuple(new)

    def weights_pv(zs, t, carry, diag):
        return diag_step(zs, t, carry) if diag else past_step(zs, t, carry)

    def init():
        return tuple((jnp.zeros((1, tq), F32), jnp.zeros((HEAD_DIM, tq), F32))
                     for _ in range(n_heads))

    def final(carry):
        outs = []
        for hh in range(n_heads):
            o = carry[hh][1]
            outs.append(o * lax.rsqrt(jnp.mean(o * o, axis=0, keepdims=True) + NORM_EPS))
        return (jnp.concatenate(outs, axis=0).T * gh_ref[...]).astype(BF16)

    return _TileUnit(order=lambda n_past: list(range(n_past, -1, -1)), scores=scores,
                     step=weights_pv, init=init, final=final)


def _moba_unit(q_ref, k_ref, v_ref, gh_ref, kaug_ref, vaug_ref, rhi_ref, rlo_ref, i,
               *, tq, tk, topk):
    S = k_ref.shape[0]
    n_heads = q_ref.shape[1] // HEAD_DIM

    @pl.when(i == 0)
    def _prepare_keys():
        kp = k_ref[...]
        vp = v_ref[...]
        srow = lax.broadcasted_iota(jnp.int32, (S, HEAD_DIM), 0)
        scol = lax.broadcasted_iota(jnp.int32, (S, HEAD_DIM), 1)
        onehot = jnp.where(srow // MOBA_BLOCK == scol, 1.0, 0.0).astype(BF16)
        arow = lax.broadcasted_iota(jnp.int32, (LANES, S), 0) - HEAD_DIM
        acol = lax.broadcasted_iota(jnp.int32, (LANES, S), 1) // MOBA_BLOCK
        avg = jnp.where(arow == acol, 1.0 / MOBA_BLOCK, 0.0).astype(BF16)
        zeros = jnp.zeros((S, HEAD_DIM), BF16)
        ones = jnp.ones((S, HEAD_DIM), BF16)
        for hh in range(n_heads):
            kh = kp[:, hh * HEAD_DIM:(hh + 1) * HEAD_DIM]
            kaug_ref[hh] = jnp.concatenate([kh, onehot], axis=1)
            vaug_ref[hh] = jnp.concatenate([vp[:, hh * HEAD_DIM:(hh + 1) * HEAD_DIM], ones], axis=1)
            kmean = _dot(avg, jnp.concatenate([kh, zeros], axis=1))
            hi, lo = _split_bf16(kmean)
            rhi_ref[hh] = hi
            rlo_ref[hh] = lo

    qt = q_ref[...].astype(F32).T
    group = 8
    bidx = lax.broadcasted_iota(jnp.int32, (group, tq), 0)
    zeros_t = jnp.zeros((HEAD_DIM, tq), F32)
    own = (i * tq + lax.broadcasted_iota(jnp.int32, (group, tq), 1)) // MOBA_BLOCK
    qaugs = []
    for hh in range(n_heads):
        qh = qt[hh * HEAD_DIM:(hh + 1) * HEAD_DIM]
        qz = jnp.concatenate([qh, zeros_t], axis=0).astype(BF16)
        gate = (_dot(rhi_ref[hh], qz) + _dot(rlo_ref[hh], qz))[HEAD_DIM:HEAD_DIM + group]
        valid = bidx < own
        gm = jnp.where(valid, gate, NEG)
        rank = jnp.zeros((group, tq), F32)
        for d in range(1, group):
            nb = pltpu.roll(gm, d, axis=0)
            rank = rank + jnp.where(bidx >= d, jnp.where(nb >= gm, 1.0, 0.0),
                                    jnp.where(nb > gm, 1.0, 0.0))
        allowed = (valid & (rank < topk)) | (bidx == own)
        bias = jnp.where(allowed, 0.0, NEG)
        qaugs.append(jnp.concatenate(
            [qh, bias, jnp.zeros((HEAD_DIM - group, tq), F32)], axis=0).astype(BF16))

    assert tq == tk
    n_sub = tk // MOBA_BLOCK

    def scores(t, diag):
        if not diag:
            return [_dot(kaug_ref[hh, t * tk:(t + 1) * tk, :], qaugs[hh])
                    for hh in range(n_heads)]
        return [[_dot(kaug_ref[hh, t * tk + sb * MOBA_BLOCK:t * tk + (sb + 1) * MOBA_BLOCK, :],
                      qaugs[hh][:, sb * MOBA_BLOCK:]) for sb in range(n_sub)]
                for hh in range(n_heads)]

    def past_step(sts, t, carry):
        pts, stats = [], []
        for hh in range(n_heads):
            m = carry[hh][0]
            m_new = jnp.maximum(m, jnp.max(sts[hh], axis=0, keepdims=True))
            pts.append(jnp.exp2(sts[hh] - m_new).astype(BF16))
            stats.append((m_new, jnp.exp2(m - m_new)))
        new = []
        for hh in range(n_heads):
            _, l, acc = carry[hh]
            m_new, alpha = stats[hh]
            pv = lax.dot_general(vaug_ref[hh, t * tk:(t + 1) * tk, :], pts[hh],
                                 (((0,), (0,)), ((), ())), preferred_element_type=F32)
            new.append((m_new, alpha * l + pv[HEAD_DIM:HEAD_DIM + 1],
                        alpha * acc + pv[:HEAD_DIM]))
        return tuple(new)

    def diag_step(sts, t, carry):
        new = []
        for hh in range(n_heads):
            m, l, acc = carry[hh]
            masked = []
            m_new = m
            for sb in range(n_sub):
                off = sb * MOBA_BLOCK
                st = sts[hh][sb]
                krow = lax.broadcasted_iota(jnp.int32, st.shape, 0)
                qcol = lax.broadcasted_iota(jnp.int32, st.shape, 1)
                st = jnp.where(krow <= qcol, st, NEG)
                masked.append(st)
                top = jnp.max(st, axis=0, keepdims=True)
                if off:
                    top = jnp.concatenate([jnp.full((1, off), NEG, F32), top], axis=1)
                m_new = jnp.maximum(m_new, top)
            alpha = jnp.exp2(m - m_new)
            pv = None
            for sb in range(n_sub):
                off = sb * MOBA_BLOCK
                pt = jnp.exp2(masked[sb] - m_new[:, off:]).astype(BF16)
                part = lax.dot_general(
                    vaug_ref[hh, t * tk + off:t * tk + off + MOBA_BLOCK, :], pt,
                    (((0,), (0,)), ((), ())), preferred_element_type=F32)
                if off:
                    part = jnp.concatenate([jnp.zeros((LANES, off), F32), part], axis=1)
                pv = part if pv is None else pv + part
            new.append((m_new, alpha * l + pv[HEAD_DIM:HEAD_DIM + 1],
                        alpha * acc + pv[:HEAD_DIM]))
        return tuple(new)

    def softmax_pv(sts, t, carry, diag):
        return diag_step(sts, t, carry) if diag else past_step(sts, t, carry)

    def init():
        return tuple((jnp.full((1, tq), NEG, F32), jnp.zeros((1, tq), F32),
                      jnp.zeros((HEAD_DIM, tq), F32)) for _ in range(n_heads))

    def final(carry):
        outs = []
        for hh in range(n_heads):
            _, l, acc = carry[hh]
            o = acc / l
            outs.append(o * lax.rsqrt(jnp.mean(o * o, axis=0, keepdims=True) + NORM_EPS))
        return (jnp.concatenate(outs, axis=0).T * gh_ref[...]).astype(BF16)

    return _TileUnit(order=lambda n_past: [n_past] + list(range(n_past)), scores=scores,
                     step=softmax_pv, init=init, final=final)


ATTN_TQ = 2 * MOBA_BLOCK
MOBA_COLS = 2 * LANES


def _attn_kernel(qa_ref, ka_ref, va_ref, qb_ref, kb_ref, vb_ref, gha_ref, ghb_ref,
                 oa_ref, ob_ref, kaug_ref, vaug_ref, rhi_ref, rlo_ref, *, tq, topk):
    i = pl.program_id(2)
    S = ka_ref.shape[0]
    sb = _sb_unit(qa_ref, ka_ref, va_ref, gha_ref, i, tq=tq, tk=tq, sub=MOBA_BLOCK)
    mb = _moba_unit(qb_ref, kb_ref, vb_ref, ghb_ref, kaug_ref, vaug_ref, rhi_ref, rlo_ref, i,
                    tq=tq, tk=tq, topk=topk)

    def attend(n_past):
        ca, cb = _run_tiles([sb, mb], n_past)
        oa_ref[...] = sb.final(ca)
        ob_ref[...] = mb.final(cb)

    for n_past in range(S // tq):
        pl.when(i == n_past)(functools.partial(attend, n_past))


def _attention(proj, gh_sb, gh_moba, B, S):
    tq = ATTN_TQ
    assert S % tq == 0 and W_MOBA // MOBA_COLS == W_SB // LANES
    nq = S // tq
    n_blk = S // MOBA_BLOCK
    assert n_blk <= 8
    topk = min(MOBA_TOPK, max(n_blk - 1, 1))
    n_mb_heads = MOBA_COLS // HEAD_DIM
    q_row = lambda b, p, i: b * nq + i
    return pl.pallas_call(
        functools.partial(_attn_kernel, tq=tq, topk=topk),
        grid=(B, W_SB // LANES, nq),
        in_specs=[pl.BlockSpec((tq, LANES), lambda b, p, i: (q_row(b, p, i), C_QA // LANES + p)),
                  pl.BlockSpec((S, LANES), lambda b, p, i: (b, C_KA // LANES + p)),
                  pl.BlockSpec((S, LANES), lambda b, p, i: (b, C_VA // LANES + p)),
                  pl.BlockSpec((tq, MOBA_COLS),
                               lambda b, p, i: (q_row(b, p, i), C_QB // MOBA_COLS + p)),
                  pl.BlockSpec((S, MOBA_COLS), lambda b, p, i: (b, C_KB // MOBA_COLS + p)),
                  pl.BlockSpec((S, MOBA_COLS), lambda b, p, i: (b, C_VB // MOBA_COLS + p)),
                  pl.BlockSpec((1, LANES), lambda b, p, i: (0, p)),
                  pl.BlockSpec((1, MOBA_COLS), lambda b, p, i: (0, p))],
        out_specs=[pl.BlockSpec((tq, LANES), lambda b, p, i: (q_row(b, p, i), p)),
                   pl.BlockSpec((tq, MOBA_COLS), lambda b, p, i: (q_row(b, p, i), p))],
        out_shape=[jax.ShapeDtypeStruct((B * S, W_SB), BF16),
                   jax.ShapeDtypeStruct((B * S, W_MOBA), BF16)],
        scratch_shapes=[pltpu.VMEM((n_mb_heads, S, LANES), BF16),
                        pltpu.VMEM((n_mb_heads, S, LANES), BF16),
                        pltpu.VMEM((n_mb_heads, LANES, LANES), BF16),
                        pltpu.VMEM((n_mb_heads, LANES, LANES), BF16)],
        compiler_params=pltpu.CompilerParams(
            dimension_semantics=("arbitrary", "arbitrary", "arbitrary"),
            vmem_limit_bytes=VMEM_LIMIT),
        name="attention",
    )(proj, proj, proj, proj, proj, proj, gh_sb, gh_moba)


def _gmlp_kernel(u_ref, v_ref, gv_ref, ws_ref, b_ref, gh_ref, o_ref, *, tm):
    row = lax.broadcasted_iota(jnp.int32, (GMLP_CHUNK, GMLP_CHUNK), 0)
    col = lax.broadcasted_iota(jnp.int32, (GMLP_CHUNK, GMLP_CHUNK), 1)
    seg_mean = jnp.where(row // HEAD_DIM == col // HEAD_DIM, 1.0 / HEAD_DIM, 0.0).astype(BF16)
    first_group = lax.broadcasted_iota(jnp.int32, (GMLP_CHUNK, LANES), 1) < HEAD_DIM

    def group_rms(x, g):
        hi, lo = _split_bf16(x * x)
        ms = _dot(hi, seg_mean) + _dot(lo, seg_mean)
        return x * lax.rsqrt(ms + NORM_EPS) * g

    for p in range(W_GMLP // LANES):
        lanes = slice(p * LANES, (p + 1) * LANES)
        gu = jax.nn.gelu(u_ref[:, lanes].astype(F32))
        vn = group_rms(jax.nn.gelu(v_ref[:, lanes].astype(F32)), gv_ref[:, lanes]).astype(BF16)
        wm = [jnp.where(col <= row, ws_ref[2 * p + j], 0.0).astype(BF16) for j in range(2)]
        bias = b_ref[:, lanes]
        mixed = jnp.concatenate(
            [jnp.where(first_group, _dot(wm[0], vn[c * GMLP_CHUNK:(c + 1) * GMLP_CHUNK]),
                       _dot(wm[1], vn[c * GMLP_CHUNK:(c + 1) * GMLP_CHUNK])) + bias
             for c in range(tm // GMLP_CHUNK)], axis=0)
        o_ref[:, lanes] = group_rms(gu * mixed, gh_ref[:, lanes]).astype(BF16)


def _gmlp(proj, gv, ws, b_exp, gh, tm):
    T = proj.shape[0]
    nu, nv = C_UC // W_GMLP, C_VC // W_GMLP
    return pl.pallas_call(
        functools.partial(_gmlp_kernel, tm=tm),
        grid=(T // tm,),
        in_specs=[pl.BlockSpec((tm, W_GMLP), lambda i: (i, nu)),
                  pl.BlockSpec((tm, W_GMLP), lambda i: (i, nv)),
                  pl.BlockSpec((1, W_GMLP), lambda i: (0, 0)),
                  pl.BlockSpec((N_GROUPS_GMLP, GMLP_CHUNK, GMLP_CHUNK), lambda i: (0, 0, 0)),
                  pl.BlockSpec((GMLP_CHUNK, W_GMLP), lambda i: (0, 0)),
                  pl.BlockSpec((1, W_GMLP), lambda i: (0, 0))],
        out_specs=pl.BlockSpec((tm, W_GMLP), lambda i: (i, 0)),
        out_shape=jax.ShapeDtypeStruct((T, W_GMLP), BF16),
        compiler_params=pltpu.CompilerParams(dimension_semantics=("arbitrary",),
                                             vmem_limit_bytes=VMEM_LIMIT),
        name="gmlp",
    )(proj, proj, gv, ws, b_exp, gh)


def _out_kernel(osb_ref, omoba_ref, ogmlp_ref, x_ref, w_ref, g_ref, rhi_ref, rlo_ref,
                x1_ref, h_ref, lg_ref):
    o = jnp.concatenate([osb_ref[...], omoba_ref[...], ogmlp_ref[...]], axis=1)
    x1 = x_ref[...] + _dot(o, w_ref[...])
    x1_ref[...] = x1
    hn = _rms(x1, g_ref[...])
    hi, lo = _split_bf16(hn)
    h_ref[...] = _pack_halves(hn)
    stacked = _nt_dot(jnp.concatenate([rhi_ref[...], rlo_ref[...]], axis=0), hi)
    lg_ref[...] = stacked[:ROUTER_ROWS] + stacked[ROUTER_ROWS:] + _nt_dot(rhi_ref[...], lo)


def _out_proj(o_sb, o_moba, o_gmlp, x2d, w_bf16, g, r_hi, r_lo, tm):
    T, D = x2d.shape
    row = lambda i: (i, 0)
    const = lambda i: (0, 0)
    return pl.pallas_call(
        _out_kernel,
        grid=(T // tm,),
        in_specs=[pl.BlockSpec((tm, W_SB), row), pl.BlockSpec((tm, W_MOBA), row),
                  pl.BlockSpec((tm, W_GMLP), row), pl.BlockSpec((tm, D), row),
                  pl.BlockSpec((D, D), const), pl.BlockSpec((1, D), const),
                  pl.BlockSpec((ROUTER_ROWS, D), const), pl.BlockSpec((ROUTER_ROWS, D), const)],
        out_specs=[pl.BlockSpec((tm, D), row), pl.BlockSpec((tm, D // 2), row),
                   pl.BlockSpec((ROUTER_ROWS, tm), lambda i: (0, i))],
        out_shape=[jax.ShapeDtypeStruct((T, D), F32), jax.ShapeDtypeStruct((T, D // 2), jnp.uint32),
                   jax.ShapeDtypeStruct((ROUTER_ROWS, T), F32)],
        compiler_params=pltpu.CompilerParams(dimension_semantics=("arbitrary",),
                                             vmem_limit_bytes=VMEM_LIMIT),
        name="outproj",
    )(o_sb, o_moba, o_gmlp, x2d, w_bf16, g, r_hi, r_lo)


ROUTER_EXPERT_ROW = 8
GW_ROWS = 8
MOE_STEP_BLOCKS = 4
ROUTER_TM = 2048
ROUTER_CHUNK = 512


def _first_max(p):
    rows = lax.broadcasted_iota(jnp.int32, p.shape, 0).astype(F32)
    top = jnp.max(p, axis=0, keepdims=True)
    idx = jnp.min(jnp.where(p == top, rows, float(p.shape[0])), axis=0, keepdims=True)
    return top, idx, rows


def _router_kernel(lg_ref, pos_ref, gw_ref, cnt_ref, cnt_acc, base_ref, *, tm):
    phase = pl.program_id(0)
    i = pl.program_id(1)
    lg = lg_ref[...]
    gl = lg[0:N_EXPERT_GROUPS]
    ge = jnp.exp(gl - jnp.max(gl, axis=0, keepdims=True))
    p_group = ge / jnp.sum(ge, axis=0, keepdims=True)
    p_g, g_sel, _ = _first_max(p_group)
    le = jnp.zeros((EXPERTS_PER_GROUP, tm), F32)
    for g in range(N_EXPERT_GROUPS):
        r0 = ROUTER_EXPERT_ROW + g * EXPERTS_PER_GROUP
        le = jnp.where(g_sel == g, lg[r0:r0 + EXPERTS_PER_GROUP], le)
    ee = jnp.exp(le - jnp.max(le, axis=0, keepdims=True))
    p = ee / jnp.sum(ee, axis=0, keepdims=True)
    p0, i0, rows = _first_max(p)
    p1, i1, _ = _first_max(jnp.where(rows == i0, -1.0, p))
    e0 = g_sel * EXPERTS_PER_GROUP + i0
    e1 = g_sel * EXPERTS_PER_GROUP + i1
    xrow = lax.broadcasted_iota(jnp.int32, (N_EXPERTS, tm), 0).astype(F32)
    oh0 = xrow == e0
    oh1 = xrow == e1
    slots = jnp.where(oh0, 1.0, 0.0) + jnp.where(oh1, 1.0, 0.0)
    tile_cnt = slots[:, 0:LANES]
    for c in range(1, tm // LANES):
        tile_cnt = tile_cnt + slots[:, c * LANES:(c + 1) * LANES]

    @pl.when((phase == 0) & (i == 0))
    def _init():
        cnt_acc[...] = jnp.zeros_like(cnt_acc)

    @pl.when(phase == 0)
    def _count():
        cnt_acc[...] += tile_cnt

    @pl.when((phase == 1) & (i == 0))
    def _starts():
        counts = jnp.sum(cnt_acc[...], axis=1, keepdims=True)
        n_blk = jnp.floor((counts + (MOE_BLOCK - 1)) * (1.0 / MOE_BLOCK))
        er = lax.broadcasted_iota(jnp.int32, (N_EXPERTS, N_EXPERTS), 0)
        ec = lax.broadcasted_iota(jnp.int32, (N_EXPERTS, N_EXPERTS), 1)
        before = jnp.where(ec < er, 1.0, 0.0).astype(BF16)
        start_blk = _dot(before, jnp.broadcast_to(n_blk, (N_EXPERTS, LANES)).astype(BF16))
        base_ref[...] = start_blk * MOE_BLOCK
        cnt_ref[...] = jnp.broadcast_to(counts, (N_EXPERTS, LANES)).astype(jnp.int32)

    @pl.when(phase == 1)
    def _assign():
        chunk = min(ROUTER_CHUNK, tm)
        tr = lax.broadcasted_iota(jnp.int32, (chunk, chunk), 0)
        tc = lax.broadcasted_iota(jnp.int32, (chunk, chunk), 1)
        earlier = jnp.where(tr < tc, 1.0, 0.0).astype(BF16)
        run = base_ref[:, 0:1]
        parts = []
        for c in range(tm // chunk):
            part = slots[:, c * chunk:(c + 1) * chunk]
            parts.append(_dot(part.astype(BF16), earlier) + run)
            run = run + jnp.sum(part, axis=1, keepdims=True)
        row_of = jnp.concatenate(parts, axis=1)
        pos0 = jnp.sum(jnp.where(oh0, row_of, 0.0), axis=0, keepdims=True)
        pos1 = jnp.sum(jnp.where(oh1, row_of, 0.0), axis=0, keepdims=True)
        pos_ref[...] = jnp.concatenate([pos0, pos1], axis=0).astype(jnp.int32)
        scale = p_g / (p0 + p1)
        gw_ref[...] = jnp.concatenate(
            [p0 * scale, p1 * scale, jnp.zeros((GW_ROWS - MOE_TOP_K, tm), F32)], axis=0)
        base_ref[...] += jnp.sum(tile_cnt, axis=1, keepdims=True)


def _router(logits_t, tm):
    T = logits_t.shape[1]
    tok = lambda p, i: (0, i * p)
    return pl.pallas_call(
        functools.partial(_router_kernel, tm=tm),
        grid=(2, T // tm),
        in_specs=[pl.BlockSpec((ROUTER_ROWS, tm), lambda p, i: (0, i))],
        out_specs=[pl.BlockSpec((MOE_TOP_K, tm), tok), pl.BlockSpec((GW_ROWS, tm), tok),
                   pl.BlockSpec((N_EXPERTS, LANES), lambda p, i: (0, 0))],
        out_shape=[jax.ShapeDtypeStruct((MOE_TOP_K, T), jnp.int32),
                   jax.ShapeDtypeStruct((GW_ROWS, T), F32),
                   jax.ShapeDtypeStruct((N_EXPERTS, LANES), jnp.int32)],
        scratch_shapes=[pltpu.VMEM((N_EXPERTS, LANES), F32), pltpu.VMEM((N_EXPERTS, LANES), F32)],
        compiler_params=pltpu.CompilerParams(dimension_semantics=("arbitrary", "arbitrary")),
        name="router",
    )(logits_t)


def _moe_kernel(be_ref, first_ref, slot_ref, next_ref, rows_ref, na_ref, xs_ref, wg_hbm, wu_hbm,
                wd_hbm, ys_ref, wgf, wuf, wdf, wgb, wub, wdb, sem, *, layer):
    def fetch(expert, s):
        return (pltpu.make_async_copy(wg_hbm.at[layer, expert], wgf.at[s], sem.at[s, 0]),
                pltpu.make_async_copy(wu_hbm.at[layer, expert], wuf.at[s], sem.at[s, 1]),
                pltpu.make_async_copy(wd_hbm.at[layer, expert], wdf.at[s], sem.at[s, 2]))

    @pl.when(pl.program_id(0) == 0)
    def _first_fetch():
        for c in fetch(be_ref[0], slot_ref[0]):
            c.start()

    def one_block(b, rows):
        e = be_ref[b]
        slot = slot_ref[b]
        active = b < na_ref[0]

        @pl.when(active & (first_ref[b] == 1))
        def _load_expert():
            for c in fetch(e, slot):
                c.wait()
            wgb[...] = wgf[slot].astype(BF16)
            wub[...] = wuf[slot].astype(BF16)
            wdb[...] = wdf[slot].astype(BF16)
            nxt = next_ref[b]

            @pl.when(nxt >= 0)
            def _prefetch():
                for c in fetch(nxt, 1 - slot):
                    c.start(priority=1)

        @pl.when(active)
        def _compute():
            row = lax.broadcasted_iota(jnp.int32, (MOE_BLOCK, xs_ref.shape[1]), 0)
            words = jnp.where(row < rows_ref[b], xs_ref[rows, :], jnp.uint32(0))
            xb = _unpack_halves(words).astype(BF16)
            a = jnp.concatenate(
                [(jax.nn.silu(_dot(xb, wgb[:, c:c + MXU_COLS]))
                  * _dot(xb, wub[:, c:c + MXU_COLS])).astype(BF16)
                 for c in range(0, wgb.shape[1], MXU_COLS)], axis=1)
            ys_ref[rows, :] = _pack_halves(_dot(a, wdb[...]))

    for j in range(MOE_STEP_BLOCKS):
        one_block(pl.program_id(0) * MOE_STEP_BLOCKS + j, slice(j * MOE_BLOCK, (j + 1) * MOE_BLOCK))


def _moe_experts(plan, xs, w_gate, w_up, w_down, layer):
    n_rows = xs.shape[0]
    D = 2 * xs.shape[1]
    step_rows = MOE_STEP_BLOCKS * MOE_BLOCK
    assert n_rows % step_rows == 0
    DE = w_gate.shape[-1]
    rows = lambda g, *plan_refs: (jnp.minimum(g, (plan_refs[-1][0] - 1) // MOE_STEP_BLOCKS), 0)
    grid_spec = pltpu.PrefetchScalarGridSpec(
        num_scalar_prefetch=len(plan),
        grid=(n_rows // step_rows,),
        in_specs=[pl.BlockSpec((step_rows, D // 2), rows),
                  pl.BlockSpec(memory_space=pl.ANY), pl.BlockSpec(memory_space=pl.ANY),
                  pl.BlockSpec(memory_space=pl.ANY)],
        out_specs=pl.BlockSpec((step_rows, D // 2), rows),
        scratch_shapes=[pltpu.VMEM((2, D, DE), F32), pltpu.VMEM((2, D, DE), F32),
                        pltpu.VMEM((2, DE, D), F32),
                        pltpu.VMEM((D, DE), BF16), pltpu.VMEM((D, DE), BF16),
                        pltpu.VMEM((DE, D), BF16),
                        pltpu.SemaphoreType.DMA((2, 3))])
    return pl.pallas_call(
        functools.partial(_moe_kernel, layer=layer),
        grid_spec=grid_spec,
        out_shape=jax.ShapeDtypeStruct((n_rows, D // 2), jnp.uint32),
        compiler_params=pltpu.CompilerParams(dimension_semantics=("arbitrary",),
                                             vmem_limit_bytes=VMEM_LIMIT),
        name="moe_experts",
    )(*plan, xs, w_gate, w_up, w_down)


def _final_kernel(x1_ref, yg_ref, gw_ref, g_ref, o_ref):
    o_ref[...] = _rms(_combine(x1_ref, yg_ref, gw_ref), g_ref[...])


def _final_norm(x2d, yg, gw, g, tm):
    T, D = x2d.shape
    row = lambda i: (i, 0)
    return pl.pallas_call(
        _final_kernel,
        grid=(T // tm,),
        in_specs=[pl.BlockSpec((tm, D), row),
                  pl.BlockSpec((MOE_TOP_K, tm, D // 2), lambda i: (0, i, 0)),
                  pl.BlockSpec((GW_ROWS, tm), lambda i: (0, i)),
                  pl.BlockSpec((1, D), lambda i: (0, 0))],
        out_specs=pl.BlockSpec((tm, D), row),
        out_shape=jax.ShapeDtypeStruct((T, D), F32),
        compiler_params=pltpu.CompilerParams(dimension_semantics=("arbitrary",)),
        name="final_norm",
    )(x2d, yg, gw, g)


SC_GATHER_ROWS = 64


def _sc_gather(table, idx):
    info = plsc.get_sparse_core_info()
    n_cores, n_workers = info.num_cores, info.num_cores * info.num_subcores
    N, W = idx.shape[0], table.shape[1]
    per_w = N // n_workers
    n_ch = per_w // SC_GATHER_ROWS
    assert per_w * n_workers == N and n_ch * SC_GATHER_ROWS == per_w
    mesh = plsc.VectorSubcoreMesh(core_axis_name="c", subcore_axis_name="s")

    @functools.partial(
        pl.kernel, mesh=mesh,
        out_type=jax.ShapeDtypeStruct((N, W), table.dtype),
        scratch_types=[pltpu.VMEM((n_ch, SC_GATHER_ROWS), jnp.int32),
                       pltpu.VMEM((2, SC_GATHER_ROWS, W), table.dtype),
                       pltpu.SemaphoreType.DMA((2,)),
                       pltpu.SemaphoreType.DMA((2,))])
    def gather_kernel(table_hbm, idx_hbm, out_hbm, idx_v, rows_v, gsem, ssem):
        wid = lax.axis_index("s") * n_cores + lax.axis_index("c")
        base = wid * per_w
        pltpu.sync_copy(idx_hbm.at[wid], idx_v)

        def fetch(c):
            return pltpu.make_async_copy(table_hbm.at[idx_v.at[c]], rows_v.at[c % 2],
                                         gsem.at[c % 2])

        def write(c):
            return pltpu.make_async_copy(
                rows_v.at[c % 2], out_hbm.at[pl.ds(base + c * SC_GATHER_ROWS, SC_GATHER_ROWS)],
                ssem.at[c % 2])

        fetch(0).start()
        for c in range(n_ch):
            if c + 1 < n_ch:
                if c >= 1:
                    write(c - 1).wait()
                fetch(c + 1).start()
            fetch(c).wait()
            write(c).start()
        if n_ch >= 2:
            write(n_ch - 2).wait()
        write(n_ch - 1).wait()

    return gather_kernel(table, idx.reshape(n_workers, n_ch, SC_GATHER_ROWS))


def _sc_scatter(rows, pos, n_out):
    info = plsc.get_sparse_core_info()
    n_cores, n_workers = info.num_cores, info.num_cores * info.num_subcores
    K, T = pos.shape
    W = rows.shape[1]
    per_w = T // n_workers
    n_ch = per_w // SC_GATHER_ROWS
    assert per_w * n_workers == T and n_ch * SC_GATHER_ROWS == per_w
    mesh = plsc.VectorSubcoreMesh(core_axis_name="c", subcore_axis_name="s")

    @functools.partial(
        pl.kernel, mesh=mesh,
        out_type=jax.ShapeDtypeStruct((n_out, W), rows.dtype),
        scratch_types=[pltpu.VMEM((K, n_ch, SC_GATHER_ROWS), jnp.int32),
                       pltpu.VMEM((2, SC_GATHER_ROWS, W), rows.dtype),
                       pltpu.SemaphoreType.DMA((2,)),
                       pltpu.SemaphoreType.DMA((2, K))])
    def scatter_kernel(rows_hbm, idx_hbm, out_hbm, idx_v, buf_v, lsem, ssem):
        wid = lax.axis_index("s") * n_cores + lax.axis_index("c")
        base = wid * per_w
        for k in range(K):
            pltpu.sync_copy(idx_hbm.at[k, wid], idx_v.at[k])

        def load(c):
            return pltpu.make_async_copy(
                rows_hbm.at[pl.ds(base + c * SC_GATHER_ROWS, SC_GATHER_ROWS)], buf_v.at[c % 2],
                lsem.at[c % 2])

        def store(c, k):
            return pltpu.make_async_copy(buf_v.at[c % 2], out_hbm.at[idx_v.at[k, c]],
                                         ssem.at[c % 2, k])

        load(0).start()
        for c in range(n_ch):
            if c + 1 < n_ch:
                if c >= 1:
                    for k in range(K):
                        store(c - 1, k).wait()
                load(c + 1).start()
            load(c).wait()
            for k in range(K):
                store(c, k).start()
        for c in range(max(n_ch - 2, 0), n_ch):
            for k in range(K):
                store(c, k).wait()

    return scatter_kernel(rows, pos.reshape(K, n_workers, n_ch, SC_GATHER_ROWS))


def _rope_table(positions):
    half = ROT_DIM // 2
    inv_freq = jnp.power(ROPE_THETA, -jnp.arange(half, dtype=F32) / half)
    ang = positions.astype(F32).reshape(-1)[:, None] * inv_freq
    cos, sin = jnp.cos(ang), jnp.sin(ang)
    ones = jnp.ones((ang.shape[0], HEAD_DIM - ROT_DIM - half), F32)
    return jnp.tile(jnp.concatenate([cos, cos, sin, ones], axis=1), (1, LANES // HEAD_DIM))


def _block_plan(counts, T):
    i32 = jnp.int32
    n_blocks = -(-(T * MOE_TOP_K) // MOE_BLOCK) + N_EXPERTS
    experts = jnp.arange(N_EXPERTS, dtype=i32)
    blocks = jnp.arange(n_blocks, dtype=i32)
    n_blk = (counts + MOE_BLOCK - 1) // MOE_BLOCK
    blocks_end = jnp.cumsum(n_blk)
    blk_expert = jnp.minimum(jnp.sum(blocks_end[None, :] <= blocks[:, None], axis=1),
                             N_EXPERTS - 1).astype(i32)
    is_active = n_blk > 0
    ordinal = jnp.cumsum(is_active.astype(i32)) - 1
    later_active = is_active[None, :] & (experts[None, :] > experts[:, None])
    next_active = jnp.min(jnp.where(later_active, experts[None, :], N_EXPERTS), axis=1)
    next_active = jnp.where(next_active < N_EXPERTS, next_active, -1).astype(i32)
    tables = jnp.stack([blocks_end - n_blk, counts, ordinal, next_active], axis=1).astype(i32)
    onehot = (blk_expert[:, None] == experts[None, :]).astype(i32)
    looked = jnp.sum(onehot[:, :, None] * tables[None, :, :], axis=1)
    blk_in_expert = blocks - looked[:, 0]
    blk_first = (blk_in_expert == 0).astype(i32)
    blk_rows = jnp.clip(looked[:, 1] - blk_in_expert * MOE_BLOCK, 0, MOE_BLOCK).astype(i32)
    plan = (blk_expert, blk_first, (looked[:, 2] % 2).astype(i32), looked[:, 3], blk_rows,
            blocks_end[-1:].astype(i32))
    return n_blocks, plan


def kernel(x, positions, w_in, w_out, g_mix_norm, g_head_norm, g_gmlp_vnorm, w_spatial, b_spatial,
           g_ffn_norm, w_router_group, w_router_expert, w_expert_gate, w_expert_up, w_expert_down,
           g_final):
    B, S, D = x.shape
    T = B * S
    depth = w_in.shape[0]
    tm = min(512, T)
    rope = _rope_table(positions)
    xc = x.reshape(T, D)
    moe_out = None
    for l in range(depth):
        gh = g_head_norm[l].reshape(1, -1)
        proj, xc = _inproj(xc, moe_out, g_mix_norm[l].reshape(1, D), w_in[l].astype(BF16),
                           rope, tm)
        o_sb, o_moba = _attention(proj, gh[:, :W_SB], gh[:, W_SB:W_SB + W_MOBA], B, S)
        b_exp = jnp.repeat(b_spatial[l].T, HEAD_DIM, axis=1)
        o_gmlp = _gmlp(proj, g_gmlp_vnorm[l].reshape(1, -1), w_spatial[l], b_exp,
                       gh[:, W_SB + W_MOBA:], min(2 * tm, T))
        w_r = jnp.concatenate(
            [w_router_group[l].T, jnp.zeros((ROUTER_EXPERT_ROW - N_EXPERT_GROUPS, D), F32),
             w_router_expert[l].T,
             jnp.zeros((ROUTER_ROWS - ROUTER_EXPERT_ROW - N_EXPERTS, D), F32)], axis=0)
        r_hi, r_lo = _split_bf16(w_r)
        xc, h, logits_t = _out_proj(o_sb, o_moba, o_gmlp, xc, w_out[l].astype(BF16),
                                    g_ffn_norm[l].reshape(1, D), r_hi, r_lo, min(2 * tm, T))
        pos, gate_w, counts = _router(logits_t, min(ROUTER_TM, T))
        n_blocks, plan = _block_plan(counts[:, 0], T)
        xs = _sc_scatter(h, pos, n_blocks * MOE_BLOCK)
        ys = _moe_experts(plan, xs, w_expert_gate, w_expert_up, w_expert_down, l)
        yg = _sc_gather(ys, pos.reshape(-1)).reshape(MOE_TOP_K, T, D // 2)
        moe_out = (yg, gate_w)
    return _final_norm(xc, moe_out[0], moe_out[1], g_final.reshape(1, D), tm).reshape(B, S, D)
```
